```python
import math
import jax, jax.numpy as jnp
from jax import lax
import numpy as np

D_MODEL = 1024
BATCH = 32
SEQ = 2048
DEPTH = 1
DEC_BATCH = 8
DEC_SEQ = 32
PAST_LEN = 4096

CHUNK = 64
GDN_HEADS = D_MODEL // 256
GDN_DK = 128
GDN_DV = 128
CONV_W = 4
DIFF_HEADS = D_MODEL // 256
DIFF_DH = 64
DIFF_DV = 2 * DIFF_DH
Q_BLOCK = 128
GDN_QK = GDN_HEADS * GDN_DK
GDN_WIDTH = GDN_HEADS * GDN_DV
GDN_CONV_CH = 2 * GDN_QK + GDN_WIDTH
DIFF_QK = DIFF_HEADS * 2 * DIFF_DH
DIFF_WIDTH = DIFF_HEADS * DIFF_DV
MIX_WIDTH = GDN_WIDTH + DIFF_WIDTH
IN_COLS = GDN_CONV_CH + GDN_WIDTH + 2 * GDN_HEADS + 2 * DIFF_QK + DIFF_WIDTH
N_GROUPS = 4
EXPERTS_PER_GROUP = 8
N_EXPERTS = N_GROUPS * EXPERTS_PER_GROUP
TOP_K = 2
D_EXPERT = D_MODEL // 2
MOE_BLOCK = 128
DEEPNORM_ALPHA = (2 * DEPTH) ** 0.25
DEEPNORM_BETA = (8 * DEPTH) ** -0.25
NEG_INF = -1e30

kernel_name = 'hybrid_gdn_diffattn_hmoe_stream_step'


def _layernorm(x, g, b, eps=1e-5):
    xf = x.astype(jnp.float32)
    mu = xf.mean(-1, keepdims=True)
    var = jnp.square(xf - mu).mean(-1, keepdims=True)
    return ((xf - mu) * lax.rsqrt(var + eps) * g.astype(jnp.float32) + b.astype(jnp.float32)).astype(x.dtype)


def _rmsnorm(x, w, eps=1e-6):
    xf = x.astype(jnp.float32)
    return xf * lax.rsqrt(jnp.mean(xf * xf, -1, keepdims=True) + eps) * w.astype(jnp.float32)


def _l2norm(x, eps=1e-6):
    return x * lax.rsqrt(jnp.sum(x * x, -1, keepdims=True) + eps)


def _alibi_slopes():
    return jnp.asarray(2.0 ** (-8.0 * (np.arange(DIFF_HEADS) + 1) / DIFF_HEADS), dtype=jnp.float32)


def _causal_conv(xin, conv_prev, conv_w):
    L = xin.shape[1]
    xc = jnp.concatenate([conv_prev.astype(xin.dtype), xin], axis=1)
    y = conv_w[0] * xc[:, 0:L]
    for w in range(1, CONV_W):
        y = y + conv_w[w] * xc[:, w:w + L]
    return jax.nn.silu(y), xc[:, L:]


def _gdn_chunk(S, chunk):
    q, k, v, g, beta = chunk
    C = q.shape[2]
    gam = jnp.cumsum(g, axis=-1)
    incl = jnp.tril(jnp.ones((C, C), dtype=bool))
    strict = jnp.tril(jnp.ones((C, C), dtype=bool), k=-1)
    rel = gam[..., :, None] - gam[..., None, :]
    decay = jnp.where(incl, jnp.exp(jnp.where(incl, rel, 0.0)), 0.0)
    a_strict = jnp.where(strict, beta[..., :, None] * jnp.einsum('bhid,bhjd->bhij', k, k) * decay, 0.0)
    rhs = jnp.concatenate([v * beta[..., None], k * (beta * jnp.exp(gam))[..., None]], axis=-1)
    sol = lax.linalg.triangular_solve(a_strict + jnp.eye(C, dtype=a_strict.dtype), rhs,
                                      left_side=True, lower=True, unit_diagonal=True)
    u = sol[..., :GDN_DV] - jnp.einsum('bhck,bhkv->bhcv', sol[..., GDN_DV:], S)
    o = (jnp.einsum('bhck,bhkv->bhcv', q * jnp.exp(gam)[..., None], S)
         + jnp.einsum('bhij,bhjv->bhiv', jnp.einsum('bhik,bhjk->bhij', q, k) * decay, u))
    g_last = gam[..., -1:]
    S_new = S * jnp.exp(g_last)[..., None] + jnp.einsum('bhck,bhcv->bhkv', k * jnp.exp(g_last - gam)[..., None], u)
    return S_new, o


def _gated_delta(q, k, v, g, beta, S0):
    B, L, H, _ = q.shape
    C = min(L, CHUNK)
    n = L // C

    def to_chunks(t):
        t = t.reshape((B, n, C, H) + t.shape[3:])
        return jnp.moveaxis(t, (1, 3), (0, 2))

    S_final, o = lax.scan(_gdn_chunk, S0, (to_chunks(q), to_chunks(k), to_chunks(v), to_chunks(g), to_chunks(beta)))
    o = jnp.moveaxis(o, (0, 2), (1, 3)).reshape(B, L, H, GDN_DV)
    return o, S_final


def _diff_attend(q, k, v, q_pos, k_pos, lam):
    qf = q.astype(jnp.float32) * (DIFF_DH ** -0.5)
    kf = k.astype(jnp.float32)
    dist = jnp.abs(q_pos[:, None] - k_pos[None, :]).astype(jnp.float32)
    allowed = (k_pos[None, :] // CHUNK) <= (q_pos[:, None] // CHUNK)
    bias = jnp.where(allowed[None], -_alibi_slopes()[:, None, None] * dist[None], NEG_INF)
    s1 = jnp.einsum('bqhd,bkhd->bhqk', qf[..., :DIFF_DH], kf[..., :DIFF_DH]) + bias
    s2 = jnp.einsum('bqhd,bkhd->bhqk', qf[..., DIFF_DH:], kf[..., DIFF_DH:]) + bias
    attn = jax.nn.softmax(s1, axis=-1) - lam * jax.nn.softmax(s2, axis=-1)
    return jnp.einsum('bhqk,bkhe->bqhe', attn, v.astype(jnp.float32))


def _diff_attention(q, k, v, q_pos, k_pos, lam):
    B, Lq, H, E = q.shape
    QB = min(Lq, Q_BLOCK)
    n = Lq // QB
    qb = jnp.moveaxis(q.reshape(B, n, QB, H, E), 1, 0)
    pb = q_pos.reshape(n, QB)
    ob = lax.map(lambda a: _diff_attend(a[0], k, v, a[1], k_pos, lam), (qb, pb))
    return jnp.moveaxis(ob, 0, 1).reshape(B, Lq, H, DIFF_DV)


def _dispatch_experts(xf, expert, gate, w1, w3, w2):
    N, D = xf.shape
    P = N * TOP_K
    n_blocks = -(-P // MOE_BLOCK) + N_EXPERTS
    P_pad = n_blocks * MOE_BLOCK
    flat_e = expert.reshape(P)
    flat_tok = jnp.repeat(jnp.arange(N, dtype=jnp.int32), TOP_K)
    flat_w = gate.reshape(P)
    order = jnp.argsort(flat_e)
    sorted_e = flat_e[order]
    counts = jnp.bincount(flat_e, length=N_EXPERTS)
    starts = jnp.cumsum(counts) - counts
    padded = (counts + MOE_BLOCK - 1) // MOE_BLOCK * MOE_BLOCK
    padded_end = jnp.cumsum(padded)
    padded_start = padded_end - padded
    dest = padded_start[sorted_e] + jnp.arange(P, dtype=jnp.int32) - starts[sorted_e]
    tok_pad = jnp.full((P_pad,), N, jnp.int32).at[dest].set(flat_tok[order])
    w_pad = jnp.zeros((P_pad,), jnp.float32).at[dest].set(flat_w[order])
    block_expert = jnp.minimum(
        jnp.searchsorted(padded_end, jnp.arange(n_blocks, dtype=jnp.int32) * MOE_BLOCK, side='right'),
        N_EXPERTS - 1)
    x_ext = jnp.concatenate([xf, jnp.zeros((1, D), xf.dtype)], axis=0)
    xb = x_ext[tok_pad].reshape(n_blocks, MOE_BLOCK, D)

    def expert_block(args):
        xblk, e = args
        h = jax.nn.silu(xblk @ w1[e]) * (xblk @ w3[e])
        return h @ w2[e]

    yb = lax.map(expert_block, (xb, block_expert))
    y = yb.reshape(P_pad, D).astype(jnp.float32) * w_pad[:, None]
    return jax.ops.segment_sum(y, tok_pad, num_segments=N + 1)[:N].astype(xf.dtype)


def _hier_moe(x, rg_w, rg_b, re_w, re_b, w1, w3, w2):
    B, L, D = x.shape
    N = B * L
    xf = x.reshape(N, D)
    g_logits = (xf @ rg_w).astype(jnp.float32) + rg_b.astype(jnp.float32)
    grp = jnp.argmax(g_logits, axis=-1)
    g_gate = jnp.take_along_axis(jax.nn.softmax(g_logits, axis=-1), grp[:, None], axis=-1)
    e_logits = ((xf @ re_w).astype(jnp.float32) + re_b.astype(jnp.float32)).reshape(N, N_GROUPS, EXPERTS_PER_GROUP)
    e_logits = jnp.take_along_axis(e_logits, grp[:, None, None], axis=1)[:, 0]
    top_val, top_idx = lax.top_k(e_logits, TOP_K)
    gate = g_gate * jax.nn.softmax(top_val, axis=-1)
    expert = grp[:, None].astype(jnp.int32) * EXPERTS_PER_GROUP + top_idx.astype(jnp.int32)
    return _dispatch_experts(xf, expert, gate, w1, w3, w2).reshape(B, L, D)


def _hybrid_layer(x, conv_prev, S_prev, k_prev, v_prev, p, layer_idx):
    B, L, _ = x.shape
    past = 0 if k_prev is None else k_prev.shape[1]
    proj = x @ p['w_in']
    splits = [GDN_CONV_CH, GDN_CONV_CH + GDN_WIDTH, GDN_CONV_CH + GDN_WIDTH + GDN_HEADS,
              GDN_CONV_CH + GDN_WIDTH + 2 * GDN_HEADS, GDN_CONV_CH + GDN_WIDTH + 2 * GDN_HEADS + DIFF_QK,
              GDN_CONV_CH + GDN_WIDTH + 2 * GDN_HEADS + 2 * DIFF_QK]
    qkv_g, z, a, b, q_d, k_d, v_d = jnp.split(proj, splits, axis=-1)

    qkv_c, conv_new = _causal_conv(qkv_g, conv_prev, p['conv_w'])
    q_g, k_g, v_g = jnp.split(qkv_c.astype(jnp.float32), [GDN_QK, 2 * GDN_QK], axis=-1)
    q_g = _l2norm(q_g.reshape(B, L, GDN_HEADS, GDN_DK)) * (GDN_DK ** -0.5)
    k_g = _l2norm(k_g.reshape(B, L, GDN_HEADS, GDN_DK))
    v_g = v_g.reshape(B, L, GDN_HEADS, GDN_DV)
    g = -jnp.exp(p['a_log'].astype(jnp.float32)) * jax.nn.softplus(a.astype(jnp.float32) + p['dt_bias'].astype(jnp.float32))
    beta = jax.nn.sigmoid(b.astype(jnp.float32))
    o_g, S_new = _gated_delta(q_g, k_g, v_g, g, beta, S_prev.astype(jnp.float32))
    o_g = _rmsnorm(o_g, p['gdn_norm_w']) * jax.nn.silu(z.astype(jnp.float32).reshape(B, L, GDN_HEADS, GDN_DV))

    q_d = q_d.reshape(B, L, DIFF_HEADS, 2 * DIFF_DH)
    k_d = k_d.reshape(B, L, DIFF_HEADS, 2 * DIFF_DH)
    v_d = v_d.reshape(B, L, DIFF_HEADS, DIFF_DV)
    if k_prev is None:
        k_all, v_all = k_d, v_d
    else:
        k_all = jnp.concatenate([k_prev.astype(k_d.dtype), k_d], axis=1)
        v_all = jnp.concatenate([v_prev.astype(v_d.dtype), v_d], axis=1)
    q_pos = past + jnp.arange(L, dtype=jnp.int32)
    k_pos = jnp.arange(past + L, dtype=jnp.int32)
    lam_init = 0.8 - 0.6 * math.exp(-0.3 * layer_idx)
    f = jnp.float32
    lam = (jnp.exp(jnp.sum(p['lam_q1'].astype(f) * p['lam_k1'].astype(f)))
           - jnp.exp(jnp.sum(p['lam_q2'].astype(f) * p['lam_k2'].astype(f))) + lam_init)
    o_d = _diff_attention(q_d, k_all, v_all, q_pos, k_pos, lam)
    o_d = _rmsnorm(o_d, p['subln_w']) * (1.0 - lam_init)

    mixed = jnp.concatenate([o_g.reshape(B, L, GDN_WIDTH), o_d.reshape(B, L, DIFF_WIDTH)], axis=-1).astype(x.dtype) @ p['w_o']
    x = _layernorm(DEEPNORM_ALPHA * x + mixed, p['ln1_g'], p['ln1_b'])
    moe = _hier_moe(x, p['router_g_w'], p['router_g_b'], p['router_e_w'], p['router_e_b'], p['w1'], p['w3'], p['w2'])
    x = _layernorm(DEEPNORM_ALPHA * x + moe, p['ln2_g'], p['ln2_b'])
    return x, conv_new, S_new.astype(x.dtype), k_d, v_d


def setup_inputs(seed: int = 0) -> dict:
    key = jax.random.key(seed)
    ks = jax.random.split(key, 32)
    f32 = jnp.float32
    nrm = lambda k, s: jax.random.normal(k, s, f32)
    col_scale = np.ones((IN_COLS,), np.float32)
    col_scale[2 * GDN_QK:GDN_CONV_CH] = DEEPNORM_BETA
    col_scale[IN_COLS - DIFF_WIDTH:] = DEEPNORM_BETA
    dt = jnp.exp(jax.random.uniform(ks[9], (DEPTH, GDN_HEADS), f32, math.log(1e-3), math.log(1e-1)))
    return {
        'x_prompt': nrm(ks[0], (BATCH, SEQ, D_MODEL)),
        'x_sample': nrm(ks[1], (DEC_BATCH, DEC_SEQ, D_MODEL)),
        'cache_attn_k': nrm(ks[2], (DEPTH, DEC_BATCH, PAST_LEN, DIFF_HEADS, 2 * DIFF_DH)),
        'cache_attn_v': nrm(ks[3], (DEPTH, DEC_BATCH, PAST_LEN, DIFF_HEADS, DIFF_DV)),
        'state_gdn': 0.1 * nrm(ks[4], (DEPTH, DEC_BATCH, GDN_HEADS, GDN_DK, GDN_DV)),
        'state_conv': nrm(ks[5], (DEPTH, DEC_BATCH, CONV_W - 1, GDN_CONV_CH)),
        'ln_in_g': 1.0 + 0.02 * nrm(ks[6], (D_MODEL,)),
        'ln_in_b': 0.02 * nrm(ks[7], (D_MODEL,)),
        'w_in': nrm(ks[8], (DEPTH, D_MODEL, IN_COLS)) * (D_MODEL ** -0.5) * jnp.asarray(col_scale),
        'conv_w': nrm(ks[10], (DEPTH, CONV_W, GDN_CONV_CH)) * (CONV_W ** -0.5),
        'gdn_a_log': jnp.log(jax.random.uniform(ks[11], (DEPTH, GDN_HEADS), f32, 1.0, 16.0)),
        'gdn_dt_bias': dt + jnp.log(-jnp.expm1(-dt)),
        'gdn_norm_w': 1.0 + 0.02 * nrm(ks[12], (DEPTH, GDN_DV)),
        'lam_q1': 0.1 * nrm(ks[13], (DEPTH, DIFF_DH)),
        'lam_k1': 0.1 * nrm(ks[14], (DEPTH, DIFF_DH)),
        'lam_q2': 0.1 * nrm(ks[15], (DEPTH, DIFF_DH)),
        'lam_k2': 0.1 * nrm(ks[16], (DEPTH, DIFF_DH)),
        'subln_w': 1.0 + 0.02 * nrm(ks[17], (DEPTH, DIFF_DV)),
        'w_o': nrm(ks[18], (DEPTH, MIX_WIDTH, D_MODEL)) * (MIX_WIDTH ** -0.5) * DEEPNORM_BETA,
        'ln1_g': 1.0 + 0.02 * nrm(ks[19], (DEPTH, D_MODEL)),
        'ln1_b': 0.02 * nrm(ks[20], (DEPTH, D_MODEL)),
        'router_g_w': nrm(ks[21], (DEPTH, D_MODEL, N_GROUPS)) * (D_MODEL ** -0.5),
        'router_g_b': 0.01 * nrm(ks[22], (DEPTH, N_GROUPS)),
        'router_e_w': nrm(ks[23], (DEPTH, D_MODEL, N_EXPERTS)) * (D_MODEL ** -0.5),
        'router_e_b': 0.01 * nrm(ks[24], (DEPTH, N_EXPERTS)),
        'w1': nrm(ks[25], (DEPTH, N_EXPERTS, D_MODEL, D_EXPERT)) * (D_MODEL ** -0.5),
        'w3': nrm(ks[26], (DEPTH, N_EXPERTS, D_MODEL, D_EXPERT)) * (D_MODEL ** -0.5),
        'w2': nrm(ks[27], (DEPTH, N_EXPERTS, D_EXPERT, D_MODEL)) * (D_EXPERT ** -0.5) * DEEPNORM_BETA,
        'ln2_g': 1.0 + 0.02 * nrm(ks[28], (DEPTH, D_MODEL)),
        'ln2_b': 0.02 * nrm(ks[29], (DEPTH, D_MODEL)),
    }


def reference(x_prompt, x_sample, cache_attn_k, cache_attn_v, state_gdn, state_conv,
              ln_in_g, ln_in_b, w_in, conv_w, gdn_a_log, gdn_dt_bias, gdn_norm_w,
              lam_q1, lam_k1, lam_q2, lam_k2, subln_w, w_o, ln1_g, ln1_b,
              router_g_w, router_g_b, router_e_w, router_e_b, w1, w3, w2, ln2_g, ln2_b):
    xp = _layernorm(x_prompt, ln_in_g, ln_in_b)
    xs = _layernorm(x_sample, ln_in_g, ln_in_b)
    Bp = x_prompt.shape[0]
    kp_l, vp_l, sp_l, cp_l, ks_l, vs_l, ss_l, cs_l = [], [], [], [], [], [], [], []
    for l in range(DEPTH):
        p = {'w_in': w_in[l], 'conv_w': conv_w[l], 'a_log': gdn_a_log[l], 'dt_bias': gdn_dt_bias[l],
             'gdn_norm_w': gdn_norm_w[l], 'lam_q1': lam_q1[l], 'lam_k1': lam_k1[l], 'lam_q2': lam_q2[l],
             'lam_k2': lam_k2[l], 'subln_w': subln_w[l], 'w_o': w_o[l], 'ln1_g': ln1_g[l], 'ln1_b': ln1_b[l],
             'router_g_w': router_g_w[l], 'router_g_b': router_g_b[l], 'router_e_w': router_e_w[l],
             'router_e_b': router_e_b[l], 'w1': w1[l], 'w3': w3[l], 'w2': w2[l],
             'ln2_g': ln2_g[l], 'ln2_b': ln2_b[l]}
        conv0 = jnp.zeros((Bp, CONV_W - 1, GDN_CONV_CH), xp.dtype)
        S0 = jnp.zeros((Bp, GDN_HEADS, GDN_DK, GDN_DV), jnp.float32)
        xp, cp, sp, kp, vp = _hybrid_layer(xp, conv0, S0, None, None, p, l)
        xs, cs, ss, kss, vss = _hybrid_layer(xs, state_conv[l], state_gdn[l], cache_attn_k[l], cache_attn_v[l], p, l)
        kp_l.append(kp); vp_l.append(vp); sp_l.append(sp); cp_l.append(cp)
        ks_l.append(kss); vs_l.append(vss); ss_l.append(ss); cs_l.append(cs)
    new_k_prompt = jnp.stack(kp_l, 0)
    new_v_prompt = jnp.stack(vp_l, 0)
    new_gdn_prompt = jnp.stack(sp_l, 0)
    new_conv_prompt = jnp.stack(cp_l, 0)
    new_k_sample = jnp.stack(ks_l, 0)
    new_v_sample = jnp.stack(vs_l, 0)
    new_gdn_sample = jnp.stack(ss_l, 0)
    new_conv_sample = jnp.stack(cs_l, 0)
    return (xp, xs, new_k_prompt, new_v_prompt, new_gdn_prompt, new_conv_prompt,
            new_k_sample, new_v_sample, new_gdn_sample, new_conv_sample)
```

```python
import functools
import math

import jax
import jax.numpy as jnp
import numpy as np
from jax import lax
from jax.experimental import pallas as pl
from jax.experimental.pallas import tpu as pltpu

F32 = jnp.float32
BF16 = jnp.bfloat16

CHUNK = 64
CONV_W = 4
GDN_HEADS = 4
GDN_D = 128
DIFF_HEADS = 4
DIFF_DH = 64
DIFF_DV = 2 * DIFF_DH
N_GROUPS = 4
EXPERTS_PER_GROUP = 8
N_EXPERTS = N_GROUPS * EXPERTS_PER_GROUP
TOP_K = 2
NEG_INF = -1e30
LANES = 128
CONV_PAD = 8

TOKEN_TILE = 512
GDN_TILE = 256
ATTN_TILE = 256
CACHE_TILE = 1024
EXPERT_BLOCK_MAX = 512
EXPERT_BLOCK_MIN = 128
VMEM_LIMIT = 48 * 1024 * 1024


def _params(*sem):
    return pltpu.CompilerParams(dimension_semantics=sem, vmem_limit_bytes=VMEM_LIMIT)


def _dot(a, b):
    return jnp.dot(a, b, preferred_element_type=F32)


def _dot_nt(a, b):
    return lax.dot_general(a, b, (((1,), (1,)), ((), ())), preferred_element_type=F32)


def _sigmoid(x):
    return 1.0 / (1.0 + jnp.exp(-x))


def _silu(x):
    return x * _sigmoid(x)


def _softplus(x):
    return jnp.maximum(x, 0.0) + jnp.log(1.0 + jnp.exp(-jnp.abs(x)))


def _layernorm(x, g, b, eps=1e-5):
    mu = jnp.mean(x, axis=-1, keepdims=True)
    xc = x - mu
    var = jnp.mean(xc * xc, axis=-1, keepdims=True)
    return xc * lax.rsqrt(var + eps) * g + b


def _split3(x):
    hi = x.astype(BF16)
    r1 = x - hi.astype(F32)
    mid = r1.astype(BF16)
    lo = (r1 - mid.astype(F32)).astype(BF16)
    return hi, mid, lo


def _proj_kernel(x_ref, g_ref, b_ref, w_ref, wab_ref,
                 qkv_ref, z_ref, q_ref, k_ref, v_ref, kb_ref, vb_ref, ab_ref):
    xb = _layernorm(x_ref[...], g_ref[...], b_ref[...]).astype(BF16)
    w = 512
    for j in range(3):
        qkv_ref[:, j * w:(j + 1) * w] = _dot(xb, w_ref[:, j * w:(j + 1) * w]).astype(BF16)
    z_ref[...] = _dot(xb, w_ref[:, 3 * w:4 * w]).astype(BF16)
    q_ref[...] = _dot(xb, w_ref[:, 4 * w:5 * w]).astype(BF16)
    k = _dot(xb, w_ref[:, 5 * w:6 * w])
    k_ref[...] = k
    kb_ref[...] = k.astype(BF16)
    v = _dot(xb, w_ref[:, 6 * w:7 * w])
    v_ref[...] = v
    vb_ref[...] = v.astype(BF16)
    ab_ref[...] = _dot(xb, wab_ref[...])


def _proj(x2, ln_g, ln_b, w_main, w_ab):
    n, d = x2.shape
    tm = min(TOKEN_TILE, n)
    row = lambda i: (i, 0)
    fix = lambda i: (0, 0)
    outs = [(1536, BF16), (512, BF16), (512, BF16), (512, F32), (512, F32), (512, BF16), (512, BF16),
            (LANES, F32)]
    return pl.pallas_call(
        _proj_kernel,
        grid=(n // tm,),
        in_specs=[pl.BlockSpec((tm, d), row), pl.BlockSpec((1, d), fix), pl.BlockSpec((1, d), fix),
                  pl.BlockSpec(w_main.shape, fix), pl.BlockSpec(w_ab.shape, fix)],
        out_specs=[pl.BlockSpec((tm, c), row) for c, _ in outs],
        out_shape=[jax.ShapeDtypeStruct((n, c), t) for c, t in outs],
        compiler_params=_params("parallel"),
        name="proj",
    )(x2, ln_g, ln_b, w_main, w_ab)


def _unit_lower_inverse(a, c):
    rows = lax.broadcasted_iota(jnp.int32, (c, c), 0)
    cols = lax.broadcasted_iota(jnp.int32, (c, c), 1)
    p = jnp.where(rows == cols, 1.0, 0.0) - a
    ak = a
    for _ in range(int(math.log2(c)) - 1):
        akb = ak.astype(BF16)
        ak = _dot(akb, akb)
        p = p + _dot(p.astype(BF16), ak.astype(BF16))
    return p


def _gdn_kernel(qkv_ref, z_ref, ab_ref, convp_ref, s0_ref, cw_ref, alog_ref, dtb_ref, nw_ref,
                o_ref, convn_ref, sn_ref, xc_scr, y_scr, s_scr, *, tl, c):
    l = pl.program_id(1)
    d = GDN_D
    hw = GDN_HEADS * d

    @pl.when(l == 0)
    def _():
        xc_scr[0:CONV_PAD, :] = convp_ref[0]
        s_scr[...] = s0_ref[0]

    @pl.when(l > 0)
    def _():
        xc_scr[0:CONV_PAD, :] = xc_scr[tl:tl + CONV_PAD, :]

    xc_scr[CONV_PAD:CONV_PAD + tl, :] = qkv_ref[0].astype(F32)
    cw = cw_ref[...]
    y = cw[0:1] * xc_scr[CONV_PAD - 3:CONV_PAD - 3 + tl, :]
    for w in range(1, CONV_W):
        y = y + cw[w:w + 1] * xc_scr[CONV_PAD - 3 + w:CONV_PAD - 3 + w + tl, :]
    y_scr[...] = _silu(y)
    convn_ref[0] = xc_scr[tl:tl + CONV_PAD, :]

    ab = ab_ref[0]
    g_all = -jnp.exp(alog_ref[...]) * _softplus(ab + dtb_ref[...])
    beta_all = _sigmoid(ab)
    nw = nw_ref[...]

    rows = lax.broadcasted_iota(jnp.int32, (c, c), 0)
    cols = lax.broadcasted_iota(jnp.int32, (c, c), 1)
    incl = cols <= rows
    strict = cols < rows
    tri = jnp.where(incl, 1.0, 0.0).astype(BF16)
    grow = lax.broadcasted_iota(jnp.int32, (c, LANES), 0)
    glane = lax.broadcasted_iota(jnp.int32, (c, LANES), 1)

    for ci in range(tl // c):
        r0 = ci * c
        for h in range(GDN_HEADS):
            g_col = g_all[r0:r0 + c, h:h + 1]
            beta = beta_all[r0:r0 + c, GDN_HEADS + h:GDN_HEADS + h + 1]
            gmat = jnp.where(glane < c, jnp.where(grow > glane, g_col, 0.0), g_col)
            g_hi, g_mid, g_lo = _split3(gmat)
            cum = _dot(tri, g_hi) + _dot(tri, g_mid) + _dot(tri, g_lo)
            rel = cum[:, 0:c]
            gam = cum[:, c:c + 1]
            g_last = gam[c - 1:c, :]
            decay = jnp.where(incl, jnp.exp(jnp.where(incl, rel, 0.0)), 0.0)
            eg = jnp.exp(gam)
            ek = jnp.exp(g_last - gam)

            qh = y_scr[r0:r0 + c, h * d:(h + 1) * d]
            kh = y_scr[r0:r0 + c, hw + h * d:hw + (h + 1) * d]
            vh = y_scr[r0:r0 + c, 2 * hw + h * d:2 * hw + (h + 1) * d]
            qn = qh * lax.rsqrt(jnp.sum(qh * qh, axis=-1, keepdims=True) + 1e-6) * (d ** -0.5)
            kn = kh * lax.rsqrt(jnp.sum(kh * kh, axis=-1, keepdims=True) + 1e-6)
            qb = qn.astype(BF16)
            kb = kn.astype(BF16)
            kk = _dot_nt(kb, kb)
            qk = _dot_nt(qb, kb)
            a = jnp.where(strict, beta * kk * decay, 0.0)
            tinv = _unit_lower_inverse(a, c).astype(BF16)
            sol_v = _dot(tinv, (vh * beta).astype(BF16))
            sol_k = _dot(tinv, (kn * (beta * eg)).astype(BF16))

            s = s_scr[h]
            sb = s.astype(BF16)
            u = sol_v - _dot(sol_k.astype(BF16), sb)
            ub = u.astype(BF16)
            o = _dot((qn * eg).astype(BF16), sb) + _dot((qk * decay).astype(BF16), ub)
            kd_t = jnp.transpose(kn * ek).astype(BF16)
            s_scr[h] = s * jnp.exp(g_last) + _dot(kd_t, ub)

            zh = z_ref[0, r0:r0 + c, h * d:(h + 1) * d].astype(F32)
            on = o * lax.rsqrt(jnp.mean(o * o, axis=-1, keepdims=True) + 1e-6) * nw
            o_ref[0, r0:r0 + c, h * d:(h + 1) * d] = (on * _silu(zh)).astype(BF16)

    @pl.when(l == pl.num_programs(1) - 1)
    def _():
        sn_ref[0] = s_scr[...]


def _gdn(qkv, z, ab, conv_prev8, s0, conv_w, alog_row, dtb_row, nw_row):
    b, l, cc = qkv.shape
    c = min(l, CHUNK)
    tl = min(l, GDN_TILE)
    hw = GDN_HEADS * GDN_D
    tile = lambda i, j: (i, j, 0)
    per_b = lambda i, j: (i, 0, 0)
    fix = lambda i, j: (0, 0)
    return pl.pallas_call(
        functools.partial(_gdn_kernel, tl=tl, c=c),
        grid=(b, l // tl),
        in_specs=[pl.BlockSpec((1, tl, cc), tile), pl.BlockSpec((1, tl, hw), tile),
                  pl.BlockSpec((1, tl, LANES), tile), pl.BlockSpec((1, CONV_PAD, cc), per_b),
                  pl.BlockSpec((1, GDN_HEADS, GDN_D, GDN_D), lambda i, j: (i, 0, 0, 0)),
                  pl.BlockSpec((CONV_W, cc), fix), pl.BlockSpec((1, LANES), fix),
                  pl.BlockSpec((1, LANES), fix), pl.BlockSpec((1, GDN_D), fix)],
        out_specs=[pl.BlockSpec((1, tl, hw), tile), pl.BlockSpec((1, CONV_PAD, cc), per_b),
                   pl.BlockSpec((1, GDN_HEADS, GDN_D, GDN_D), lambda i, j: (i, 0, 0, 0))],
        out_shape=[jax.ShapeDtypeStruct((b, l, hw), BF16),
                   jax.ShapeDtypeStruct((b, CONV_PAD, cc), F32),
                   jax.ShapeDtypeStruct((b, GDN_HEADS, GDN_D, GDN_D), F32)],
        scratch_shapes=[pltpu.VMEM((tl + CONV_PAD, cc), F32), pltpu.VMEM((tl, cc), F32),
                        pltpu.VMEM((GDN_HEADS, GDN_D, GDN_D), F32)],
        compiler_params=_params("parallel", "arbitrary"),
        name="gdn",
    )(qkv, z, ab, conv_prev8, s0, conv_w, alog_row, dtb_row, nw_row)


def _lambda(lamv, lam_init):
    l1 = jnp.sum(lamv[0:1] * lamv[1:2], axis=-1, keepdims=True)
    l2 = jnp.sum(lamv[2:3] * lamv[3:4], axis=-1, keepdims=True)
    return jnp.exp(l1) - jnp.exp(l2) + lam_init


def _attn_bias(q0, k0, nq, nk, slope):
    qpos = q0 + lax.broadcasted_iota(jnp.int32, (nq, nk), 0)
    kpos = k0 + lax.broadcasted_iota(jnp.int32, (nq, nk), 1)
    dist = jnp.abs(qpos - kpos).astype(F32)
    shift = int(math.log2(CHUNK))
    allowed = (kpos >> shift) <= (qpos >> shift)
    return jnp.where(allowed, -slope * dist, NEG_INF)


def _attn_init(stats):
    for m, l, a in stats:
        m[...] = jnp.full(m.shape, NEG_INF, F32)
        l[...] = jnp.zeros(l.shape, F32)
        a[...] = jnp.zeros(a.shape, F32)


def _attn_update(q, k, v, bias, stats):
    for i, (m, l, a) in enumerate(stats):
        lo, hi = i * DIFF_DH, (i + 1) * DIFF_DH
        s = _dot_nt(q[:, lo:hi], k[:, lo:hi]) + bias
        m_new = jnp.maximum(m[...], jnp.max(s, axis=-1, keepdims=True))
        alpha = jnp.exp(m[...] - m_new)
        p = jnp.exp(s - m_new)
        l[...] = alpha * l[...] + jnp.sum(p, axis=-1, keepdims=True)
        a[...] = alpha * a[...] + _dot(p.astype(BF16), v)
        m[...] = m_new


def _attn_finish(stats, lam, sw, lam_init):
    (_, l1, a1), (_, l2, a2) = stats
    o = a1[...] / l1[...] - lam * (a2[...] / l2[...])
    return o * lax.rsqrt(jnp.mean(o * o, axis=-1, keepdims=True) + 1e-6) * sw * (1.0 - lam_init)


def _attn_prompt_kernel(slopes_ref, lamv_ref, sw_ref, q_ref, k_ref, v_ref, o_ref,
                        m1, l1, a1, m2, l2, a2, *, seq, tile, lam_init):
    slope = slopes_ref[pl.program_id(1)]
    lam = _lambda(lamv_ref[...], lam_init)
    sw = sw_ref[...]
    stats = ((m1, l1, a1), (m2, l2, a2))

    def q_body(qi, carry):
        q0 = pl.multiple_of(qi * tile, tile)
        q = (q_ref[0, pl.ds(q0, tile), :].astype(F32) * (DIFF_DH ** -0.5)).astype(BF16)
        _attn_init(stats)

        def k_body(kj, c2):
            k0 = pl.multiple_of(kj * tile, tile)
            k = k_ref[0, pl.ds(k0, tile), :]
            v = v_ref[0, pl.ds(k0, tile), :]
            _attn_update(q, k, v, _attn_bias(q0, k0, tile, tile, slope), stats)
            return c2

        lax.fori_loop(0, qi + 1, k_body, 0)
        o_ref[0, pl.ds(q0, tile), :] = _attn_finish(stats, lam, sw, lam_init).astype(BF16)
        return carry

    lax.fori_loop(0, seq // tile, q_body, 0)


def _attn_prompt(q, kb, vb, slopes, lamv, sw_row, lam_init):
    b, l, _ = q.shape
    tile = min(ATTN_TILE, l)
    assert tile % CHUNK == 0 and l % tile == 0
    head = lambda i, h: (i, 0, h)
    fix = lambda i, h: (0, 0)
    stat = [pltpu.VMEM((tile, 1), F32), pltpu.VMEM((tile, 1), F32), pltpu.VMEM((tile, DIFF_DV), F32)]
    return pl.pallas_call(
        functools.partial(_attn_prompt_kernel, seq=l, tile=tile, lam_init=lam_init),
        grid=(b, DIFF_HEADS),
        in_specs=[pl.BlockSpec(memory_space=pltpu.SMEM),
                  pl.BlockSpec(lamv.shape, fix), pl.BlockSpec(sw_row.shape, fix),
                  pl.BlockSpec((1, l, DIFF_DV), head), pl.BlockSpec((1, l, DIFF_DV), head),
                  pl.BlockSpec((1, l, DIFF_DV), head)],
        out_specs=pl.BlockSpec((1, l, DIFF_DV), head),
        out_shape=jax.ShapeDtypeStruct((b, l, DIFF_HEADS * DIFF_DV), BF16),
        scratch_shapes=stat + stat,
        compiler_params=_params("parallel", "parallel"),
        name="attn_prompt",
    )(slopes, lamv, sw_row, q, kb, vb)


def _attn_sample_kernel(slopes_ref, lamv_ref, sw_ref, q_ref, kc_ref, vc_ref, kn_ref, vn_ref, o_ref,
                        m1, l1, a1, m2, l2, a2, *, past, tk, lam_init):
    j = pl.program_id(2)
    n_cache = past // tk
    slope = slopes_ref[pl.program_id(1)]
    stats = ((m1, l1, a1), (m2, l2, a2))
    lq = q_ref.shape[1]
    q = (q_ref[0].astype(F32) * (DIFF_DH ** -0.5)).astype(BF16)

    @pl.when(j == 0)
    def _():
        _attn_init(stats)

    @pl.when(j < n_cache)
    def _():
        k = kc_ref[0].astype(BF16)
        v = vc_ref[0].astype(BF16)
        _attn_update(q, k, v, _attn_bias(past, j * tk, lq, tk, slope), stats)

    @pl.when(j == n_cache)
    def _():
        _attn_update(q, kn_ref[0], vn_ref[0], _attn_bias(past, past, lq, lq, slope), stats)
        lam = _lambda(lamv_ref[...], lam_init)
        o_ref[0] = _attn_finish(stats, lam, sw_ref[...], lam_init).astype(BF16)


def _attn_sample(q, kb, vb, k_cache, v_cache, slopes, lamv, sw_row, lam_init):
    b, l, _ = q.shape
    past = k_cache.shape[1]
    tk = min(CACHE_TILE, past)
    assert past % tk == 0 and past % CHUNK == 0 and l <= CHUNK
    n_cache = past // tk
    head = lambda i, h, j: (i, 0, h)
    cache = lambda i, h, j: (i, jnp.minimum(j, n_cache - 1), h)
    fix = lambda i, h, j: (0, 0)
    stat = [pltpu.VMEM((l, 1), F32), pltpu.VMEM((l, 1), F32), pltpu.VMEM((l, DIFF_DV), F32)]
    return pl.pallas_call(
        functools.partial(_attn_sample_kernel, past=past, tk=tk, lam_init=lam_init),
        grid=(b, DIFF_HEADS, n_cache + 1),
        in_specs=[pl.BlockSpec(memory_space=pltpu.SMEM),
                  pl.BlockSpec(lamv.shape, fix), pl.BlockSpec(sw_row.shape, fix),
                  pl.BlockSpec((1, l, DIFF_DV), head),
                  pl.BlockSpec((1, tk, DIFF_DV), cache), pl.BlockSpec((1, tk, DIFF_DV), cache),
                  pl.BlockSpec((1, l, DIFF_DV), head), pl.BlockSpec((1, l, DIFF_DV), head)],
        out_specs=pl.BlockSpec((1, l, DIFF_DV), head),
        out_shape=jax.ShapeDtypeStruct((b, l, DIFF_HEADS * DIFF_DV), BF16),
        scratch_shapes=stat + stat,
        compiler_params=_params("parallel", "parallel", "arbitrary"),
        name="attn_sample",
    )(slopes, lamv, sw_row, q, k_cache, v_cache, kb, vb)


def _mix_kernel(x_ref, og_ref, od_ref, lig_ref, lib_ref, wo_ref, g1_ref, b1_ref, rw_ref, rb_ref,
                x1_ref, x1b_ref, route_ref, *, alpha):
    xn = _layernorm(x_ref[...], lig_ref[...], lib_ref[...])
    hw = og_ref.shape[1]
    mixed = _dot(og_ref[...], wo_ref[0:hw, :]) + _dot(od_ref[...], wo_ref[hw:, :])
    x1 = _layernorm(alpha * xn + mixed, g1_ref[...], b1_ref[...])
    x1_ref[...] = x1
    x1b_ref[...] = x1.astype(BF16)

    x_hi = x1.astype(BF16)
    x_lo = (x1 - x_hi.astype(F32)).astype(BF16)
    lg = (_dot(x_hi, rw_ref[0]) + _dot(x_lo, rw_ref[0]) + _dot(x_hi, rw_ref[1])) + rb_ref[...]
    lane = lax.broadcasted_iota(jnp.int32, lg.shape, 1)
    lanef = lane.astype(F32)
    big = float(LANES)

    gl = jnp.where(lane < N_GROUPS, lg, -jnp.inf)
    gmax = jnp.max(gl, axis=-1, keepdims=True)
    grp = jnp.min(jnp.where(gl == gmax, lanef, big), axis=-1, keepdims=True)
    g_gate = 1.0 / jnp.sum(jnp.where(lane < N_GROUPS, jnp.exp(gl - gmax), 0.0), axis=-1, keepdims=True)

    e_lo = N_GROUPS + grp * EXPERTS_PER_GROUP
    el = jnp.where((lanef >= e_lo) & (lanef < e_lo + EXPERTS_PER_GROUP), lg, -jnp.inf)
    v1 = jnp.max(el, axis=-1, keepdims=True)
    i1 = jnp.min(jnp.where(el == v1, lanef, big), axis=-1, keepdims=True)
    el2 = jnp.where(lanef == i1, -jnp.inf, el)
    v2 = jnp.max(el2, axis=-1, keepdims=True)
    i2 = jnp.min(jnp.where(el2 == v2, lanef, big), axis=-1, keepdims=True)
    e21 = jnp.exp(v2 - v1)
    p1 = 1.0 / (1.0 + e21)
    route_ref[...] = jnp.where(lane == 0, i1 - N_GROUPS,
                     jnp.where(lane == 1, i2 - N_GROUPS,
                     jnp.where(lane == 2, g_gate * p1,
                     jnp.where(lane == 3, g_gate * (e21 * p1), 0.0))))


def _mix(x2, og, od, li_g, li_b, wo, g1, b1, rw, rb, alpha):
    n, d = x2.shape
    tm = min(TOKEN_TILE, n)
    row = lambda i: (i, 0)
    fix = lambda i: (0, 0)
    return pl.pallas_call(
        functools.partial(_mix_kernel, alpha=alpha),
        grid=(n // tm,),
        in_specs=[pl.BlockSpec((tm, d), row), pl.BlockSpec((tm, og.shape[1]), row),
                  pl.BlockSpec((tm, od.shape[1]), row),
                  pl.BlockSpec((1, d), fix), pl.BlockSpec((1, d), fix), pl.BlockSpec(wo.shape, fix),
                  pl.BlockSpec((1, d), fix), pl.BlockSpec((1, d), fix),
                  pl.BlockSpec(rw.shape, lambda i: (0, 0, 0)), pl.BlockSpec((1, LANES), fix)],
        out_specs=[pl.BlockSpec((tm, d), row), pl.BlockSpec((tm, d), row), pl.BlockSpec((tm, LANES), row)],
        out_shape=[jax.ShapeDtypeStruct((n, d), F32), jax.ShapeDtypeStruct((n, d), BF16),
                   jax.ShapeDtypeStruct((n, LANES), F32)],
        compiler_params=_params("parallel"),
        name="mix",
    )(x2, og, od, li_g, li_b, wo, g1, b1, rw, rb)


def _expert_kernel(be_ref, nu_ref, x_ref, w1_ref, w3_ref, w2_ref, y_ref):
    i = pl.program_id(0)

    @pl.when(i < nu_ref[0])
    def _():
        x = x_ref[...]
        h1 = _dot(x, w1_ref[0])
        h3 = _dot(x, w3_ref[0])
        y_ref[...] = _dot((_silu(h1) * h3).astype(BF16), w2_ref[0]).astype(BF16)

    @pl.when(i >= nu_ref[0])
    def _():
        y_ref[...] = jnp.zeros(y_ref.shape, y_ref.dtype)


def _expert_block(n_tokens):
    per_expert = max(1, n_tokens * TOP_K // N_EXPERTS)
    return int(min(EXPERT_BLOCK_MAX, max(EXPERT_BLOCK_MIN, 2 ** int(math.log2(per_expert)))))


def _experts(xb, block_expert, n_used, w1, w3, w2, blk):
    p_pad, d = xb.shape
    de = w1.shape[2]
    row = lambda i, be, nu: (i, 0)
    grid_spec = pltpu.PrefetchScalarGridSpec(
        num_scalar_prefetch=2,
        grid=(p_pad // blk,),
        in_specs=[pl.BlockSpec((blk, d), row),
                  pl.BlockSpec((1, d, de), lambda i, be, nu: (be[i], 0, 0)),
                  pl.BlockSpec((1, d, de), lambda i, be, nu: (be[i], 0, 0)),
                  pl.BlockSpec((1, de, d), lambda i, be, nu: (be[i], 0, 0))],
        out_specs=pl.BlockSpec((blk, d), row),
    )
    return pl.pallas_call(
        _expert_kernel,
        grid_spec=grid_spec,
        out_shape=jax.ShapeDtypeStruct((p_pad, d), BF16),
        compiler_params=_params("arbitrary"),
        name="experts",
    )(block_expert, n_used, xb, w1, w3, w2)


def _dispatch_plan(expert, blk):
    n = expert.shape[0]
    p = n * TOP_K
    n_blocks = -(-p // blk) + N_EXPERTS
    flat_e = expert.reshape(p)
    order = jnp.argsort(flat_e).astype(jnp.int32)
    sorted_e = flat_e[order]
    eids = jnp.arange(N_EXPERTS, dtype=jnp.int32)
    starts = jnp.searchsorted(sorted_e, eids, side="left").astype(jnp.int32)
    ends = jnp.searchsorted(sorted_e, eids, side="right").astype(jnp.int32)
    counts = ends - starts
    padded = (counts + blk - 1) // blk * blk
    padded_end = jnp.cumsum(padded)
    padded_start = padded_end - padded
    dest_sorted = padded_start[sorted_e] + jnp.arange(p, dtype=jnp.int32) - starts[sorted_e]
    tok_pad = jnp.zeros((n_blocks * blk,), jnp.int32).at[dest_sorted].set(order // TOP_K)
    dest = jnp.zeros((p,), jnp.int32).at[order].set(dest_sorted)
    block_expert = jnp.minimum(
        jnp.searchsorted(padded_end, jnp.arange(n_blocks, dtype=jnp.int32) * blk, side="right"),
        N_EXPERTS - 1).astype(jnp.int32)
    n_used = (padded_end[-1:] // blk).astype(jnp.int32)
    return tok_pad, dest, block_expert, n_used


def _final_kernel(x1_ref, yg_ref, route_ref, g_ref, b_ref, o_ref, *, alpha):
    d = x1_ref.shape[1]
    route = route_ref[...]
    moe = (route[:, 2:3] * yg_ref[:, 0:d].astype(F32) + route[:, 3:4] * yg_ref[:, d:2 * d].astype(F32))
    o_ref[...] = _layernorm(alpha * x1_ref[...] + moe, g_ref[...], b_ref[...])


def _final(x1, yg, route, g2, b2, alpha):
    n, d = x1.shape
    tm = min(TOKEN_TILE, n)
    row = lambda i: (i, 0)
    fix = lambda i: (0, 0)
    return pl.pallas_call(
        functools.partial(_final_kernel, alpha=alpha),
        grid=(n // tm,),
        in_specs=[pl.BlockSpec((tm, d), row), pl.BlockSpec((tm, TOP_K * d), row),
                  pl.BlockSpec((tm, LANES), row), pl.BlockSpec((1, d), fix), pl.BlockSpec((1, d), fix)],
        out_specs=pl.BlockSpec((tm, d), row),
        out_shape=jax.ShapeDtypeStruct((n, d), F32),
        compiler_params=_params("parallel"),
        name="final",
    )(x1, yg, route, g2, b2)


def _pad_lanes(v, width=LANES):
    v = v.reshape(1, -1).astype(F32)
    return jnp.pad(v, ((0, 0), (0, width - v.shape[1])))


def _layer(x, conv_prev, s_prev, k_cache, v_cache, p, alpha, lam_init):
    b, l, d = x.shape
    n = b * l
    x2 = x.reshape(n, d)
    qkv, z, q, k, v, kb, vb, ab = _proj(x2, p["ln_in_g"], p["ln_in_b"], p["w_main"], p["w_ab"])
    cc = qkv.shape[1]

    conv_prev8 = jnp.pad(conv_prev.astype(F32), ((0, 0), (CONV_PAD - (CONV_W - 1), 0), (0, 0)))
    og, convn, s_new = _gdn(qkv.reshape(b, l, cc), z.reshape(b, l, -1), ab.reshape(b, l, LANES),
                            conv_prev8, s_prev.astype(F32), p["conv_w"], p["alog_row"], p["dtb_row"],
                            p["nw_row"])
    conv_new = convn[:, CONV_PAD - (CONV_W - 1):, :]

    q3 = q.reshape(b, l, -1)
    kb3 = kb.reshape(b, l, -1)
    vb3 = vb.reshape(b, l, -1)
    if k_cache is None:
        od = _attn_prompt(q3, kb3, vb3, p["slopes"], p["lamv"], p["sw_row"], lam_init)
    else:
        past = k_cache.shape[1]
        od = _attn_sample(q3, kb3, vb3, k_cache.reshape(b, past, -1), v_cache.reshape(b, past, -1),
                          p["slopes"], p["lamv"], p["sw_row"], lam_init)

    x1, x1b, route = _mix(x2, og.reshape(n, -1), od.reshape(n, -1), p["ln_in_g"], p["ln_in_b"], p["wo"],
                          p["ln1_g"], p["ln1_b"], p["rw"], p["rb"], alpha)

    expert = route[:, 0:TOP_K].astype(jnp.int32)
    blk = _expert_block(n)
    tok_pad, dest, block_expert, n_used = _dispatch_plan(expert, blk)
    xb = jnp.take(x1b, tok_pad, axis=0, mode="clip")
    yb = _experts(xb, block_expert, n_used, p["w1"], p["w3"], p["w2"], blk)
    yg = jnp.take(yb, dest, axis=0, mode="clip").reshape(n, TOP_K * d)
    y = _final(x1, yg, route, p["ln2_g"], p["ln2_b"], alpha)

    return (y.reshape(b, l, d), conv_new, s_new,
            k.reshape(b, l, DIFF_HEADS, 2 * DIFF_DH), v.reshape(b, l, DIFF_HEADS, DIFF_DV))


def kernel(x_prompt, x_sample, cache_attn_k, cache_attn_v, state_gdn, state_conv, ln_in_g, ln_in_b, w_in, conv_w, gdn_a_log, gdn_dt_bias, gdn_norm_w, lam_q1, lam_k1, lam_q2, lam_k2, subln_w, w_o, ln1_g, ln1_b, router_g_w, router_g_b, router_e_w, router_e_b, w1, w3, w2, ln2_g, ln2_b):
    depth = w_in.shape[0]
    assert depth == 1, "single-layer step"
    d = x_prompt.shape[-1]
    alpha = (2 * depth) ** 0.25
    lam_init = 0.8 - 0.6 * math.exp(-0.3 * 0)
    row = lambda t: t.reshape(1, -1).astype(F32)

    wi = w_in[0]
    conv_ch = conv_w.shape[-1]
    gw = GDN_HEADS * GDN_D
    c_ab = conv_ch + gw
    c_q = c_ab + 2 * GDN_HEADS
    w_main = jnp.concatenate([wi[:, :c_ab], wi[:, c_q:]], axis=1).astype(BF16)
    w_ab = jnp.pad(wi[:, c_ab:c_q], ((0, 0), (0, LANES - 2 * GDN_HEADS))).astype(BF16)
    rcat = jnp.concatenate([router_g_w[0], router_e_w[0]], axis=1)
    rcat = jnp.pad(rcat, ((0, 0), (0, LANES - rcat.shape[1])))
    r_hi = rcat.astype(BF16)
    r_lo = (rcat - r_hi.astype(F32)).astype(BF16)
    p = {
        "ln_in_g": row(ln_in_g), "ln_in_b": row(ln_in_b), "w_main": w_main, "w_ab": w_ab,
        "conv_w": conv_w[0].astype(F32), "alog_row": _pad_lanes(gdn_a_log[0]),
        "dtb_row": _pad_lanes(gdn_dt_bias[0]), "nw_row": row(gdn_norm_w[0]),
        "slopes": jnp.asarray(2.0 ** (-8.0 * (np.arange(DIFF_HEADS) + 1) / DIFF_HEADS), F32),
        "lamv": jnp.stack([lam_q1[0], lam_k1[0], lam_q2[0], lam_k2[0]]).astype(F32),
        "sw_row": row(subln_w[0]), "wo": w_o[0].astype(BF16),
        "ln1_g": row(ln1_g[0]), "ln1_b": row(ln1_b[0]),
        "rw": jnp.stack([r_hi, r_lo]), "rb": _pad_lanes(jnp.concatenate([router_g_b[0], router_e_b[0]])),
        "w1": w1[0].astype(BF16), "w3": w3[0].astype(BF16), "w2": w2[0].astype(BF16),
        "ln2_g": row(ln2_g[0]), "ln2_b": row(ln2_b[0]),
    }

    bp = x_prompt.shape[0]
    conv0 = jnp.zeros((bp, CONV_W - 1, conv_ch), F32)
    s0 = jnp.zeros((bp, GDN_HEADS, GDN_D, GDN_D), F32)
    yp, cp, sp, kp, vp = _layer(x_prompt, conv0, s0, None, None, p, alpha, lam_init)
    ys, cs, ss, ks, vs = _layer(x_sample, state_conv[0], state_gdn[0], cache_attn_k[0], cache_attn_v[0],
                                p, alpha, lam_init)
    return (yp, ys, kp[None], vp[None], sp[None], cp[None], ks[None], vs[None], ss[None], cs[None])
```

```python
import functools
import math

import jax
import jax.numpy as jnp
import numpy as np
from jax import lax
from jax.experimental import pallas as pl
from jax.experimental.pallas import tpu as pltpu

F32 = jnp.float32
BF16 = jnp.bfloat16

CHUNK = 64
CONV_W = 4
GDN_HEADS = 4
GDN_D = 128
DIFF_HEADS = 4
DIFF_DH = 64
DIFF_DV = 2 * DIFF_DH
N_GROUPS = 4
EXPERTS_PER_GROUP = 8
N_EXPERTS = N_GROUPS * EXPERTS_PER_GROUP
TOP_K = 2
NEG_INF = -1e30
LANES = 128
CONV_PAD = 8

TOKEN_TILE = 512
GDN_TILE = 256
ATTN_TILE = 256
ATTN_ROWS = 64
CACHE_TILE = 1024
EXPERT_BLOCK_MAX = 512
EXPERT_BLOCK_MIN = 128
VMEM_LIMIT = 48 * 1024 * 1024


def _params(*sem):
    return pltpu.CompilerParams(dimension_semantics=sem, vmem_limit_bytes=VMEM_LIMIT)


def _dot(a, b):
    return jnp.dot(a, b, preferred_element_type=F32)


def _dot_nt(a, b):
    return lax.dot_general(a, b, (((1,), (1,)), ((), ())), preferred_element_type=F32)


def _sigmoid(x):
    return 1.0 / (1.0 + jnp.exp(-x))


def _silu(x):
    return x * _sigmoid(x)


def _softplus(x):
    return jnp.maximum(x, 0.0) + jnp.log(1.0 + jnp.exp(-jnp.abs(x)))


def _layernorm(x, g, b, eps=1e-5):
    mu = jnp.mean(x, axis=-1, keepdims=True)
    xc = x - mu
    var = jnp.mean(xc * xc, axis=-1, keepdims=True)
    return xc * lax.rsqrt(var + eps) * g + b


def _split3(x):
    hi = x.astype(BF16)
    r1 = x - hi.astype(F32)
    mid = r1.astype(BF16)
    lo = (r1 - mid.astype(F32)).astype(BF16)
    return hi, mid, lo


def _proj_kernel(x_ref, g_ref, b_ref, w_ref, wab_ref,
                 qkv_ref, z_ref, q_ref, k_ref, v_ref, kb_ref, vb_ref, ab_ref):
    xb = _layernorm(x_ref[...], g_ref[...], b_ref[...]).astype(BF16)
    w = 512
    for j in range(3):
        qkv_ref[:, j * w:(j + 1) * w] = _dot(xb, w_ref[:, j * w:(j + 1) * w]).astype(BF16)
    z_ref[...] = _dot(xb, w_ref[:, 3 * w:4 * w]).astype(BF16)
    q_ref[...] = _dot(xb, w_ref[:, 4 * w:5 * w]).astype(BF16)
    k = _dot(xb, w_ref[:, 5 * w:6 * w])
    k_ref[...] = k
    kb_ref[...] = k.astype(BF16)
    v = _dot(xb, w_ref[:, 6 * w:7 * w])
    v_ref[...] = v
    vb_ref[...] = v.astype(BF16)
    ab_ref[...] = _dot(xb, wab_ref[...])


def _proj(x2, ln_g, ln_b, w_main, w_ab):
    n, d = x2.shape
    tm = min(TOKEN_TILE, n)
    row = lambda i: (i, 0)
    fix = lambda i: (0, 0)
    outs = [(1536, BF16), (512, BF16), (512, BF16), (512, F32), (512, F32), (512, BF16), (512, BF16),
            (LANES, F32)]
    return pl.pallas_call(
        _proj_kernel,
        grid=(n // tm,),
        in_specs=[pl.BlockSpec((tm, d), row), pl.BlockSpec((1, d), fix), pl.BlockSpec((1, d), fix),
                  pl.BlockSpec(w_main.shape, fix), pl.BlockSpec(w_ab.shape, fix)],
        out_specs=[pl.BlockSpec((tm, c), row) for c, _ in outs],
        out_shape=[jax.ShapeDtypeStruct((n, c), t) for c, t in outs],
        compiler_params=_params("parallel"),
        name="proj",
    )(x2, ln_g, ln_b, w_main, w_ab)


def _gdn_kernel(qkv_ref, z_ref, ab_ref, convp_ref, s0_ref, cw_ref, alog_ref, dtb_ref, nw_ref,
                o_ref, convn_ref, sn_ref,
                xc_scr, y_scr, s_scr, cum_scr, a_scr, p_scr, attn_scr, rhs_scr, sol_scr, qg_scr, kdt_scr,
                *, tl, c):
    l = pl.program_id(1)
    d = GDN_D
    hw = GDN_HEADS * d

    @pl.when(l == 0)
    def _():
        xc_scr[0:CONV_PAD, :] = convp_ref[0]
        s_scr[...] = s0_ref[0]

    @pl.when(l > 0)
    def _():
        xc_scr[0:CONV_PAD, :] = xc_scr[tl:tl + CONV_PAD, :]

    xc_scr[CONV_PAD:CONV_PAD + tl, :] = qkv_ref[0].astype(F32)
    cw = cw_ref[...]
    y = cw[0:1] * xc_scr[CONV_PAD - 3:CONV_PAD - 3 + tl, :]
    for w in range(1, CONV_W):
        y = y + cw[w:w + 1] * xc_scr[CONV_PAD - 3 + w:CONV_PAD - 3 + w + tl, :]
    y_scr[...] = _silu(y)
    convn_ref[0] = xc_scr[tl:tl + CONV_PAD, :]

    ab = ab_ref[0]
    g_all = -jnp.exp(alog_ref[...]) * _softplus(ab + dtb_ref[...])
    beta_all = _sigmoid(ab)
    nw = nw_ref[...]

    rows = lax.broadcasted_iota(jnp.int32, (c, c), 0)
    cols = lax.broadcasted_iota(jnp.int32, (c, c), 1)
    incl = cols <= rows
    strict = cols < rows
    tri = jnp.where(incl, 1.0, 0.0).astype(BF16)
    grow = lax.broadcasted_iota(jnp.int32, (c, LANES), 0)
    glane = lax.broadcasted_iota(jnp.int32, (c, LANES), 1)

    n_chunks = tl // c
    groups = [(ci, h) for ci in range(n_chunks) for h in range(GDN_HEADS)]
    eye = jnp.where(rows == cols, 1.0, 0.0)

    gmat = jnp.concatenate(
        [jnp.where(glane < c, jnp.where(grow > glane, g_all[ci * c:(ci + 1) * c, h:h + 1], 0.0),
                   g_all[ci * c:(ci + 1) * c, h:h + 1]) for ci, h in groups], axis=1)
    g_hi, g_mid, g_lo = _split3(gmat)
    cum_scr[...] = _dot(tri, g_hi) + _dot(tri, g_mid) + _dot(tri, g_lo)

    for g, (ci, h) in enumerate(groups):
        r0 = ci * c
        beta = beta_all[r0:r0 + c, GDN_HEADS + h:GDN_HEADS + h + 1]
        rel = cum_scr[:, g * LANES:g * LANES + c]
        gam = cum_scr[:, g * LANES + c:g * LANES + c + 1]
        g_last = gam[c - 1:c, :]
        decay = jnp.where(incl, jnp.exp(jnp.where(incl, rel, 0.0)), 0.0)
        eg = jnp.exp(gam)
        ek = jnp.exp(g_last - gam)

        qh = y_scr[r0:r0 + c, h * d:(h + 1) * d]
        kh = y_scr[r0:r0 + c, hw + h * d:hw + (h + 1) * d]
        vh = y_scr[r0:r0 + c, 2 * hw + h * d:2 * hw + (h + 1) * d]
        qn = qh * lax.rsqrt(jnp.sum(qh * qh, axis=-1, keepdims=True) + 1e-6) * (d ** -0.5)
        kn = kh * lax.rsqrt(jnp.sum(kh * kh, axis=-1, keepdims=True) + 1e-6)
        qb = qn.astype(BF16)
        kb = kn.astype(BF16)
        a = jnp.where(strict, beta * _dot_nt(kb, kb) * decay, 0.0)
        a_scr[g] = a
        p_scr[g] = eye - a
        attn_scr[g] = (_dot_nt(qb, kb) * decay).astype(BF16)
        rhs_scr[g, :, 0:d] = (vh * beta).astype(BF16)
        rhs_scr[g, :, d:2 * d] = (kn * (beta * eg)).astype(BF16)
        qg_scr[g] = (qn * eg).astype(BF16)
        kdt_scr[g] = jnp.transpose(kn * ek).astype(BF16)

    n_sq = int(math.log2(c)) - 1
    for it in range(n_sq):
        for g in range(len(groups)):
            akb = a_scr[g].astype(BF16)
            ak = _dot(akb, akb)
            if it + 1 < n_sq:
                a_scr[g] = ak
            p = p_scr[g]
            p_scr[g] = p + _dot(p.astype(BF16), ak.astype(BF16))

    for g in range(len(groups)):
        sol_scr[g] = _dot(p_scr[g].astype(BF16), rhs_scr[g])

    for g, (ci, h) in enumerate(groups):
        r0 = ci * c
        s = s_scr[h]
        sb = s.astype(BF16)
        u = sol_scr[g, :, 0:d] - _dot(sol_scr[g, :, d:2 * d].astype(BF16), sb)
        ub = u.astype(BF16)
        o = _dot(qg_scr[g], sb) + _dot(attn_scr[g], ub)
        g_last = cum_scr[c - 1:c, g * LANES + c:g * LANES + c + 1]
        s_scr[h] = s * jnp.exp(g_last) + _dot(kdt_scr[g], ub)

        zh = z_ref[0, r0:r0 + c, h * d:(h + 1) * d].astype(F32)
        on = o * lax.rsqrt(jnp.mean(o * o, axis=-1, keepdims=True) + 1e-6) * nw
        o_ref[0, r0:r0 + c, h * d:(h + 1) * d] = (on * _silu(zh)).astype(BF16)

    @pl.when(l == pl.num_programs(1) - 1)
    def _():
        sn_ref[0] = s_scr[...]


def _gdn(qkv, z, ab, conv_prev8, s0, conv_w, alog_row, dtb_row, nw_row):
    b, l, cc = qkv.shape
    c = min(l, CHUNK)
    tl = min(l, GDN_TILE)
    ng = (tl // c) * GDN_HEADS
    hw = GDN_HEADS * GDN_D
    tile = lambda i, j: (i, j, 0)
    per_b = lambda i, j: (i, 0, 0)
    fix = lambda i, j: (0, 0)
    return pl.pallas_call(
        functools.partial(_gdn_kernel, tl=tl, c=c),
        grid=(b, l // tl),
        in_specs=[pl.BlockSpec((1, tl, cc), tile), pl.BlockSpec((1, tl, hw), tile),
                  pl.BlockSpec((1, tl, LANES), tile), pl.BlockSpec((1, CONV_PAD, cc), per_b),
                  pl.BlockSpec((1, GDN_HEADS, GDN_D, GDN_D), lambda i, j: (i, 0, 0, 0)),
                  pl.BlockSpec((CONV_W, cc), fix), pl.BlockSpec((1, LANES), fix),
                  pl.BlockSpec((1, LANES), fix), pl.BlockSpec((1, GDN_D), fix)],
        out_specs=[pl.BlockSpec((1, tl, hw), tile), pl.BlockSpec((1, CONV_PAD, cc), per_b),
                   pl.BlockSpec((1, GDN_HEADS, GDN_D, GDN_D), lambda i, j: (i, 0, 0, 0))],
        out_shape=[jax.ShapeDtypeStruct((b, l, hw), BF16),
                   jax.ShapeDtypeStruct((b, CONV_PAD, cc), F32),
                   jax.ShapeDtypeStruct((b, GDN_HEADS, GDN_D, GDN_D), F32)],
        scratch_shapes=[pltpu.VMEM((tl + CONV_PAD, cc), F32), pltpu.VMEM((tl, cc), F32),
                        pltpu.VMEM((GDN_HEADS, GDN_D, GDN_D), F32),
                        pltpu.VMEM((c, ng * LANES), F32), pltpu.VMEM((ng, c, c), F32),
                        pltpu.VMEM((ng, c, c), F32), pltpu.VMEM((ng, c, c), BF16),
                        pltpu.VMEM((ng, c, 2 * GDN_D), BF16), pltpu.VMEM((ng, c, 2 * GDN_D), F32),
                        pltpu.VMEM((ng, c, GDN_D), BF16), pltpu.VMEM((ng, GDN_D, c), BF16)],
        compiler_params=_params("parallel", "arbitrary"),
        name="gdn",
    )(qkv, z, ab, conv_prev8, s0, conv_w, alog_row, dtb_row, nw_row)


def _lambda(lamv, lam_init):
    l1 = jnp.sum(lamv[0:1] * lamv[1:2], axis=-1, keepdims=True)
    l2 = jnp.sum(lamv[2:3] * lamv[3:4], axis=-1, keepdims=True)
    return jnp.exp(l1) - jnp.exp(l2) + lam_init


def _attn_bias(q0, k0, nq, nk, slope):
    qpos = q0 + lax.broadcasted_iota(jnp.int32, (nq, nk), 0)
    kpos = k0 + lax.broadcasted_iota(jnp.int32, (nq, nk), 1)
    dist = jnp.abs(qpos - kpos).astype(F32)
    shift = int(math.log2(CHUNK))
    allowed = (kpos >> shift) <= (qpos >> shift)
    return jnp.where(allowed, -slope * dist, NEG_INF)


def _attn_init(stats):
    for m, l, a in stats:
        m[...] = jnp.full(m.shape, NEG_INF, F32)
        l[...] = jnp.zeros(l.shape, F32)
        a[...] = jnp.zeros(a.shape, F32)


def _attn_update(q, k, v, bias, stats):
    for i, (m, l, a) in enumerate(stats):
        lo, hi = i * DIFF_DH, (i + 1) * DIFF_DH
        s = _dot_nt(q[:, lo:hi], k[:, lo:hi]) + bias
        m_new = jnp.maximum(m[...], jnp.max(s, axis=-1, keepdims=True))
        alpha = jnp.exp(m[...] - m_new)
        p = jnp.exp(s - m_new)
        l[...] = alpha * l[...] + jnp.sum(p, axis=-1, keepdims=True)
        a[...] = alpha * a[...] + _dot(p.astype(BF16), v)
        m[...] = m_new


def _attn_finish(stats, lam, sw, lam_init):
    (_, l1, a1), (_, l2, a2) = stats
    o = a1[...] / l1[...] - lam * (a2[...] / l2[...])
    return o * lax.rsqrt(jnp.mean(o * o, axis=-1, keepdims=True) + 1e-6) * sw * (1.0 - lam_init)


def _attn_prompt_kernel(slopes_ref, lamv_ref, sw_ref, q_ref, k_ref, v_ref, o_ref,
                        vx_scr, b0_scr, bd_scr, q_scr, m_scr, al_scr, acc_scr, p_scr,
                        *, seq, tile, lam_init):
    t = tile
    r = 2 * t
    rb = ATTN_ROWS
    dv = DIFF_DV
    slope = slopes_ref[pl.program_id(1)]
    lam = _lambda(lamv_ref[...], lam_init)
    sw = sw_ref[...]
    shift = int(math.log2(CHUNK))

    vx_scr[:, 0:dv] = v_ref[0]
    vx_scr[:, dv:2 * dv] = jnp.ones((seq, dv), BF16)
    qi_ = lax.broadcasted_iota(jnp.int32, (t, t), 0)
    kj_ = lax.broadcasted_iota(jnp.int32, (t, t), 1)
    rel = (qi_ - kj_).astype(F32)
    b0_scr[...] = -slope * rel
    bd_scr[...] = jnp.where((kj_ >> shift) <= (qi_ >> shift), -slope * jnp.abs(rel), NEG_INF)
    lane = lax.broadcasted_iota(jnp.int32, (t, dv), 1)

    def step(k0, bias_ref, c):
        s_all = _dot_nt(q_scr[...], k_ref[0, pl.ds(k0, t), :])
        for i in range(r // rb):
            r0 = i * rb
            b_lo = r0 % t
            s = s_all[r0:r0 + rb, :] + bias_ref[b_lo:b_lo + rb, :]
            cols = [s[:, j * LANES:(j + 1) * LANES] for j in range(t // LANES)]
            m_prev = m_scr[r0:r0 + rb, :]
            m_cur = jnp.max(functools.reduce(jnp.maximum, cols), axis=-1, keepdims=True) + c
            m_new = jnp.maximum(m_prev, m_cur)
            al_scr[r0:r0 + rb, :] = jnp.exp(m_prev - m_new)
            mm = m_new - c
            for j, col in enumerate(cols):
                p_scr[r0:r0 + rb, j * LANES:(j + 1) * LANES] = jnp.exp(col - mm).astype(BF16)
            m_scr[r0:r0 + rb, :] = m_new
        pv = _dot(p_scr[...], vx_scr[pl.ds(k0, t), :])
        for i in range(r // rb):
            r0 = i * rb
            al = al_scr[r0:r0 + rb, :]
            acc_scr[r0:r0 + rb, :] = (jnp.concatenate([al, al], axis=1) * acc_scr[r0:r0 + rb, :]
                                      + pv[r0:r0 + rb, :])

    def q_body(qi, carry):
        q0 = pl.multiple_of(qi * t, t)
        qf = q_ref[0, pl.ds(q0, t), :].astype(F32) * (DIFF_DH ** -0.5)
        q_scr[0:t, :] = jnp.where(lane < DIFF_DH, qf, 0.0).astype(BF16)
        q_scr[t:r, :] = jnp.where(lane >= DIFF_DH, qf, 0.0).astype(BF16)
        m_scr[...] = jnp.full(m_scr.shape, NEG_INF, F32)
        acc_scr[...] = jnp.zeros(acc_scr.shape, F32)

        def k_body(kj, c2):
            k0 = pl.multiple_of(kj * t, t)
            step(k0, b0_scr, -slope * (q0 - k0).astype(F32))
            return c2

        lax.fori_loop(0, qi, k_body, 0)
        step(q0, bd_scr, 0.0)

        o1 = acc_scr[0:t, 0:dv] / acc_scr[0:t, dv:2 * dv]
        o2 = acc_scr[t:r, 0:dv] / acc_scr[t:r, dv:2 * dv]
        o = o1 - lam * o2
        on = o * lax.rsqrt(jnp.mean(o * o, axis=-1, keepdims=True) + 1e-6) * sw * (1.0 - lam_init)
        o_ref[0, pl.ds(q0, t), :] = on.astype(BF16)
        return carry

    lax.fori_loop(0, seq // t, q_body, 0)


def _attn_prompt(q, kb, vb, slopes, lamv, sw_row, lam_init):
    b, l, _ = q.shape
    t = min(ATTN_TILE, l)
    assert t % CHUNK == 0 and l % t == 0 and t % LANES == 0 and t % ATTN_ROWS == 0
    head = lambda i, h: (i, 0, h)
    fix = lambda i, h: (0, 0)
    dv = DIFF_DV
    scratch = [pltpu.VMEM((l, 2 * dv), BF16), pltpu.VMEM((t, t), F32), pltpu.VMEM((t, t), F32),
               pltpu.VMEM((2 * t, dv), BF16), pltpu.VMEM((2 * t, LANES), F32),
               pltpu.VMEM((2 * t, LANES), F32), pltpu.VMEM((2 * t, 2 * dv), F32),
               pltpu.VMEM((2 * t, t), BF16)]
    return pl.pallas_call(
        functools.partial(_attn_prompt_kernel, seq=l, tile=t, lam_init=lam_init),
        grid=(b, DIFF_HEADS),
        in_specs=[pl.BlockSpec(memory_space=pltpu.SMEM),
                  pl.BlockSpec(lamv.shape, fix), pl.BlockSpec(sw_row.shape, fix),
                  pl.BlockSpec((1, l, dv), head), pl.BlockSpec((1, l, dv), head),
                  pl.BlockSpec((1, l, dv), head)],
        out_specs=pl.BlockSpec((1, l, dv), head),
        out_shape=jax.ShapeDtypeStruct((b, l, DIFF_HEADS * dv), BF16),
        scratch_shapes=scratch,
        compiler_params=_params("parallel", "parallel"),
        name="attn_prompt",
    )(slopes, lamv, sw_row, q, kb, vb)


def _attn_sample_kernel(slopes_ref, lamv_ref, sw_ref, q_ref, kc_ref, vc_ref, kn_ref, vn_ref, o_ref,
                        m1, l1, a1, m2, l2, a2, *, past, tk, lam_init):
    j = pl.program_id(2)
    n_cache = past // tk
    slope = slopes_ref[pl.program_id(1)]
    stats = ((m1, l1, a1), (m2, l2, a2))
    lq = q_ref.shape[1]
    q = (q_ref[0].astype(F32) * (DIFF_DH ** -0.5)).astype(BF16)

    @pl.when(j == 0)
    def _():
        _attn_init(stats)

    @pl.when(j < n_cache)
    def _():
        k = kc_ref[0].astype(BF16)
        v = vc_ref[0].astype(BF16)
        _attn_update(q, k, v, _attn_bias(past, j * tk, lq, tk, slope), stats)

    @pl.when(j == n_cache)
    def _():
        _attn_update(q, kn_ref[0], vn_ref[0], _attn_bias(past, past, lq, lq, slope), stats)
        lam = _lambda(lamv_ref[...], lam_init)
        o_ref[0] = _attn_finish(stats, lam, sw_ref[...], lam_init).astype(BF16)


def _attn_sample(q, kb, vb, k_cache, v_cache, slopes, lamv, sw_row, lam_init):
    b, l, _ = q.shape
    past = k_cache.shape[1]
    tk = min(CACHE_TILE, past)
    assert past % tk == 0 and past % CHUNK == 0 and l <= CHUNK
    n_cache = past // tk
    head = lambda i, h, j: (i, 0, h)
    cache = lambda i, h, j: (i, jnp.minimum(j, n_cache - 1), h)
    fix = lambda i, h, j: (0, 0)
    stat = [pltpu.VMEM((l, 1), F32), pltpu.VMEM((l, 1), F32), pltpu.VMEM((l, DIFF_DV), F32)]
    return pl.pallas_call(
        functools.partial(_attn_sample_kernel, past=past, tk=tk, lam_init=lam_init),
        grid=(b, DIFF_HEADS, n_cache + 1),
        in_specs=[pl.BlockSpec(memory_space=pltpu.SMEM),
                  pl.BlockSpec(lamv.shape, fix), pl.BlockSpec(sw_row.shape, fix),
                  pl.BlockSpec((1, l, DIFF_DV), head),
                  pl.BlockSpec((1, tk, DIFF_DV), cache), pl.BlockSpec((1, tk, DIFF_DV), cache),
                  pl.BlockSpec((1, l, DIFF_DV), head), pl.BlockSpec((1, l, DIFF_DV), head)],
        out_specs=pl.BlockSpec((1, l, DIFF_DV), head),
        out_shape=jax.ShapeDtypeStruct((b, l, DIFF_HEADS * DIFF_DV), BF16),
        scratch_shapes=stat + stat,
        compiler_params=_params("parallel", "parallel", "arbitrary"),
        name="attn_sample",
    )(slopes, lamv, sw_row, q, k_cache, v_cache, kb, vb)


def _mix_kernel(x_ref, og_ref, od_ref, lig_ref, lib_ref, wo_ref, g1_ref, b1_ref, rw_ref, rb_ref,
                x1_ref, x1b_ref, route_ref, *, alpha):
    xn = _layernorm(x_ref[...], lig_ref[...], lib_ref[...])
    hw = og_ref.shape[1]
    mixed = _dot(og_ref[...], wo_ref[0:hw, :]) + _dot(od_ref[...], wo_ref[hw:, :])
    x1 = _layernorm(alpha * xn + mixed, g1_ref[...], b1_ref[...])
    x1_ref[...] = x1
    x1b_ref[...] = x1.astype(BF16)

    x_hi = x1.astype(BF16)
    x_lo = (x1 - x_hi.astype(F32)).astype(BF16)
    lg = (_dot(x_hi, rw_ref[0]) + _dot(x_lo, rw_ref[0]) + _dot(x_hi, rw_ref[1])) + rb_ref[...]
    lane = lax.broadcasted_iota(jnp.int32, lg.shape, 1)
    lanef = lane.astype(F32)
    big = float(LANES)

    gl = jnp.where(lane < N_GROUPS, lg, -jnp.inf)
    gmax = jnp.max(gl, axis=-1, keepdims=True)
    grp = jnp.min(jnp.where(gl == gmax, lanef, big), axis=-1, keepdims=True)
    g_gate = 1.0 / jnp.sum(jnp.where(lane < N_GROUPS, jnp.exp(gl - gmax), 0.0), axis=-1, keepdims=True)

    e_lo = N_GROUPS + grp * EXPERTS_PER_GROUP
    el = jnp.where((lanef >= e_lo) & (lanef < e_lo + EXPERTS_PER_GROUP), lg, -jnp.inf)
    v1 = jnp.max(el, axis=-1, keepdims=True)
    i1 = jnp.min(jnp.where(el == v1, lanef, big), axis=-1, keepdims=True)
    el2 = jnp.where(lanef == i1, -jnp.inf, el)
    v2 = jnp.max(el2, axis=-1, keepdims=True)
    i2 = jnp.min(jnp.where(el2 == v2, lanef, big), axis=-1, keepdims=True)
    e21 = jnp.exp(v2 - v1)
    p1 = 1.0 / (1.0 + e21)
    route_ref[...] = jnp.where(lane == 0, i1 - N_GROUPS,
                     jnp.where(lane == 1, i2 - N_GROUPS,
                     jnp.where(lane == 2, g_gate * p1,
                     jnp.where(lane == 3, g_gate * (e21 * p1), 0.0))))


def _mix(x2, og, od, li_g, li_b, wo, g1, b1, rw, rb, alpha):
    n, d = x2.shape
    tm = min(TOKEN_TILE, n)
    row = lambda i: (i, 0)
    fix = lambda i: (0, 0)
    return pl.pallas_call(
        functools.partial(_mix_kernel, alpha=alpha),
        grid=(n // tm,),
        in_specs=[pl.BlockSpec((tm, d), row), pl.BlockSpec((tm, og.shape[1]), row),
                  pl.BlockSpec((tm, od.shape[1]), row),
                  pl.BlockSpec((1, d), fix), pl.BlockSpec((1, d), fix), pl.BlockSpec(wo.shape, fix),
                  pl.BlockSpec((1, d), fix), pl.BlockSpec((1, d), fix),
                  pl.BlockSpec(rw.shape, lambda i: (0, 0, 0)), pl.BlockSpec((1, LANES), fix)],
        out_specs=[pl.BlockSpec((tm, d), row), pl.BlockSpec((tm, d), row), pl.BlockSpec((tm, LANES), row)],
        out_shape=[jax.ShapeDtypeStruct((n, d), F32), jax.ShapeDtypeStruct((n, d), BF16),
                   jax.ShapeDtypeStruct((n, LANES), F32)],
        compiler_params=_params("parallel"),
        name="mix",
    )(x2, og, od, li_g, li_b, wo, g1, b1, rw, rb)


def _expert_kernel(be_ref, nu_ref, x_ref, w1_ref, w3_ref, w2_ref, y_ref, w1b, w3b, w2b):
    i = pl.program_id(0)

    @pl.when((i == 0) | (be_ref[i] != be_ref[jnp.maximum(i - 1, 0)]))
    def _():
        w1b[...] = w1_ref[0].astype(BF16)
        w3b[...] = w3_ref[0].astype(BF16)
        w2b[...] = w2_ref[0].astype(BF16)

    @pl.when(i < nu_ref[0])
    def _():
        x = x_ref[...]
        h1 = _dot(x, w1b[...])
        h3 = _dot(x, w3b[...])
        y_ref[...] = _dot((_silu(h1) * h3).astype(BF16), w2b[...]).astype(BF16)

    @pl.when(i >= nu_ref[0])
    def _():
        y_ref[...] = jnp.zeros(y_ref.shape, y_ref.dtype)


def _expert_block(n_tokens):
    per_expert = max(1, n_tokens * TOP_K // N_EXPERTS)
    return int(min(EXPERT_BLOCK_MAX, max(EXPERT_BLOCK_MIN, 2 ** int(math.log2(per_expert)))))


def _experts(xb, block_expert, n_used, w1, w3, w2, blk):
    p_pad, d = xb.shape
    de = w1.shape[2]
    row = lambda i, be, nu: (i, 0)
    grid_spec = pltpu.PrefetchScalarGridSpec(
        num_scalar_prefetch=2,
        grid=(p_pad // blk,),
        in_specs=[pl.BlockSpec((blk, d), row),
                  pl.BlockSpec((1, d, de), lambda i, be, nu: (be[i], 0, 0)),
                  pl.BlockSpec((1, d, de), lambda i, be, nu: (be[i], 0, 0)),
                  pl.BlockSpec((1, de, d), lambda i, be, nu: (be[i], 0, 0))],
        out_specs=pl.BlockSpec((blk, d), row),
        scratch_shapes=[pltpu.VMEM((d, de), BF16), pltpu.VMEM((d, de), BF16), pltpu.VMEM((de, d), BF16)],
    )
    return pl.pallas_call(
        _expert_kernel,
        grid_spec=grid_spec,
        out_shape=jax.ShapeDtypeStruct((p_pad, d), BF16),
        compiler_params=_params("arbitrary"),
        name="experts",
    )(block_expert, n_used, xb, w1, w3, w2)


def _dispatch_plan(expert, blk):
    n = expert.shape[0]
    p = n * TOP_K
    n_blocks = -(-p // blk) + N_EXPERTS
    flat_e = expert.reshape(p)
    eids = jnp.arange(N_EXPERTS, dtype=jnp.int32)
    counts = jnp.sum((flat_e[:, None] == eids[None, :]).astype(jnp.int32), axis=0)
    starts = jnp.cumsum(counts) - counts
    padded = (counts + blk - 1) // blk * blk
    padded_end = jnp.cumsum(padded)
    padded_start = padded_end - padded
    order = jnp.argsort(flat_e).astype(jnp.int32)
    rank = jnp.argsort(order).astype(jnp.int32)
    offset = padded_start - starts
    dest = rank + offset[flat_e]
    block_start = jnp.arange(n_blocks, dtype=jnp.int32) * blk
    block_expert = jnp.minimum(
        jnp.sum((padded_end[None, :] <= block_start[:, None]).astype(jnp.int32), axis=1), N_EXPERTS - 1)
    e_row = jnp.repeat(block_expert, blk)
    j = jnp.arange(n_blocks * blk, dtype=jnp.int32) - padded_start[e_row]
    src = jnp.clip(starts[e_row] + j, 0, p - 1)
    tok_pad = jnp.where(j < counts[e_row], order[src] // TOP_K, 0)
    n_used = (padded_end[-1:] // blk).astype(jnp.int32)
    return tok_pad, dest.reshape(n, TOP_K), block_expert, n_used


def _final_kernel(x1_ref, y0_ref, y1_ref, route_ref, g_ref, b_ref, o_ref, *, alpha):
    route = route_ref[...]
    moe = route[:, 2:3] * y0_ref[...].astype(F32) + route[:, 3:4] * y1_ref[...].astype(F32)
    o_ref[...] = _layernorm(alpha * x1_ref[...] + moe, g_ref[...], b_ref[...])


def _final(x1, y0, y1, route, g2, b2, alpha):
    n, d = x1.shape
    tm = min(TOKEN_TILE, n)
    row = lambda i: (i, 0)
    fix = lambda i: (0, 0)
    return pl.pallas_call(
        functools.partial(_final_kernel, alpha=alpha),
        grid=(n // tm,),
        in_specs=[pl.BlockSpec((tm, d), row), pl.BlockSpec((tm, d), row), pl.BlockSpec((tm, d), row),
                  pl.BlockSpec((tm, LANES), row), pl.BlockSpec((1, d), fix), pl.BlockSpec((1, d), fix)],
        out_specs=pl.BlockSpec((tm, d), row),
        out_shape=jax.ShapeDtypeStruct((n, d), F32),
        compiler_params=_params("parallel"),
        name="final",
    )(x1, y0, y1, route, g2, b2)


def _pad_lanes(v, width=LANES):
    v = v.reshape(1, -1).astype(F32)
    return jnp.pad(v, ((0, 0), (0, width - v.shape[1])))


def _layer(x, conv_prev, s_prev, k_cache, v_cache, p, alpha, lam_init):
    b, l, d = x.shape
    n = b * l
    x2 = x.reshape(n, d)
    qkv, z, q, k, v, kb, vb, ab = _proj(x2, p["ln_in_g"], p["ln_in_b"], p["w_main"], p["w_ab"])
    cc = qkv.shape[1]

    conv_prev8 = jnp.pad(conv_prev.astype(F32), ((0, 0), (CONV_PAD - (CONV_W - 1), 0), (0, 0)))
    og, convn, s_new = _gdn(qkv.reshape(b, l, cc), z.reshape(b, l, -1), ab.reshape(b, l, LANES),
                            conv_prev8, s_prev.astype(F32), p["conv_w"], p["alog_row"], p["dtb_row"],
                            p["nw_row"])
    conv_new = convn[:, CONV_PAD - (CONV_W - 1):, :]

    q3 = q.reshape(b, l, -1)
    kb3 = kb.reshape(b, l, -1)
    vb3 = vb.reshape(b, l, -1)
    if k_cache is None:
        od = _attn_prompt(q3, kb3, vb3, p["slopes"], p["lamv"], p["sw_row"], lam_init)
    else:
        past = k_cache.shape[1]
        od = _attn_sample(q3, kb3, vb3, k_cache.reshape(b, past, -1), v_cache.reshape(b, past, -1),
                          p["slopes"], p["lamv"], p["sw_row"], lam_init)

    x1, x1b, route = _mix(x2, og.reshape(n, -1), od.reshape(n, -1), p["ln_in_g"], p["ln_in_b"], p["wo"],
                          p["ln1_g"], p["ln1_b"], p["rw"], p["rb"], alpha)

    expert = route[:, 0:TOP_K].astype(jnp.int32)
    blk = _expert_block(n)
    tok_pad, dest, block_expert, n_used = _dispatch_plan(expert, blk)
    xb = jnp.take(x1b, tok_pad, axis=0, mode="clip")
    yb = _experts(xb, block_expert, n_used, p["w1"], p["w3"], p["w2"], blk)
    y0 = jnp.take(yb, dest[:, 0], axis=0, mode="clip")
    y1 = jnp.take(yb, dest[:, 1], axis=0, mode="clip")
    y = _final(x1, y0, y1, route, p["ln2_g"], p["ln2_b"], alpha)

    return (y.reshape(b, l, d), conv_new, s_new,
            k.reshape(b, l, DIFF_HEADS, 2 * DIFF_DH), v.reshape(b, l, DIFF_HEADS, DIFF_DV))


def kernel(x_prompt, x_sample, cache_attn_k, cache_attn_v, state_gdn, state_conv, ln_in_g, ln_in_b, w_in, conv_w, gdn_a_log, gdn_dt_bias, gdn_norm_w, lam_q1, lam_k1, lam_q2, lam_k2, subln_w, w_o, ln1_g, ln1_b, router_g_w, router_g_b, router_e_w, router_e_b, w1, w3, w2, ln2_g, ln2_b):
    depth = w_in.shape[0]
    assert depth == 1, "single-layer step"
    d = x_prompt.shape[-1]
    alpha = (2 * depth) ** 0.25
    lam_init = 0.8 - 0.6 * math.exp(-0.3 * 0)
    row = lambda t: t.reshape(1, -1).astype(F32)

    wi = w_in[0]
    conv_ch = conv_w.shape[-1]
    gw = GDN_HEADS * GDN_D
    c_ab = conv_ch + gw
    c_q = c_ab + 2 * GDN_HEADS
    w_main = jnp.concatenate([wi[:, :c_ab], wi[:, c_q:]], axis=1).astype(BF16)
    w_ab = jnp.pad(wi[:, c_ab:c_q], ((0, 0), (0, LANES - 2 * GDN_HEADS))).astype(BF16)
    rcat = jnp.concatenate([router_g_w[0], router_e_w[0]], axis=1)
    rcat = jnp.pad(rcat, ((0, 0), (0, LANES - rcat.shape[1])))
    r_hi = rcat.astype(BF16)
    r_lo = (rcat - r_hi.astype(F32)).astype(BF16)
    p = {
        "ln_in_g": row(ln_in_g), "ln_in_b": row(ln_in_b), "w_main": w_main, "w_ab": w_ab,
        "conv_w": conv_w[0].astype(F32), "alog_row": _pad_lanes(gdn_a_log[0]),
        "dtb_row": _pad_lanes(gdn_dt_bias[0]), "nw_row": row(gdn_norm_w[0]),
        "slopes": jnp.asarray(2.0 ** (-8.0 * (np.arange(DIFF_HEADS) + 1) / DIFF_HEADS), F32),
        "lamv": jnp.stack([lam_q1[0], lam_k1[0], lam_q2[0], lam_k2[0]]).astype(F32),
        "sw_row": row(subln_w[0]), "wo": w_o[0].astype(BF16),
        "ln1_g": row(ln1_g[0]), "ln1_b": row(ln1_b[0]),
        "rw": jnp.stack([r_hi, r_lo]), "rb": _pad_lanes(jnp.concatenate([router_g_b[0], router_e_b[0]])),
        "w1": w1[0], "w3": w3[0], "w2": w2[0],
        "ln2_g": row(ln2_g[0]), "ln2_b": row(ln2_b[0]),
    }

    bp = x_prompt.shape[0]
    conv0 = jnp.zeros((bp, CONV_W - 1, conv_ch), F32)
    s0 = jnp.zeros((bp, GDN_HEADS, GDN_D, GDN_D), F32)
    yp, cp, sp, kp, vp = _layer(x_prompt, conv0, s0, None, None, p, alpha, lam_init)
    ys, cs, ss, ks, vs = _layer(x_sample, state_conv[0], state_gdn[0], cache_attn_k[0], cache_attn_v[0],
                                p, alpha, lam_init)
    return (yp, ys, kp[None], vp[None], sp[None], cp[None], ks[None], vs[None], ss[None], cs[None])
```

```python
import functools
import math

import jax
import jax.numpy as jnp
import numpy as np
from jax import lax
from jax.experimental import pallas as pl
from jax.experimental.pallas import tpu as pltpu

F32 = jnp.float32
BF16 = jnp.bfloat16

CHUNK = 64
CONV_W = 4
GDN_HEADS = 4
GDN_D = 128
DIFF_HEADS = 4
DIFF_DH = 64
DIFF_DV = 2 * DIFF_DH
N_GROUPS = 4
EXPERTS_PER_GROUP = 8
N_EXPERTS = N_GROUPS * EXPERTS_PER_GROUP
TOP_K = 2
NEG_INF = -1e30
LANES = 128
CONV_PAD = 8

TOKEN_TILE = 512
GDN_TILE = 256
ATTN_TILE = 256
ATTN_ROWS = 64
CACHE_TILE = 1024
EXPERT_BLOCK_MAX = 512
EXPERT_BLOCK_MIN = 128
MOE_CHUNKS = 2
VMEM_LIMIT = 48 * 1024 * 1024


def _params(*sem):
    return pltpu.CompilerParams(dimension_semantics=sem, vmem_limit_bytes=VMEM_LIMIT)


def _dot(a, b):
    return jnp.dot(a, b, preferred_element_type=F32)


def _dot_nt(a, b):
    return lax.dot_general(a, b, (((1,), (1,)), ((), ())), preferred_element_type=F32)


def _sigmoid(x):
    return 1.0 / (1.0 + jnp.exp(-x))


def _silu(x):
    return x * _sigmoid(x)


def _softplus(x):
    return jnp.maximum(x, 0.0) + jnp.log(1.0 + jnp.exp(-jnp.abs(x)))


def _layernorm(x, g, b, eps=1e-5):
    mu = jnp.mean(x, axis=-1, keepdims=True)
    xc = x - mu
    var = jnp.mean(xc * xc, axis=-1, keepdims=True)
    return xc * lax.rsqrt(var + eps) * g + b


def _split3(x):
    hi = x.astype(BF16)
    r1 = x - hi.astype(F32)
    mid = r1.astype(BF16)
    lo = (r1 - mid.astype(F32)).astype(BF16)
    return hi, mid, lo


def _proj_kernel(x_ref, g_ref, b_ref, w_ref, wab_ref, convp_ref, cw_ref,
                 y_ref, z_ref, q_ref, k_ref, v_ref, kb_ref, vb_ref, ab_ref, convn_ref,
                 xc_scr, *, rows, n_seq, tiles_per_seq):
    xb = _layernorm(x_ref[...], g_ref[...], b_ref[...]).astype(BF16)
    w = 512
    stride = rows + CONV_PAD
    if tiles_per_seq > 1:
        first = pl.program_id(0) % tiles_per_seq == 0

        @pl.when(first)
        def _():
            xc_scr[0:CONV_PAD, :] = convp_ref[0]

        @pl.when(jnp.logical_not(first))
        def _():
            xc_scr[0:CONV_PAD, :] = xc_scr[rows:rows + CONV_PAD, :]
    else:
        for s in range(n_seq):
            xc_scr[s * stride:s * stride + CONV_PAD, :] = convp_ref[s]
    cw = cw_ref[...]
    for j in range(3):
        cs = slice(j * w, (j + 1) * w)
        qkv = _dot(xb, w_ref[:, cs])
        for s in range(n_seq):
            base = s * stride
            xc_scr[base + CONV_PAD:base + CONV_PAD + rows, cs] = qkv[s * rows:(s + 1) * rows, :]
            lo = base + CONV_PAD - (CONV_W - 1)
            y = cw[0:1, cs] * xc_scr[lo:lo + rows, cs]
            for t in range(1, CONV_W):
                y = y + cw[t:t + 1, cs] * xc_scr[lo + t:lo + t + rows, cs]
            y_ref[s * rows:(s + 1) * rows, cs] = _silu(y).astype(BF16)
    for s in range(n_seq):
        convn_ref[s] = xc_scr[s * stride + rows:s * stride + rows + CONV_PAD, :]
    z_ref[...] = _dot(xb, w_ref[:, 3 * w:4 * w]).astype(BF16)
    q_ref[...] = _dot(xb, w_ref[:, 4 * w:5 * w]).astype(BF16)
    k = _dot(xb, w_ref[:, 5 * w:6 * w])
    k_ref[...] = k
    kb_ref[...] = k.astype(BF16)
    v = _dot(xb, w_ref[:, 6 * w:7 * w])
    v_ref[...] = v
    vb_ref[...] = v.astype(BF16)
    ab_ref[...] = _dot(xb, wab_ref[...])


def _proj(x2, seq_len, ln_g, ln_b, w_main, w_ab, conv_prev8, conv_w):
    n, d = x2.shape
    tm = min(TOKEN_TILE, n)
    cc = conv_w.shape[1]
    rows = min(seq_len, tm)
    n_seq = tm // rows
    tiles_per_seq = seq_len // rows
    assert tm % rows == 0 and seq_len % rows == 0 and rows % CONV_PAD == 0
    row = lambda i: (i, 0)
    fix = lambda i: (0, 0)
    seq = lambda i: (i // tiles_per_seq, 0, 0)
    outs = [(cc, BF16), (512, BF16), (512, BF16), (512, F32), (512, F32), (512, BF16), (512, BF16),
            (LANES, F32)]
    return pl.pallas_call(
        functools.partial(_proj_kernel, rows=rows, n_seq=n_seq, tiles_per_seq=tiles_per_seq),
        grid=(n // tm,),
        in_specs=[pl.BlockSpec((tm, d), row), pl.BlockSpec((1, d), fix), pl.BlockSpec((1, d), fix),
                  pl.BlockSpec(w_main.shape, fix), pl.BlockSpec(w_ab.shape, fix),
                  pl.BlockSpec((n_seq, CONV_PAD, cc), seq), pl.BlockSpec((CONV_W, cc), fix)],
        out_specs=[pl.BlockSpec((tm, c), row) for c, _ in outs]
                  + [pl.BlockSpec((n_seq, CONV_PAD, cc), seq)],
        out_shape=[jax.ShapeDtypeStruct((n, c), t) for c, t in outs]
                  + [jax.ShapeDtypeStruct(conv_prev8.shape, F32)],
        scratch_shapes=[pltpu.VMEM((n_seq * (rows + CONV_PAD), cc), F32)],
        compiler_params=_params("arbitrary"),
        name="proj",
    )(x2, ln_g, ln_b, w_main, w_ab, conv_prev8, conv_w)


def _gdn_kernel(y_ref, z_ref, ab_ref, s0_ref, alog_ref, dtb_ref, nw_ref,
                o_ref, sn_ref,
                s_scr, cum_scr, a_scr, p_scr, attn_scr, rhs_scr, sol_scr, qg_scr, kdt_scr,
                *, tl, c):
    l = pl.program_id(1)
    d = GDN_D
    hw = GDN_HEADS * d

    @pl.when(l == 0)
    def _():
        s_scr[...] = s0_ref[0]

    ab = ab_ref[0]
    g_all = -jnp.exp(alog_ref[...]) * _softplus(ab + dtb_ref[...])
    beta_all = _sigmoid(ab)
    nw = nw_ref[...]

    rows = lax.broadcasted_iota(jnp.int32, (c, c), 0)
    cols = lax.broadcasted_iota(jnp.int32, (c, c), 1)
    incl = cols <= rows
    strict = cols < rows
    tri = jnp.where(incl, 1.0, 0.0).astype(BF16)
    grow = lax.broadcasted_iota(jnp.int32, (c, LANES), 0)
    glane = lax.broadcasted_iota(jnp.int32, (c, LANES), 1)

    n_chunks = tl // c
    groups = [(ci, h) for ci in range(n_chunks) for h in range(GDN_HEADS)]
    eye = jnp.where(rows == cols, 1.0, 0.0)

    gmat = jnp.concatenate(
        [jnp.where(glane < c, jnp.where(grow > glane, g_all[ci * c:(ci + 1) * c, h:h + 1], 0.0),
                   g_all[ci * c:(ci + 1) * c, h:h + 1]) for ci, h in groups], axis=1)
    g_hi, g_mid, g_lo = _split3(gmat)
    cum_scr[...] = _dot(tri, g_hi) + _dot(tri, g_mid) + _dot(tri, g_lo)

    for g, (ci, h) in enumerate(groups):
        r0 = ci * c
        beta = beta_all[r0:r0 + c, GDN_HEADS + h:GDN_HEADS + h + 1]
        rel = cum_scr[:, g * LANES:g * LANES + c]
        gam = cum_scr[:, g * LANES + c:g * LANES + c + 1]
        g_last = gam[c - 1:c, :]
        decay = jnp.where(incl, jnp.exp(jnp.where(incl, rel, 0.0)), 0.0)
        eg = jnp.exp(gam)
        ek = jnp.exp(g_last - gam)

        qh = y_ref[0, r0:r0 + c, h * d:(h + 1) * d].astype(F32)
        kh = y_ref[0, r0:r0 + c, hw + h * d:hw + (h + 1) * d].astype(F32)
        vh = y_ref[0, r0:r0 + c, 2 * hw + h * d:2 * hw + (h + 1) * d].astype(F32)
        qn = qh * lax.rsqrt(jnp.sum(qh * qh, axis=-1, keepdims=True) + 1e-6) * (d ** -0.5)
        kn = kh * lax.rsqrt(jnp.sum(kh * kh, axis=-1, keepdims=True) + 1e-6)
        qb = qn.astype(BF16)
        kb = kn.astype(BF16)
        a = jnp.where(strict, beta * _dot_nt(kb, kb) * decay, 0.0)
        a_scr[g] = a
        p_scr[g] = eye - a
        attn_scr[g] = (_dot_nt(qb, kb) * decay).astype(BF16)
        rhs_scr[g, :, 0:d] = (vh * beta).astype(BF16)
        rhs_scr[g, :, d:2 * d] = (kn * (beta * eg)).astype(BF16)
        qg_scr[g] = (qn * eg).astype(BF16)
        kdt_scr[g] = jnp.transpose(kn * ek).astype(BF16)

    n_sq = int(math.log2(c)) - 1
    for it in range(n_sq):
        for g in range(len(groups)):
            akb = a_scr[g].astype(BF16)
            ak = _dot(akb, akb)
            if it + 1 < n_sq:
                a_scr[g] = ak
            p = p_scr[g]
            p_scr[g] = p + _dot(p.astype(BF16), ak.astype(BF16))

    for g in range(len(groups)):
        sol_scr[g] = _dot(p_scr[g].astype(BF16), rhs_scr[g])

    for g, (ci, h) in enumerate(groups):
        r0 = ci * c
        s = s_scr[h]
        sb = s.astype(BF16)
        u = sol_scr[g, :, 0:d] - _dot(sol_scr[g, :, d:2 * d].astype(BF16), sb)
        ub = u.astype(BF16)
        o = _dot(qg_scr[g], sb) + _dot(attn_scr[g], ub)
        g_last = cum_scr[c - 1:c, g * LANES + c:g * LANES + c + 1]
        s_scr[h] = s * jnp.exp(g_last) + _dot(kdt_scr[g], ub)

        zh = z_ref[0, r0:r0 + c, h * d:(h + 1) * d].astype(F32)
        on = o * lax.rsqrt(jnp.mean(o * o, axis=-1, keepdims=True) + 1e-6) * nw
        o_ref[0, r0:r0 + c, h * d:(h + 1) * d] = (on * _silu(zh)).astype(BF16)

    @pl.when(l == pl.num_programs(1) - 1)
    def _():
        sn_ref[0] = s_scr[...]


def _gdn(y, z, ab, s0, alog_row, dtb_row, nw_row):
    b, l, cc = y.shape
    c = min(l, CHUNK)
    tl = min(l, GDN_TILE)
    ng = (tl // c) * GDN_HEADS
    hw = GDN_HEADS * GDN_D
    tile = lambda i, j: (i, j, 0)
    fix = lambda i, j: (0, 0)
    return pl.pallas_call(
        functools.partial(_gdn_kernel, tl=tl, c=c),
        grid=(b, l // tl),
        in_specs=[pl.BlockSpec((1, tl, cc), tile), pl.BlockSpec((1, tl, hw), tile),
                  pl.BlockSpec((1, tl, LANES), tile),
                  pl.BlockSpec((1, GDN_HEADS, GDN_D, GDN_D), lambda i, j: (i, 0, 0, 0)),
                  pl.BlockSpec((1, LANES), fix),
                  pl.BlockSpec((1, LANES), fix), pl.BlockSpec((1, GDN_D), fix)],
        out_specs=[pl.BlockSpec((1, tl, hw), tile),
                   pl.BlockSpec((1, GDN_HEADS, GDN_D, GDN_D), lambda i, j: (i, 0, 0, 0))],
        out_shape=[jax.ShapeDtypeStruct((b, l, hw), BF16),
                   jax.ShapeDtypeStruct((b, GDN_HEADS, GDN_D, GDN_D), F32)],
        scratch_shapes=[pltpu.VMEM((GDN_HEADS, GDN_D, GDN_D), F32),
                        pltpu.VMEM((c, ng * LANES), F32), pltpu.VMEM((ng, c, c), F32),
                        pltpu.VMEM((ng, c, c), F32), pltpu.VMEM((ng, c, c), BF16),
                        pltpu.VMEM((ng, c, 2 * GDN_D), BF16), pltpu.VMEM((ng, c, 2 * GDN_D), F32),
                        pltpu.VMEM((ng, c, GDN_D), BF16), pltpu.VMEM((ng, GDN_D, c), BF16)],
        compiler_params=_params("parallel", "arbitrary"),
        name="gdn",
    )(y, z, ab, s0, alog_row, dtb_row, nw_row)


def _lambda(lamv, lam_init):
    l1 = jnp.sum(lamv[0:1] * lamv[1:2], axis=-1, keepdims=True)
    l2 = jnp.sum(lamv[2:3] * lamv[3:4], axis=-1, keepdims=True)
    return jnp.exp(l1) - jnp.exp(l2) + lam_init


def _attn_bias(q0, k0, nq, nk, slope):
    qpos = q0 + lax.broadcasted_iota(jnp.int32, (nq, nk), 0)
    kpos = k0 + lax.broadcasted_iota(jnp.int32, (nq, nk), 1)
    dist = jnp.abs(qpos - kpos).astype(F32)
    shift = int(math.log2(CHUNK))
    allowed = (kpos >> shift) <= (qpos >> shift)
    return jnp.where(allowed, -slope * dist, NEG_INF)


def _attn_init(stats):
    for m, l, a in stats:
        m[...] = jnp.full(m.shape, NEG_INF, F32)
        l[...] = jnp.zeros(l.shape, F32)
        a[...] = jnp.zeros(a.shape, F32)


def _attn_update(q, k, v, bias, stats):
    for i, (m, l, a) in enumerate(stats):
        lo, hi = i * DIFF_DH, (i + 1) * DIFF_DH
        s = _dot_nt(q[:, lo:hi], k[:, lo:hi]) + bias
        m_new = jnp.maximum(m[...], jnp.max(s, axis=-1, keepdims=True))
        alpha = jnp.exp(m[...] - m_new)
        p = jnp.exp(s - m_new)
        l[...] = alpha * l[...] + jnp.sum(p, axis=-1, keepdims=True)
        a[...] = alpha * a[...] + _dot(p.astype(BF16), v)
        m[...] = m_new


def _attn_finish(stats, lam, sw, lam_init):
    (_, l1, a1), (_, l2, a2) = stats
    o = a1[...] / l1[...] - lam * (a2[...] / l2[...])
    return o * lax.rsqrt(jnp.mean(o * o, axis=-1, keepdims=True) + 1e-6) * sw * (1.0 - lam_init)


def _attn_prompt_kernel(slopes_ref, lamv_ref, sw_ref, q_ref, k_ref, v_ref, o_ref,
                        vx_scr, bias_scr, q_scr, m_scr, acc_scr, pa_scr, pb_scr, ala_scr, alb_scr,
                        *, seq, tile, lam_init):
    t = tile
    r = 2 * t
    rb = ATTN_ROWS
    dv = DIFF_DV
    slope = slopes_ref[pl.program_id(1)]
    lam = _lambda(lamv_ref[...], lam_init)
    sw = sw_ref[...]
    shift = int(math.log2(CHUNK))

    vx_scr[:, 0:dv] = v_ref[0]
    vx_scr[:, dv:2 * dv] = jnp.ones((seq, dv), BF16)
    qi_ = lax.broadcasted_iota(jnp.int32, (t, t), 0)
    kj_ = lax.broadcasted_iota(jnp.int32, (t, t), 1)
    rel = (qi_ - kj_).astype(F32)
    bias_scr[0] = -slope * rel
    bias_scr[1] = jnp.where((kj_ >> shift) <= (qi_ >> shift), -slope * jnp.abs(rel), NEG_INF)
    lane = lax.broadcasted_iota(jnp.int32, (t, dv), 1)

    def q_body(qi, carry):
        q0 = pl.multiple_of(qi * t, t)
        qf = q_ref[0, pl.ds(q0, t), :].astype(F32) * (DIFF_DH ** -0.5)
        q_scr[0:t, :] = jnp.where(lane < DIFF_DH, qf, 0.0).astype(BF16)
        q_scr[t:r, :] = jnp.where(lane >= DIFF_DH, qf, 0.0).astype(BF16)
        m_scr[...] = jnp.full(m_scr.shape, NEG_INF, F32)
        acc_scr[...] = jnp.zeros(acc_scr.shape, F32)

        def scores(kj, p_buf, al_buf):
            k0 = pl.multiple_of(kj * t, t)
            diag = (kj == qi).astype(jnp.int32)
            c = -slope * (q0 - k0).astype(F32)
            s_all = _dot_nt(q_scr[...], k_ref[0, pl.ds(k0, t), :])
            for i in range(r // rb):
                r0 = i * rb
                b_lo = r0 % t
                s = s_all[r0:r0 + rb, :] + bias_scr[diag, b_lo:b_lo + rb, :]
                cols = [s[:, j * LANES:(j + 1) * LANES] for j in range(t // LANES)]
                m_prev = m_scr[r0:r0 + rb, :]
                m_cur = jnp.max(functools.reduce(jnp.maximum, cols), axis=-1, keepdims=True) + c
                m_new = jnp.maximum(m_prev, m_cur)
                al_buf[r0:r0 + rb, :] = jnp.exp(m_prev - m_new)
                mm = m_new - c
                for j, col in enumerate(cols):
                    p_buf[r0:r0 + rb, j * LANES:(j + 1) * LANES] = jnp.exp(col - mm).astype(BF16)
                m_scr[r0:r0 + rb, :] = m_new

        def accumulate(kj, p_buf, al_buf):
            k0 = pl.multiple_of(kj * t, t)
            pv = _dot(p_buf[...], vx_scr[pl.ds(k0, t), :])
            for i in range(r // rb):
                r0 = i * rb
                al = al_buf[r0:r0 + rb, :]
                acc_scr[r0:r0 + rb, :] = (jnp.concatenate([al, al], axis=1) * acc_scr[r0:r0 + rb, :]
                                          + pv[r0:r0 + rb, :])

        scores(0, pa_scr, ala_scr)

        def pair_body(i, c2):
            kj = 2 * i + 1
            accumulate(kj - 1, pa_scr, ala_scr)
            scores(kj, pb_scr, alb_scr)
            accumulate(kj, pb_scr, alb_scr)
            scores(kj + 1, pa_scr, ala_scr)
            return c2

        lax.fori_loop(0, qi // 2, pair_body, 0)

        @pl.when(qi % 2 == 1)
        def _():
            accumulate(qi - 1, pa_scr, ala_scr)
            scores(qi, pb_scr, alb_scr)
            accumulate(qi, pb_scr, alb_scr)

        @pl.when(qi % 2 == 0)
        def _():
            accumulate(qi, pa_scr, ala_scr)

        o1 = acc_scr[0:t, 0:dv] / acc_scr[0:t, dv:2 * dv]
        o2 = acc_scr[t:r, 0:dv] / acc_scr[t:r, dv:2 * dv]
        o = o1 - lam * o2
        on = o * lax.rsqrt(jnp.mean(o * o, axis=-1, keepdims=True) + 1e-6) * sw * (1.0 - lam_init)
        o_ref[0, pl.ds(q0, t), :] = on.astype(BF16)
        return carry

    lax.fori_loop(0, seq // t, q_body, 0)


def _attn_prompt(q, kb, vb, slopes, lamv, sw_row, lam_init):
    b, l, _ = q.shape
    t = min(ATTN_TILE, l)
    assert t % CHUNK == 0 and l % t == 0 and t % LANES == 0 and t % ATTN_ROWS == 0
    head = lambda i, h: (i, 0, h)
    fix = lambda i, h: (0, 0)
    dv = DIFF_DV
    scratch = [pltpu.VMEM((l, 2 * dv), BF16), pltpu.VMEM((2, t, t), F32),
               pltpu.VMEM((2 * t, dv), BF16), pltpu.VMEM((2 * t, LANES), F32),
               pltpu.VMEM((2 * t, 2 * dv), F32),
               pltpu.VMEM((2 * t, t), BF16), pltpu.VMEM((2 * t, t), BF16),
               pltpu.VMEM((2 * t, LANES), F32), pltpu.VMEM((2 * t, LANES), F32)]
    return pl.pallas_call(
        functools.partial(_attn_prompt_kernel, seq=l, tile=t, lam_init=lam_init),
        grid=(b, DIFF_HEADS),
        in_specs=[pl.BlockSpec(memory_space=pltpu.SMEM),
                  pl.BlockSpec(lamv.shape, fix), pl.BlockSpec(sw_row.shape, fix),
                  pl.BlockSpec((1, l, dv), head), pl.BlockSpec((1, l, dv), head),
                  pl.BlockSpec((1, l, dv), head)],
        out_specs=pl.BlockSpec((1, l, dv), head),
        out_shape=jax.ShapeDtypeStruct((b, l, DIFF_HEADS * dv), BF16),
        scratch_shapes=scratch,
        compiler_params=_params("parallel", "parallel"),
        name="attn_prompt",
    )(slopes, lamv, sw_row, q, kb, vb)


def _attn_sample_kernel(slopes_ref, lamv_ref, sw_ref, q_ref, kc_ref, vc_ref, kn_ref, vn_ref, o_ref,
                        m1, l1, a1, m2, l2, a2, *, past, tk, lam_init):
    j = pl.program_id(2)
    n_cache = past // tk
    slope = slopes_ref[pl.program_id(1)]
    stats = ((m1, l1, a1), (m2, l2, a2))
    lq = q_ref.shape[1]
    q = (q_ref[0].astype(F32) * (DIFF_DH ** -0.5)).astype(BF16)

    @pl.when(j == 0)
    def _():
        _attn_init(stats)

    @pl.when(j < n_cache)
    def _():
        k = kc_ref[0].astype(BF16)
        v = vc_ref[0].astype(BF16)
        _attn_update(q, k, v, _attn_bias(past, j * tk, lq, tk, slope), stats)

    @pl.when(j == n_cache)
    def _():
        _attn_update(q, kn_ref[0], vn_ref[0], _attn_bias(past, past, lq, lq, slope), stats)
        lam = _lambda(lamv_ref[...], lam_init)
        o_ref[0] = _attn_finish(stats, lam, sw_ref[...], lam_init).astype(BF16)


def _attn_sample(q, kb, vb, k_cache, v_cache, slopes, lamv, sw_row, lam_init):
    b, l, _ = q.shape
    past = k_cache.shape[1]
    tk = min(CACHE_TILE, past)
    assert past % tk == 0 and past % CHUNK == 0 and l <= CHUNK
    n_cache = past // tk
    head = lambda i, h, j: (i, 0, h)
    cache = lambda i, h, j: (i, jnp.minimum(j, n_cache - 1), h)
    fix = lambda i, h, j: (0, 0)
    stat = [pltpu.VMEM((l, 1), F32), pltpu.VMEM((l, 1), F32), pltpu.VMEM((l, DIFF_DV), F32)]
    return pl.pallas_call(
        functools.partial(_attn_sample_kernel, past=past, tk=tk, lam_init=lam_init),
        grid=(b, DIFF_HEADS, n_cache + 1),
        in_specs=[pl.BlockSpec(memory_space=pltpu.SMEM),
                  pl.BlockSpec(lamv.shape, fix), pl.BlockSpec(sw_row.shape, fix),
                  pl.BlockSpec((1, l, DIFF_DV), head),
                  pl.BlockSpec((1, tk, DIFF_DV), cache), pl.BlockSpec((1, tk, DIFF_DV), cache),
                  pl.BlockSpec((1, l, DIFF_DV), head), pl.BlockSpec((1, l, DIFF_DV), head)],
        out_specs=pl.BlockSpec((1, l, DIFF_DV), head),
        out_shape=jax.ShapeDtypeStruct((b, l, DIFF_HEADS * DIFF_DV), BF16),
        scratch_shapes=stat + stat,
        compiler_params=_params("parallel", "parallel", "arbitrary"),
        name="attn_sample",
    )(slopes, lamv, sw_row, q, k_cache, v_cache, kb, vb)


def _mix_kernel(x_ref, og_ref, od_ref, lig_ref, lib_ref, wo_ref, g1_ref, b1_ref, rw_ref, rb_ref,
                x1_ref, x1b_ref, route_ref, *, alpha):
    xn = _layernorm(x_ref[...], lig_ref[...], lib_ref[...])
    hw = og_ref.shape[1]
    mixed = _dot(og_ref[...], wo_ref[0:hw, :]) + _dot(od_ref[...], wo_ref[hw:, :])
    x1 = _layernorm(alpha * xn + mixed, g1_ref[...], b1_ref[...])
    x1_ref[...] = x1
    x1b_ref[...] = x1.astype(BF16)

    x_hi = x1.astype(BF16)
    x_lo = (x1 - x_hi.astype(F32)).astype(BF16)
    lg = (_dot(x_hi, rw_ref[0]) + _dot(x_lo, rw_ref[0]) + _dot(x_hi, rw_ref[1])) + rb_ref[...]
    lane = lax.broadcasted_iota(jnp.int32, lg.shape, 1)
    lanef = lane.astype(F32)
    big = float(LANES)

    gl = jnp.where(lane < N_GROUPS, lg, -jnp.inf)
    gmax = jnp.max(gl, axis=-1, keepdims=True)
    grp = jnp.min(jnp.where(gl == gmax, lanef, big), axis=-1, keepdims=True)
    g_gate = 1.0 / jnp.sum(jnp.where(lane < N_GROUPS, jnp.exp(gl - gmax), 0.0), axis=-1, keepdims=True)

    e_lo = N_GROUPS + grp * EXPERTS_PER_GROUP
    el = jnp.where((lanef >= e_lo) & (lanef < e_lo + EXPERTS_PER_GROUP), lg, -jnp.inf)
    v1 = jnp.max(el, axis=-1, keepdims=True)
    i1 = jnp.min(jnp.where(el == v1, lanef, big), axis=-1, keepdims=True)
    el2 = jnp.where(lanef == i1, -jnp.inf, el)
    v2 = jnp.max(el2, axis=-1, keepdims=True)
    i2 = jnp.min(jnp.where(el2 == v2, lanef, big), axis=-1, keepdims=True)
    e21 = jnp.exp(v2 - v1)
    p1 = 1.0 / (1.0 + e21)
    route_ref[...] = jnp.where(lane == 0, i1 - N_GROUPS,
                     jnp.where(lane == 1, i2 - N_GROUPS,
                     jnp.where(lane == 2, g_gate * p1,
                     jnp.where(lane == 3, g_gate * (e21 * p1), 0.0))))


def _mix(x2, og, od, li_g, li_b, wo, g1, b1, rw, rb, alpha):
    n, d = x2.shape
    tm = min(TOKEN_TILE, n)
    row = lambda i: (i, 0)
    fix = lambda i: (0, 0)
    return pl.pallas_call(
        functools.partial(_mix_kernel, alpha=alpha),
        grid=(n // tm,),
        in_specs=[pl.BlockSpec((tm, d), row), pl.BlockSpec((tm, og.shape[1]), row),
                  pl.BlockSpec((tm, od.shape[1]), row),
                  pl.BlockSpec((1, d), fix), pl.BlockSpec((1, d), fix), pl.BlockSpec(wo.shape, fix),
                  pl.BlockSpec((1, d), fix), pl.BlockSpec((1, d), fix),
                  pl.BlockSpec(rw.shape, lambda i: (0, 0, 0)), pl.BlockSpec((1, LANES), fix)],
        out_specs=[pl.BlockSpec((tm, d), row), pl.BlockSpec((tm, d), row), pl.BlockSpec((tm, LANES), row)],
        out_shape=[jax.ShapeDtypeStruct((n, d), F32), jax.ShapeDtypeStruct((n, d), BF16),
                   jax.ShapeDtypeStruct((n, LANES), F32)],
        compiler_params=_params("parallel"),
        name="mix",
    )(x2, og, od, li_g, li_b, wo, g1, b1, rw, rb)


def _expert_kernel(be_ref, nu_ref, x_ref, w1_ref, w3_ref, w2_ref, y_ref, w1b, w3b, w2b):
    i = pl.program_id(0)

    @pl.when((i == 0) | (be_ref[i] != be_ref[jnp.maximum(i - 1, 0)]))
    def _():
        w1b[...] = w1_ref[0].astype(BF16)
        w3b[...] = w3_ref[0].astype(BF16)
        w2b[...] = w2_ref[0].astype(BF16)

    @pl.when(i < nu_ref[0])
    def _():
        x = x_ref[...]
        h1 = _dot(x, w1b[...])
        h3 = _dot(x, w3b[...])
        y_ref[...] = _dot((_silu(h1) * h3).astype(BF16), w2b[...]).astype(BF16)

    @pl.when(i >= nu_ref[0])
    def _():
        y_ref[...] = jnp.zeros(y_ref.shape, y_ref.dtype)


def _expert_block(n_tokens):
    per_expert = max(1, n_tokens * TOP_K // N_EXPERTS)
    return int(min(EXPERT_BLOCK_MAX, max(EXPERT_BLOCK_MIN, 2 ** int(math.log2(per_expert)))))


def _experts(xb, block_expert, n_used, w1, w3, w2, blk):
    p_pad, d = xb.shape
    de = w1.shape[2]
    row = lambda i, be, nu: (i, 0)
    grid_spec = pltpu.PrefetchScalarGridSpec(
        num_scalar_prefetch=2,
        grid=(p_pad // blk,),
        in_specs=[pl.BlockSpec((blk, d), row),
                  pl.BlockSpec((1, d, de), lambda i, be, nu: (be[i], 0, 0)),
                  pl.BlockSpec((1, d, de), lambda i, be, nu: (be[i], 0, 0)),
                  pl.BlockSpec((1, de, d), lambda i, be, nu: (be[i], 0, 0))],
        out_specs=pl.BlockSpec((blk, d), row),
        scratch_shapes=[pltpu.VMEM((d, de), BF16), pltpu.VMEM((d, de), BF16), pltpu.VMEM((de, d), BF16)],
    )
    return pl.pallas_call(
        _expert_kernel,
        grid_spec=grid_spec,
        out_shape=jax.ShapeDtypeStruct((p_pad, d), BF16),
        compiler_params=_params("arbitrary"),
        name="experts",
    )(block_expert, n_used, xb, w1, w3, w2)


def _dispatch_plan(expert, blk):
    n = expert.shape[0]
    p = n * TOP_K
    n_blocks = -(-p // blk) + N_EXPERTS
    flat_e = expert.reshape(p)
    eids = jnp.arange(N_EXPERTS, dtype=jnp.int32)
    counts = jnp.sum((flat_e[:, None] == eids[None, :]).astype(jnp.int32), axis=0)
    starts = jnp.cumsum(counts) - counts
    padded = (counts + blk - 1) // blk * blk
    padded_end = jnp.cumsum(padded)
    padded_start = padded_end - padded
    order = jnp.argsort(flat_e).astype(jnp.int32)
    rank = jnp.argsort(order).astype(jnp.int32)
    offset = padded_start - starts
    dest = rank + offset[flat_e]
    block_start = jnp.arange(n_blocks, dtype=jnp.int32) * blk
    block_expert = jnp.minimum(
        jnp.sum((padded_end[None, :] <= block_start[:, None]).astype(jnp.int32), axis=1), N_EXPERTS - 1)
    e_row = jnp.repeat(block_expert, blk)
    j = jnp.arange(n_blocks * blk, dtype=jnp.int32) - padded_start[e_row]
    src = jnp.clip(starts[e_row] + j, 0, p - 1)
    tok_pad = jnp.where(j < counts[e_row], order[src] // TOP_K, 0)
    n_used = (padded_end[-1:] // blk).astype(jnp.int32)
    return tok_pad, dest.reshape(n, TOP_K), block_expert, n_used


def _final_kernel(x1_ref, route_ref, g_ref, b_ref, *refs, alpha, tiles_per_chunk):
    y_refs, o_ref = refs[:-1], refs[-1]
    chunk = pl.program_id(0) // tiles_per_chunk
    for c in range(len(y_refs) // TOP_K):
        @pl.when(chunk == c)
        def _(c=c):
            route = route_ref[...]
            moe = (route[:, 2:3] * y_refs[TOP_K * c][...].astype(F32)
                   + route[:, 3:4] * y_refs[TOP_K * c + 1][...].astype(F32))
            o_ref[...] = _layernorm(alpha * x1_ref[...] + moe, g_ref[...], b_ref[...])


def _final(x1, ys, route, g2, b2, alpha):
    n, d = x1.shape
    tm = min(TOKEN_TILE, n)
    tpc = n // len(ys) // tm
    row = lambda i: (i, 0)
    fix = lambda i: (0, 0)
    y_specs = [pl.BlockSpec((tm, d), lambda i, c=c: (jnp.clip(i - c * tpc, 0, tpc - 1), 0))
               for c in range(len(ys)) for _ in range(TOP_K)]
    return pl.pallas_call(
        functools.partial(_final_kernel, alpha=alpha, tiles_per_chunk=tpc),
        grid=(n // tm,),
        in_specs=[pl.BlockSpec((tm, d), row), pl.BlockSpec((tm, LANES), row),
                  pl.BlockSpec((1, d), fix), pl.BlockSpec((1, d), fix)] + y_specs,
        out_specs=pl.BlockSpec((tm, d), row),
        out_shape=jax.ShapeDtypeStruct((n, d), F32),
        compiler_params=_params("arbitrary"),
        name="final",
    )(x1, route, g2, b2, *[y for pair in ys for y in pair])


def _pad_lanes(v, width=LANES):
    v = v.reshape(1, -1).astype(F32)
    return jnp.pad(v, ((0, 0), (0, width - v.shape[1])))


def _layer(x, conv_prev, s_prev, k_cache, v_cache, p, alpha, lam_init):
    b, l, d = x.shape
    n = b * l
    x2 = x.reshape(n, d)
    conv_prev8 = jnp.pad(conv_prev.astype(F32), ((0, 0), (CONV_PAD - (CONV_W - 1), 0), (0, 0)))
    yc, z, q, k, v, kb, vb, ab, convn = _proj(x2, l, p["ln_in_g"], p["ln_in_b"], p["w_main"], p["w_ab"],
                                              conv_prev8, p["conv_w"])
    conv_new = convn[:, CONV_PAD - (CONV_W - 1):, :]
    og, s_new = _gdn(yc.reshape(b, l, -1), z.reshape(b, l, -1), ab.reshape(b, l, LANES),
                     s_prev.astype(F32), p["alog_row"], p["dtb_row"], p["nw_row"])

    q3 = q.reshape(b, l, -1)
    kb3 = kb.reshape(b, l, -1)
    vb3 = vb.reshape(b, l, -1)
    if k_cache is None:
        od = _attn_prompt(q3, kb3, vb3, p["slopes"], p["lamv"], p["sw_row"], lam_init)
    else:
        past = k_cache.shape[1]
        od = _attn_sample(q3, kb3, vb3, k_cache.reshape(b, past, -1), v_cache.reshape(b, past, -1),
                          p["slopes"], p["lamv"], p["sw_row"], lam_init)

    x1, x1b, route = _mix(x2, og.reshape(n, -1), od.reshape(n, -1), p["ln_in_g"], p["ln_in_b"], p["wo"],
                          p["ln1_g"], p["ln1_b"], p["rw"], p["rb"], alpha)

    expert = route[:, 0:TOP_K].astype(jnp.int32)
    n_chunks = MOE_CHUNKS if n % (MOE_CHUNKS * TOKEN_TILE) == 0 else 1
    nc = n // n_chunks
    blk = _expert_block(nc)
    ys = []
    for c in range(n_chunks):
        tok_pad, dest, block_expert, n_used = _dispatch_plan(expert[c * nc:(c + 1) * nc], blk)
        xb = jnp.take(x1b, tok_pad + c * nc, axis=0, mode="clip")
        yb = _experts(xb, block_expert, n_used, p["w1"], p["w3"], p["w2"], blk)
        ys.append((jnp.take(yb, dest[:, 0], axis=0, mode="clip"),
                   jnp.take(yb, dest[:, 1], axis=0, mode="clip")))
    y = _final(x1, ys, route, p["ln2_g"], p["ln2_b"], alpha)

    return (y.reshape(b, l, d), conv_new, s_new,
            k.reshape(b, l, DIFF_HEADS, 2 * DIFF_DH), v.reshape(b, l, DIFF_HEADS, DIFF_DV))


def kernel(x_prompt, x_sample, cache_attn_k, cache_attn_v, state_gdn, state_conv, ln_in_g, ln_in_b, w_in, conv_w, gdn_a_log, gdn_dt_bias, gdn_norm_w, lam_q1, lam_k1, lam_q2, lam_k2, subln_w, w_o, ln1_g, ln1_b, router_g_w, router_g_b, router_e_w, router_e_b, w1, w3, w2, ln2_g, ln2_b):
    depth = w_in.shape[0]
    assert depth == 1, "single-layer step"
    d = x_prompt.shape[-1]
    alpha = (2 * depth) ** 0.25
    lam_init = 0.8 - 0.6 * math.exp(-0.3 * 0)
    row = lambda t: t.reshape(1, -1).astype(F32)

    wi = w_in[0]
    conv_ch = conv_w.shape[-1]
    gw = GDN_HEADS * GDN_D
    c_ab = conv_ch + gw
    c_q = c_ab + 2 * GDN_HEADS
    w_main = jnp.concatenate([wi[:, :c_ab], wi[:, c_q:]], axis=1).astype(BF16)
    w_ab = jnp.pad(wi[:, c_ab:c_q], ((0, 0), (0, LANES - 2 * GDN_HEADS))).astype(BF16)
    rcat = jnp.concatenate([router_g_w[0], router_e_w[0]], axis=1)
    rcat = jnp.pad(rcat, ((0, 0), (0, LANES - rcat.shape[1])))
    r_hi = rcat.astype(BF16)
    r_lo = (rcat - r_hi.astype(F32)).astype(BF16)
    p = {
        "ln_in_g": row(ln_in_g), "ln_in_b": row(ln_in_b), "w_main": w_main, "w_ab": w_ab,
        "conv_w": conv_w[0].astype(F32), "alog_row": _pad_lanes(gdn_a_log[0]),
        "dtb_row": _pad_lanes(gdn_dt_bias[0]), "nw_row": row(gdn_norm_w[0]),
        "slopes": jnp.asarray(2.0 ** (-8.0 * (np.arange(DIFF_HEADS) + 1) / DIFF_HEADS), F32),
        "lamv": jnp.stack([lam_q1[0], lam_k1[0], lam_q2[0], lam_k2[0]]).astype(F32),
        "sw_row": row(subln_w[0]), "wo": w_o[0].astype(BF16),
        "ln1_g": row(ln1_g[0]), "ln1_b": row(ln1_b[0]),
        "rw": jnp.stack([r_hi, r_lo]), "rb": _pad_lanes(jnp.concatenate([router_g_b[0], router_e_b[0]])),
        "w1": w1[0], "w3": w3[0], "w2": w2[0],
        "ln2_g": row(ln2_g[0]), "ln2_b": row(ln2_b[0]),
    }

    bp = x_prompt.shape[0]
    conv0 = jnp.zeros((bp, CONV_W - 1, conv_ch), F32)
    s0 = jnp.zeros((bp, GDN_HEADS, GDN_D, GDN_D), F32)
    yp, cp, sp, kp, vp = _layer(x_prompt, conv0, s0, None, None, p, alpha, lam_init)
    ys, cs, ss, ks, vs = _layer(x_sample, state_conv[0], state_gdn[0], cache_attn_k[0], cache_attn_v[0],
                                p, alpha, lam_init)
    return (yp, ys, kp[None], vp[None], sp[None], cp[None], ks[None], vs[None], ss[None], cs[None])
```

```python
import functools
import math

import jax
import jax.numpy as jnp
import numpy as np
from jax import lax
from jax.experimental import pallas as pl
from jax.experimental.pallas import tpu as pltpu

F32 = jnp.float32
BF16 = jnp.bfloat16

CHUNK = 64
CONV_W = 4
GDN_HEADS = 4
GDN_D = 128
DIFF_HEADS = 4
DIFF_DH = 64
DIFF_DV = 2 * DIFF_DH
N_GROUPS = 4
EXPERTS_PER_GROUP = 8
N_EXPERTS = N_GROUPS * EXPERTS_PER_GROUP
TOP_K = 2
NEG_INF = -1e30
LANES = 128
CONV_PAD = 8

TOKEN_TILE = 512
GDN_TILE = 64
GDN_SEQS = 4
ATTN_TILE = 256
ATTN_ROWS = 64
CACHE_TILE = 1024
EXPERT_BLOCK_MAX = 512
EXPERT_BLOCK_MIN = 128
VMEM_LIMIT = 48 * 1024 * 1024


def _params(*sem):
    return pltpu.CompilerParams(dimension_semantics=sem, vmem_limit_bytes=VMEM_LIMIT)


def _dot(a, b):
    return jnp.dot(a, b, preferred_element_type=F32)


def _dot_nt(a, b):
    return lax.dot_general(a, b, (((1,), (1,)), ((), ())), preferred_element_type=F32)


def _sigmoid(x):
    return 1.0 / (1.0 + jnp.exp(-x))


def _silu(x):
    return x * _sigmoid(x)


def _softplus(x):
    return jnp.maximum(x, 0.0) + jnp.log(1.0 + jnp.exp(-jnp.abs(x)))


def _layernorm(x, g, b, eps=1e-5):
    mu = jnp.mean(x, axis=-1, keepdims=True)
    xc = x - mu
    var = jnp.mean(xc * xc, axis=-1, keepdims=True)
    return xc * lax.rsqrt(var + eps) * g + b


def _split3(x):
    hi = x.astype(BF16)
    r1 = x - hi.astype(F32)
    mid = r1.astype(BF16)
    lo = (r1 - mid.astype(F32)).astype(BF16)
    return hi, mid, lo


def _proj_kernel(x_ref, g_ref, b_ref, w_ref, wab_ref, convp_ref, cw_ref,
                 y_ref, z_ref, q_ref, k_ref, v_ref, kb_ref, vb_ref, ab_ref, convn_ref,
                 xc_scr, *, rows, n_seq, tiles_per_seq):
    xb = _layernorm(x_ref[...], g_ref[...], b_ref[...]).astype(BF16)
    w = 512
    stride = rows + CONV_PAD
    if tiles_per_seq > 1:
        first = pl.program_id(0) % tiles_per_seq == 0

        @pl.when(first)
        def _():
            xc_scr[0:CONV_PAD, :] = convp_ref[0]

        @pl.when(jnp.logical_not(first))
        def _():
            xc_scr[0:CONV_PAD, :] = xc_scr[rows:rows + CONV_PAD, :]
    else:
        for s in range(n_seq):
            xc_scr[s * stride:s * stride + CONV_PAD, :] = convp_ref[s]
    cw = cw_ref[...]
    for j in range(3):
        cs = slice(j * w, (j + 1) * w)
        qkv = _dot(xb, w_ref[:, cs])
        for s in range(n_seq):
            base = s * stride
            xc_scr[base + CONV_PAD:base + CONV_PAD + rows, cs] = qkv[s * rows:(s + 1) * rows, :]
            lo = base + CONV_PAD - (CONV_W - 1)
            y = cw[0:1, cs] * xc_scr[lo:lo + rows, cs]
            for t in range(1, CONV_W):
                y = y + cw[t:t + 1, cs] * xc_scr[lo + t:lo + t + rows, cs]
            y_ref[s * rows:(s + 1) * rows, cs] = _silu(y).astype(BF16)
    for s in range(n_seq):
        convn_ref[s] = xc_scr[s * stride + rows:s * stride + rows + CONV_PAD, :]
    z_ref[...] = _dot(xb, w_ref[:, 3 * w:4 * w]).astype(BF16)
    q_ref[...] = _dot(xb, w_ref[:, 4 * w:5 * w]).astype(BF16)
    tm = x_ref.shape[0]
    k = _dot(xb, w_ref[:, 5 * w:6 * w])
    kb_ref[...] = k.astype(BF16)
    v = _dot(xb, w_ref[:, 6 * w:7 * w])
    vb_ref[...] = v.astype(BF16)
    for h in range(DIFF_HEADS):
        k_ref[pl.ds(h, tm, stride=DIFF_HEADS), :] = k[:, h * DIFF_DV:(h + 1) * DIFF_DV]
        v_ref[pl.ds(h, tm, stride=DIFF_HEADS), :] = v[:, h * DIFF_DV:(h + 1) * DIFF_DV]
    ab_ref[...] = _dot(xb, wab_ref[...])


def _proj(x2, seq_len, ln_g, ln_b, w_main, w_ab, conv_prev8, conv_w):
    n, d = x2.shape
    tm = min(TOKEN_TILE, n)
    cc = conv_w.shape[1]
    rows = min(seq_len, tm)
    n_seq = tm // rows
    tiles_per_seq = seq_len // rows
    assert tm % rows == 0 and seq_len % rows == 0 and rows % CONV_PAD == 0
    row = lambda i: (i, 0)
    fix = lambda i: (0, 0)
    seq = lambda i: (i // tiles_per_seq, 0, 0)
    hd = DIFF_HEADS * DIFF_DV
    outs = [(1, cc, BF16), (1, hd, BF16), (1, hd, BF16), (DIFF_HEADS, DIFF_DV, F32),
            (DIFF_HEADS, DIFF_DV, F32), (1, hd, BF16), (1, hd, BF16), (1, LANES, F32)]
    return pl.pallas_call(
        functools.partial(_proj_kernel, rows=rows, n_seq=n_seq, tiles_per_seq=tiles_per_seq),
        grid=(n // tm,),
        in_specs=[pl.BlockSpec((tm, d), row), pl.BlockSpec((1, d), fix), pl.BlockSpec((1, d), fix),
                  pl.BlockSpec(w_main.shape, fix), pl.BlockSpec(w_ab.shape, fix),
                  pl.BlockSpec((n_seq, CONV_PAD, cc), seq), pl.BlockSpec((CONV_W, cc), fix)],
        out_specs=[pl.BlockSpec((tm * r, c), row) for r, c, _ in outs]
                  + [pl.BlockSpec((n_seq, CONV_PAD, cc), seq)],
        out_shape=[jax.ShapeDtypeStruct((n * r, c), t) for r, c, t in outs]
                  + [jax.ShapeDtypeStruct(conv_prev8.shape, F32)],
        scratch_shapes=[pltpu.VMEM((n_seq * (rows + CONV_PAD), cc), F32)],
        compiler_params=_params("arbitrary"),
        name="proj",
    )(x2, ln_g, ln_b, w_main, w_ab, conv_prev8, conv_w)


def _gdn_kernel(y_ref, z_ref, ab_ref, s0_ref, alog_ref, dtb_ref, nw_ref,
                o_ref, sn_ref,
                s_scr, cum_scr, a_scr, p_scr, attn_scr, rhs_scr, sol_scr, qg_scr, kdt_scr, u_scr, os_scr,
                *, nb, tl, c):
    l = pl.program_id(1)
    d = GDN_D
    hw = GDN_HEADS * d

    @pl.when(l == 0)
    def _():
        s_scr[...] = s0_ref[...]

    g_all = [-jnp.exp(alog_ref[...]) * _softplus(ab_ref[bi] + dtb_ref[...]) for bi in range(nb)]
    beta_all = [_sigmoid(ab_ref[bi]) for bi in range(nb)]
    nw = nw_ref[...]

    rows = lax.broadcasted_iota(jnp.int32, (c, c), 0)
    cols = lax.broadcasted_iota(jnp.int32, (c, c), 1)
    incl = cols <= rows
    strict = cols < rows
    tri = jnp.where(incl, 1.0, 0.0).astype(BF16)
    grow = lax.broadcasted_iota(jnp.int32, (c, LANES), 0)
    glane = lax.broadcasted_iota(jnp.int32, (c, LANES), 1)

    n_chunks = tl // c
    groups = [(ci, bi, h) for ci in range(n_chunks) for bi in range(nb) for h in range(GDN_HEADS)]
    eye = jnp.where(rows == cols, 1.0, 0.0)

    gmat = jnp.concatenate(
        [jnp.where(glane < c, jnp.where(grow > glane, g_all[bi][ci * c:(ci + 1) * c, h:h + 1], 0.0),
                   g_all[bi][ci * c:(ci + 1) * c, h:h + 1]) for ci, bi, h in groups], axis=1)
    g_hi, g_mid, g_lo = _split3(gmat)
    cum_scr[...] = _dot(tri, g_hi) + _dot(tri, g_mid) + _dot(tri, g_lo)

    for g, (ci, bi, h) in enumerate(groups):
        r0 = ci * c
        beta = beta_all[bi][r0:r0 + c, GDN_HEADS + h:GDN_HEADS + h + 1]
        rel = cum_scr[:, g * LANES:g * LANES + c]
        gam = cum_scr[:, g * LANES + c:g * LANES + c + 1]
        g_last = gam[c - 1:c, :]
        decay = jnp.where(incl, jnp.exp(jnp.where(incl, rel, 0.0)), 0.0)
        eg = jnp.exp(gam)
        ek = jnp.exp(g_last - gam)

        qh = y_ref[bi, r0:r0 + c, h * d:(h + 1) * d].astype(F32)
        kh = y_ref[bi, r0:r0 + c, hw + h * d:hw + (h + 1) * d].astype(F32)
        vh = y_ref[bi, r0:r0 + c, 2 * hw + h * d:2 * hw + (h + 1) * d].astype(F32)
        qn = qh * lax.rsqrt(jnp.sum(qh * qh, axis=-1, keepdims=True) + 1e-6) * (d ** -0.5)
        kn = kh * lax.rsqrt(jnp.sum(kh * kh, axis=-1, keepdims=True) + 1e-6)
        qb = qn.astype(BF16)
        kb = kn.astype(BF16)
        a = jnp.where(strict, beta * _dot_nt(kb, kb) * decay, 0.0)
        a_scr[g] = a
        p_scr[g] = eye - a
        attn_scr[g] = (_dot_nt(qb, kb) * decay).astype(BF16)
        rhs_scr[g, :, 0:d] = (vh * beta).astype(BF16)
        rhs_scr[g, :, d:2 * d] = (kn * (beta * eg)).astype(BF16)
        qg_scr[g] = (qn * eg).astype(BF16)
        kdt_scr[g] = jnp.transpose(kn * ek).astype(BF16)

    n_sq = int(math.log2(c)) - 1
    for it in range(n_sq):
        for g in range(len(groups)):
            akb = a_scr[g].astype(BF16)
            a_scr[g] = _dot(akb, akb)
        for g in range(len(groups)):
            p = p_scr[g]
            p_scr[g] = p + _dot(p.astype(BF16), a_scr[g].astype(BF16))

    for g in range(len(groups)):
        sol_scr[g] = _dot(p_scr[g].astype(BF16), rhs_scr[g])

    for g, (ci, bi, h) in enumerate(groups):
        sb = s_scr[bi, h].astype(BF16)
        u_scr[g] = (sol_scr[g, :, 0:d] - _dot(sol_scr[g, :, d:2 * d].astype(BF16), sb)).astype(BF16)
        os_scr[g] = _dot(qg_scr[g], sb)

    for g, (ci, bi, h) in enumerate(groups):
        r0 = ci * c
        ub = u_scr[g]
        o = os_scr[g] + _dot(attn_scr[g], ub)
        g_last = cum_scr[c - 1:c, g * LANES + c:g * LANES + c + 1]
        s_scr[bi, h] = s_scr[bi, h] * jnp.exp(g_last) + _dot(kdt_scr[g], ub)

        zh = z_ref[bi, r0:r0 + c, h * d:(h + 1) * d].astype(F32)
        on = o * lax.rsqrt(jnp.mean(o * o, axis=-1, keepdims=True) + 1e-6) * nw
        o_ref[bi, r0:r0 + c, h * d:(h + 1) * d] = (on * _silu(zh)).astype(BF16)

    @pl.when(l == pl.num_programs(1) - 1)
    def _():
        sn_ref[...] = s_scr[...]


def _gdn(y, z, ab, s0, alog_row, dtb_row, nw_row):
    b, l, cc = y.shape
    c = min(l, CHUNK)
    tl = min(l, GDN_TILE)
    nb = math.gcd(b, GDN_SEQS)
    assert tl == c, "one chunk per grid step: the state recurrence is staged across groups"
    ng = nb * GDN_HEADS
    hw = GDN_HEADS * GDN_D
    tile = lambda i, j: (i, j, 0)
    fix = lambda i, j: (0, 0)
    return pl.pallas_call(
        functools.partial(_gdn_kernel, nb=nb, tl=tl, c=c),
        grid=(b // nb, l // tl),
        in_specs=[pl.BlockSpec((nb, tl, cc), tile), pl.BlockSpec((nb, tl, hw), tile),
                  pl.BlockSpec((nb, tl, LANES), tile),
                  pl.BlockSpec((nb, GDN_HEADS, GDN_D, GDN_D), lambda i, j: (i, 0, 0, 0)),
                  pl.BlockSpec((1, LANES), fix),
                  pl.BlockSpec((1, LANES), fix), pl.BlockSpec((1, GDN_D), fix)],
        out_specs=[pl.BlockSpec((nb, tl, hw), tile),
                   pl.BlockSpec((nb, GDN_HEADS, GDN_D, GDN_D), lambda i, j: (i, 0, 0, 0))],
        out_shape=[jax.ShapeDtypeStruct((b, l, hw), BF16),
                   jax.ShapeDtypeStruct((b, GDN_HEADS, GDN_D, GDN_D), F32)],
        scratch_shapes=[pltpu.VMEM((nb, GDN_HEADS, GDN_D, GDN_D), F32),
                        pltpu.VMEM((c, ng * LANES), F32), pltpu.VMEM((ng, c, c), F32),
                        pltpu.VMEM((ng, c, c), F32), pltpu.VMEM((ng, c, c), BF16),
                        pltpu.VMEM((ng, c, 2 * GDN_D), BF16), pltpu.VMEM((ng, c, 2 * GDN_D), F32),
                        pltpu.VMEM((ng, c, GDN_D), BF16), pltpu.VMEM((ng, GDN_D, c), BF16),
                        pltpu.VMEM((ng, c, GDN_D), BF16), pltpu.VMEM((ng, c, GDN_D), F32)],
        compiler_params=_params("parallel", "arbitrary"),
        name="gdn",
    )(y, z, ab, s0, alog_row, dtb_row, nw_row)


def _lambda(lamv, lam_init):
    l1 = jnp.sum(lamv[0:1] * lamv[1:2], axis=-1, keepdims=True)
    l2 = jnp.sum(lamv[2:3] * lamv[3:4], axis=-1, keepdims=True)
    return jnp.exp(l1) - jnp.exp(l2) + lam_init


def _attn_bias(q0, k0, nq, nk, slope):
    qpos = q0 + lax.broadcasted_iota(jnp.int32, (nq, nk), 0)
    kpos = k0 + lax.broadcasted_iota(jnp.int32, (nq, nk), 1)
    dist = jnp.abs(qpos - kpos).astype(F32)
    shift = int(math.log2(CHUNK))
    allowed = (kpos >> shift) <= (qpos >> shift)
    return jnp.where(allowed, -slope * dist, NEG_INF)


def _attn_init(stats):
    for m, l, a in stats:
        m[...] = jnp.full(m.shape, NEG_INF, F32)
        l[...] = jnp.zeros(l.shape, F32)
        a[...] = jnp.zeros(a.shape, F32)


def _attn_update(q, k, v, bias, stats):
    for i, (m, l, a) in enumerate(stats):
        lo, hi = i * DIFF_DH, (i + 1) * DIFF_DH
        s = _dot_nt(q[:, lo:hi], k[:, lo:hi]) + bias
        m_new = jnp.maximum(m[...], jnp.max(s, axis=-1, keepdims=True))
        alpha = jnp.exp(m[...] - m_new)
        p = jnp.exp(s - m_new)
        l[...] = alpha * l[...] + jnp.sum(p, axis=-1, keepdims=True)
        a[...] = alpha * a[...] + _dot(p.astype(BF16), v)
        m[...] = m_new


def _attn_finish(stats, lam, sw, lam_init):
    (_, l1, a1), (_, l2, a2) = stats
    o = a1[...] / l1[...] - lam * (a2[...] / l2[...])
    return o * lax.rsqrt(jnp.mean(o * o, axis=-1, keepdims=True) + 1e-6) * sw * (1.0 - lam_init)


def _attn_prompt_kernel(slopes_ref, lamv_ref, sw_ref, q_ref, k_ref, v_ref, o_ref,
                        vx_scr, bias_scr, q_scr, m_scr, acc_scr, pa_scr, pb_scr, ala_scr, alb_scr,
                        *, seq, tile, lam_init):
    t = tile
    r = 2 * t
    rb = ATTN_ROWS
    dv = DIFF_DV
    slope = slopes_ref[pl.program_id(1)]
    lam = _lambda(lamv_ref[...], lam_init)
    sw = sw_ref[...]
    shift = int(math.log2(CHUNK))

    vx_scr[:, 0:dv] = v_ref[0]
    vx_scr[:, dv:2 * dv] = jnp.ones((seq, dv), BF16)
    qi_ = lax.broadcasted_iota(jnp.int32, (t, t), 0)
    kj_ = lax.broadcasted_iota(jnp.int32, (t, t), 1)
    rel = (qi_ - kj_).astype(F32)
    bias_scr[0] = -slope * rel
    bias_scr[1] = jnp.where((kj_ >> shift) <= (qi_ >> shift), -slope * jnp.abs(rel), NEG_INF)
    lane = lax.broadcasted_iota(jnp.int32, (t, dv), 1)

    def q_body(qi, carry):
        q0 = pl.multiple_of(qi * t, t)
        qf = q_ref[0, pl.ds(q0, t), :].astype(F32) * (DIFF_DH ** -0.5)
        q_scr[0:t, :] = jnp.where(lane < DIFF_DH, qf, 0.0).astype(BF16)
        q_scr[t:r, :] = jnp.where(lane >= DIFF_DH, qf, 0.0).astype(BF16)
        m_scr[...] = jnp.full(m_scr.shape, NEG_INF, F32)
        acc_scr[...] = jnp.zeros(acc_scr.shape, F32)

        def scores(kj, p_buf, al_buf):
            k0 = pl.multiple_of(kj * t, t)
            diag = (kj == qi).astype(jnp.int32)
            c = -slope * (q0 - k0).astype(F32)
            s_all = _dot_nt(q_scr[...], k_ref[0, pl.ds(k0, t), :])
            for i in range(r // rb):
                r0 = i * rb
                b_lo = r0 % t
                s = s_all[r0:r0 + rb, :] + bias_scr[diag, b_lo:b_lo + rb, :]
                cols = [s[:, j * LANES:(j + 1) * LANES] for j in range(t // LANES)]
                m_prev = m_scr[r0:r0 + rb, :]
                m_cur = jnp.max(functools.reduce(jnp.maximum, cols), axis=-1, keepdims=True) + c
                m_new = jnp.maximum(m_prev, m_cur)
                al_buf[r0:r0 + rb, :] = jnp.exp(m_prev - m_new)
                mm = m_new - c
                for j, col in enumerate(cols):
                    p_buf[r0:r0 + rb, j * LANES:(j + 1) * LANES] = jnp.exp(col - mm).astype(BF16)
                m_scr[r0:r0 + rb, :] = m_new

        def accumulate(kj, p_buf, al_buf):
            k0 = pl.multiple_of(kj * t, t)
            pv = _dot(p_buf[...], vx_scr[pl.ds(k0, t), :])
            for i in range(r // rb):
                r0 = i * rb
                al = al_buf[r0:r0 + rb, :]
                acc_scr[r0:r0 + rb, :] = (jnp.concatenate([al, al], axis=1) * acc_scr[r0:r0 + rb, :]
                                          + pv[r0:r0 + rb, :])

        scores(0, pa_scr, ala_scr)

        def pair_body(i, c2):
            kj = 2 * i + 1
            accumulate(kj - 1, pa_scr, ala_scr)
            scores(kj, pb_scr, alb_scr)
            accumulate(kj, pb_scr, alb_scr)
            scores(kj + 1, pa_scr, ala_scr)
            return c2

        lax.fori_loop(0, qi // 2, pair_body, 0)

        @pl.when(qi % 2 == 1)
        def _():
            accumulate(qi - 1, pa_scr, ala_scr)
            scores(qi, pb_scr, alb_scr)
            accumulate(qi, pb_scr, alb_scr)

        @pl.when(qi % 2 == 0)
        def _():
            accumulate(qi, pa_scr, ala_scr)

        o1 = acc_scr[0:t, 0:dv] / acc_scr[0:t, dv:2 * dv]
        o2 = acc_scr[t:r, 0:dv] / acc_scr[t:r, dv:2 * dv]
        o = o1 - lam * o2
        on = o * lax.rsqrt(jnp.mean(o * o, axis=-1, keepdims=True) + 1e-6) * sw * (1.0 - lam_init)
        o_ref[0, pl.ds(q0, t), :] = on.astype(BF16)
        return carry

    lax.fori_loop(0, seq // t, q_body, 0)


def _attn_prompt(q, kb, vb, slopes, lamv, sw_row, lam_init):
    b, l, _ = q.shape
    t = min(ATTN_TILE, l)
    assert t % CHUNK == 0 and l % t == 0 and t % LANES == 0 and t % ATTN_ROWS == 0
    head = lambda i, h: (i, 0, h)
    fix = lambda i, h: (0, 0)
    dv = DIFF_DV
    scratch = [pltpu.VMEM((l, 2 * dv), BF16), pltpu.VMEM((2, t, t), F32),
               pltpu.VMEM((2 * t, dv), BF16), pltpu.VMEM((2 * t, LANES), F32),
               pltpu.VMEM((2 * t, 2 * dv), F32),
               pltpu.VMEM((2 * t, t), BF16), pltpu.VMEM((2 * t, t), BF16),
               pltpu.VMEM((2 * t, LANES), F32), pltpu.VMEM((2 * t, LANES), F32)]
    return pl.pallas_call(
        functools.partial(_attn_prompt_kernel, seq=l, tile=t, lam_init=lam_init),
        grid=(b, DIFF_HEADS),
        in_specs=[pl.BlockSpec(memory_space=pltpu.SMEM),
                  pl.BlockSpec(lamv.shape, fix), pl.BlockSpec(sw_row.shape, fix),
                  pl.BlockSpec((1, l, dv), head), pl.BlockSpec((1, l, dv), head),
                  pl.BlockSpec((1, l, dv), head)],
        out_specs=pl.BlockSpec((1, l, dv), head),
        out_shape=jax.ShapeDtypeStruct((b, l, DIFF_HEADS * dv), BF16),
        scratch_shapes=scratch,
        compiler_params=_params("parallel", "parallel"),
        name="attn_prompt",
    )(slopes, lamv, sw_row, q, kb, vb)


def _attn_sample_kernel(slopes_ref, lamv_ref, sw_ref, q_ref, kc_ref, vc_ref, kn_ref, vn_ref, o_ref,
                        m1, l1, a1, m2, l2, a2, *, past, tk, lam_init):
    j = pl.program_id(1)
    n_cache = past // tk
    lq = q_ref.shape[1]
    for h in range(DIFF_HEADS):
        cs = slice(h * DIFF_DV, (h + 1) * DIFF_DV)
        slope = slopes_ref[h]
        stats = ((m1.at[h], l1.at[h], a1.at[h]), (m2.at[h], l2.at[h], a2.at[h]))
        q = (q_ref[0, :, cs].astype(F32) * (DIFF_DH ** -0.5)).astype(BF16)

        @pl.when(j == 0)
        def _(stats=stats):
            _attn_init(stats)

        @pl.when(j < n_cache)
        def _(h=h, q=q, slope=slope, stats=stats):
            k = kc_ref[0, pl.ds(h, tk, stride=DIFF_HEADS), :].astype(BF16)
            v = vc_ref[0, pl.ds(h, tk, stride=DIFF_HEADS), :].astype(BF16)
            _attn_update(q, k, v, _attn_bias(past, j * tk, lq, tk, slope), stats)

        @pl.when(j == n_cache)
        def _(cs=cs, q=q, slope=slope, stats=stats):
            _attn_update(q, kn_ref[0, :, cs], vn_ref[0, :, cs], _attn_bias(past, past, lq, lq, slope), stats)
            lam = _lambda(lamv_ref[...], lam_init)
            o_ref[0, :, cs] = _attn_finish(stats, lam, sw_ref[...], lam_init).astype(BF16)


def _attn_sample(q, kb, vb, k_cache, v_cache, slopes, lamv, sw_row, lam_init):
    b, l, hd = q.shape
    past = k_cache.shape[1] // DIFF_HEADS
    tk = min(CACHE_TILE, past)
    assert past % tk == 0 and past % CHUNK == 0 and l <= CHUNK
    n_cache = past // tk
    per_b = lambda i, j: (i, 0, 0)
    cache = lambda i, j: (i, jnp.minimum(j, n_cache - 1), 0)
    fix = lambda i, j: (0, 0)
    stat = [pltpu.VMEM((DIFF_HEADS, l, 1), F32), pltpu.VMEM((DIFF_HEADS, l, 1), F32),
            pltpu.VMEM((DIFF_HEADS, l, DIFF_DV), F32)]
    return pl.pallas_call(
        functools.partial(_attn_sample_kernel, past=past, tk=tk, lam_init=lam_init),
        grid=(b, n_cache + 1),
        in_specs=[pl.BlockSpec(memory_space=pltpu.SMEM),
                  pl.BlockSpec(lamv.shape, fix), pl.BlockSpec(sw_row.shape, fix),
                  pl.BlockSpec((1, l, hd), per_b),
                  pl.BlockSpec((1, tk * DIFF_HEADS, DIFF_DV), cache),
                  pl.BlockSpec((1, tk * DIFF_HEADS, DIFF_DV), cache),
                  pl.BlockSpec((1, l, hd), per_b), pl.BlockSpec((1, l, hd), per_b)],
        out_specs=pl.BlockSpec((1, l, hd), per_b),
        out_shape=jax.ShapeDtypeStruct((b, l, hd), BF16),
        scratch_shapes=stat + stat,
        compiler_params=_params("parallel", "arbitrary"),
        name="attn_sample",
    )(slopes, lamv, sw_row, q, k_cache, v_cache, kb, vb)


def _mix_kernel(x_ref, og_ref, od_ref, lig_ref, lib_ref, wo_ref, g1_ref, b1_ref, rw_ref, rb_ref,
                x1_ref, x1b_ref, route_ref, *, alpha):
    xn = _layernorm(x_ref[...], lig_ref[...], lib_ref[...])
    hw = og_ref.shape[1]
    mixed = _dot(og_ref[...], wo_ref[0:hw, :]) + _dot(od_ref[...], wo_ref[hw:, :])
    x1 = _layernorm(alpha * xn + mixed, g1_ref[...], b1_ref[...])
    x1_ref[...] = x1
    x1b_ref[...] = x1.astype(BF16)

    x_hi = x1.astype(BF16)
    x_lo = (x1 - x_hi.astype(F32)).astype(BF16)
    lg = (_dot(x_hi, rw_ref[0]) + _dot(x_lo, rw_ref[0]) + _dot(x_hi, rw_ref[1])) + rb_ref[...]
    lane = lax.broadcasted_iota(jnp.int32, lg.shape, 1)
    lanef = lane.astype(F32)
    big = float(LANES)

    gl = jnp.where(lane < N_GROUPS, lg, -jnp.inf)
    gmax = jnp.max(gl, axis=-1, keepdims=True)
    grp = jnp.min(jnp.where(gl == gmax, lanef, big), axis=-1, keepdims=True)
    g_gate = 1.0 / jnp.sum(jnp.where(lane < N_GROUPS, jnp.exp(gl - gmax), 0.0), axis=-1, keepdims=True)

    e_lo = N_GROUPS + grp * EXPERTS_PER_GROUP
    el = jnp.where((lanef >= e_lo) & (lanef < e_lo + EXPERTS_PER_GROUP), lg, -jnp.inf)
    v1 = jnp.max(el, axis=-1, keepdims=True)
    i1 = jnp.min(jnp.where(el == v1, lanef, big), axis=-1, keepdims=True)
    el2 = jnp.where(lanef == i1, -jnp.inf, el)
    v2 = jnp.max(el2, axis=-1, keepdims=True)
    i2 = jnp.min(jnp.where(el2 == v2, lanef, big), axis=-1, keepdims=True)
    e21 = jnp.exp(v2 - v1)
    p1 = 1.0 / (1.0 + e21)
    route_ref[...] = jnp.where(lane == 0, i1 - N_GROUPS,
                     jnp.where(lane == 1, i2 - N_GROUPS,
                     jnp.where(lane == 2, g_gate * p1,
                     jnp.where(lane == 3, g_gate * (e21 * p1), 0.0))))


def _mix(x2, og, od, li_g, li_b, wo, g1, b1, rw, rb, alpha):
    n, d = x2.shape
    tm = min(TOKEN_TILE, n)
    row = lambda i: (i, 0)
    fix = lambda i: (0, 0)
    return pl.pallas_call(
        functools.partial(_mix_kernel, alpha=alpha),
        grid=(n // tm,),
        in_specs=[pl.BlockSpec((tm, d), row), pl.BlockSpec((tm, og.shape[1]), row),
                  pl.BlockSpec((tm, od.shape[1]), row),
                  pl.BlockSpec((1, d), fix), pl.BlockSpec((1, d), fix), pl.BlockSpec(wo.shape, fix),
                  pl.BlockSpec((1, d), fix), pl.BlockSpec((1, d), fix),
                  pl.BlockSpec(rw.shape, lambda i: (0, 0, 0)), pl.BlockSpec((1, LANES), fix)],
        out_specs=[pl.BlockSpec((tm, d), row), pl.BlockSpec((tm, d), row), pl.BlockSpec((tm, LANES), row)],
        out_shape=[jax.ShapeDtypeStruct((n, d), F32), jax.ShapeDtypeStruct((n, d), BF16),
                   jax.ShapeDtypeStruct((n, LANES), F32)],
        compiler_params=_params("parallel"),
        name="mix",
    )(x2, og, od, li_g, li_b, wo, g1, b1, rw, rb)


def _expert_kernel(be_ref, nu_ref, x_ref, w1_ref, w3_ref, w2_ref, y_ref, w1b, w3b, w2b):
    i = pl.program_id(0)

    @pl.when((i == 0) | (be_ref[i] != be_ref[jnp.maximum(i - 1, 0)]))
    def _():
        w1b[...] = w1_ref[0].astype(BF16)
        w3b[...] = w3_ref[0].astype(BF16)
        w2b[...] = w2_ref[0].astype(BF16)

    @pl.when(i < nu_ref[0])
    def _():
        x = x_ref[...]
        h1 = _dot(x, w1b[...])
        h3 = _dot(x, w3b[...])
        y_ref[...] = _dot((_silu(h1) * h3).astype(BF16), w2b[...]).astype(BF16)

    @pl.when(i >= nu_ref[0])
    def _():
        y_ref[...] = jnp.zeros(y_ref.shape, y_ref.dtype)


def _expert_block(n_tokens):
    per_expert = max(1, n_tokens * TOP_K // N_EXPERTS)
    return int(min(EXPERT_BLOCK_MAX, max(EXPERT_BLOCK_MIN, 2 ** int(math.log2(per_expert)))))


def _experts(xb, block_expert, n_used, w1, w3, w2, blk):
    p_pad, d = xb.shape
    de = w1.shape[2]
    row = lambda i, be, nu: (i, 0)
    grid_spec = pltpu.PrefetchScalarGridSpec(
        num_scalar_prefetch=2,
        grid=(p_pad // blk,),
        in_specs=[pl.BlockSpec((blk, d), row),
                  pl.BlockSpec((1, d, de), lambda i, be, nu: (be[i], 0, 0)),
                  pl.BlockSpec((1, d, de), lambda i, be, nu: (be[i], 0, 0)),
                  pl.BlockSpec((1, de, d), lambda i, be, nu: (be[i], 0, 0))],
        out_specs=pl.BlockSpec((blk, d), row),
        scratch_shapes=[pltpu.VMEM((d, de), BF16), pltpu.VMEM((d, de), BF16), pltpu.VMEM((de, d), BF16)],
    )
    return pl.pallas_call(
        _expert_kernel,
        grid_spec=grid_spec,
        out_shape=jax.ShapeDtypeStruct((p_pad, d), BF16),
        compiler_params=_params("arbitrary"),
        name="experts",
    )(block_expert, n_used, xb, w1, w3, w2)


def _dispatch_plan(expert, blk):
    n = expert.shape[0]
    p = n * TOP_K
    n_blocks = -(-p // blk) + N_EXPERTS
    flat_e = expert.reshape(p)
    eids = jnp.arange(N_EXPERTS, dtype=jnp.int32)
    counts = jnp.sum((flat_e[:, None] == eids[None, :]).astype(jnp.int32), axis=0)
    starts = jnp.cumsum(counts) - counts
    padded = (counts + blk - 1) // blk * blk
    padded_end = jnp.cumsum(padded)
    padded_start = padded_end - padded
    order = jnp.argsort(flat_e).astype(jnp.int32)
    rank = jnp.argsort(order).astype(jnp.int32)
    offset = padded_start - starts
    dest = rank + offset[flat_e]
    block_start = jnp.arange(n_blocks, dtype=jnp.int32) * blk
    block_expert = jnp.minimum(
        jnp.sum((padded_end[None, :] <= block_start[:, None]).astype(jnp.int32), axis=1), N_EXPERTS - 1)
    e_row = jnp.repeat(block_expert, blk)
    row_id = jnp.arange(n_blocks * blk, dtype=jnp.int32)
    j = row_id - padded_start[e_row]
    src = jnp.clip(starts[e_row] + j, 0, p - 1)
    tok_pad = jnp.where(j < counts[e_row], order[src] // TOP_K, row_id % n)
    n_used = (padded_end[-1:] // blk).astype(jnp.int32)
    return tok_pad, dest.reshape(n, TOP_K), block_expert, n_used


def _final_kernel(x1_ref, y0_ref, y1_ref, route_ref, g_ref, b_ref, o_ref, *, alpha):
    route = route_ref[...]
    moe = route[:, 2:3] * y0_ref[...].astype(F32) + route[:, 3:4] * y1_ref[...].astype(F32)
    o_ref[...] = _layernorm(alpha * x1_ref[...] + moe, g_ref[...], b_ref[...])


def _final(x1, y0, y1, route, g2, b2, alpha):
    n, d = x1.shape
    tm = min(TOKEN_TILE, n)
    row = lambda i: (i, 0)
    fix = lambda i: (0, 0)
    return pl.pallas_call(
        functools.partial(_final_kernel, alpha=alpha),
        grid=(n // tm,),
        in_specs=[pl.BlockSpec((tm, d), row), pl.BlockSpec((tm, d), row), pl.BlockSpec((tm, d), row),
                  pl.BlockSpec((tm, LANES), row), pl.BlockSpec((1, d), fix), pl.BlockSpec((1, d), fix)],
        out_specs=pl.BlockSpec((tm, d), row),
        out_shape=jax.ShapeDtypeStruct((n, d), F32),
        compiler_params=_params("parallel"),
        name="final",
    )(x1, y0, y1, route, g2, b2)


def _pad_lanes(v, width=LANES):
    v = v.reshape(1, -1).astype(F32)
    return jnp.pad(v, ((0, 0), (0, width - v.shape[1])))


def _layer(x, conv_prev, s_prev, k_cache, v_cache, p, alpha, lam_init):
    b, l, d = x.shape
    n = b * l
    x2 = x.reshape(n, d)
    conv_prev8 = jnp.pad(conv_prev.astype(F32), ((0, 0), (CONV_PAD - (CONV_W - 1), 0), (0, 0)))
    yc, z, q, k, v, kb, vb, ab, convn = _proj(x2, l, p["ln_in_g"], p["ln_in_b"], p["w_main"], p["w_ab"],
                                              conv_prev8, p["conv_w"])
    conv_new = convn[:, CONV_PAD - (CONV_W - 1):, :]
    og, s_new = _gdn(yc.reshape(b, l, -1), z.reshape(b, l, -1), ab.reshape(b, l, LANES),
                     s_prev.astype(F32), p["alog_row"], p["dtb_row"], p["nw_row"])

    q3 = q.reshape(b, l, -1)
    kb3 = kb.reshape(b, l, -1)
    vb3 = vb.reshape(b, l, -1)
    if k_cache is None:
        od = _attn_prompt(q3, kb3, vb3, p["slopes"], p["lamv"], p["sw_row"], lam_init)
    else:
        past = k_cache.shape[1]
        od = _attn_sample(q3, kb3, vb3, k_cache.reshape(b, past * DIFF_HEADS, DIFF_DV),
                          v_cache.reshape(b, past * DIFF_HEADS, DIFF_DV),
                          p["slopes"], p["lamv"], p["sw_row"], lam_init)

    x1, x1b, route = _mix(x2, og.reshape(n, -1), od.reshape(n, -1), p["ln_in_g"], p["ln_in_b"], p["wo"],
                          p["ln1_g"], p["ln1_b"], p["rw"], p["rb"], alpha)

    expert = route[:, 0:TOP_K].astype(jnp.int32)
    blk = _expert_block(n)
    tok_pad, dest, block_expert, n_used = _dispatch_plan(expert, blk)
    xb = jnp.take(x1b, tok_pad, axis=0, mode="clip")
    yb = _experts(xb, block_expert, n_used, p["w1"], p["w3"], p["w2"], blk)
    y0 = jnp.take(yb, dest[:, 0], axis=0, mode="clip")
    y1 = jnp.take(yb, dest[:, 1], axis=0, mode="clip")
    y = _final(x1, y0, y1, route, p["ln2_g"], p["ln2_b"], alpha)

    return (y.reshape(b, l, d), conv_new, s_new,
            k.reshape(b, l, DIFF_HEADS, 2 * DIFF_DH), v.reshape(b, l, DIFF_HEADS, DIFF_DV))


def kernel(x_prompt, x_sample, cache_attn_k, cache_attn_v, state_gdn, state_conv, ln_in_g, ln_in_b, w_in, conv_w, gdn_a_log, gdn_dt_bias, gdn_norm_w, lam_q1, lam_k1, lam_q2, lam_k2, subln_w, w_o, ln1_g, ln1_b, router_g_w, router_g_b, router_e_w, router_e_b, w1, w3, w2, ln2_g, ln2_b):
    depth = w_in.shape[0]
    assert depth == 1, "single-layer step"
    d = x_prompt.shape[-1]
    alpha = (2 * depth) ** 0.25
    lam_init = 0.8 - 0.6 * math.exp(-0.3 * 0)
    row = lambda t: t.reshape(1, -1).astype(F32)

    wi = w_in[0]
    conv_ch = conv_w.shape[-1]
    gw = GDN_HEADS * GDN_D
    c_ab = conv_ch + gw
    c_q = c_ab + 2 * GDN_HEADS
    w_main = jnp.concatenate([wi[:, :c_ab], wi[:, c_q:]], axis=1).astype(BF16)
    w_ab = jnp.pad(wi[:, c_ab:c_q], ((0, 0), (0, LANES - 2 * GDN_HEADS))).astype(BF16)
    rcat = jnp.concatenate([router_g_w[0], router_e_w[0]], axis=1)
    rcat = jnp.pad(rcat, ((0, 0), (0, LANES - rcat.shape[1])))
    r_hi = rcat.astype(BF16)
    r_lo = (rcat - r_hi.astype(F32)).astype(BF16)
    p = {
        "ln_in_g": row(ln_in_g), "ln_in_b": row(ln_in_b), "w_main": w_main, "w_ab": w_ab,
        "conv_w": conv_w[0].astype(F32), "alog_row": _pad_lanes(gdn_a_log[0]),
        "dtb_row": _pad_lanes(gdn_dt_bias[0]), "nw_row": row(gdn_norm_w[0]),
        "slopes": jnp.asarray(2.0 ** (-8.0 * (np.arange(DIFF_HEADS) + 1) / DIFF_HEADS), F32),
        "lamv": jnp.stack([lam_q1[0], lam_k1[0], lam_q2[0], lam_k2[0]]).astype(F32),
        "sw_row": row(subln_w[0]), "wo": w_o[0].astype(BF16),
        "ln1_g": row(ln1_g[0]), "ln1_b": row(ln1_b[0]),
        "rw": jnp.stack([r_hi, r_lo]), "rb": _pad_lanes(jnp.concatenate([router_g_b[0], router_e_b[0]])),
        "w1": w1[0], "w3": w3[0], "w2": w2[0],
        "ln2_g": row(ln2_g[0]), "ln2_b": row(ln2_b[0]),
    }

    bp = x_prompt.shape[0]
    conv0 = jnp.zeros((bp, CONV_W - 1, conv_ch), F32)
    s0 = jnp.zeros((bp, GDN_HEADS, GDN_D, GDN_D), F32)
    yp, cp, sp, kp, vp = _layer(x_prompt, conv0, s0, None, None, p, alpha, lam_init)
    ys, cs, ss, ks, vs = _layer(x_sample, state_conv[0], state_gdn[0], cache_attn_k[0], cache_attn_v[0],
                                p, alpha, lam_init)
    return (yp, ys, kp[None], vp[None], sp[None], cp[None], ks[None], vs[None], ss[None], cs[None])
```

```python
import functools
import math

import jax
import jax.numpy as jnp
import numpy as np
from jax import lax
from jax.experimental import pallas as pl
from jax.experimental.pallas import tpu as pltpu

F32 = jnp.float32
BF16 = jnp.bfloat16

CHUNK = 64
CONV_W = 4
GDN_HEADS = 4
GDN_D = 128
DIFF_HEADS = 4
DIFF_DH = 64
DIFF_DV = 2 * DIFF_DH
N_GROUPS = 4
EXPERTS_PER_GROUP = 8
N_EXPERTS = N_GROUPS * EXPERTS_PER_GROUP
TOP_K = 2
NEG_INF = -1e30
LOG2E = math.log2(math.e)
LANES = 128
CONV_PAD = 8

TOKEN_TILE = 512
GDN_TILE = 64
GDN_SEQS = 4
ATTN_TILE = 256
ATTN_ROWS = 64
CACHE_TILE = 1024
EXPERT_BLOCK_MAX = 512
EXPERT_BLOCK_MIN = 128
VMEM_LIMIT = 48 * 1024 * 1024


def _params(*sem):
    return pltpu.CompilerParams(dimension_semantics=sem, vmem_limit_bytes=VMEM_LIMIT)


def _dot(a, b):
    return jnp.dot(a, b, preferred_element_type=F32)


def _dot_nt(a, b):
    return lax.dot_general(a, b, (((1,), (1,)), ((), ())), preferred_element_type=F32)


def _sigmoid(x):
    return 1.0 / (1.0 + jnp.exp(-x))


def _silu(x):
    return x * _sigmoid(x)


def _softplus(x):
    return jnp.maximum(x, 0.0) + jnp.log(1.0 + jnp.exp(-jnp.abs(x)))


def _layernorm(x, g, b, eps=1e-5):
    mu = jnp.mean(x, axis=-1, keepdims=True)
    xc = x - mu
    var = jnp.mean(xc * xc, axis=-1, keepdims=True)
    return xc * lax.rsqrt(var + eps) * g + b


def _split3(x):
    hi = x.astype(BF16)
    r1 = x - hi.astype(F32)
    mid = r1.astype(BF16)
    lo = (r1 - mid.astype(F32)).astype(BF16)
    return hi, mid, lo


def _proj_kernel(x_ref, g_ref, b_ref, w_ref, wab_ref, convp_ref, cw_ref,
                 y_ref, z_ref, q_ref, k_ref, v_ref, kb_ref, vb_ref, ab_ref, convn_ref,
                 xc_scr, *, rows, n_seq, tiles_per_seq):
    xb = _layernorm(x_ref[...], g_ref[...], b_ref[...]).astype(BF16)
    w = 512
    stride = rows + CONV_PAD
    if tiles_per_seq > 1:
        first = pl.program_id(0) % tiles_per_seq == 0

        @pl.when(first)
        def _():
            xc_scr[0:CONV_PAD, :] = convp_ref[0]

        @pl.when(jnp.logical_not(first))
        def _():
            xc_scr[0:CONV_PAD, :] = xc_scr[rows:rows + CONV_PAD, :]
    else:
        for s in range(n_seq):
            xc_scr[s * stride:s * stride + CONV_PAD, :] = convp_ref[s]
    cw = cw_ref[...]

    for j in range(3):
        cs = slice(j * w, (j + 1) * w)
        qkv = _dot(xb, w_ref[:, cs])
        for s in range(n_seq):
            base = s * stride + CONV_PAD
            xc_scr[base:base + rows, cs] = qkv[s * rows:(s + 1) * rows, :]
            lo = base - (CONV_W - 1)
            y = cw[0:1, cs] * xc_scr[lo:lo + rows, cs]
            for t in range(1, CONV_W):
                y = y + cw[t:t + 1, cs] * xc_scr[lo + t:lo + t + rows, cs]
            y_ref[s * rows:(s + 1) * rows, cs] = _silu(y).astype(BF16)
    for s in range(n_seq):
        convn_ref[s] = xc_scr[s * stride + rows:s * stride + rows + CONV_PAD, :]
    z_ref[...] = _dot(xb, w_ref[:, 3 * w:4 * w]).astype(BF16)
    q_ref[...] = _dot(xb, w_ref[:, 4 * w:5 * w]).astype(BF16)
    tm = x_ref.shape[0]
    k = _dot(xb, w_ref[:, 5 * w:6 * w])
    kb_ref[...] = k.astype(BF16)
    v = _dot(xb, w_ref[:, 6 * w:7 * w])
    vb_ref[...] = v.astype(BF16)
    for h in range(DIFF_HEADS):
        k_ref[pl.ds(h, tm, stride=DIFF_HEADS), :] = k[:, h * DIFF_DV:(h + 1) * DIFF_DV]
        v_ref[pl.ds(h, tm, stride=DIFF_HEADS), :] = v[:, h * DIFF_DV:(h + 1) * DIFF_DV]
    ab_ref[...] = _dot(xb, wab_ref[...])


def _proj(x2, seq_len, ln_g, ln_b, w_main, w_ab, conv_prev8, conv_w):
    n, d = x2.shape
    tm = min(TOKEN_TILE, n)
    cc = conv_w.shape[1]
    rows = min(seq_len, tm)
    n_seq = tm // rows
    tiles_per_seq = seq_len // rows
    assert tm % rows == 0 and seq_len % rows == 0 and rows % CONV_PAD == 0
    row = lambda i: (i, 0)
    fix = lambda i: (0, 0)
    seq = lambda i: (i // tiles_per_seq, 0, 0)
    hd = DIFF_HEADS * DIFF_DV
    outs = [(1, cc, BF16), (1, hd, BF16), (1, hd, BF16), (DIFF_HEADS, DIFF_DV, F32),
            (DIFF_HEADS, DIFF_DV, F32), (1, hd, BF16), (1, hd, BF16), (1, LANES, F32)]
    return pl.pallas_call(
        functools.partial(_proj_kernel, rows=rows, n_seq=n_seq, tiles_per_seq=tiles_per_seq),
        grid=(n // tm,),
        in_specs=[pl.BlockSpec((tm, d), row), pl.BlockSpec((1, d), fix), pl.BlockSpec((1, d), fix),
                  pl.BlockSpec(w_main.shape, fix), pl.BlockSpec(w_ab.shape, fix),
                  pl.BlockSpec((n_seq, CONV_PAD, cc), seq), pl.BlockSpec((CONV_W, cc), fix)],
        out_specs=[pl.BlockSpec((tm * r, c), row) for r, c, _ in outs]
                  + [pl.BlockSpec((n_seq, CONV_PAD, cc), seq)],
        out_shape=[jax.ShapeDtypeStruct((n * r, c), t) for r, c, t in outs]
                  + [jax.ShapeDtypeStruct(conv_prev8.shape, F32)],
        scratch_shapes=[pltpu.VMEM((n_seq * (rows + CONV_PAD), cc), F32)],
        compiler_params=_params("arbitrary"),
        name="proj",
    )(x2, ln_g, ln_b, w_main, w_ab, conv_prev8, conv_w)


def _gdn_kernel(y_ref, z_ref, ab_ref, s0_ref, alog_ref, dtb_ref, nw_ref,
                o_ref, sn_ref,
                s_scr, cum_scr, a_scr, p_scr, attn_scr, rhs_scr, sol_scr, qg_scr, kdt_scr, u_scr, os_scr,
                *, nb, tl, c):
    l = pl.program_id(1)
    d = GDN_D
    hw = GDN_HEADS * d

    @pl.when(l == 0)
    def _():
        s_scr[...] = s0_ref[...]

    g_all = [-jnp.exp(alog_ref[...]) * _softplus(ab_ref[bi] + dtb_ref[...]) for bi in range(nb)]
    beta_all = [_sigmoid(ab_ref[bi]) for bi in range(nb)]
    nw = nw_ref[...]

    rows = lax.broadcasted_iota(jnp.int32, (c, c), 0)
    cols = lax.broadcasted_iota(jnp.int32, (c, c), 1)
    incl = cols <= rows
    strict = cols < rows
    tri = jnp.where(incl, 1.0, 0.0).astype(BF16)
    grow = lax.broadcasted_iota(jnp.int32, (c, LANES), 0)
    glane = lax.broadcasted_iota(jnp.int32, (c, LANES), 1)

    n_chunks = tl // c
    groups = [(ci, bi, h) for ci in range(n_chunks) for bi in range(nb) for h in range(GDN_HEADS)]
    eye = jnp.where(rows == cols, 1.0, 0.0)

    gmat = jnp.concatenate(
        [jnp.where(glane < c, jnp.where(grow > glane, g_all[bi][ci * c:(ci + 1) * c, h:h + 1], 0.0),
                   g_all[bi][ci * c:(ci + 1) * c, h:h + 1]) for ci, bi, h in groups], axis=1)
    g_hi, g_mid, g_lo = _split3(gmat)
    cum_scr[...] = _dot(tri, g_hi) + _dot(tri, g_mid) + _dot(tri, g_lo)

    for g, (ci, bi, h) in enumerate(groups):
        r0 = ci * c
        beta = beta_all[bi][r0:r0 + c, GDN_HEADS + h:GDN_HEADS + h + 1]
        rel = cum_scr[:, g * LANES:g * LANES + c]
        gam = cum_scr[:, g * LANES + c:g * LANES + c + 1]
        g_last = gam[c - 1:c, :]
        decay = jnp.where(incl, jnp.exp(jnp.where(incl, rel, 0.0)), 0.0)
        eg = jnp.exp(gam)
        ek = jnp.exp(g_last - gam)

        qh = y_ref[bi, r0:r0 + c, h * d:(h + 1) * d].astype(F32)
        kh = y_ref[bi, r0:r0 + c, hw + h * d:hw + (h + 1) * d].astype(F32)
        vh = y_ref[bi, r0:r0 + c, 2 * hw + h * d:2 * hw + (h + 1) * d].astype(F32)
        qn = qh * lax.rsqrt(jnp.sum(qh * qh, axis=-1, keepdims=True) + 1e-6) * (d ** -0.5)
        kn = kh * lax.rsqrt(jnp.sum(kh * kh, axis=-1, keepdims=True) + 1e-6)
        qb = qn.astype(BF16)
        kb = kn.astype(BF16)
        a = jnp.where(strict, beta * _dot_nt(kb, kb) * decay, 0.0)
        a_scr[g] = a
        p_scr[g] = eye - a
        attn_scr[g] = (_dot_nt(qb, kb) * decay).astype(BF16)
        rhs_scr[g, :, 0:d] = (vh * beta).astype(BF16)
        rhs_scr[g, :, d:2 * d] = (kn * (beta * eg)).astype(BF16)
        qg_scr[g] = (qn * eg).astype(BF16)
        kdt_scr[g] = jnp.transpose(kn * ek).astype(BF16)

    n_sq = int(math.log2(c)) - 1
    for it in range(n_sq):
        for g in range(len(groups)):
            akb = a_scr[g].astype(BF16)
            a_scr[g] = _dot(akb, akb)
        for g in range(len(groups)):
            p = p_scr[g]
            p_scr[g] = p + _dot(p.astype(BF16), a_scr[g].astype(BF16))

    for g in range(len(groups)):
        sol_scr[g] = _dot(p_scr[g].astype(BF16), rhs_scr[g])

    for g, (ci, bi, h) in enumerate(groups):
        sb = s_scr[bi, h].astype(BF16)
        u_scr[g] = (sol_scr[g, :, 0:d] - _dot(sol_scr[g, :, d:2 * d].astype(BF16), sb)).astype(BF16)
        os_scr[g] = _dot(qg_scr[g], sb)

    for g, (ci, bi, h) in enumerate(groups):
        r0 = ci * c
        ub = u_scr[g]
        o = os_scr[g] + _dot(attn_scr[g], ub)
        g_last = cum_scr[c - 1:c, g * LANES + c:g * LANES + c + 1]
        s_scr[bi, h] = s_scr[bi, h] * jnp.exp(g_last) + _dot(kdt_scr[g], ub)

        zh = z_ref[bi, r0:r0 + c, h * d:(h + 1) * d].astype(F32)
        on = o * lax.rsqrt(jnp.mean(o * o, axis=-1, keepdims=True) + 1e-6) * nw
        o_ref[bi, r0:r0 + c, h * d:(h + 1) * d] = (on * _silu(zh)).astype(BF16)

    @pl.when(l == pl.num_programs(1) - 1)
    def _():
        sn_ref[...] = s_scr[...]


def _gdn(y, z, ab, s0, alog_row, dtb_row, nw_row):
    b, l, cc = y.shape
    c = min(l, CHUNK)
    tl = min(l, GDN_TILE)
    nb = math.gcd(b, GDN_SEQS)
    assert tl == c, "one chunk per grid step: the state recurrence is staged across groups"
    ng = nb * GDN_HEADS
    hw = GDN_HEADS * GDN_D
    tile = lambda i, j: (i, j, 0)
    fix = lambda i, j: (0, 0)
    return pl.pallas_call(
        functools.partial(_gdn_kernel, nb=nb, tl=tl, c=c),
        grid=(b // nb, l // tl),
        in_specs=[pl.BlockSpec((nb, tl, cc), tile), pl.BlockSpec((nb, tl, hw), tile),
                  pl.BlockSpec((nb, tl, LANES), tile),
                  pl.BlockSpec((nb, GDN_HEADS, GDN_D, GDN_D), lambda i, j: (i, 0, 0, 0)),
                  pl.BlockSpec((1, LANES), fix),
                  pl.BlockSpec((1, LANES), fix), pl.BlockSpec((1, GDN_D), fix)],
        out_specs=[pl.BlockSpec((nb, tl, hw), tile),
                   pl.BlockSpec((nb, GDN_HEADS, GDN_D, GDN_D), lambda i, j: (i, 0, 0, 0))],
        out_shape=[jax.ShapeDtypeStruct((b, l, hw), BF16),
                   jax.ShapeDtypeStruct((b, GDN_HEADS, GDN_D, GDN_D), F32)],
        scratch_shapes=[pltpu.VMEM((nb, GDN_HEADS, GDN_D, GDN_D), F32),
                        pltpu.VMEM((c, ng * LANES), F32), pltpu.VMEM((ng, c, c), F32),
                        pltpu.VMEM((ng, c, c), F32), pltpu.VMEM((ng, c, c), BF16),
                        pltpu.VMEM((ng, c, 2 * GDN_D), BF16), pltpu.VMEM((ng, c, 2 * GDN_D), F32),
                        pltpu.VMEM((ng, c, GDN_D), BF16), pltpu.VMEM((ng, GDN_D, c), BF16),
                        pltpu.VMEM((ng, c, GDN_D), BF16), pltpu.VMEM((ng, c, GDN_D), F32)],
        compiler_params=_params("parallel", "arbitrary"),
        name="gdn",
    )(y, z, ab, s0, alog_row, dtb_row, nw_row)


def _lambda(lamv, lam_init):
    l1 = jnp.sum(lamv[0:1] * lamv[1:2], axis=-1, keepdims=True)
    l2 = jnp.sum(lamv[2:3] * lamv[3:4], axis=-1, keepdims=True)
    return jnp.exp(l1) - jnp.exp(l2) + lam_init


def _attn_bias(q0, k0, nq, nk, slope):
    qpos = q0 + lax.broadcasted_iota(jnp.int32, (nq, nk), 0)
    kpos = k0 + lax.broadcasted_iota(jnp.int32, (nq, nk), 1)
    dist = jnp.abs(qpos - kpos).astype(F32)
    shift = int(math.log2(CHUNK))
    allowed = (kpos >> shift) <= (qpos >> shift)
    return jnp.where(allowed, -slope * dist, NEG_INF)


def _attn_init(stats):
    for m, l, a in stats:
        m[...] = jnp.full(m.shape, NEG_INF, F32)
        l[...] = jnp.zeros(l.shape, F32)
        a[...] = jnp.zeros(a.shape, F32)


def _attn_update(q, k, v, bias, stats):
    for i, (m, l, a) in enumerate(stats):
        lo, hi = i * DIFF_DH, (i + 1) * DIFF_DH
        s = _dot_nt(q[:, lo:hi], k[:, lo:hi]) + bias
        m_new = jnp.maximum(m[...], jnp.max(s, axis=-1, keepdims=True))
        alpha = jnp.exp(m[...] - m_new)
        p = jnp.exp(s - m_new)
        l[...] = alpha * l[...] + jnp.sum(p, axis=-1, keepdims=True)
        a[...] = alpha * a[...] + _dot(p.astype(BF16), v)
        m[...] = m_new


def _attn_finish(stats, lam, sw, lam_init):
    (_, l1, a1), (_, l2, a2) = stats
    o = a1[...] / l1[...] - lam * (a2[...] / l2[...])
    return o * lax.rsqrt(jnp.mean(o * o, axis=-1, keepdims=True) + 1e-6) * sw * (1.0 - lam_init)


def _attn_prompt_kernel(slopes_ref, lamv_ref, sw_ref, q_ref, k_ref, v_ref, o_ref,
                        vx_scr, bias_scr, q_scr, m_scr, acc_scr, sa_scr, sb_scr, pa_scr, pb_scr, ala_scr, alb_scr,
                        *, seq, tile, lam_init):
    t = tile
    r = 2 * t
    rb = ATTN_ROWS
    dv = DIFF_DV
    slope = slopes_ref[pl.program_id(1)] * LOG2E
    q_scale = (DIFF_DH ** -0.5) * LOG2E
    lam = _lambda(lamv_ref[...], lam_init)
    sw = sw_ref[...]
    shift = int(math.log2(CHUNK))

    vx_scr[:, 0:dv] = v_ref[0]
    vx_scr[:, dv:2 * dv] = jnp.ones((seq, dv), BF16)
    qi_ = lax.broadcasted_iota(jnp.int32, (t, t), 0)
    kj_ = lax.broadcasted_iota(jnp.int32, (t, t), 1)
    rel = (qi_ - kj_).astype(F32)
    bias_scr[0] = -slope * rel
    bias_scr[1] = jnp.where((kj_ >> shift) <= (qi_ >> shift), -slope * jnp.abs(rel), NEG_INF)
    lane = lax.broadcasted_iota(jnp.int32, (t, dv), 1)

    def q_body(qi, carry):
        q0 = pl.multiple_of(qi * t, t)
        qf = q_ref[0, pl.ds(q0, t), :].astype(F32) * q_scale
        q_scr[0:t, :] = jnp.where(lane < DIFF_DH, qf, 0.0).astype(BF16)
        q_scr[t:r, :] = jnp.where(lane >= DIFF_DH, qf, 0.0).astype(BF16)
        m_scr[...] = jnp.full(m_scr.shape, NEG_INF, F32)
        acc_scr[...] = jnp.zeros(acc_scr.shape, F32)

        def qk(kj, s_buf):
            k0 = pl.multiple_of(jnp.minimum(kj, qi) * t, t)
            s_buf[...] = _dot_nt(q_scr[...], k_ref[0, pl.ds(k0, t), :])

        def softmax_accumulate(kj, s_buf, p_buf, al_buf):
            k0 = pl.multiple_of(kj * t, t)
            diag = jnp.asarray(kj == qi, jnp.int32)
            c = -slope * jnp.asarray(q0 - k0, F32)
            for i in range(r // rb):
                r0 = i * rb
                b_lo = r0 % t
                s = s_buf[r0:r0 + rb, :] + bias_scr[diag, b_lo:b_lo + rb, :]
                cols = [s[:, j * LANES:(j + 1) * LANES] for j in range(t // LANES)]
                m_prev = m_scr[r0:r0 + rb, :]
                m_cur = jnp.max(functools.reduce(jnp.maximum, cols), axis=-1, keepdims=True) + c
                m_new = jnp.maximum(m_prev, m_cur)
                al_buf[r0:r0 + rb, :] = jnp.exp2(m_prev - m_new)
                mm = m_new - c
                for j, col in enumerate(cols):
                    p_buf[r0:r0 + rb, j * LANES:(j + 1) * LANES] = jnp.exp2(col - mm).astype(BF16)
                m_scr[r0:r0 + rb, :] = m_new
            pv = _dot(p_buf[...], vx_scr[pl.ds(k0, t), :])
            for i in range(r // rb):
                r0 = i * rb
                al = al_buf[r0:r0 + rb, :]
                acc_scr[r0:r0 + rb, :] = (jnp.concatenate([al, al], axis=1) * acc_scr[r0:r0 + rb, :]
                                          + pv[r0:r0 + rb, :])

        def step(kj, cur, nxt):
            qk(kj + 1, nxt[0])
            softmax_accumulate(kj, *cur)

        set_a = (sa_scr, pa_scr, ala_scr)
        set_b = (sb_scr, pb_scr, alb_scr)
        qk(0, sa_scr)

        def pair_body(i, c2):
            step(2 * i, set_a, set_b)
            step(2 * i + 1, set_b, set_a)
            return c2

        n_tiles = qi + 1
        lax.fori_loop(0, n_tiles // 2, pair_body, 0)

        @pl.when(n_tiles % 2 == 1)
        def _():
            step(qi, set_a, set_b)

        o1 = acc_scr[0:t, 0:dv] / acc_scr[0:t, dv:2 * dv]
        o2 = acc_scr[t:r, 0:dv] / acc_scr[t:r, dv:2 * dv]
        o = o1 - lam * o2
        on = o * lax.rsqrt(jnp.mean(o * o, axis=-1, keepdims=True) + 1e-6) * sw * (1.0 - lam_init)
        o_ref[0, pl.ds(q0, t), :] = on.astype(BF16)
        return carry

    lax.fori_loop(0, seq // t, q_body, 0)


def _attn_prompt(q, kb, vb, slopes, lamv, sw_row, lam_init):
    b, l, _ = q.shape
    t = min(ATTN_TILE, l)
    assert t % CHUNK == 0 and l % t == 0 and t % LANES == 0 and t % ATTN_ROWS == 0
    head = lambda i, h: (i, 0, h)
    fix = lambda i, h: (0, 0)
    dv = DIFF_DV
    scratch = [pltpu.VMEM((l, 2 * dv), BF16), pltpu.VMEM((2, t, t), F32),
               pltpu.VMEM((2 * t, dv), BF16), pltpu.VMEM((2 * t, LANES), F32),
               pltpu.VMEM((2 * t, 2 * dv), F32),
               pltpu.VMEM((2 * t, t), F32), pltpu.VMEM((2 * t, t), F32),
               pltpu.VMEM((2 * t, t), BF16), pltpu.VMEM((2 * t, t), BF16),
               pltpu.VMEM((2 * t, LANES), F32), pltpu.VMEM((2 * t, LANES), F32)]
    return pl.pallas_call(
        functools.partial(_attn_prompt_kernel, seq=l, tile=t, lam_init=lam_init),
        grid=(b, DIFF_HEADS),
        in_specs=[pl.BlockSpec(memory_space=pltpu.SMEM),
                  pl.BlockSpec(lamv.shape, fix), pl.BlockSpec(sw_row.shape, fix),
                  pl.BlockSpec((1, l, dv), head), pl.BlockSpec((1, l, dv), head),
                  pl.BlockSpec((1, l, dv), head)],
        out_specs=pl.BlockSpec((1, l, dv), head),
        out_shape=jax.ShapeDtypeStruct((b, l, DIFF_HEADS * dv), BF16),
        scratch_shapes=scratch,
        compiler_params=_params("parallel", "parallel"),
        name="attn_prompt",
    )(slopes, lamv, sw_row, q, kb, vb)


def _attn_sample_kernel(slopes_ref, lamv_ref, sw_ref, q_ref, kc_ref, vc_ref, kn_ref, vn_ref, o_ref,
                        m1, l1, a1, m2, l2, a2, *, past, tk, lam_init):
    j = pl.program_id(1)
    n_cache = past // tk
    lq = q_ref.shape[1]
    for h in range(DIFF_HEADS):
        cs = slice(h * DIFF_DV, (h + 1) * DIFF_DV)
        slope = slopes_ref[h]
        stats = ((m1.at[h], l1.at[h], a1.at[h]), (m2.at[h], l2.at[h], a2.at[h]))
        q = (q_ref[0, :, cs].astype(F32) * (DIFF_DH ** -0.5)).astype(BF16)

        @pl.when(j == 0)
        def _(stats=stats):
            _attn_init(stats)

        @pl.when(j < n_cache)
        def _(h=h, q=q, slope=slope, stats=stats):
            k = kc_ref[0, pl.ds(h, tk, stride=DIFF_HEADS), :].astype(BF16)
            v = vc_ref[0, pl.ds(h, tk, stride=DIFF_HEADS), :].astype(BF16)
            _attn_update(q, k, v, _attn_bias(past, j * tk, lq, tk, slope), stats)

        @pl.when(j == n_cache)
        def _(cs=cs, q=q, slope=slope, stats=stats):
            _attn_update(q, kn_ref[0, :, cs], vn_ref[0, :, cs], _attn_bias(past, past, lq, lq, slope), stats)
            lam = _lambda(lamv_ref[...], lam_init)
            o_ref[0, :, cs] = _attn_finish(stats, lam, sw_ref[...], lam_init).astype(BF16)


def _attn_sample(q, kb, vb, k_cache, v_cache, slopes, lamv, sw_row, lam_init):
    b, l, hd = q.shape
    past = k_cache.shape[1] // DIFF_HEADS
    tk = min(CACHE_TILE, past)
    assert past % tk == 0 and past % CHUNK == 0 and l <= CHUNK
    n_cache = past // tk
    per_b = lambda i, j: (i, 0, 0)
    cache = lambda i, j: (i, jnp.minimum(j, n_cache - 1), 0)
    fix = lambda i, j: (0, 0)
    stat = [pltpu.VMEM((DIFF_HEADS, l, 1), F32), pltpu.VMEM((DIFF_HEADS, l, 1), F32),
            pltpu.VMEM((DIFF_HEADS, l, DIFF_DV), F32)]
    return pl.pallas_call(
        functools.partial(_attn_sample_kernel, past=past, tk=tk, lam_init=lam_init),
        grid=(b, n_cache + 1),
        in_specs=[pl.BlockSpec(memory_space=pltpu.SMEM),
                  pl.BlockSpec(lamv.shape, fix), pl.BlockSpec(sw_row.shape, fix),
                  pl.BlockSpec((1, l, hd), per_b),
                  pl.BlockSpec((1, tk * DIFF_HEADS, DIFF_DV), cache),
                  pl.BlockSpec((1, tk * DIFF_HEADS, DIFF_DV), cache),
                  pl.BlockSpec((1, l, hd), per_b), pl.BlockSpec((1, l, hd), per_b)],
        out_specs=pl.BlockSpec((1, l, hd), per_b),
        out_shape=jax.ShapeDtypeStruct((b, l, hd), BF16),
        scratch_shapes=stat + stat,
        compiler_params=_params("parallel", "arbitrary"),
        name="attn_sample",
    )(slopes, lamv, sw_row, q, k_cache, v_cache, kb, vb)


def _mix_kernel(x_ref, og_ref, od_ref, lig_ref, lib_ref, wo_ref, g1_ref, b1_ref, rw_ref, rb_ref,
                x1_ref, x1b_ref, route_ref, *, alpha):
    xn = _layernorm(x_ref[...], lig_ref[...], lib_ref[...])
    hw = og_ref.shape[1]
    mixed = _dot(og_ref[...], wo_ref[0:hw, :]) + _dot(od_ref[...], wo_ref[hw:, :])
    x1 = _layernorm(alpha * xn + mixed, g1_ref[...], b1_ref[...])
    x1_ref[...] = x1
    x1b_ref[...] = x1.astype(BF16)

    x_hi = x1.astype(BF16)
    x_lo = (x1 - x_hi.astype(F32)).astype(BF16)
    lg = (_dot(x_hi, rw_ref[0]) + _dot(x_lo, rw_ref[0]) + _dot(x_hi, rw_ref[1])) + rb_ref[...]
    lane = lax.broadcasted_iota(jnp.int32, lg.shape, 1)
    lanef = lane.astype(F32)
    big = float(LANES)

    gl = jnp.where(lane < N_GROUPS, lg, -jnp.inf)
    gmax = jnp.max(gl, axis=-1, keepdims=True)
    grp = jnp.min(jnp.where(gl == gmax, lanef, big), axis=-1, keepdims=True)
    g_gate = 1.0 / jnp.sum(jnp.where(lane < N_GROUPS, jnp.exp(gl - gmax), 0.0), axis=-1, keepdims=True)

    e_lo = N_GROUPS + grp * EXPERTS_PER_GROUP
    el = jnp.where((lanef >= e_lo) & (lanef < e_lo + EXPERTS_PER_GROUP), lg, -jnp.inf)
    v1 = jnp.max(el, axis=-1, keepdims=True)
    i1 = jnp.min(jnp.where(el == v1, lanef, big), axis=-1, keepdims=True)
    el2 = jnp.where(lanef == i1, -jnp.inf, el)
    v2 = jnp.max(el2, axis=-1, keepdims=True)
    i2 = jnp.min(jnp.where(el2 == v2, lanef, big), axis=-1, keepdims=True)
    e21 = jnp.exp(v2 - v1)
    p1 = 1.0 / (1.0 + e21)
    route_ref[...] = jnp.where(lane == 0, i1 - N_GROUPS,
                     jnp.where(lane == 1, i2 - N_GROUPS,
                     jnp.where(lane == 2, g_gate * p1,
                     jnp.where(lane == 3, g_gate * (e21 * p1), 0.0))))


def _mix(x2, og, od, li_g, li_b, wo, g1, b1, rw, rb, alpha):
    n, d = x2.shape
    tm = min(TOKEN_TILE, n)
    row = lambda i: (i, 0)
    fix = lambda i: (0, 0)
    return pl.pallas_call(
        functools.partial(_mix_kernel, alpha=alpha),
        grid=(n // tm,),
        in_specs=[pl.BlockSpec((tm, d), row), pl.BlockSpec((tm, og.shape[1]), row),
                  pl.BlockSpec((tm, od.shape[1]), row),
                  pl.BlockSpec((1, d), fix), pl.BlockSpec((1, d), fix), pl.BlockSpec(wo.shape, fix),
                  pl.BlockSpec((1, d), fix), pl.BlockSpec((1, d), fix),
                  pl.BlockSpec(rw.shape, lambda i: (0, 0, 0)), pl.BlockSpec((1, LANES), fix)],
        out_specs=[pl.BlockSpec((tm, d), row), pl.BlockSpec((tm, d), row), pl.BlockSpec((tm, LANES), row)],
        out_shape=[jax.ShapeDtypeStruct((n, d), F32), jax.ShapeDtypeStruct((n, d), BF16),
                   jax.ShapeDtypeStruct((n, LANES), F32)],
        compiler_params=_params("parallel"),
        name="mix",
    )(x2, og, od, li_g, li_b, wo, g1, b1, rw, rb)


def _expert_kernel(be_ref, nu_ref, x_ref, w1_ref, w3_ref, w2_ref, y_ref, w1b, w3b, w2b):
    i = pl.program_id(0)

    @pl.when((i == 0) | (be_ref[i] != be_ref[jnp.maximum(i - 1, 0)]))
    def _():
        w1b[...] = w1_ref[0].astype(BF16)
        w3b[...] = w3_ref[0].astype(BF16)
        w2b[...] = w2_ref[0].astype(BF16)

    @pl.when(i < nu_ref[0])
    def _():
        x = x_ref[...]
        h1 = _dot(x, w1b[...])
        h3 = _dot(x, w3b[...])
        y_ref[...] = _dot((_silu(h1) * h3).astype(BF16), w2b[...]).astype(BF16)

    @pl.when(i >= nu_ref[0])
    def _():
        y_ref[...] = jnp.zeros(y_ref.shape, y_ref.dtype)


def _expert_block(n_tokens):
    per_expert = max(1, n_tokens * TOP_K // N_EXPERTS)
    return int(min(EXPERT_BLOCK_MAX, max(EXPERT_BLOCK_MIN, 2 ** int(math.log2(per_expert)))))


def _experts(xb, block_expert, n_used, w1, w3, w2, blk):
    p_pad, d = xb.shape
    de = w1.shape[2]
    row = lambda i, be, nu: (i, 0)
    grid_spec = pltpu.PrefetchScalarGridSpec(
        num_scalar_prefetch=2,
        grid=(p_pad // blk,),
        in_specs=[pl.BlockSpec((blk, d), row),
                  pl.BlockSpec((1, d, de), lambda i, be, nu: (be[i], 0, 0)),
                  pl.BlockSpec((1, d, de), lambda i, be, nu: (be[i], 0, 0)),
                  pl.BlockSpec((1, de, d), lambda i, be, nu: (be[i], 0, 0))],
        out_specs=pl.BlockSpec((blk, d), row),
        scratch_shapes=[pltpu.VMEM((d, de), BF16), pltpu.VMEM((d, de), BF16), pltpu.VMEM((de, d), BF16)],
    )
    return pl.pallas_call(
        _expert_kernel,
        grid_spec=grid_spec,
        out_shape=jax.ShapeDtypeStruct((p_pad, d), BF16),
        compiler_params=_params("arbitrary"),
        name="experts",
    )(block_expert, n_used, xb, w1, w3, w2)


def _dispatch_plan(expert, blk):
    n = expert.shape[0]
    p = n * TOP_K
    n_blocks = -(-p // blk) + N_EXPERTS
    flat_e = expert.reshape(p)
    eids = jnp.arange(N_EXPERTS, dtype=jnp.int32)
    counts = jnp.sum((flat_e[:, None] == eids[None, :]).astype(jnp.int32), axis=0)
    starts = jnp.cumsum(counts) - counts
    padded = (counts + blk - 1) // blk * blk
    padded_end = jnp.cumsum(padded)
    padded_start = padded_end - padded
    order = jnp.argsort(flat_e).astype(jnp.int32)
    rank = jnp.argsort(order).astype(jnp.int32)
    offset = padded_start - starts
    dest = rank + offset[flat_e]
    block_start = jnp.arange(n_blocks, dtype=jnp.int32) * blk
    block_expert = jnp.minimum(
        jnp.sum((padded_end[None, :] <= block_start[:, None]).astype(jnp.int32), axis=1), N_EXPERTS - 1)
    e_row = jnp.repeat(block_expert, blk)
    row_id = jnp.arange(n_blocks * blk, dtype=jnp.int32)
    j = row_id - padded_start[e_row]
    src = jnp.clip(starts[e_row] + j, 0, p - 1)
    tok_pad = jnp.where(j < counts[e_row], order[src] // TOP_K, row_id % n)
    n_used = (padded_end[-1:] // blk).astype(jnp.int32)
    return tok_pad, dest.reshape(n, TOP_K), block_expert, n_used


def _final_kernel(x1_ref, y0_ref, y1_ref, route_ref, g_ref, b_ref, o_ref, *, alpha):
    route = route_ref[...]
    moe = route[:, 2:3] * y0_ref[...].astype(F32) + route[:, 3:4] * y1_ref[...].astype(F32)
    o_ref[...] = _layernorm(alpha * x1_ref[...] + moe, g_ref[...], b_ref[...])


def _final(x1, y0, y1, route, g2, b2, alpha):
    n, d = x1.shape
    tm = min(TOKEN_TILE, n)
    row = lambda i: (i, 0)
    fix = lambda i: (0, 0)
    return pl.pallas_call(
        functools.partial(_final_kernel, alpha=alpha),
        grid=(n // tm,),
        in_specs=[pl.BlockSpec((tm, d), row), pl.BlockSpec((tm, d), row), pl.BlockSpec((tm, d), row),
                  pl.BlockSpec((tm, LANES), row), pl.BlockSpec((1, d), fix), pl.BlockSpec((1, d), fix)],
        out_specs=pl.BlockSpec((tm, d), row),
        out_shape=jax.ShapeDtypeStruct((n, d), F32),
        compiler_params=_params("parallel"),
        name="final",
    )(x1, y0, y1, route, g2, b2)


def _pad_lanes(v, width=LANES):
    v = v.reshape(1, -1).astype(F32)
    return jnp.pad(v, ((0, 0), (0, width - v.shape[1])))


def _layer(x, conv_prev, s_prev, k_cache, v_cache, p, alpha, lam_init):
    b, l, d = x.shape
    n = b * l
    x2 = x.reshape(n, d)
    conv_prev8 = jnp.pad(conv_prev.astype(F32), ((0, 0), (CONV_PAD - (CONV_W - 1), 0), (0, 0)))
    yc, z, q, k, v, kb, vb, ab, convn = _proj(x2, l, p["ln_in_g"], p["ln_in_b"], p["w_main"], p["w_ab"],
                                              conv_prev8, p["conv_w"])
    conv_new = convn[:, CONV_PAD - (CONV_W - 1):, :]
    og, s_new = _gdn(yc.reshape(b, l, -1), z.reshape(b, l, -1), ab.reshape(b, l, LANES),
                     s_prev.astype(F32), p["alog_row"], p["dtb_row"], p["nw_row"])

    q3 = q.reshape(b, l, -1)
    kb3 = kb.reshape(b, l, -1)
    vb3 = vb.reshape(b, l, -1)
    if k_cache is None:
        od = _attn_prompt(q3, kb3, vb3, p["slopes"], p["lamv"], p["sw_row"], lam_init)
    else:
        past = k_cache.shape[1]
        od = _attn_sample(q3, kb3, vb3, k_cache.reshape(b, past * DIFF_HEADS, DIFF_DV),
                          v_cache.reshape(b, past * DIFF_HEADS, DIFF_DV),
                          p["slopes"], p["lamv"], p["sw_row"], lam_init)

    x1, x1b, route = _mix(x2, og.reshape(n, -1), od.reshape(n, -1), p["ln_in_g"], p["ln_in_b"], p["wo"],
                          p["ln1_g"], p["ln1_b"], p["rw"], p["rb"], alpha)

    expert = route[:, 0:TOP_K].astype(jnp.int32)
    blk = _expert_block(n)
    tok_pad, dest, block_expert, n_used = _dispatch_plan(expert, blk)
    xb = jnp.take(x1b, tok_pad, axis=0, mode="clip")
    yb = _experts(xb, block_expert, n_used, p["w1"], p["w3"], p["w2"], blk)
    y0 = jnp.take(yb, dest[:, 0], axis=0, mode="clip")
    y1 = jnp.take(yb, dest[:, 1], axis=0, mode="clip")
    y = _final(x1, y0, y1, route, p["ln2_g"], p["ln2_b"], alpha)

    return (y.reshape(b, l, d), conv_new, s_new,
            k.reshape(b, l, DIFF_HEADS, 2 * DIFF_DH), v.reshape(b, l, DIFF_HEADS, DIFF_DV))


def kernel(x_prompt, x_sample, cache_attn_k, cache_attn_v, state_gdn, state_conv, ln_in_g, ln_in_b, w_in, conv_w, gdn_a_log, gdn_dt_bias, gdn_norm_w, lam_q1, lam_k1, lam_q2, lam_k2, subln_w, w_o, ln1_g, ln1_b, router_g_w, router_g_b, router_e_w, router_e_b, w1, w3, w2, ln2_g, ln2_b):
    depth = w_in.shape[0]
    assert depth == 1, "single-layer step"
    d = x_prompt.shape[-1]
    alpha = (2 * depth) ** 0.25
    lam_init = 0.8 - 0.6 * math.exp(-0.3 * 0)
    row = lambda t: t.reshape(1, -1).astype(F32)

    wi = w_in[0]
    conv_ch = conv_w.shape[-1]
    gw = GDN_HEADS * GDN_D
    c_ab = conv_ch + gw
    c_q = c_ab + 2 * GDN_HEADS
    w_main = jnp.concatenate([wi[:, :c_ab], wi[:, c_q:]], axis=1).astype(BF16)
    w_ab = jnp.pad(wi[:, c_ab:c_q], ((0, 0), (0, LANES - 2 * GDN_HEADS))).astype(BF16)
    rcat = jnp.concatenate([router_g_w[0], router_e_w[0]], axis=1)
    rcat = jnp.pad(rcat, ((0, 0), (0, LANES - rcat.shape[1])))
    r_hi = rcat.astype(BF16)
    r_lo = (rcat - r_hi.astype(F32)).astype(BF16)
    p = {
        "ln_in_g": row(ln_in_g), "ln_in_b": row(ln_in_b), "w_main": w_main, "w_ab": w_ab,
        "conv_w": conv_w[0].astype(F32), "alog_row": _pad_lanes(gdn_a_log[0]),
        "dtb_row": _pad_lanes(gdn_dt_bias[0]), "nw_row": row(gdn_norm_w[0]),
        "slopes": jnp.asarray(2.0 ** (-8.0 * (np.arange(DIFF_HEADS) + 1) / DIFF_HEADS), F32),
        "lamv": jnp.stack([lam_q1[0], lam_k1[0], lam_q2[0], lam_k2[0]]).astype(F32),
        "sw_row": row(subln_w[0]), "wo": w_o[0].astype(BF16),
        "ln1_g": row(ln1_g[0]), "ln1_b": row(ln1_b[0]),
        "rw": jnp.stack([r_hi, r_lo]), "rb": _pad_lanes(jnp.concatenate([router_g_b[0], router_e_b[0]])),
        "w1": w1[0], "w3": w3[0], "w2": w2[0],
        "ln2_g": row(ln2_g[0]), "ln2_b": row(ln2_b[0]),
    }

    bp = x_prompt.shape[0]
    conv0 = jnp.zeros((bp, CONV_W - 1, conv_ch), F32)
    s0 = jnp.zeros((bp, GDN_HEADS, GDN_D, GDN_D), F32)
    yp, cp, sp, kp, vp = _layer(x_prompt, conv0, s0, None, None, p, alpha, lam_init)
    ys, cs, ss, ks, vs = _layer(x_sample, state_conv[0], state_gdn[0], cache_attn_k[0], cache_attn_v[0],
                                p, alpha, lam_init)
    return (yp, ys, kp[None], vp[None], sp[None], cp[None], ks[None], vs[None], ss[None], cs[None])
```

```python
import functools
import math

import jax
import jax.numpy as jnp
import numpy as np
from jax import lax
from jax.experimental import pallas as pl
from jax.experimental.pallas import tpu as pltpu

F32 = jnp.float32
BF16 = jnp.bfloat16

CHUNK = 64
CONV_W = 4
GDN_HEADS = 4
GDN_D = 128
DIFF_HEADS = 4
DIFF_DH = 64
DIFF_DV = 2 * DIFF_DH
N_GROUPS = 4
EXPERTS_PER_GROUP = 8
N_EXPERTS = N_GROUPS * EXPERTS_PER_GROUP
TOP_K = 2
NEG_INF = -1e30
LOG2E = math.log2(math.e)
LANES = 128
CONV_PAD = 8

TOKEN_TILE = 512
GDN_TILE = 64
GDN_SEQS = 4
ATTN_TILE = 256
ATTN_ROWS = 64
CACHE_TILE = 1024
EXPERT_BLOCK_MAX = 512
EXPERT_BLOCK_MIN = 128
VMEM_LIMIT = 48 * 1024 * 1024


def _params(*sem):
    return pltpu.CompilerParams(dimension_semantics=sem, vmem_limit_bytes=VMEM_LIMIT)


def _dot(a, b):
    return jnp.dot(a, b, preferred_element_type=F32)


def _dot_nt(a, b):
    return lax.dot_general(a, b, (((1,), (1,)), ((), ())), preferred_element_type=F32)


def _sigmoid(x):
    return 1.0 / (1.0 + jnp.exp(-x))


def _silu(x):
    return x * _sigmoid(x)


def _softplus(x):
    return jnp.maximum(x, 0.0) + jnp.log(1.0 + jnp.exp(-jnp.abs(x)))


def _layernorm(x, g, b, eps=1e-5):
    mu = jnp.mean(x, axis=-1, keepdims=True)
    xc = x - mu
    var = jnp.mean(xc * xc, axis=-1, keepdims=True)
    return xc * lax.rsqrt(var + eps) * g + b


def _split3(x):
    hi = x.astype(BF16)
    r1 = x - hi.astype(F32)
    mid = r1.astype(BF16)
    lo = (r1 - mid.astype(F32)).astype(BF16)
    return hi, mid, lo


def _proj_kernel(x_ref, g_ref, b_ref, w_ref, wab_ref, convp_ref, cw_ref,
                 y_ref, z_ref, q_ref, k_ref, v_ref, kb_ref, vb_ref, ab_ref, convn_ref,
                 xc_scr, *, rows, n_seq, tiles_per_seq):
    xb = _layernorm(x_ref[...], g_ref[...], b_ref[...]).astype(BF16)
    w = 512
    stride = rows + CONV_PAD
    if tiles_per_seq > 1:
        first = pl.program_id(0) % tiles_per_seq == 0

        @pl.when(first)
        def _():
            xc_scr[0:CONV_PAD, :] = convp_ref[0]

        @pl.when(jnp.logical_not(first))
        def _():
            xc_scr[0:CONV_PAD, :] = xc_scr[rows:rows + CONV_PAD, :]
    else:
        for s in range(n_seq):
            xc_scr[s * stride:s * stride + CONV_PAD, :] = convp_ref[s]
    cw = cw_ref[...]

    for j in range(3):
        cs = slice(j * w, (j + 1) * w)
        qkv = _dot(xb, w_ref[:, cs])
        for s in range(n_seq):
            base = s * stride + CONV_PAD
            xc_scr[base:base + rows, cs] = qkv[s * rows:(s + 1) * rows, :]
            lo = base - (CONV_W - 1)
            y = cw[0:1, cs] * xc_scr[lo:lo + rows, cs]
            for t in range(1, CONV_W):
                y = y + cw[t:t + 1, cs] * xc_scr[lo + t:lo + t + rows, cs]
            hy = 0.5 * y
            y_ref[s * rows:(s + 1) * rows, cs] = (hy + hy * jnp.tanh(hy)).astype(BF16)
    for s in range(n_seq):
        convn_ref[s] = xc_scr[s * stride + rows:s * stride + rows + CONV_PAD, :]
    z_ref[...] = _dot(xb, w_ref[:, 3 * w:4 * w]).astype(BF16)
    q_ref[...] = _dot(xb, w_ref[:, 4 * w:5 * w]).astype(BF16)
    tm = x_ref.shape[0]
    k = _dot(xb, w_ref[:, 5 * w:6 * w])
    kb_ref[...] = k.astype(BF16)
    v = _dot(xb, w_ref[:, 6 * w:7 * w])
    vb_ref[...] = v.astype(BF16)
    for h in range(DIFF_HEADS):
        k_ref[pl.ds(h, tm, stride=DIFF_HEADS), :] = k[:, h * DIFF_DV:(h + 1) * DIFF_DV]
        v_ref[pl.ds(h, tm, stride=DIFF_HEADS), :] = v[:, h * DIFF_DV:(h + 1) * DIFF_DV]
    ab_ref[...] = _dot(xb, wab_ref[...])


def _proj(x2, seq_len, ln_g, ln_b, w_main, w_ab, conv_prev8, conv_w):
    n, d = x2.shape
    tm = min(TOKEN_TILE, n)
    cc = conv_w.shape[1]
    rows = min(seq_len, tm)
    n_seq = tm // rows
    tiles_per_seq = seq_len // rows
    assert tm % rows == 0 and seq_len % rows == 0 and rows % CONV_PAD == 0
    row = lambda i: (i, 0)
    fix = lambda i: (0, 0)
    seq = lambda i: (i // tiles_per_seq, 0, 0)
    hd = DIFF_HEADS * DIFF_DV
    outs = [(1, cc, BF16), (1, hd, BF16), (1, hd, BF16), (DIFF_HEADS, DIFF_DV, F32),
            (DIFF_HEADS, DIFF_DV, F32), (1, hd, BF16), (1, hd, BF16), (1, LANES, F32)]
    return pl.pallas_call(
        functools.partial(_proj_kernel, rows=rows, n_seq=n_seq, tiles_per_seq=tiles_per_seq),
        grid=(n // tm,),
        in_specs=[pl.BlockSpec((tm, d), row), pl.BlockSpec((1, d), fix), pl.BlockSpec((1, d), fix),
                  pl.BlockSpec(w_main.shape, fix), pl.BlockSpec(w_ab.shape, fix),
                  pl.BlockSpec((n_seq, CONV_PAD, cc), seq), pl.BlockSpec((CONV_W, cc), fix)],
        out_specs=[pl.BlockSpec((tm * r, c), row) for r, c, _ in outs]
                  + [pl.BlockSpec((n_seq, CONV_PAD, cc), seq)],
        out_shape=[jax.ShapeDtypeStruct((n * r, c), t) for r, c, t in outs]
                  + [jax.ShapeDtypeStruct(conv_prev8.shape, F32)],
        scratch_shapes=[pltpu.VMEM((n_seq * (rows + CONV_PAD), cc), F32)],
        compiler_params=_params("arbitrary"),
        name="proj",
    )(x2, ln_g, ln_b, w_main, w_ab, conv_prev8, conv_w)


def _gdn_kernel(y_ref, z_ref, ab_ref, s0_ref, alog_ref, dtb_ref, nw_ref,
                o_ref, sn_ref,
                s_scr, cum_scr, a_scr, p_scr, attn_scr, rhs_scr, sol_scr, qg_scr, kdt_scr, u_scr, os_scr,
                *, nb, tl, c):
    l = pl.program_id(1)
    d = GDN_D
    hw = GDN_HEADS * d

    @pl.when(l == 0)
    def _():
        s_scr[...] = s0_ref[...]

    g_all = [-jnp.exp(alog_ref[...]) * _softplus(ab_ref[bi] + dtb_ref[...]) for bi in range(nb)]
    beta_all = [_sigmoid(ab_ref[bi]) for bi in range(nb)]
    nw = nw_ref[...]

    rows = lax.broadcasted_iota(jnp.int32, (c, c), 0)
    cols = lax.broadcasted_iota(jnp.int32, (c, c), 1)
    incl = cols <= rows
    strict = cols < rows
    tri = jnp.where(incl, 1.0, 0.0).astype(BF16)
    grow = lax.broadcasted_iota(jnp.int32, (c, LANES), 0)
    glane = lax.broadcasted_iota(jnp.int32, (c, LANES), 1)

    n_chunks = tl // c
    groups = [(ci, bi, h) for ci in range(n_chunks) for bi in range(nb) for h in range(GDN_HEADS)]
    eye = jnp.where(rows == cols, 1.0, 0.0)

    gmat = jnp.concatenate(
        [jnp.where(glane < c, jnp.where(grow > glane, g_all[bi][ci * c:(ci + 1) * c, h:h + 1], 0.0),
                   g_all[bi][ci * c:(ci + 1) * c, h:h + 1]) for ci, bi, h in groups], axis=1)
    g_hi, g_mid, g_lo = _split3(gmat)
    cum_scr[...] = _dot(tri, g_hi) + _dot(tri, g_mid) + _dot(tri, g_lo)

    for g, (ci, bi, h) in enumerate(groups):
        r0 = ci * c
        beta = beta_all[bi][r0:r0 + c, GDN_HEADS + h:GDN_HEADS + h + 1]
        rel = cum_scr[:, g * LANES:g * LANES + c]
        gam = cum_scr[:, g * LANES + c:g * LANES + c + 1]
        g_last = gam[c - 1:c, :]
        decay = jnp.where(incl, jnp.exp(jnp.where(incl, rel, 0.0)), 0.0)
        eg = jnp.exp(gam)
        ek = jnp.exp(g_last - gam)

        qh = y_ref[bi, r0:r0 + c, h * d:(h + 1) * d].astype(F32)
        kh = y_ref[bi, r0:r0 + c, hw + h * d:hw + (h + 1) * d].astype(F32)
        vh = y_ref[bi, r0:r0 + c, 2 * hw + h * d:2 * hw + (h + 1) * d].astype(F32)
        qn = qh * lax.rsqrt(jnp.sum(qh * qh, axis=-1, keepdims=True) + 1e-6) * (d ** -0.5)
        kn = kh * lax.rsqrt(jnp.sum(kh * kh, axis=-1, keepdims=True) + 1e-6)
        qb = qn.astype(BF16)
        kb = kn.astype(BF16)
        a = jnp.where(strict, beta * _dot_nt(kb, kb) * decay, 0.0)
        a_scr[g] = a
        p_scr[g] = eye - a
        attn_scr[g] = (_dot_nt(qb, kb) * decay).astype(BF16)
        rhs_scr[g, :, 0:d] = (vh * beta).astype(BF16)
        rhs_scr[g, :, d:2 * d] = (kn * (beta * eg)).astype(BF16)
        qg_scr[g] = (qn * eg).astype(BF16)
        kdt_scr[g] = jnp.transpose(kn * ek).astype(BF16)

    n_sq = int(math.log2(c)) - 1
    for it in range(n_sq):
        for g in range(len(groups)):
            akb = a_scr[g].astype(BF16)
            a_scr[g] = _dot(akb, akb)
        for g in range(len(groups)):
            p = p_scr[g]
            p_scr[g] = p + _dot(p.astype(BF16), a_scr[g].astype(BF16))

    for g in range(len(groups)):
        sol_scr[g] = _dot(p_scr[g].astype(BF16), rhs_scr[g])

    for g, (ci, bi, h) in enumerate(groups):
        sb = s_scr[bi, h].astype(BF16)
        u_scr[g] = (sol_scr[g, :, 0:d] - _dot(sol_scr[g, :, d:2 * d].astype(BF16), sb)).astype(BF16)
        os_scr[g] = _dot(qg_scr[g], sb)

    for g, (ci, bi, h) in enumerate(groups):
        r0 = ci * c
        ub = u_scr[g]
        o = os_scr[g] + _dot(attn_scr[g], ub)
        g_last = cum_scr[c - 1:c, g * LANES + c:g * LANES + c + 1]
        s_scr[bi, h] = s_scr[bi, h] * jnp.exp(g_last) + _dot(kdt_scr[g], ub)

        zh = z_ref[bi, r0:r0 + c, h * d:(h + 1) * d].astype(F32)
        on = o * lax.rsqrt(jnp.mean(o * o, axis=-1, keepdims=True) + 1e-6) * nw
        o_ref[bi, r0:r0 + c, h * d:(h + 1) * d] = (on * _silu(zh)).astype(BF16)

    @pl.when(l == pl.num_programs(1) - 1)
    def _():
        sn_ref[...] = s_scr[...]


def _gdn(y, z, ab, s0, alog_row, dtb_row, nw_row):
    b, l, cc = y.shape
    c = min(l, CHUNK)
    tl = min(l, GDN_TILE)
    nb = math.gcd(b, GDN_SEQS)
    assert tl == c, "one chunk per grid step: the state recurrence is staged across groups"
    ng = nb * GDN_HEADS
    hw = GDN_HEADS * GDN_D
    tile = lambda i, j: (i, j, 0)
    fix = lambda i, j: (0, 0)
    return pl.pallas_call(
        functools.partial(_gdn_kernel, nb=nb, tl=tl, c=c),
        grid=(b // nb, l // tl),
        in_specs=[pl.BlockSpec((nb, tl, cc), tile), pl.BlockSpec((nb, tl, hw), tile),
                  pl.BlockSpec((nb, tl, LANES), tile),
                  pl.BlockSpec((nb, GDN_HEADS, GDN_D, GDN_D), lambda i, j: (i, 0, 0, 0)),
                  pl.BlockSpec((1, LANES), fix),
                  pl.BlockSpec((1, LANES), fix), pl.BlockSpec((1, GDN_D), fix)],
        out_specs=[pl.BlockSpec((nb, tl, hw), tile),
                   pl.BlockSpec((nb, GDN_HEADS, GDN_D, GDN_D), lambda i, j: (i, 0, 0, 0))],
        out_shape=[jax.ShapeDtypeStruct((b, l, hw), BF16),
                   jax.ShapeDtypeStruct((b, GDN_HEADS, GDN_D, GDN_D), F32)],
        scratch_shapes=[pltpu.VMEM((nb, GDN_HEADS, GDN_D, GDN_D), F32),
                        pltpu.VMEM((c, ng * LANES), F32), pltpu.VMEM((ng, c, c), F32),
                        pltpu.VMEM((ng, c, c), F32), pltpu.VMEM((ng, c, c), BF16),
                        pltpu.VMEM((ng, c, 2 * GDN_D), BF16), pltpu.VMEM((ng, c, 2 * GDN_D), F32),
                        pltpu.VMEM((ng, c, GDN_D), BF16), pltpu.VMEM((ng, GDN_D, c), BF16),
                        pltpu.VMEM((ng, c, GDN_D), BF16), pltpu.VMEM((ng, c, GDN_D), F32)],
        compiler_params=_params("parallel", "arbitrary"),
        name="gdn",
    )(y, z, ab, s0, alog_row, dtb_row, nw_row)


def _lambda(lamv, lam_init):
    l1 = jnp.sum(lamv[0:1] * lamv[1:2], axis=-1, keepdims=True)
    l2 = jnp.sum(lamv[2:3] * lamv[3:4], axis=-1, keepdims=True)
    return jnp.exp(l1) - jnp.exp(l2) + lam_init


def _attn_bias(q0, k0, nq, nk, slope):
    qpos = q0 + lax.broadcasted_iota(jnp.int32, (nq, nk), 0)
    kpos = k0 + lax.broadcasted_iota(jnp.int32, (nq, nk), 1)
    dist = jnp.abs(qpos - kpos).astype(F32)
    shift = int(math.log2(CHUNK))
    allowed = (kpos >> shift) <= (qpos >> shift)
    return jnp.where(allowed, -slope * dist, NEG_INF)


def _attn_init(stats):
    for m, l, a in stats:
        m[...] = jnp.full(m.shape, NEG_INF, F32)
        l[...] = jnp.zeros(l.shape, F32)
        a[...] = jnp.zeros(a.shape, F32)


def _attn_update(q, k, v, bias, stats):
    for i, (m, l, a) in enumerate(stats):
        lo, hi = i * DIFF_DH, (i + 1) * DIFF_DH
        s = _dot_nt(q[:, lo:hi], k[:, lo:hi]) + bias
        m_new = jnp.maximum(m[...], jnp.max(s, axis=-1, keepdims=True))
        alpha = jnp.exp(m[...] - m_new)
        p = jnp.exp(s - m_new)
        l[...] = alpha * l[...] + jnp.sum(p, axis=-1, keepdims=True)
        a[...] = alpha * a[...] + _dot(p.astype(BF16), v)
        m[...] = m_new


def _attn_finish(stats, lam, sw, lam_init):
    (_, l1, a1), (_, l2, a2) = stats
    o = a1[...] / l1[...] - lam * (a2[...] / l2[...])
    return o * lax.rsqrt(jnp.mean(o * o, axis=-1, keepdims=True) + 1e-6) * sw * (1.0 - lam_init)


def _attn_prompt_kernel(slopes_ref, lamv_ref, sw_ref, q_ref, k_ref, v_ref, o_ref,
                        vx_scr, bias_scr, q_scr, m_scr, acc_scr, sa_scr, sb_scr, pa_scr, pb_scr, ala_scr, alb_scr,
                        *, seq, tile, lam_init):
    t = tile
    r = 2 * t
    rb = ATTN_ROWS
    dv = DIFF_DV
    slope = slopes_ref[pl.program_id(1)] * LOG2E
    q_scale = (DIFF_DH ** -0.5) * LOG2E
    lam = _lambda(lamv_ref[...], lam_init)
    sw = sw_ref[...]
    shift = int(math.log2(CHUNK))

    vx_scr[:, 0:dv] = v_ref[0]
    vx_scr[:, dv:2 * dv] = jnp.ones((seq, dv), BF16)
    qi_ = lax.broadcasted_iota(jnp.int32, (t, t), 0)
    kj_ = lax.broadcasted_iota(jnp.int32, (t, t), 1)
    rel = (qi_ - kj_).astype(F32)
    bias_scr[0] = -slope * rel
    bias_scr[1] = jnp.where((kj_ >> shift) <= (qi_ >> shift), -slope * jnp.abs(rel), NEG_INF)
    lane = lax.broadcasted_iota(jnp.int32, (t, dv), 1)
    frame_shift = slope * t

    def finalize(qi):
        q0 = qi * t
        o1 = acc_scr[0:t, 0:dv] / acc_scr[0:t, dv:2 * dv]
        o2 = acc_scr[t:r, 0:dv] / acc_scr[t:r, dv:2 * dv]
        o = o1 - lam * o2
        on = o * lax.rsqrt(jnp.mean(o * o, axis=-1, keepdims=True) + 1e-6) * sw * (1.0 - lam_init)
        o_ref[0, q0:q0 + t, :] = on.astype(BF16)

    def q_body(qi):
        q0 = qi * t
        qf = q_ref[0, q0:q0 + t, :].astype(F32) * q_scale
        q_scr[0:t, :] = jnp.where(lane < DIFF_DH, qf, 0.0).astype(BF16)
        q_scr[t:r, :] = jnp.where(lane >= DIFF_DH, qf, 0.0).astype(BF16)
        m_scr[...] = jnp.full(m_scr.shape, NEG_INF, F32)
        if qi > 0:
            finalize(qi - 1)
        acc_scr[...] = jnp.zeros(acc_scr.shape, F32)

        def qk(kj, s_buf):
            k0 = min(kj, qi) * t
            s_buf[...] = _dot_nt(q_scr[...], k_ref[0, k0:k0 + t, :])

        def softmax_accumulate(kj, s_buf, p_buf, al_buf):
            k0 = kj * t
            diag = int(kj == qi)
            for i in range(r // rb):
                r0 = i * rb
                b_lo = r0 % t
                s = s_buf[r0:r0 + rb, :] + bias_scr[diag, b_lo:b_lo + rb, :]
                cols = [s[:, j * LANES:(j + 1) * LANES] for j in range(t // LANES)]
                m_prev = m_scr[r0:r0 + rb, :] - frame_shift
                m_cur = jnp.max(functools.reduce(jnp.maximum, cols), axis=-1, keepdims=True)
                m_new = jnp.maximum(m_prev, m_cur)
                al_buf[r0:r0 + rb, :] = jnp.exp2(m_prev - m_new)
                for j, col in enumerate(cols):
                    p_buf[r0:r0 + rb, j * LANES:(j + 1) * LANES] = jnp.exp2(col - m_new).astype(BF16)
                m_scr[r0:r0 + rb, :] = m_new
            pv = _dot(p_buf[...], vx_scr[k0:k0 + t, :])
            for i in range(r // rb):
                r0 = i * rb
                al = al_buf[r0:r0 + rb, :]
                acc_scr[r0:r0 + rb, :] = (jnp.concatenate([al, al], axis=1) * acc_scr[r0:r0 + rb, :]
                                          + pv[r0:r0 + rb, :])

        sets = ((sa_scr, pa_scr, ala_scr), (sb_scr, pb_scr, alb_scr))
        qk(0, sa_scr)
        for kj in range(qi + 1):
            qk(kj + 1, sets[(kj + 1) % 2][0])
            softmax_accumulate(kj, *sets[kj % 2])

    for qi in range(seq // t):
        q_body(qi)
    finalize(seq // t - 1)


def _attn_prompt(q, kb, vb, slopes, lamv, sw_row, lam_init):
    b, l, _ = q.shape
    t = min(ATTN_TILE, l)
    assert t % CHUNK == 0 and l % t == 0 and t % LANES == 0 and t % ATTN_ROWS == 0
    head = lambda i, h: (i, 0, h)
    fix = lambda i, h: (0, 0)
    dv = DIFF_DV
    scratch = [pltpu.VMEM((l, 2 * dv), BF16), pltpu.VMEM((2, t, t), F32),
               pltpu.VMEM((2 * t, dv), BF16), pltpu.VMEM((2 * t, LANES), F32),
               pltpu.VMEM((2 * t, 2 * dv), F32),
               pltpu.VMEM((2 * t, t), F32), pltpu.VMEM((2 * t, t), F32),
               pltpu.VMEM((2 * t, t), BF16), pltpu.VMEM((2 * t, t), BF16),
               pltpu.VMEM((2 * t, LANES), F32), pltpu.VMEM((2 * t, LANES), F32)]
    return pl.pallas_call(
        functools.partial(_attn_prompt_kernel, seq=l, tile=t, lam_init=lam_init),
        grid=(b, DIFF_HEADS),
        in_specs=[pl.BlockSpec(memory_space=pltpu.SMEM),
                  pl.BlockSpec(lamv.shape, fix), pl.BlockSpec(sw_row.shape, fix),
                  pl.BlockSpec((1, l, dv), head), pl.BlockSpec((1, l, dv), head),
                  pl.BlockSpec((1, l, dv), head)],
        out_specs=pl.BlockSpec((1, l, dv), head),
        out_shape=jax.ShapeDtypeStruct((b, l, DIFF_HEADS * dv), BF16),
        scratch_shapes=scratch,
        compiler_params=_params("parallel", "parallel"),
        name="attn_prompt",
    )(slopes, lamv, sw_row, q, kb, vb)


def _attn_sample_kernel(slopes_ref, lamv_ref, sw_ref, q_ref, kc_ref, vc_ref, kn_ref, vn_ref, o_ref,
                        m1, l1, a1, m2, l2, a2, *, past, tk, lam_init):
    j = pl.program_id(1)
    n_cache = past // tk
    lq = q_ref.shape[1]
    for h in range(DIFF_HEADS):
        cs = slice(h * DIFF_DV, (h + 1) * DIFF_DV)
        slope = slopes_ref[h]
        stats = ((m1.at[h], l1.at[h], a1.at[h]), (m2.at[h], l2.at[h], a2.at[h]))
        q = (q_ref[0, :, cs].astype(F32) * (DIFF_DH ** -0.5)).astype(BF16)

        @pl.when(j == 0)
        def _(stats=stats):
            _attn_init(stats)

        @pl.when(j < n_cache)
        def _(h=h, q=q, slope=slope, stats=stats):
            k = kc_ref[0, pl.ds(h, tk, stride=DIFF_HEADS), :].astype(BF16)
            v = vc_ref[0, pl.ds(h, tk, stride=DIFF_HEADS), :].astype(BF16)
            _attn_update(q, k, v, _attn_bias(past, j * tk, lq, tk, slope), stats)

        @pl.when(j == n_cache)
        def _(cs=cs, q=q, slope=slope, stats=stats):
            _attn_update(q, kn_ref[0, :, cs], vn_ref[0, :, cs], _attn_bias(past, past, lq, lq, slope), stats)
            lam = _lambda(lamv_ref[...], lam_init)
            o_ref[0, :, cs] = _attn_finish(stats, lam, sw_ref[...], lam_init).astype(BF16)


def _attn_sample(q, kb, vb, k_cache, v_cache, slopes, lamv, sw_row, lam_init):
    b, l, hd = q.shape
    past = k_cache.shape[1] // DIFF_HEADS
    tk = min(CACHE_TILE, past)
    assert past % tk == 0 and past % CHUNK == 0 and l <= CHUNK
    n_cache = past // tk
    per_b = lambda i, j: (i, 0, 0)
    cache = lambda i, j: (i, jnp.minimum(j, n_cache - 1), 0)
    fix = lambda i, j: (0, 0)
    stat = [pltpu.VMEM((DIFF_HEADS, l, 1), F32), pltpu.VMEM((DIFF_HEADS, l, 1), F32),
            pltpu.VMEM((DIFF_HEADS, l, DIFF_DV), F32)]
    return pl.pallas_call(
        functools.partial(_attn_sample_kernel, past=past, tk=tk, lam_init=lam_init),
        grid=(b, n_cache + 1),
        in_specs=[pl.BlockSpec(memory_space=pltpu.SMEM),
                  pl.BlockSpec(lamv.shape, fix), pl.BlockSpec(sw_row.shape, fix),
                  pl.BlockSpec((1, l, hd), per_b),
                  pl.BlockSpec((1, tk * DIFF_HEADS, DIFF_DV), cache),
                  pl.BlockSpec((1, tk * DIFF_HEADS, DIFF_DV), cache),
                  pl.BlockSpec((1, l, hd), per_b), pl.BlockSpec((1, l, hd), per_b)],
        out_specs=pl.BlockSpec((1, l, hd), per_b),
        out_shape=jax.ShapeDtypeStruct((b, l, hd), BF16),
        scratch_shapes=stat + stat,
        compiler_params=_params("parallel", "arbitrary"),
        name="attn_sample",
    )(slopes, lamv, sw_row, q, k_cache, v_cache, kb, vb)


def _mix_kernel(x_ref, og_ref, od_ref, lig_ref, lib_ref, wo_ref, g1_ref, b1_ref, rw_ref, rb_ref,
                x1_ref, x1b_ref, route_ref, *, alpha):
    xn = _layernorm(x_ref[...], lig_ref[...], lib_ref[...])
    hw = og_ref.shape[1]
    mixed = _dot(og_ref[...], wo_ref[0:hw, :]) + _dot(od_ref[...], wo_ref[hw:, :])
    x1 = _layernorm(alpha * xn + mixed, g1_ref[...], b1_ref[...])
    x1_ref[...] = x1
    x1b_ref[...] = x1.astype(BF16)

    x_hi = x1.astype(BF16)
    x_lo = (x1 - x_hi.astype(F32)).astype(BF16)
    lg = (_dot(x_hi, rw_ref[0]) + _dot(x_lo, rw_ref[0]) + _dot(x_hi, rw_ref[1])) + rb_ref[...]
    lane = lax.broadcasted_iota(jnp.int32, lg.shape, 1)
    lanef = lane.astype(F32)
    big = float(LANES)

    gl = jnp.where(lane < N_GROUPS, lg, -jnp.inf)
    gmax = jnp.max(gl, axis=-1, keepdims=True)
    grp = jnp.min(jnp.where(gl == gmax, lanef, big), axis=-1, keepdims=True)
    g_gate = 1.0 / jnp.sum(jnp.where(lane < N_GROUPS, jnp.exp(gl - gmax), 0.0), axis=-1, keepdims=True)

    e_lo = N_GROUPS + grp * EXPERTS_PER_GROUP
    el = jnp.where((lanef >= e_lo) & (lanef < e_lo + EXPERTS_PER_GROUP), lg, -jnp.inf)
    v1 = jnp.max(el, axis=-1, keepdims=True)
    i1 = jnp.min(jnp.where(el == v1, lanef, big), axis=-1, keepdims=True)
    el2 = jnp.where(lanef == i1, -jnp.inf, el)
    v2 = jnp.max(el2, axis=-1, keepdims=True)
    i2 = jnp.min(jnp.where(el2 == v2, lanef, big), axis=-1, keepdims=True)
    e21 = jnp.exp(v2 - v1)
    p1 = 1.0 / (1.0 + e21)
    route_ref[...] = jnp.where(lane == 0, i1 - N_GROUPS,
                     jnp.where(lane == 1, i2 - N_GROUPS,
                     jnp.where(lane == 2, g_gate * p1,
                     jnp.where(lane == 3, g_gate * (e21 * p1), 0.0))))


def _mix(x2, og, od, li_g, li_b, wo, g1, b1, rw, rb, alpha):
    n, d = x2.shape
    tm = min(TOKEN_TILE, n)
    row = lambda i: (i, 0)
    fix = lambda i: (0, 0)
    return pl.pallas_call(
        functools.partial(_mix_kernel, alpha=alpha),
        grid=(n // tm,),
        in_specs=[pl.BlockSpec((tm, d), row), pl.BlockSpec((tm, og.shape[1]), row),
                  pl.BlockSpec((tm, od.shape[1]), row),
                  pl.BlockSpec((1, d), fix), pl.BlockSpec((1, d), fix), pl.BlockSpec(wo.shape, fix),
                  pl.BlockSpec((1, d), fix), pl.BlockSpec((1, d), fix),
                  pl.BlockSpec(rw.shape, lambda i: (0, 0, 0)), pl.BlockSpec((1, LANES), fix)],
        out_specs=[pl.BlockSpec((tm, d), row), pl.BlockSpec((tm, d), row), pl.BlockSpec((tm, LANES), row)],
        out_shape=[jax.ShapeDtypeStruct((n, d), F32), jax.ShapeDtypeStruct((n, d), BF16),
                   jax.ShapeDtypeStruct((n, LANES), F32)],
        compiler_params=_params("parallel"),
        name="mix",
    )(x2, og, od, li_g, li_b, wo, g1, b1, rw, rb)


def _expert_kernel(be_ref, nu_ref, x_ref, w1_ref, w3_ref, w2_ref, y_ref, w1b, w3b, w2b):
    i = pl.program_id(0)

    @pl.when((i == 0) | (be_ref[i] != be_ref[jnp.maximum(i - 1, 0)]))
    def _():
        w1b[...] = w1_ref[0].astype(BF16)
        w3b[...] = w3_ref[0].astype(BF16)
        w2b[...] = w2_ref[0].astype(BF16)

    @pl.when(i < nu_ref[0])
    def _():
        x = x_ref[...]
        h1 = _dot(x, w1b[...])
        h3 = _dot(x, w3b[...])
        y_ref[...] = _dot((_silu(h1) * h3).astype(BF16), w2b[...]).astype(BF16)

    @pl.when(i >= nu_ref[0])
    def _():
        y_ref[...] = jnp.zeros(y_ref.shape, y_ref.dtype)


def _expert_block(n_tokens):
    per_expert = max(1, n_tokens * TOP_K // N_EXPERTS)
    return int(min(EXPERT_BLOCK_MAX, max(EXPERT_BLOCK_MIN, 2 ** int(math.log2(per_expert)))))


def _experts(xb, block_expert, n_used, w1, w3, w2, blk):
    p_pad, d = xb.shape
    de = w1.shape[2]
    row = lambda i, be, nu: (i, 0)
    grid_spec = pltpu.PrefetchScalarGridSpec(
        num_scalar_prefetch=2,
        grid=(p_pad // blk,),
        in_specs=[pl.BlockSpec((blk, d), row),
                  pl.BlockSpec((1, d, de), lambda i, be, nu: (be[i], 0, 0)),
                  pl.BlockSpec((1, d, de), lambda i, be, nu: (be[i], 0, 0)),
                  pl.BlockSpec((1, de, d), lambda i, be, nu: (be[i], 0, 0))],
        out_specs=pl.BlockSpec((blk, d), row),
        scratch_shapes=[pltpu.VMEM((d, de), BF16), pltpu.VMEM((d, de), BF16), pltpu.VMEM((de, d), BF16)],
    )
    return pl.pallas_call(
        _expert_kernel,
        grid_spec=grid_spec,
        out_shape=jax.ShapeDtypeStruct((p_pad, d), BF16),
        compiler_params=_params("arbitrary"),
        name="experts",
    )(block_expert, n_used, xb, w1, w3, w2)


def _dispatch_plan(expert, blk):
    n = expert.shape[0]
    p = n * TOP_K
    n_blocks = -(-p // blk) + N_EXPERTS
    flat_e = expert.reshape(p)
    eids = jnp.arange(N_EXPERTS, dtype=jnp.int32)
    counts = jnp.sum((flat_e[:, None] == eids[None, :]).astype(jnp.int32), axis=0)
    starts = jnp.cumsum(counts) - counts
    padded = (counts + blk - 1) // blk * blk
    padded_end = jnp.cumsum(padded)
    padded_start = padded_end - padded
    order = jnp.argsort(flat_e).astype(jnp.int32)
    rank = jnp.argsort(order).astype(jnp.int32)
    offset = padded_start - starts
    dest = rank + offset[flat_e]
    block_start = jnp.arange(n_blocks, dtype=jnp.int32) * blk
    block_expert = jnp.minimum(
        jnp.sum((padded_end[None, :] <= block_start[:, None]).astype(jnp.int32), axis=1), N_EXPERTS - 1)
    e_row = jnp.repeat(block_expert, blk)
    row_id = jnp.arange(n_blocks * blk, dtype=jnp.int32)
    j = row_id - padded_start[e_row]
    src = jnp.clip(starts[e_row] + j, 0, p - 1)
    tok_pad = jnp.where(j < counts[e_row], order[src] // TOP_K, row_id % n)
    n_used = (padded_end[-1:] // blk).astype(jnp.int32)
    return tok_pad, dest.reshape(n, TOP_K), block_expert, n_used


def _final_kernel(x1_ref, y0_ref, y1_ref, route_ref, g_ref, b_ref, o_ref, *, alpha):
    route = route_ref[...]
    moe = route[:, 2:3] * y0_ref[...].astype(F32) + route[:, 3:4] * y1_ref[...].astype(F32)
    o_ref[...] = _layernorm(alpha * x1_ref[...] + moe, g_ref[...], b_ref[...])


def _final(x1, y0, y1, route, g2, b2, alpha):
    n, d = x1.shape
    tm = min(TOKEN_TILE, n)
    row = lambda i: (i, 0)
    fix = lambda i: (0, 0)
    return pl.pallas_call(
        functools.partial(_final_kernel, alpha=alpha),
        grid=(n // tm,),
        in_specs=[pl.BlockSpec((tm, d), row), pl.BlockSpec((tm, d), row), pl.BlockSpec((tm, d), row),
                  pl.BlockSpec((tm, LANES), row), pl.BlockSpec((1, d), fix), pl.BlockSpec((1, d), fix)],
        out_specs=pl.BlockSpec((tm, d), row),
        out_shape=jax.ShapeDtypeStruct((n, d), F32),
        compiler_params=_params("parallel"),
        name="final",
    )(x1, y0, y1, route, g2, b2)


def _pad_lanes(v, width=LANES):
    v = v.reshape(1, -1).astype(F32)
    return jnp.pad(v, ((0, 0), (0, width - v.shape[1])))


def _layer(x, conv_prev, s_prev, k_cache, v_cache, p, alpha, lam_init):
    b, l, d = x.shape
    n = b * l
    x2 = x.reshape(n, d)
    conv_prev8 = jnp.pad(conv_prev.astype(F32), ((0, 0), (CONV_PAD - (CONV_W - 1), 0), (0, 0)))
    yc, z, q, k, v, kb, vb, ab, convn = _proj(x2, l, p["ln_in_g"], p["ln_in_b"], p["w_main"], p["w_ab"],
                                              conv_prev8, p["conv_w"])
    conv_new = convn[:, CONV_PAD - (CONV_W - 1):, :]
    og, s_new = _gdn(yc.reshape(b, l, -1), z.reshape(b, l, -1), ab.reshape(b, l, LANES),
                     s_prev.astype(F32), p["alog_row"], p["dtb_row"], p["nw_row"])

    q3 = q.reshape(b, l, -1)
    kb3 = kb.reshape(b, l, -1)
    vb3 = vb.reshape(b, l, -1)
    if k_cache is None:
        od = _attn_prompt(q3, kb3, vb3, p["slopes"], p["lamv"], p["sw_row"], lam_init)
    else:
        past = k_cache.shape[1]
        od = _attn_sample(q3, kb3, vb3, k_cache.reshape(b, past * DIFF_HEADS, DIFF_DV),
                          v_cache.reshape(b, past * DIFF_HEADS, DIFF_DV),
                          p["slopes"], p["lamv"], p["sw_row"], lam_init)

    x1, x1b, route = _mix(x2, og.reshape(n, -1), od.reshape(n, -1), p["ln_in_g"], p["ln_in_b"], p["wo"],
                          p["ln1_g"], p["ln1_b"], p["rw"], p["rb"], alpha)

    expert = route[:, 0:TOP_K].astype(jnp.int32)
    blk = _expert_block(n)
    tok_pad, dest, block_expert, n_used = _dispatch_plan(expert, blk)
    xb = jnp.take(x1b, tok_pad, axis=0, mode="clip")
    yb = _experts(xb, block_expert, n_used, p["w1"], p["w3"], p["w2"], blk)
    y0 = jnp.take(yb, dest[:, 0], axis=0, mode="clip")
    y1 = jnp.take(yb, dest[:, 1], axis=0, mode="clip")
    y = _final(x1, y0, y1, route, p["ln2_g"], p["ln2_b"], alpha)

    return (y.reshape(b, l, d), conv_new, s_new,
            k.reshape(b, l, DIFF_HEADS, 2 * DIFF_DH), v.reshape(b, l, DIFF_HEADS, DIFF_DV))


def kernel(x_prompt, x_sample, cache_attn_k, cache_attn_v, state_gdn, state_conv, ln_in_g, ln_in_b, w_in, conv_w, gdn_a_log, gdn_dt_bias, gdn_norm_w, lam_q1, lam_k1, lam_q2, lam_k2, subln_w, w_o, ln1_g, ln1_b, router_g_w, router_g_b, router_e_w, router_e_b, w1, w3, w2, ln2_g, ln2_b):
    depth = w_in.shape[0]
    assert depth == 1, "single-layer step"
    d = x_prompt.shape[-1]
    alpha = (2 * depth) ** 0.25
    lam_init = 0.8 - 0.6 * math.exp(-0.3 * 0)
    row = lambda t: t.reshape(1, -1).astype(F32)

    wi = w_in[0]
    conv_ch = conv_w.shape[-1]
    gw = GDN_HEADS * GDN_D
    c_ab = conv_ch + gw
    c_q = c_ab + 2 * GDN_HEADS
    w_main = jnp.concatenate([wi[:, :c_ab], wi[:, c_q:]], axis=1).astype(BF16)
    w_ab = jnp.pad(wi[:, c_ab:c_q], ((0, 0), (0, LANES - 2 * GDN_HEADS))).astype(BF16)
    rcat = jnp.concatenate([router_g_w[0], router_e_w[0]], axis=1)
    rcat = jnp.pad(rcat, ((0, 0), (0, LANES - rcat.shape[1])))
    r_hi = rcat.astype(BF16)
    r_lo = (rcat - r_hi.astype(F32)).astype(BF16)
    p = {
        "ln_in_g": row(ln_in_g), "ln_in_b": row(ln_in_b), "w_main": w_main, "w_ab": w_ab,
        "conv_w": conv_w[0].astype(F32), "alog_row": _pad_lanes(gdn_a_log[0]),
        "dtb_row": _pad_lanes(gdn_dt_bias[0]), "nw_row": row(gdn_norm_w[0]),
        "slopes": jnp.asarray(2.0 ** (-8.0 * (np.arange(DIFF_HEADS) + 1) / DIFF_HEADS), F32),
        "lamv": jnp.stack([lam_q1[0], lam_k1[0], lam_q2[0], lam_k2[0]]).astype(F32),
        "sw_row": row(subln_w[0]), "wo": w_o[0].astype(BF16),
        "ln1_g": row(ln1_g[0]), "ln1_b": row(ln1_b[0]),
        "rw": jnp.stack([r_hi, r_lo]), "rb": _pad_lanes(jnp.concatenate([router_g_b[0], router_e_b[0]])),
        "w1": w1[0], "w3": w3[0], "w2": w2[0],
        "ln2_g": row(ln2_g[0]), "ln2_b": row(ln2_b[0]),
    }

    bp = x_prompt.shape[0]
    conv0 = jnp.zeros((bp, CONV_W - 1, conv_ch), F32)
    s0 = jnp.zeros((bp, GDN_HEADS, GDN_D, GDN_D), F32)
    yp, cp, sp, kp, vp = _layer(x_prompt, conv0, s0, None, None, p, alpha, lam_init)
    ys, cs, ss, ks, vs = _layer(x_sample, state_conv[0], state_gdn[0], cache_attn_k[0], cache_attn_v[0],
                                p, alpha, lam_init)
    return (yp, ys, kp[None], vp[None], sp[None], cp[None], ks[None], vs[None], ss[None], cs[None])
```

```python
import functools
import math

import jax
import jax.numpy as jnp
import numpy as np
from jax import lax
from jax.experimental import pallas as pl
from jax.experimental.pallas import tpu as pltpu

F32 = jnp.float32
BF16 = jnp.bfloat16

CHUNK = 64
CONV_W = 4
GDN_HEADS = 4
GDN_D = 128
DIFF_HEADS = 4
DIFF_DH = 64
DIFF_DV = 2 * DIFF_DH
N_GROUPS = 4
EXPERTS_PER_GROUP = 8
N_EXPERTS = N_GROUPS * EXPERTS_PER_GROUP
TOP_K = 2
NEG_INF = -1e30
LOG2E = math.log2(math.e)
LANES = 128
CONV_PAD = 8

TOKEN_TILE = 512
GDN_TILE = 64
GDN_SEQS = 4
ATTN_TILE = 256
ATTN_ROWS = 64
CACHE_TILE = 1024
EXPERT_BLOCK_MAX = 512
EXPERT_BLOCK_MIN = 128
PLAN_ROWS = 8
VMEM_LIMIT = 48 * 1024 * 1024


def _params(*sem):
    return pltpu.CompilerParams(dimension_semantics=sem, vmem_limit_bytes=VMEM_LIMIT)


def _dot(a, b):
    return jnp.dot(a, b, preferred_element_type=F32)


def _dot_nt(a, b):
    return lax.dot_general(a, b, (((1,), (1,)), ((), ())), preferred_element_type=F32)


def _sigmoid(x):
    return 1.0 / (1.0 + jnp.exp(-x))


def _silu(x):
    return x * _sigmoid(x)


def _softplus(x):
    return jnp.maximum(x, 0.0) + jnp.log(1.0 + jnp.exp(-jnp.abs(x)))


def _layernorm(x, g, b, eps=1e-5):
    mu = jnp.mean(x, axis=-1, keepdims=True)
    xc = x - mu
    var = jnp.mean(xc * xc, axis=-1, keepdims=True)
    return xc * lax.rsqrt(var + eps) * g + b


def _split3(x):
    hi = x.astype(BF16)
    r1 = x - hi.astype(F32)
    mid = r1.astype(BF16)
    lo = (r1 - mid.astype(F32)).astype(BF16)
    return hi, mid, lo


def _proj_kernel(x_ref, g_ref, b_ref, w_ref, wab_ref, convp_ref, cw_ref,
                 y_ref, z_ref, q_ref, k_ref, v_ref, kb_ref, vb_ref, ab_ref, convn_ref,
                 xc_scr, *, rows, n_seq, tiles_per_seq):
    xb = _layernorm(x_ref[...], g_ref[...], b_ref[...]).astype(BF16)
    w = 512
    stride = rows + CONV_PAD
    if tiles_per_seq > 1:
        first = pl.program_id(0) % tiles_per_seq == 0

        @pl.when(first)
        def _():
            xc_scr[0:CONV_PAD, :] = convp_ref[0]

        @pl.when(jnp.logical_not(first))
        def _():
            xc_scr[0:CONV_PAD, :] = xc_scr[rows:rows + CONV_PAD, :]
    else:
        for s in range(n_seq):
            xc_scr[s * stride:s * stride + CONV_PAD, :] = convp_ref[s]
    cw = cw_ref[...]

    for j in range(3):
        cs = slice(j * w, (j + 1) * w)
        qkv = _dot(xb, w_ref[:, cs])
        for s in range(n_seq):
            base = s * stride + CONV_PAD
            xc_scr[base:base + rows, cs] = qkv[s * rows:(s + 1) * rows, :]
            lo = base - (CONV_W - 1)
            y = cw[0:1, cs] * xc_scr[lo:lo + rows, cs]
            for t in range(1, CONV_W):
                y = y + cw[t:t + 1, cs] * xc_scr[lo + t:lo + t + rows, cs]
            hy = 0.5 * y
            y_ref[s * rows:(s + 1) * rows, cs] = (hy + hy * jnp.tanh(hy)).astype(BF16)
    for s in range(n_seq):
        convn_ref[s] = xc_scr[s * stride + rows:s * stride + rows + CONV_PAD, :]
    z_ref[...] = _dot(xb, w_ref[:, 3 * w:4 * w]).astype(BF16)
    q_ref[...] = _dot(xb, w_ref[:, 4 * w:5 * w]).astype(BF16)
    tm = x_ref.shape[0]
    k = _dot(xb, w_ref[:, 5 * w:6 * w])
    kb_ref[...] = k.astype(BF16)
    v = _dot(xb, w_ref[:, 6 * w:7 * w])
    vb_ref[...] = v.astype(BF16)
    for h in range(DIFF_HEADS):
        k_ref[pl.ds(h, tm, stride=DIFF_HEADS), :] = k[:, h * DIFF_DV:(h + 1) * DIFF_DV]
        v_ref[pl.ds(h, tm, stride=DIFF_HEADS), :] = v[:, h * DIFF_DV:(h + 1) * DIFF_DV]
    ab_ref[...] = _dot(xb, wab_ref[...])


def _proj(x2, seq_len, ln_g, ln_b, w_main, w_ab, conv_prev8, conv_w):
    n, d = x2.shape
    tm = min(TOKEN_TILE, n)
    cc = conv_w.shape[1]
    rows = min(seq_len, tm)
    n_seq = tm // rows
    tiles_per_seq = seq_len // rows
    assert tm % rows == 0 and seq_len % rows == 0 and rows % CONV_PAD == 0
    row = lambda i: (i, 0)
    fix = lambda i: (0, 0)
    seq = lambda i: (i // tiles_per_seq, 0, 0)
    hd = DIFF_HEADS * DIFF_DV
    outs = [(1, cc, BF16), (1, hd, BF16), (1, hd, BF16), (DIFF_HEADS, DIFF_DV, F32),
            (DIFF_HEADS, DIFF_DV, F32), (1, hd, BF16), (1, hd, BF16), (1, LANES, F32)]
    return pl.pallas_call(
        functools.partial(_proj_kernel, rows=rows, n_seq=n_seq, tiles_per_seq=tiles_per_seq),
        grid=(n // tm,),
        in_specs=[pl.BlockSpec((tm, d), row), pl.BlockSpec((1, d), fix), pl.BlockSpec((1, d), fix),
                  pl.BlockSpec(w_main.shape, fix), pl.BlockSpec(w_ab.shape, fix),
                  pl.BlockSpec((n_seq, CONV_PAD, cc), seq), pl.BlockSpec((CONV_W, cc), fix)],
        out_specs=[pl.BlockSpec((tm * r, c), row) for r, c, _ in outs]
                  + [pl.BlockSpec((n_seq, CONV_PAD, cc), seq)],
        out_shape=[jax.ShapeDtypeStruct((n * r, c), t) for r, c, t in outs]
                  + [jax.ShapeDtypeStruct(conv_prev8.shape, F32)],
        scratch_shapes=[pltpu.VMEM((n_seq * (rows + CONV_PAD), cc), F32)],
        compiler_params=_params("arbitrary"),
        name="proj",
    )(x2, ln_g, ln_b, w_main, w_ab, conv_prev8, conv_w)


def _gdn_kernel(y_ref, z_ref, ab_ref, s0_ref, alog_ref, dtb_ref, nw_ref,
                o_ref, sn_ref,
                s_scr, cum_scr, a_scr, p_scr, attn_scr, rhs_scr, sol_scr, qg_scr, kdt_scr, u_scr, os_scr,
                *, nb, tl, c):
    l = pl.program_id(1)
    d = GDN_D
    hw = GDN_HEADS * d

    @pl.when(l == 0)
    def _():
        s_scr[...] = s0_ref[...]

    g_all = [-jnp.exp(alog_ref[...]) * _softplus(ab_ref[bi] + dtb_ref[...]) for bi in range(nb)]
    beta_all = [_sigmoid(ab_ref[bi]) for bi in range(nb)]
    nw = nw_ref[...]

    rows = lax.broadcasted_iota(jnp.int32, (c, c), 0)
    cols = lax.broadcasted_iota(jnp.int32, (c, c), 1)
    incl = cols <= rows
    strict = cols < rows
    tri = jnp.where(incl, 1.0, 0.0).astype(BF16)
    grow = lax.broadcasted_iota(jnp.int32, (c, LANES), 0)
    glane = lax.broadcasted_iota(jnp.int32, (c, LANES), 1)

    n_chunks = tl // c
    groups = [(ci, bi, h) for ci in range(n_chunks) for bi in range(nb) for h in range(GDN_HEADS)]
    eye = jnp.where(rows == cols, 1.0, 0.0)

    gmat = jnp.concatenate(
        [jnp.where(glane < c, jnp.where(grow > glane, g_all[bi][ci * c:(ci + 1) * c, h:h + 1], 0.0),
                   g_all[bi][ci * c:(ci + 1) * c, h:h + 1]) for ci, bi, h in groups], axis=1)
    g_hi, g_mid, g_lo = _split3(gmat)
    cum_scr[...] = _dot(tri, g_hi) + _dot(tri, g_mid) + _dot(tri, g_lo)

    for g, (ci, bi, h) in enumerate(groups):
        r0 = ci * c
        beta = beta_all[bi][r0:r0 + c, GDN_HEADS + h:GDN_HEADS + h + 1]
        rel = cum_scr[:, g * LANES:g * LANES + c]
        gam = cum_scr[:, g * LANES + c:g * LANES + c + 1]
        g_last = gam[c - 1:c, :]
        decay = jnp.where(incl, jnp.exp(jnp.where(incl, rel, 0.0)), 0.0)
        eg = jnp.exp(gam)
        ek = jnp.exp(g_last - gam)

        qh = y_ref[bi, r0:r0 + c, h * d:(h + 1) * d].astype(F32)
        kh = y_ref[bi, r0:r0 + c, hw + h * d:hw + (h + 1) * d].astype(F32)
        vh = y_ref[bi, r0:r0 + c, 2 * hw + h * d:2 * hw + (h + 1) * d].astype(F32)
        qn = qh * lax.rsqrt(jnp.sum(qh * qh, axis=-1, keepdims=True) + 1e-6) * (d ** -0.5)
        kn = kh * lax.rsqrt(jnp.sum(kh * kh, axis=-1, keepdims=True) + 1e-6)
        qb = qn.astype(BF16)
        kb = kn.astype(BF16)
        a = jnp.where(strict, beta * _dot_nt(kb, kb) * decay, 0.0)
        a_scr[g] = a
        p_scr[g] = eye - a
        attn_scr[g] = (_dot_nt(qb, kb) * decay).astype(BF16)
        rhs_scr[g, :, 0:d] = (vh * beta).astype(BF16)
        rhs_scr[g, :, d:2 * d] = (kn * (beta * eg)).astype(BF16)
        qg_scr[g] = (qn * eg).astype(BF16)
        kdt_scr[g] = jnp.transpose(kn * ek).astype(BF16)

    n_sq = int(math.log2(c)) - 1
    for it in range(n_sq):
        for g in range(len(groups)):
            akb = a_scr[g].astype(BF16)
            a_scr[g] = _dot(akb, akb)
        for g in range(len(groups)):
            p = p_scr[g]
            p_scr[g] = p + _dot(p.astype(BF16), a_scr[g].astype(BF16))

    for g in range(len(groups)):
        sol_scr[g] = _dot(p_scr[g].astype(BF16), rhs_scr[g])

    for g, (ci, bi, h) in enumerate(groups):
        sb = s_scr[bi, h].astype(BF16)
        u_scr[g] = (sol_scr[g, :, 0:d] - _dot(sol_scr[g, :, d:2 * d].astype(BF16), sb)).astype(BF16)
        os_scr[g] = _dot(qg_scr[g], sb)

    for g, (ci, bi, h) in enumerate(groups):
        r0 = ci * c
        ub = u_scr[g]
        o = os_scr[g] + _dot(attn_scr[g], ub)
        g_last = cum_scr[c - 1:c, g * LANES + c:g * LANES + c + 1]
        s_scr[bi, h] = s_scr[bi, h] * jnp.exp(g_last) + _dot(kdt_scr[g], ub)

        zh = z_ref[bi, r0:r0 + c, h * d:(h + 1) * d].astype(F32)
        on = o * lax.rsqrt(jnp.mean(o * o, axis=-1, keepdims=True) + 1e-6) * nw
        o_ref[bi, r0:r0 + c, h * d:(h + 1) * d] = (on * _silu(zh)).astype(BF16)

    @pl.when(l == pl.num_programs(1) - 1)
    def _():
        sn_ref[...] = s_scr[...]


def _gdn(y, z, ab, s0, alog_row, dtb_row, nw_row):
    b, l, cc = y.shape
    c = min(l, CHUNK)
    tl = min(l, GDN_TILE)
    nb = math.gcd(b, GDN_SEQS)
    assert tl == c, "one chunk per grid step: the state recurrence is staged across groups"
    ng = nb * GDN_HEADS
    hw = GDN_HEADS * GDN_D
    tile = lambda i, j: (i, j, 0)
    fix = lambda i, j: (0, 0)
    return pl.pallas_call(
        functools.partial(_gdn_kernel, nb=nb, tl=tl, c=c),
        grid=(b // nb, l // tl),
        in_specs=[pl.BlockSpec((nb, tl, cc), tile), pl.BlockSpec((nb, tl, hw), tile),
                  pl.BlockSpec((nb, tl, LANES), tile),
                  pl.BlockSpec((nb, GDN_HEADS, GDN_D, GDN_D), lambda i, j: (i, 0, 0, 0)),
                  pl.BlockSpec((1, LANES), fix),
                  pl.BlockSpec((1, LANES), fix), pl.BlockSpec((1, GDN_D), fix)],
        out_specs=[pl.BlockSpec((nb, tl, hw), tile),
                   pl.BlockSpec((nb, GDN_HEADS, GDN_D, GDN_D), lambda i, j: (i, 0, 0, 0))],
        out_shape=[jax.ShapeDtypeStruct((b, l, hw), BF16),
                   jax.ShapeDtypeStruct((b, GDN_HEADS, GDN_D, GDN_D), F32)],
        scratch_shapes=[pltpu.VMEM((nb, GDN_HEADS, GDN_D, GDN_D), F32),
                        pltpu.VMEM((c, ng * LANES), F32), pltpu.VMEM((ng, c, c), F32),
                        pltpu.VMEM((ng, c, c), F32), pltpu.VMEM((ng, c, c), BF16),
                        pltpu.VMEM((ng, c, 2 * GDN_D), BF16), pltpu.VMEM((ng, c, 2 * GDN_D), F32),
                        pltpu.VMEM((ng, c, GDN_D), BF16), pltpu.VMEM((ng, GDN_D, c), BF16),
                        pltpu.VMEM((ng, c, GDN_D), BF16), pltpu.VMEM((ng, c, GDN_D), F32)],
        compiler_params=_params("parallel", "arbitrary"),
        name="gdn",
    )(y, z, ab, s0, alog_row, dtb_row, nw_row)


def _lambda(lamv, lam_init):
    l1 = jnp.sum(lamv[0:1] * lamv[1:2], axis=-1, keepdims=True)
    l2 = jnp.sum(lamv[2:3] * lamv[3:4], axis=-1, keepdims=True)
    return jnp.exp(l1) - jnp.exp(l2) + lam_init


def _attn_bias(q0, k0, nq, nk, slope):
    qpos = q0 + lax.broadcasted_iota(jnp.int32, (nq, nk), 0)
    kpos = k0 + lax.broadcasted_iota(jnp.int32, (nq, nk), 1)
    dist = jnp.abs(qpos - kpos).astype(F32)
    shift = int(math.log2(CHUNK))
    allowed = (kpos >> shift) <= (qpos >> shift)
    return jnp.where(allowed, -slope * dist, NEG_INF)


def _attn_init(stats):
    for m, l, a in stats:
        m[...] = jnp.full(m.shape, NEG_INF, F32)
        l[...] = jnp.zeros(l.shape, F32)
        a[...] = jnp.zeros(a.shape, F32)


def _attn_update(q, k, v, bias, stats):
    for i, (m, l, a) in enumerate(stats):
        lo, hi = i * DIFF_DH, (i + 1) * DIFF_DH
        s = _dot_nt(q[:, lo:hi], k[:, lo:hi]) + bias
        m_new = jnp.maximum(m[...], jnp.max(s, axis=-1, keepdims=True))
        alpha = jnp.exp(m[...] - m_new)
        p = jnp.exp(s - m_new)
        l[...] = alpha * l[...] + jnp.sum(p, axis=-1, keepdims=True)
        a[...] = alpha * a[...] + _dot(p.astype(BF16), v)
        m[...] = m_new


def _attn_finish(stats, lam, sw, lam_init):
    (_, l1, a1), (_, l2, a2) = stats
    o = a1[...] / l1[...] - lam * (a2[...] / l2[...])
    return o * lax.rsqrt(jnp.mean(o * o, axis=-1, keepdims=True) + 1e-6) * sw * (1.0 - lam_init)


def _attn_prompt_kernel(slopes_ref, lamv_ref, sw_ref, q_ref, k_ref, v_ref, o_ref,
                        vx_scr, bias_scr, q_scr, m_scr, acc_scr, sa_scr, sb_scr, pa_scr, pb_scr, ala_scr, alb_scr,
                        *, seq, tile, lam_init):
    t = tile
    r = 2 * t
    rb = ATTN_ROWS
    dv = DIFF_DV
    slope = slopes_ref[pl.program_id(1)] * LOG2E
    q_scale = (DIFF_DH ** -0.5) * LOG2E
    lam = _lambda(lamv_ref[...], lam_init)
    sw = sw_ref[...]
    shift = int(math.log2(CHUNK))

    vx_scr[:, 0:dv] = v_ref[0]
    vx_scr[:, dv:2 * dv] = jnp.ones((seq, dv), BF16)
    qi_ = lax.broadcasted_iota(jnp.int32, (t, t), 0)
    kj_ = lax.broadcasted_iota(jnp.int32, (t, t), 1)
    rel = (qi_ - kj_).astype(F32)
    bias_scr[0] = -slope * rel
    bias_scr[1] = jnp.where((kj_ >> shift) <= (qi_ >> shift), -slope * jnp.abs(rel), NEG_INF)
    lane = lax.broadcasted_iota(jnp.int32, (t, dv), 1)
    frame_shift = slope * t

    def finalize(qi):
        q0 = qi * t
        o1 = acc_scr[0:t, 0:dv] / acc_scr[0:t, dv:2 * dv]
        o2 = acc_scr[t:r, 0:dv] / acc_scr[t:r, dv:2 * dv]
        o = o1 - lam * o2
        on = o * lax.rsqrt(jnp.mean(o * o, axis=-1, keepdims=True) + 1e-6) * sw * (1.0 - lam_init)
        o_ref[0, q0:q0 + t, :] = on.astype(BF16)

    def q_body(qi):
        q0 = qi * t
        qf = q_ref[0, q0:q0 + t, :].astype(F32) * q_scale
        q_scr[0:t, :] = jnp.where(lane < DIFF_DH, qf, 0.0).astype(BF16)
        q_scr[t:r, :] = jnp.where(lane >= DIFF_DH, qf, 0.0).astype(BF16)
        m_scr[...] = jnp.full(m_scr.shape, NEG_INF, F32)
        if qi > 0:
            finalize(qi - 1)
        acc_scr[...] = jnp.zeros(acc_scr.shape, F32)

        def qk(kj, s_buf):
            k0 = min(kj, qi) * t
            s_buf[...] = _dot_nt(q_scr[...], k_ref[0, k0:k0 + t, :])

        def softmax_accumulate(kj, s_buf, p_buf, al_buf):
            k0 = kj * t
            diag = int(kj == qi)
            for i in range(r // rb):
                r0 = i * rb
                b_lo = r0 % t
                s = s_buf[r0:r0 + rb, :] + bias_scr[diag, b_lo:b_lo + rb, :]
                cols = [s[:, j * LANES:(j + 1) * LANES] for j in range(t // LANES)]
                m_prev = m_scr[r0:r0 + rb, :] - frame_shift
                m_cur = jnp.max(functools.reduce(jnp.maximum, cols), axis=-1, keepdims=True)
                m_new = jnp.maximum(m_prev, m_cur)
                al_buf[r0:r0 + rb, :] = jnp.exp2(m_prev - m_new)
                for j, col in enumerate(cols):
                    p_buf[r0:r0 + rb, j * LANES:(j + 1) * LANES] = jnp.exp2(col - m_new).astype(BF16)
                m_scr[r0:r0 + rb, :] = m_new
            pv = _dot(p_buf[...], vx_scr[k0:k0 + t, :])
            for i in range(r // rb):
                r0 = i * rb
                al = al_buf[r0:r0 + rb, :]
                acc_scr[r0:r0 + rb, :] = (jnp.concatenate([al, al], axis=1) * acc_scr[r0:r0 + rb, :]
                                          + pv[r0:r0 + rb, :])

        sets = ((sa_scr, pa_scr, ala_scr), (sb_scr, pb_scr, alb_scr))
        qk(0, sa_scr)
        for kj in range(qi + 1):
            qk(kj + 1, sets[(kj + 1) % 2][0])
            softmax_accumulate(kj, *sets[kj % 2])

    for qi in range(seq // t):
        q_body(qi)
    finalize(seq // t - 1)


def _attn_prompt(q, kb, vb, slopes, lamv, sw_row, lam_init):
    b, l, _ = q.shape
    t = min(ATTN_TILE, l)
    assert t % CHUNK == 0 and l % t == 0 and t % LANES == 0 and t % ATTN_ROWS == 0
    head = lambda i, h: (i, 0, h)
    fix = lambda i, h: (0, 0)
    dv = DIFF_DV
    scratch = [pltpu.VMEM((l, 2 * dv), BF16), pltpu.VMEM((2, t, t), F32),
               pltpu.VMEM((2 * t, dv), BF16), pltpu.VMEM((2 * t, LANES), F32),
               pltpu.VMEM((2 * t, 2 * dv), F32),
               pltpu.VMEM((2 * t, t), F32), pltpu.VMEM((2 * t, t), F32),
               pltpu.VMEM((2 * t, t), BF16), pltpu.VMEM((2 * t, t), BF16),
               pltpu.VMEM((2 * t, LANES), F32), pltpu.VMEM((2 * t, LANES), F32)]
    return pl.pallas_call(
        functools.partial(_attn_prompt_kernel, seq=l, tile=t, lam_init=lam_init),
        grid=(b, DIFF_HEADS),
        in_specs=[pl.BlockSpec(memory_space=pltpu.SMEM),
                  pl.BlockSpec(lamv.shape, fix), pl.BlockSpec(sw_row.shape, fix),
                  pl.BlockSpec((1, l, dv), head), pl.BlockSpec((1, l, dv), head),
                  pl.BlockSpec((1, l, dv), head)],
        out_specs=pl.BlockSpec((1, l, dv), head),
        out_shape=jax.ShapeDtypeStruct((b, l, DIFF_HEADS * dv), BF16),
        scratch_shapes=scratch,
        compiler_params=_params("parallel", "parallel"),
        name="attn_prompt",
    )(slopes, lamv, sw_row, q, kb, vb)


def _attn_sample_kernel(slopes_ref, lamv_ref, sw_ref, q_ref, kc_ref, vc_ref, kn_ref, vn_ref, o_ref,
                        m1, l1, a1, m2, l2, a2, *, past, tk, lam_init):
    j = pl.program_id(1)
    n_cache = past // tk
    lq = q_ref.shape[1]
    heads = range(DIFF_HEADS)
    cols = [slice(h * DIFF_DV, (h + 1) * DIFF_DV) for h in heads]
    stats = [((m1.at[h], l1.at[h], a1.at[h]), (m2.at[h], l2.at[h], a2.at[h])) for h in heads]
    qs = [(q_ref[0, :, cols[h]].astype(F32) * (DIFF_DH ** -0.5)).astype(BF16) for h in heads]

    @pl.when(j == 0)
    def _():
        for h in heads:
            _attn_init(stats[h])

    @pl.when(j < n_cache)
    def _():
        for h in heads:
            k = kc_ref[0, pl.ds(h, tk, stride=DIFF_HEADS), :].astype(BF16)
            v = vc_ref[0, pl.ds(h, tk, stride=DIFF_HEADS), :].astype(BF16)
            _attn_update(qs[h], k, v, _attn_bias(past, j * tk, lq, tk, slopes_ref[h]), stats[h])

    @pl.when(j == n_cache)
    def _():
        lam = _lambda(lamv_ref[...], lam_init)
        for h in heads:
            _attn_update(qs[h], kn_ref[0, :, cols[h]], vn_ref[0, :, cols[h]],
                         _attn_bias(past, past, lq, lq, slopes_ref[h]), stats[h])
            o_ref[0, :, cols[h]] = _attn_finish(stats[h], lam, sw_ref[...], lam_init).astype(BF16)


def _attn_sample(q, kb, vb, k_cache, v_cache, slopes, lamv, sw_row, lam_init):
    b, l, hd = q.shape
    past = k_cache.shape[1] // DIFF_HEADS
    tk = min(CACHE_TILE, past)
    assert past % tk == 0 and past % CHUNK == 0 and l <= CHUNK
    n_cache = past // tk
    per_b = lambda i, j: (i, 0, 0)
    cache = lambda i, j: (i, jnp.minimum(j, n_cache - 1), 0)
    fix = lambda i, j: (0, 0)
    stat = [pltpu.VMEM((DIFF_HEADS, l, 1), F32), pltpu.VMEM((DIFF_HEADS, l, 1), F32),
            pltpu.VMEM((DIFF_HEADS, l, DIFF_DV), F32)]
    return pl.pallas_call(
        functools.partial(_attn_sample_kernel, past=past, tk=tk, lam_init=lam_init),
        grid=(b, n_cache + 1),
        in_specs=[pl.BlockSpec(memory_space=pltpu.SMEM),
                  pl.BlockSpec(lamv.shape, fix), pl.BlockSpec(sw_row.shape, fix),
                  pl.BlockSpec((1, l, hd), per_b),
                  pl.BlockSpec((1, tk * DIFF_HEADS, DIFF_DV), cache),
                  pl.BlockSpec((1, tk * DIFF_HEADS, DIFF_DV), cache),
                  pl.BlockSpec((1, l, hd), per_b), pl.BlockSpec((1, l, hd), per_b)],
        out_specs=pl.BlockSpec((1, l, hd), per_b),
        out_shape=jax.ShapeDtypeStruct((b, l, hd), BF16),
        scratch_shapes=stat + stat,
        compiler_params=_params("parallel", "arbitrary"),
        name="attn_sample",
    )(slopes, lamv, sw_row, q, k_cache, v_cache, kb, vb)


def _mix_kernel(x_ref, og_ref, od_ref, lig_ref, lib_ref, wo_ref, g1_ref, b1_ref, rw_ref, rb_ref, tri_ref,
                x1_ref, x1b_ref, route_ref, cnt_ref, cnt_scr, *, alpha):
    xn = _layernorm(x_ref[...], lig_ref[...], lib_ref[...])
    hw = og_ref.shape[1]
    mixed = _dot(og_ref[...], wo_ref[0:hw, :]) + _dot(od_ref[...], wo_ref[hw:, :])
    x1 = _layernorm(alpha * xn + mixed, g1_ref[...], b1_ref[...])
    x1_ref[...] = x1
    x1b_ref[...] = x1.astype(BF16)

    x_hi = x1.astype(BF16)
    x_lo = (x1 - x_hi.astype(F32)).astype(BF16)
    lg = (_dot(x_hi, rw_ref[0]) + _dot(x_lo, rw_ref[0]) + _dot(x_hi, rw_ref[1])) + rb_ref[...]
    lane = lax.broadcasted_iota(jnp.int32, lg.shape, 1)
    lanef = lane.astype(F32)
    big = float(LANES)

    gl = jnp.where(lane < N_GROUPS, lg, -jnp.inf)
    gmax = jnp.max(gl, axis=-1, keepdims=True)
    grp = jnp.min(jnp.where(gl == gmax, lanef, big), axis=-1, keepdims=True)
    g_gate = 1.0 / jnp.sum(jnp.where(lane < N_GROUPS, jnp.exp(gl - gmax), 0.0), axis=-1, keepdims=True)

    e_lo = N_GROUPS + grp * EXPERTS_PER_GROUP
    el = jnp.where((lanef >= e_lo) & (lanef < e_lo + EXPERTS_PER_GROUP), lg, -jnp.inf)
    v1 = jnp.max(el, axis=-1, keepdims=True)
    i1 = jnp.min(jnp.where(el == v1, lanef, big), axis=-1, keepdims=True)
    el2 = jnp.where(lanef == i1, -jnp.inf, el)
    v2 = jnp.max(el2, axis=-1, keepdims=True)
    i2 = jnp.min(jnp.where(el2 == v2, lanef, big), axis=-1, keepdims=True)
    e21 = jnp.exp(v2 - v1)
    p1 = 1.0 / (1.0 + e21)

    @pl.when(pl.program_id(0) == 0)
    def _():
        cnt_scr[...] = jnp.zeros(cnt_scr.shape, F32)

    e0 = i1 - N_GROUPS
    e1 = i2 - N_GROUPS
    hot0 = lanef == e0
    hot1 = lanef == e1
    onehot = jnp.where(hot0 | hot1, 1.0, 0.0)
    before = _dot(tri_ref[...], onehot.astype(BF16)) + cnt_scr[...]
    rank0 = jnp.sum(jnp.where(hot0, before, 0.0), axis=-1, keepdims=True)
    rank1 = jnp.sum(jnp.where(hot1, before, 0.0), axis=-1, keepdims=True)
    tm = onehot.shape[0]
    cnt_scr[...] = before[tm - 1:tm, :] + onehot[tm - 1:tm, :]
    cnt_ref[...] = cnt_scr[...]

    route_ref[...] = jnp.where(lane == 0, e0,
                     jnp.where(lane == 1, e1,
                     jnp.where(lane == 2, g_gate * p1,
                     jnp.where(lane == 3, g_gate * (e21 * p1),
                     jnp.where(lane == 4, rank0,
                     jnp.where(lane == 5, rank1, 0.0))))))


def _mix(x2, og, od, li_g, li_b, wo, g1, b1, rw, rb, alpha):
    n, d = x2.shape
    tm = min(TOKEN_TILE, n)
    row = lambda i: (i, 0)
    fix = lambda i: (0, 0)
    tri = jnp.tri(tm, k=-1, dtype=BF16)
    return pl.pallas_call(
        functools.partial(_mix_kernel, alpha=alpha),
        grid=(n // tm,),
        in_specs=[pl.BlockSpec((tm, d), row), pl.BlockSpec((tm, og.shape[1]), row),
                  pl.BlockSpec((tm, od.shape[1]), row),
                  pl.BlockSpec((1, d), fix), pl.BlockSpec((1, d), fix), pl.BlockSpec(wo.shape, fix),
                  pl.BlockSpec((1, d), fix), pl.BlockSpec((1, d), fix),
                  pl.BlockSpec(rw.shape, lambda i: (0, 0, 0)), pl.BlockSpec((1, LANES), fix),
                  pl.BlockSpec((tm, tm), fix)],
        out_specs=[pl.BlockSpec((tm, d), row), pl.BlockSpec((tm, d), row), pl.BlockSpec((tm, LANES), row),
                   pl.BlockSpec((1, LANES), fix)],
        out_shape=[jax.ShapeDtypeStruct((n, d), F32), jax.ShapeDtypeStruct((n, d), BF16),
                   jax.ShapeDtypeStruct((n, LANES), F32), jax.ShapeDtypeStruct((1, LANES), F32)],
        scratch_shapes=[pltpu.VMEM((1, LANES), F32)],
        compiler_params=_params("arbitrary"),
        name="mix",
    )(x2, og, od, li_g, li_b, wo, g1, b1, rw, rb, tri)


def _expert_kernel(be_ref, nu_ref, x_ref, w1_ref, w3_ref, w2_ref, y_ref, w1b, w3b, w2b):
    i = pl.program_id(0)

    @pl.when((i == 0) | (be_ref[i] != be_ref[jnp.maximum(i - 1, 0)]))
    def _():
        w1b[...] = w1_ref[0].astype(BF16)
        w3b[...] = w3_ref[0].astype(BF16)
        w2b[...] = w2_ref[0].astype(BF16)

    @pl.when(i < nu_ref[0])
    def _():
        x = x_ref[...]
        h1 = _dot(x, w1b[...])
        h3 = _dot(x, w3b[...])
        y_ref[...] = _dot((_silu(h1) * h3).astype(BF16), w2b[...]).astype(BF16)

    @pl.when(i >= nu_ref[0])
    def _():
        y_ref[...] = jnp.zeros(y_ref.shape, y_ref.dtype)


def _expert_block(n_tokens):
    per_expert = max(1, n_tokens * TOP_K // N_EXPERTS)
    return int(min(EXPERT_BLOCK_MAX, max(EXPERT_BLOCK_MIN, 2 ** int(math.log2(per_expert)))))


def _experts(xb, block_expert, n_used, w1, w3, w2, blk):
    p_pad, d = xb.shape
    de = w1.shape[2]
    row = lambda i, be, nu: (i, 0)
    grid_spec = pltpu.PrefetchScalarGridSpec(
        num_scalar_prefetch=2,
        grid=(p_pad // blk,),
        in_specs=[pl.BlockSpec((blk, d), row),
                  pl.BlockSpec((1, d, de), lambda i, be, nu: (be[i], 0, 0)),
                  pl.BlockSpec((1, d, de), lambda i, be, nu: (be[i], 0, 0)),
                  pl.BlockSpec((1, de, d), lambda i, be, nu: (be[i], 0, 0))],
        out_specs=pl.BlockSpec((blk, d), row),
        scratch_shapes=[pltpu.VMEM((d, de), BF16), pltpu.VMEM((d, de), BF16), pltpu.VMEM((de, d), BF16)],
    )
    return pl.pallas_call(
        _expert_kernel,
        grid_spec=grid_spec,
        out_shape=jax.ShapeDtypeStruct((p_pad, d), BF16),
        compiler_params=_params("arbitrary"),
        name="experts",
    )(block_expert, n_used, xb, w1, w3, w2)


def _dispatch_plan(plan, counts_row, blk):
    n = plan.shape[1]
    p = n * TOP_K
    n_blocks = -(-p // blk) + N_EXPERTS
    counts = counts_row[0, 0:N_EXPERTS].astype(jnp.int32)
    starts = jnp.cumsum(counts) - counts
    padded = (counts + blk - 1) // blk * blk
    padded_end = jnp.cumsum(padded)
    padded_start = padded_end - padded
    expert = plan[0:TOP_K].astype(jnp.int32)
    rank = plan[4:4 + TOP_K].astype(jnp.int32)
    dest = rank + padded_start[expert]
    token = jnp.broadcast_to(jnp.arange(n, dtype=jnp.int32), (TOP_K, n))
    _, order_tok = lax.sort((dest.reshape(p), token.reshape(p)), num_keys=1)
    block_start = jnp.arange(n_blocks, dtype=jnp.int32) * blk
    block_expert = jnp.minimum(
        jnp.sum((padded_end[None, :] <= block_start[:, None]).astype(jnp.int32), axis=1), N_EXPERTS - 1)
    row_id = jnp.arange(n_blocks * blk, dtype=jnp.int32).reshape(n_blocks, blk)
    j = row_id - padded_start[block_expert][:, None]
    src = jnp.clip(starts[block_expert][:, None] + j, 0, p - 1)
    tok_pad = jnp.where(j < counts[block_expert][:, None], order_tok[src], row_id % n).reshape(-1)
    n_used = (padded_end[-1:] // blk).astype(jnp.int32)
    return tok_pad, dest, block_expert, n_used


def _final_kernel(x1_ref, y0_ref, y1_ref, route_ref, g_ref, b_ref, o_ref, *, alpha):
    route = route_ref[...]
    moe = route[:, 2:3] * y0_ref[...].astype(F32) + route[:, 3:4] * y1_ref[...].astype(F32)
    o_ref[...] = _layernorm(alpha * x1_ref[...] + moe, g_ref[...], b_ref[...])


def _final(x1, y0, y1, route, g2, b2, alpha):
    n, d = x1.shape
    tm = min(TOKEN_TILE, n)
    row = lambda i: (i, 0)
    fix = lambda i: (0, 0)
    return pl.pallas_call(
        functools.partial(_final_kernel, alpha=alpha),
        grid=(n // tm,),
        in_specs=[pl.BlockSpec((tm, d), row), pl.BlockSpec((tm, d), row), pl.BlockSpec((tm, d), row),
                  pl.BlockSpec((tm, LANES), row), pl.BlockSpec((1, d), fix), pl.BlockSpec((1, d), fix)],
        out_specs=pl.BlockSpec((tm, d), row),
        out_shape=jax.ShapeDtypeStruct((n, d), F32),
        compiler_params=_params("parallel"),
        name="final",
    )(x1, y0, y1, route, g2, b2)


def _pad_lanes(v, width=LANES):
    v = v.reshape(1, -1).astype(F32)
    return jnp.pad(v, ((0, 0), (0, width - v.shape[1])))


def _layer(x, conv_prev, s_prev, k_cache, v_cache, p, alpha, lam_init):
    b, l, d = x.shape
    n = b * l
    x2 = x.reshape(n, d)
    conv_prev8 = jnp.pad(conv_prev.astype(F32), ((0, 0), (CONV_PAD - (CONV_W - 1), 0), (0, 0)))
    yc, z, q, k, v, kb, vb, ab, convn = _proj(x2, l, p["ln_in_g"], p["ln_in_b"], p["w_main"], p["w_ab"],
                                              conv_prev8, p["conv_w"])
    conv_new = convn[:, CONV_PAD - (CONV_W - 1):, :]
    og, s_new = _gdn(yc.reshape(b, l, -1), z.reshape(b, l, -1), ab.reshape(b, l, LANES),
                     s_prev.astype(F32), p["alog_row"], p["dtb_row"], p["nw_row"])

    q3 = q.reshape(b, l, -1)
    kb3 = kb.reshape(b, l, -1)
    vb3 = vb.reshape(b, l, -1)
    if k_cache is None:
        od = _attn_prompt(q3, kb3, vb3, p["slopes"], p["lamv"], p["sw_row"], lam_init)
    else:
        past = k_cache.shape[1]
        od = _attn_sample(q3, kb3, vb3, k_cache.reshape(b, past * DIFF_HEADS, DIFF_DV),
                          v_cache.reshape(b, past * DIFF_HEADS, DIFF_DV),
                          p["slopes"], p["lamv"], p["sw_row"], lam_init)

    x1, x1b, route, counts_row = _mix(x2, og.reshape(n, -1), od.reshape(n, -1), p["ln_in_g"], p["ln_in_b"],
                                      p["wo"], p["ln1_g"], p["ln1_b"], p["rw"], p["rb"], alpha)

    blk = _expert_block(n)
    plan = jnp.transpose(route[:, 0:PLAN_ROWS])
    tok_pad, dest, block_expert, n_used = _dispatch_plan(plan, counts_row, blk)
    xb = jnp.take(x1b, tok_pad, axis=0, mode="clip")
    yb = _experts(xb, block_expert, n_used, p["w1"], p["w3"], p["w2"], blk)
    y0 = jnp.take(yb, dest[0], axis=0, mode="clip")
    y1 = jnp.take(yb, dest[1], axis=0, mode="clip")
    y = _final(x1, y0, y1, route, p["ln2_g"], p["ln2_b"], alpha)

    return (y.reshape(b, l, d), conv_new, s_new,
            k.reshape(b, l, DIFF_HEADS, 2 * DIFF_DH), v.reshape(b, l, DIFF_HEADS, DIFF_DV))


def kernel(x_prompt, x_sample, cache_attn_k, cache_attn_v, state_gdn, state_conv, ln_in_g, ln_in_b, w_in, conv_w, gdn_a_log, gdn_dt_bias, gdn_norm_w, lam_q1, lam_k1, lam_q2, lam_k2, subln_w, w_o, ln1_g, ln1_b, router_g_w, router_g_b, router_e_w, router_e_b, w1, w3, w2, ln2_g, ln2_b):
    depth = w_in.shape[0]
    assert depth == 1, "single-layer step"
    d = x_prompt.shape[-1]
    alpha = (2 * depth) ** 0.25
    lam_init = 0.8 - 0.6 * math.exp(-0.3 * 0)
    row = lambda t: t.reshape(1, -1).astype(F32)

    wi = w_in[0]
    conv_ch = conv_w.shape[-1]
    gw = GDN_HEADS * GDN_D
    c_ab = conv_ch + gw
    c_q = c_ab + 2 * GDN_HEADS
    w_main = jnp.concatenate([wi[:, :c_ab], wi[:, c_q:]], axis=1).astype(BF16)
    w_ab = jnp.pad(wi[:, c_ab:c_q], ((0, 0), (0, LANES - 2 * GDN_HEADS))).astype(BF16)
    rcat = jnp.concatenate([router_g_w[0], router_e_w[0]], axis=1)
    rcat = jnp.pad(rcat, ((0, 0), (0, LANES - rcat.shape[1])))
    r_hi = rcat.astype(BF16)
    r_lo = (rcat - r_hi.astype(F32)).astype(BF16)
    p = {
        "ln_in_g": row(ln_in_g), "ln_in_b": row(ln_in_b), "w_main": w_main, "w_ab": w_ab,
        "conv_w": conv_w[0].astype(F32), "alog_row": _pad_lanes(gdn_a_log[0]),
        "dtb_row": _pad_lanes(gdn_dt_bias[0]), "nw_row": row(gdn_norm_w[0]),
        "slopes": jnp.asarray(2.0 ** (-8.0 * (np.arange(DIFF_HEADS) + 1) / DIFF_HEADS), F32),
        "lamv": jnp.stack([lam_q1[0], lam_k1[0], lam_q2[0], lam_k2[0]]).astype(F32),
        "sw_row": row(subln_w[0]), "wo": w_o[0].astype(BF16),
        "ln1_g": row(ln1_g[0]), "ln1_b": row(ln1_b[0]),
        "rw": jnp.stack([r_hi, r_lo]), "rb": _pad_lanes(jnp.concatenate([router_g_b[0], router_e_b[0]])),
        "w1": w1[0], "w3": w3[0], "w2": w2[0],
        "ln2_g": row(ln2_g[0]), "ln2_b": row(ln2_b[0]),
    }

    bp = x_prompt.shape[0]
    conv0 = jnp.zeros((bp, CONV_W - 1, conv_ch), F32)
    s0 = jnp.zeros((bp, GDN_HEADS, GDN_D, GDN_D), F32)
    yp, cp, sp, kp, vp = _layer(x_prompt, conv0, s0, None, None, p, alpha, lam_init)
    ys, cs, ss, ks, vs = _layer(x_sample, state_conv[0], state_gdn[0], cache_attn_k[0], cache_attn_v[0],
                                p, alpha, lam_init)
    return (yp, ys, kp[None], vp[None], sp[None], cp[None], ks[None], vs[None], ss[None], cs[None])
```

```python
import functools
import math

import jax
import jax.numpy as jnp
import numpy as np
from jax import lax
from jax.experimental import pallas as pl
from jax.experimental.pallas import tpu as pltpu

F32 = jnp.float32
BF16 = jnp.bfloat16

CHUNK = 64
CONV_W = 4
GDN_HEADS = 4
GDN_D = 128
DIFF_HEADS = 4
DIFF_DH = 64
DIFF_DV = 2 * DIFF_DH
N_GROUPS = 4
EXPERTS_PER_GROUP = 8
N_EXPERTS = N_GROUPS * EXPERTS_PER_GROUP
TOP_K = 2
NEG_INF = -1e30
LOG2E = math.log2(math.e)
LANES = 128
CONV_PAD = 8

TOKEN_TILE = 512
GDN_TILE = 64
GDN_SEQS = 4
ATTN_TILE = 256
ATTN_ROWS = 64
CACHE_TILE = 1024
EXPERT_BLOCK_MAX = 512
EXPERT_BLOCK_MIN = 128
PLAN_ROWS = 8
VMEM_LIMIT = 48 * 1024 * 1024


def _params(*sem):
    return pltpu.CompilerParams(dimension_semantics=sem, vmem_limit_bytes=VMEM_LIMIT)


def _dot(a, b):
    return jnp.dot(a, b, preferred_element_type=F32)


def _dot_nt(a, b):
    return lax.dot_general(a, b, (((1,), (1,)), ((), ())), preferred_element_type=F32)


def _sigmoid(x):
    return 1.0 / (1.0 + jnp.exp(-x))


def _silu(x):
    return x * _sigmoid(x)


def _softplus(x):
    return jnp.maximum(x, 0.0) + jnp.log(1.0 + jnp.exp(-jnp.abs(x)))


def _layernorm(x, g, b, eps=1e-5):
    mu = jnp.mean(x, axis=-1, keepdims=True)
    xc = x - mu
    var = jnp.mean(xc * xc, axis=-1, keepdims=True)
    return xc * lax.rsqrt(var + eps) * g + b


def _split3(x):
    hi = x.astype(BF16)
    r1 = x - hi.astype(F32)
    mid = r1.astype(BF16)
    lo = (r1 - mid.astype(F32)).astype(BF16)
    return hi, mid, lo


def _proj_kernel(x_ref, g_ref, b_ref, w_ref, wab_ref, convp_ref, cw_ref,
                 y_ref, z_ref, q_ref, k_ref, v_ref, kb_ref, vb_ref, ab_ref, convn_ref,
                 xc_scr, *, rows, n_seq, tiles_per_seq):
    xb = _layernorm(x_ref[...], g_ref[...], b_ref[...]).astype(BF16)
    w = 512
    stride = rows + CONV_PAD
    if tiles_per_seq > 1:
        first = pl.program_id(0) % tiles_per_seq == 0

        @pl.when(first)
        def _():
            xc_scr[0:CONV_PAD, :] = convp_ref[0]

        @pl.when(jnp.logical_not(first))
        def _():
            xc_scr[0:CONV_PAD, :] = xc_scr[rows:rows + CONV_PAD, :]
    else:
        for s in range(n_seq):
            xc_scr[s * stride:s * stride + CONV_PAD, :] = convp_ref[s]
    cw = cw_ref[...]

    for j in range(3):
        cs = slice(j * w, (j + 1) * w)
        qkv = _dot(xb, w_ref[:, cs])
        for s in range(n_seq):
            base = s * stride + CONV_PAD
            xc_scr[base:base + rows, cs] = qkv[s * rows:(s + 1) * rows, :]
            lo = base - (CONV_W - 1)
            y = cw[0:1, cs] * xc_scr[lo:lo + rows, cs]
            for t in range(1, CONV_W):
                y = y + cw[t:t + 1, cs] * xc_scr[lo + t:lo + t + rows, cs]
            hy = 0.5 * y
            y_ref[s * rows:(s + 1) * rows, cs] = (hy + hy * jnp.tanh(hy)).astype(BF16)
    for s in range(n_seq):
        convn_ref[s] = xc_scr[s * stride + rows:s * stride + rows + CONV_PAD, :]
    z_ref[...] = _dot(xb, w_ref[:, 3 * w:4 * w]).astype(BF16)
    q_ref[...] = _dot(xb, w_ref[:, 4 * w:5 * w]).astype(BF16)
    tm = x_ref.shape[0]
    k = _dot(xb, w_ref[:, 5 * w:6 * w])
    kb_ref[...] = k.astype(BF16)
    v = _dot(xb, w_ref[:, 6 * w:7 * w])
    vb_ref[...] = v.astype(BF16)
    for h in range(DIFF_HEADS):
        k_ref[pl.ds(h, tm, stride=DIFF_HEADS), :] = k[:, h * DIFF_DV:(h + 1) * DIFF_DV]
        v_ref[pl.ds(h, tm, stride=DIFF_HEADS), :] = v[:, h * DIFF_DV:(h + 1) * DIFF_DV]
    ab_ref[...] = _dot(xb, wab_ref[...])


def _proj(x2, seq_len, ln_g, ln_b, w_main, w_ab, conv_prev8, conv_w):
    n, d = x2.shape
    tm = min(TOKEN_TILE, n)
    cc = conv_w.shape[1]
    rows = min(seq_len, tm)
    n_seq = tm // rows
    tiles_per_seq = seq_len // rows
    assert tm % rows == 0 and seq_len % rows == 0 and rows % CONV_PAD == 0
    row = lambda i: (i, 0)
    fix = lambda i: (0, 0)
    seq = lambda i: (i // tiles_per_seq, 0, 0)
    hd = DIFF_HEADS * DIFF_DV
    outs = [(1, cc, BF16), (1, hd, BF16), (1, hd, BF16), (DIFF_HEADS, DIFF_DV, F32),
            (DIFF_HEADS, DIFF_DV, F32), (1, hd, BF16), (1, hd, BF16), (1, LANES, F32)]
    return pl.pallas_call(
        functools.partial(_proj_kernel, rows=rows, n_seq=n_seq, tiles_per_seq=tiles_per_seq),
        grid=(n // tm,),
        in_specs=[pl.BlockSpec((tm, d), row), pl.BlockSpec((1, d), fix), pl.BlockSpec((1, d), fix),
                  pl.BlockSpec(w_main.shape, fix), pl.BlockSpec(w_ab.shape, fix),
                  pl.BlockSpec((n_seq, CONV_PAD, cc), seq), pl.BlockSpec((CONV_W, cc), fix)],
        out_specs=[pl.BlockSpec((tm * r, c), row) for r, c, _ in outs]
                  + [pl.BlockSpec((n_seq, CONV_PAD, cc), seq)],
        out_shape=[jax.ShapeDtypeStruct((n * r, c), t) for r, c, t in outs]
                  + [jax.ShapeDtypeStruct(conv_prev8.shape, F32)],
        scratch_shapes=[pltpu.VMEM((n_seq * (rows + CONV_PAD), cc), F32)],
        compiler_params=_params("arbitrary"),
        name="proj",
    )(x2, ln_g, ln_b, w_main, w_ab, conv_prev8, conv_w)


def _gdn_kernel(y_ref, z_ref, ab_ref, s0_ref, alog_ref, dtb_ref, nw_ref,
                o_ref, sn_ref,
                s_scr, cum_scr, a_scr, p_scr, attn_scr, rhs_scr, sol_scr, qg_scr, kdt_scr, u_scr, os_scr,
                *, nb, tl, c):
    l = pl.program_id(1)
    d = GDN_D
    hw = GDN_HEADS * d

    @pl.when(l == 0)
    def _():
        s_scr[...] = s0_ref[...]

    g_all = [-jnp.exp(alog_ref[...]) * _softplus(ab_ref[bi] + dtb_ref[...]) for bi in range(nb)]
    beta_all = [_sigmoid(ab_ref[bi]) for bi in range(nb)]
    nw = nw_ref[...]

    rows = lax.broadcasted_iota(jnp.int32, (c, c), 0)
    cols = lax.broadcasted_iota(jnp.int32, (c, c), 1)
    incl = cols <= rows
    strict = cols < rows
    tri = jnp.where(incl, 1.0, 0.0).astype(BF16)
    grow = lax.broadcasted_iota(jnp.int32, (c, LANES), 0)
    glane = lax.broadcasted_iota(jnp.int32, (c, LANES), 1)

    n_chunks = tl // c
    groups = [(ci, bi, h) for ci in range(n_chunks) for bi in range(nb) for h in range(GDN_HEADS)]
    eye = jnp.where(rows == cols, 1.0, 0.0)

    gmat = jnp.concatenate(
        [jnp.where(glane < c, jnp.where(grow > glane, g_all[bi][ci * c:(ci + 1) * c, h:h + 1], 0.0),
                   g_all[bi][ci * c:(ci + 1) * c, h:h + 1]) for ci, bi, h in groups], axis=1)
    g_hi, g_mid, g_lo = _split3(gmat)
    cum_scr[...] = _dot(tri, g_hi) + _dot(tri, g_mid) + _dot(tri, g_lo)

    for g, (ci, bi, h) in enumerate(groups):
        r0 = ci * c
        beta = beta_all[bi][r0:r0 + c, GDN_HEADS + h:GDN_HEADS + h + 1]
        rel = cum_scr[:, g * LANES:g * LANES + c]
        gam = cum_scr[:, g * LANES + c:g * LANES + c + 1]
        g_last = gam[c - 1:c, :]
        decay = jnp.where(incl, jnp.exp(jnp.where(incl, rel, 0.0)), 0.0)
        eg = jnp.exp(gam)
        ek = jnp.exp(g_last - gam)

        qh = y_ref[bi, r0:r0 + c, h * d:(h + 1) * d].astype(F32)
        kh = y_ref[bi, r0:r0 + c, hw + h * d:hw + (h + 1) * d].astype(F32)
        vh = y_ref[bi, r0:r0 + c, 2 * hw + h * d:2 * hw + (h + 1) * d].astype(F32)
        qn = qh * lax.rsqrt(jnp.sum(qh * qh, axis=-1, keepdims=True) + 1e-6) * (d ** -0.5)
        kn = kh * lax.rsqrt(jnp.sum(kh * kh, axis=-1, keepdims=True) + 1e-6)
        qb = qn.astype(BF16)
        kb = kn.astype(BF16)
        a = jnp.where(strict, beta * _dot_nt(kb, kb) * decay, 0.0)
        a_scr[g] = a
        p_scr[g] = eye - a
        attn_scr[g] = (_dot_nt(qb, kb) * decay).astype(BF16)
        rhs_scr[g, :, 0:d] = (vh * beta).astype(BF16)
        rhs_scr[g, :, d:2 * d] = (kn * (beta * eg)).astype(BF16)
        qg_scr[g] = (qn * eg).astype(BF16)
        kdt_scr[g] = jnp.transpose(kn * ek).astype(BF16)

    n_sq = int(math.log2(c)) - 1
    for it in range(n_sq):
        for g in range(len(groups)):
            akb = a_scr[g].astype(BF16)
            a_scr[g] = _dot(akb, akb)
        for g in range(len(groups)):
            p = p_scr[g]
            p_scr[g] = p + _dot(p.astype(BF16), a_scr[g].astype(BF16))

    for g in range(len(groups)):
        sol_scr[g] = _dot(p_scr[g].astype(BF16), rhs_scr[g])

    for g, (ci, bi, h) in enumerate(groups):
        sb = s_scr[bi, h].astype(BF16)
        u_scr[g] = (sol_scr[g, :, 0:d] - _dot(sol_scr[g, :, d:2 * d].astype(BF16), sb)).astype(BF16)
        os_scr[g] = _dot(qg_scr[g], sb)

    for g, (ci, bi, h) in enumerate(groups):
        r0 = ci * c
        ub = u_scr[g]
        o = os_scr[g] + _dot(attn_scr[g], ub)
        g_last = cum_scr[c - 1:c, g * LANES + c:g * LANES + c + 1]
        s_scr[bi, h] = s_scr[bi, h] * jnp.exp(g_last) + _dot(kdt_scr[g], ub)

        zh = z_ref[bi, r0:r0 + c, h * d:(h + 1) * d].astype(F32)
        on = o * lax.rsqrt(jnp.mean(o * o, axis=-1, keepdims=True) + 1e-6) * nw
        o_ref[bi, r0:r0 + c, h * d:(h + 1) * d] = (on * _silu(zh)).astype(BF16)

    @pl.when(l == pl.num_programs(1) - 1)
    def _():
        sn_ref[...] = s_scr[...]


def _gdn(y, z, ab, s0, alog_row, dtb_row, nw_row):
    b, l, cc = y.shape
    c = min(l, CHUNK)
    tl = min(l, GDN_TILE)
    nb = math.gcd(b, GDN_SEQS)
    assert tl == c, "one chunk per grid step: the state recurrence is staged across groups"
    ng = nb * GDN_HEADS
    hw = GDN_HEADS * GDN_D
    tile = lambda i, j: (i, j, 0)
    fix = lambda i, j: (0, 0)
    return pl.pallas_call(
        functools.partial(_gdn_kernel, nb=nb, tl=tl, c=c),
        grid=(b // nb, l // tl),
        in_specs=[pl.BlockSpec((nb, tl, cc), tile), pl.BlockSpec((nb, tl, hw), tile),
                  pl.BlockSpec((nb, tl, LANES), tile),
                  pl.BlockSpec((nb, GDN_HEADS, GDN_D, GDN_D), lambda i, j: (i, 0, 0, 0)),
                  pl.BlockSpec((1, LANES), fix),
                  pl.BlockSpec((1, LANES), fix), pl.BlockSpec((1, GDN_D), fix)],
        out_specs=[pl.BlockSpec((nb, tl, hw), tile),
                   pl.BlockSpec((nb, GDN_HEADS, GDN_D, GDN_D), lambda i, j: (i, 0, 0, 0))],
        out_shape=[jax.ShapeDtypeStruct((b, l, hw), BF16),
                   jax.ShapeDtypeStruct((b, GDN_HEADS, GDN_D, GDN_D), F32)],
        scratch_shapes=[pltpu.VMEM((nb, GDN_HEADS, GDN_D, GDN_D), F32),
                        pltpu.VMEM((c, ng * LANES), F32), pltpu.VMEM((ng, c, c), F32),
                        pltpu.VMEM((ng, c, c), F32), pltpu.VMEM((ng, c, c), BF16),
                        pltpu.VMEM((ng, c, 2 * GDN_D), BF16), pltpu.VMEM((ng, c, 2 * GDN_D), F32),
                        pltpu.VMEM((ng, c, GDN_D), BF16), pltpu.VMEM((ng, GDN_D, c), BF16),
                        pltpu.VMEM((ng, c, GDN_D), BF16), pltpu.VMEM((ng, c, GDN_D), F32)],
        compiler_params=_params("parallel", "arbitrary"),
        name="gdn",
    )(y, z, ab, s0, alog_row, dtb_row, nw_row)


def _lambda(lamv, lam_init):
    l1 = jnp.sum(lamv[0:1] * lamv[1:2], axis=-1, keepdims=True)
    l2 = jnp.sum(lamv[2:3] * lamv[3:4], axis=-1, keepdims=True)
    return jnp.exp(l1) - jnp.exp(l2) + lam_init


def _attn_bias(q0, k0, nq, nk, slope):
    qpos = q0 + lax.broadcasted_iota(jnp.int32, (nq, nk), 0)
    kpos = k0 + lax.broadcasted_iota(jnp.int32, (nq, nk), 1)
    dist = jnp.abs(qpos - kpos).astype(F32)
    shift = int(math.log2(CHUNK))
    allowed = (kpos >> shift) <= (qpos >> shift)
    return jnp.where(allowed, -slope * dist, NEG_INF)


def _attn_init(stats):
    for m, l, a in stats:
        m[...] = jnp.full(m.shape, NEG_INF, F32)
        l[...] = jnp.zeros(l.shape, F32)
        a[...] = jnp.zeros(a.shape, F32)


def _attn_update(q, k, v, bias, stats):
    for i, (m, l, a) in enumerate(stats):
        lo, hi = i * DIFF_DH, (i + 1) * DIFF_DH
        s = _dot_nt(q[:, lo:hi], k[:, lo:hi]) + bias
        m_new = jnp.maximum(m[...], jnp.max(s, axis=-1, keepdims=True))
        alpha = jnp.exp(m[...] - m_new)
        p = jnp.exp(s - m_new)
        l[...] = alpha * l[...] + jnp.sum(p, axis=-1, keepdims=True)
        a[...] = alpha * a[...] + _dot(p.astype(BF16), v)
        m[...] = m_new


def _attn_finish(stats, lam, sw, lam_init):
    (_, l1, a1), (_, l2, a2) = stats
    o = a1[...] / l1[...] - lam * (a2[...] / l2[...])
    return o * lax.rsqrt(jnp.mean(o * o, axis=-1, keepdims=True) + 1e-6) * sw * (1.0 - lam_init)


def _attn_prompt_kernel(slopes_ref, lamv_ref, sw_ref, q_ref, k_ref, v_ref, o_ref,
                        vx_scr, bias_scr, q_scr, m_scr, acc_scr, sa_scr, sb_scr, pa_scr, pb_scr, ala_scr, alb_scr,
                        *, seq, tile, lam_init):
    t = tile
    r = 2 * t
    rb = ATTN_ROWS
    dv = DIFF_DV
    slope = slopes_ref[pl.program_id(1)] * LOG2E
    q_scale = (DIFF_DH ** -0.5) * LOG2E
    lam = _lambda(lamv_ref[...], lam_init)
    sw = sw_ref[...]
    shift = int(math.log2(CHUNK))

    vx_scr[:, 0:dv] = v_ref[0]
    vx_scr[:, dv:2 * dv] = jnp.ones((seq, dv), BF16)
    qi_ = lax.broadcasted_iota(jnp.int32, (t, t), 0)
    kj_ = lax.broadcasted_iota(jnp.int32, (t, t), 1)
    rel = (qi_ - kj_).astype(F32)
    bias_scr[0] = -slope * rel
    bias_scr[1] = jnp.where((kj_ >> shift) <= (qi_ >> shift), -slope * jnp.abs(rel), NEG_INF)
    lane = lax.broadcasted_iota(jnp.int32, (t, dv), 1)
    frame_shift = slope * t

    def finalize(qi):
        q0 = qi * t
        o1 = acc_scr[0:t, 0:dv] / acc_scr[0:t, dv:2 * dv]
        o2 = acc_scr[t:r, 0:dv] / acc_scr[t:r, dv:2 * dv]
        o = o1 - lam * o2
        on = o * lax.rsqrt(jnp.mean(o * o, axis=-1, keepdims=True) + 1e-6) * sw * (1.0 - lam_init)
        o_ref[0, q0:q0 + t, :] = on.astype(BF16)

    def q_body(qi):
        q0 = qi * t
        qf = q_ref[0, q0:q0 + t, :].astype(F32) * q_scale
        q_scr[0:t, :] = jnp.where(lane < DIFF_DH, qf, 0.0).astype(BF16)
        q_scr[t:r, :] = jnp.where(lane >= DIFF_DH, qf, 0.0).astype(BF16)
        m_scr[...] = jnp.full(m_scr.shape, NEG_INF, F32)
        if qi > 0:
            finalize(qi - 1)
        acc_scr[...] = jnp.zeros(acc_scr.shape, F32)

        def qk(kj, s_buf):
            k0 = min(kj, qi) * t
            s_buf[...] = _dot_nt(q_scr[...], k_ref[0, k0:k0 + t, :])

        def softmax_accumulate(kj, s_buf, p_buf, al_buf):
            k0 = kj * t
            diag = int(kj == qi)
            for i in range(r // rb):
                r0 = i * rb
                b_lo = r0 % t
                s = s_buf[r0:r0 + rb, :] + bias_scr[diag, b_lo:b_lo + rb, :]
                cols = [s[:, j * LANES:(j + 1) * LANES] for j in range(t // LANES)]
                m_prev = m_scr[r0:r0 + rb, :] - frame_shift
                m_cur = jnp.max(functools.reduce(jnp.maximum, cols), axis=-1, keepdims=True)
                m_new = jnp.maximum(m_prev, m_cur)
                al_buf[r0:r0 + rb, :] = jnp.exp2(m_prev - m_new)
                for j, col in enumerate(cols):
                    p_buf[r0:r0 + rb, j * LANES:(j + 1) * LANES] = jnp.exp2(col - m_new).astype(BF16)
                m_scr[r0:r0 + rb, :] = m_new
            pv = _dot(p_buf[...], vx_scr[k0:k0 + t, :])
            for i in range(r // rb):
                r0 = i * rb
                al = al_buf[r0:r0 + rb, :]
                acc_scr[r0:r0 + rb, :] = (jnp.concatenate([al, al], axis=1) * acc_scr[r0:r0 + rb, :]
                                          + pv[r0:r0 + rb, :])

        sets = ((sa_scr, pa_scr, ala_scr), (sb_scr, pb_scr, alb_scr))
        qk(0, sa_scr)
        for kj in range(qi + 1):
            qk(kj + 1, sets[(kj + 1) % 2][0])
            softmax_accumulate(kj, *sets[kj % 2])

    for qi in range(seq // t):
        q_body(qi)
    finalize(seq // t - 1)


def _attn_prompt(q, kb, vb, slopes, lamv, sw_row, lam_init):
    b, l, _ = q.shape
    t = min(ATTN_TILE, l)
    assert t % CHUNK == 0 and l % t == 0 and t % LANES == 0 and t % ATTN_ROWS == 0
    head = lambda i, h: (i, 0, h)
    fix = lambda i, h: (0, 0)
    dv = DIFF_DV
    scratch = [pltpu.VMEM((l, 2 * dv), BF16), pltpu.VMEM((2, t, t), F32),
               pltpu.VMEM((2 * t, dv), BF16), pltpu.VMEM((2 * t, LANES), F32),
               pltpu.VMEM((2 * t, 2 * dv), F32),
               pltpu.VMEM((2 * t, t), F32), pltpu.VMEM((2 * t, t), F32),
               pltpu.VMEM((2 * t, t), BF16), pltpu.VMEM((2 * t, t), BF16),
               pltpu.VMEM((2 * t, LANES), F32), pltpu.VMEM((2 * t, LANES), F32)]
    return pl.pallas_call(
        functools.partial(_attn_prompt_kernel, seq=l, tile=t, lam_init=lam_init),
        grid=(b, DIFF_HEADS),
        in_specs=[pl.BlockSpec(memory_space=pltpu.SMEM),
                  pl.BlockSpec(lamv.shape, fix), pl.BlockSpec(sw_row.shape, fix),
                  pl.BlockSpec((1, l, dv), head), pl.BlockSpec((1, l, dv), head),
                  pl.BlockSpec((1, l, dv), head)],
        out_specs=pl.BlockSpec((1, l, dv), head),
        out_shape=jax.ShapeDtypeStruct((b, l, DIFF_HEADS * dv), BF16),
        scratch_shapes=scratch,
        compiler_params=_params("parallel", "parallel"),
        name="attn_prompt",
    )(slopes, lamv, sw_row, q, kb, vb)


def _attn_sample_kernel(slopes_ref, lamv_ref, sw_ref, q_ref, kc_ref, vc_ref, kn_ref, vn_ref, o_ref,
                        m1, l1, a1, m2, l2, a2, *, past, tk, lam_init):
    j = pl.program_id(1)
    n_cache = past // tk
    lq = q_ref.shape[1]
    heads = range(DIFF_HEADS)
    cols = [slice(h * DIFF_DV, (h + 1) * DIFF_DV) for h in heads]
    stats = [((m1.at[h], l1.at[h], a1.at[h]), (m2.at[h], l2.at[h], a2.at[h])) for h in heads]
    qs = [(q_ref[0, :, cols[h]].astype(F32) * (DIFF_DH ** -0.5)).astype(BF16) for h in heads]

    @pl.when(j == 0)
    def _():
        for h in heads:
            _attn_init(stats[h])

    @pl.when(j < n_cache)
    def _():
        for h in heads:
            k = kc_ref[0, pl.ds(h, tk, stride=DIFF_HEADS), :].astype(BF16)
            v = vc_ref[0, pl.ds(h, tk, stride=DIFF_HEADS), :].astype(BF16)
            _attn_update(qs[h], k, v, _attn_bias(past, j * tk, lq, tk, slopes_ref[h]), stats[h])

    @pl.when(j == n_cache)
    def _():
        lam = _lambda(lamv_ref[...], lam_init)
        for h in heads:
            _attn_update(qs[h], kn_ref[0, :, cols[h]], vn_ref[0, :, cols[h]],
                         _attn_bias(past, past, lq, lq, slopes_ref[h]), stats[h])
            o_ref[0, :, cols[h]] = _attn_finish(stats[h], lam, sw_ref[...], lam_init).astype(BF16)


def _attn_sample(q, kb, vb, k_cache, v_cache, slopes, lamv, sw_row, lam_init):
    b, l, hd = q.shape
    past = k_cache.shape[1] // DIFF_HEADS
    tk = min(CACHE_TILE, past)
    assert past % tk == 0 and past % CHUNK == 0 and l <= CHUNK
    n_cache = past // tk
    per_b = lambda i, j: (i, 0, 0)
    cache = lambda i, j: (i, jnp.minimum(j, n_cache - 1), 0)
    fix = lambda i, j: (0, 0)
    stat = [pltpu.VMEM((DIFF_HEADS, l, 1), F32), pltpu.VMEM((DIFF_HEADS, l, 1), F32),
            pltpu.VMEM((DIFF_HEADS, l, DIFF_DV), F32)]
    return pl.pallas_call(
        functools.partial(_attn_sample_kernel, past=past, tk=tk, lam_init=lam_init),
        grid=(b, n_cache + 1),
        in_specs=[pl.BlockSpec(memory_space=pltpu.SMEM),
                  pl.BlockSpec(lamv.shape, fix), pl.BlockSpec(sw_row.shape, fix),
                  pl.BlockSpec((1, l, hd), per_b),
                  pl.BlockSpec((1, tk * DIFF_HEADS, DIFF_DV), cache),
                  pl.BlockSpec((1, tk * DIFF_HEADS, DIFF_DV), cache),
                  pl.BlockSpec((1, l, hd), per_b), pl.BlockSpec((1, l, hd), per_b)],
        out_specs=pl.BlockSpec((1, l, hd), per_b),
        out_shape=jax.ShapeDtypeStruct((b, l, hd), BF16),
        scratch_shapes=stat + stat,
        compiler_params=_params("parallel", "arbitrary"),
        name="attn_sample",
    )(slopes, lamv, sw_row, q, k_cache, v_cache, kb, vb)


def _mix_kernel(x_ref, og_ref, od_ref, lig_ref, lib_ref, wo_ref, g1_ref, b1_ref, rw_ref, rb_ref, tri_ref,
                x1_ref, x1b_ref, route_ref, cnt_ref, cnt_scr, *, alpha):
    xn = _layernorm(x_ref[...], lig_ref[...], lib_ref[...])
    hw = og_ref.shape[1]
    mixed = _dot(og_ref[...], wo_ref[0:hw, :]) + _dot(od_ref[...], wo_ref[hw:, :])
    x1 = _layernorm(alpha * xn + mixed, g1_ref[...], b1_ref[...])
    x1_ref[...] = x1
    x1b_ref[...] = x1.astype(BF16)

    x_hi = x1.astype(BF16)
    x_lo = (x1 - x_hi.astype(F32)).astype(BF16)
    lg = (_dot(x_hi, rw_ref[0]) + _dot(x_lo, rw_ref[0]) + _dot(x_hi, rw_ref[1])) + rb_ref[...]
    lane = lax.broadcasted_iota(jnp.int32, lg.shape, 1)
    lanef = lane.astype(F32)
    big = float(LANES)

    gl = jnp.where(lane < N_GROUPS, lg, -jnp.inf)
    gmax = jnp.max(gl, axis=-1, keepdims=True)
    grp = jnp.min(jnp.where(gl == gmax, lanef, big), axis=-1, keepdims=True)
    g_gate = 1.0 / jnp.sum(jnp.where(lane < N_GROUPS, jnp.exp(gl - gmax), 0.0), axis=-1, keepdims=True)

    e_lo = N_GROUPS + grp * EXPERTS_PER_GROUP
    el = jnp.where((lanef >= e_lo) & (lanef < e_lo + EXPERTS_PER_GROUP), lg, -jnp.inf)
    v1 = jnp.max(el, axis=-1, keepdims=True)
    i1 = jnp.min(jnp.where(el == v1, lanef, big), axis=-1, keepdims=True)
    el2 = jnp.where(lanef == i1, -jnp.inf, el)
    v2 = jnp.max(el2, axis=-1, keepdims=True)
    i2 = jnp.min(jnp.where(el2 == v2, lanef, big), axis=-1, keepdims=True)
    e21 = jnp.exp(v2 - v1)
    p1 = 1.0 / (1.0 + e21)

    @pl.when(pl.program_id(0) == 0)
    def _():
        cnt_scr[...] = jnp.zeros(cnt_scr.shape, F32)

    e0 = i1 - N_GROUPS
    e1 = i2 - N_GROUPS
    hot0 = lanef == e0
    hot1 = lanef == e1
    onehot = jnp.where(hot0 | hot1, 1.0, 0.0)
    before = _dot(tri_ref[...], onehot.astype(BF16)) + cnt_scr[...]
    rank0 = jnp.sum(jnp.where(hot0, before, 0.0), axis=-1, keepdims=True)
    rank1 = jnp.sum(jnp.where(hot1, before, 0.0), axis=-1, keepdims=True)
    tm = onehot.shape[0]
    cnt_scr[...] = before[tm - 1:tm, :] + onehot[tm - 1:tm, :]
    cnt_ref[...] = cnt_scr[...]

    route_ref[...] = jnp.where(lane == 0, e0,
                     jnp.where(lane == 1, e1,
                     jnp.where(lane == 2, g_gate * p1,
                     jnp.where(lane == 3, g_gate * (e21 * p1),
                     jnp.where(lane == 4, rank0,
                     jnp.where(lane == 5, rank1, 0.0))))))


def _mix(x2, og, od, li_g, li_b, wo, g1, b1, rw, rb, alpha):
    n, d = x2.shape
    tm = min(TOKEN_TILE, n)
    row = lambda i: (i, 0)
    fix = lambda i: (0, 0)
    tri = jnp.tri(tm, k=-1, dtype=BF16)
    return pl.pallas_call(
        functools.partial(_mix_kernel, alpha=alpha),
        grid=(n // tm,),
        in_specs=[pl.BlockSpec((tm, d), row), pl.BlockSpec((tm, og.shape[1]), row),
                  pl.BlockSpec((tm, od.shape[1]), row),
                  pl.BlockSpec((1, d), fix), pl.BlockSpec((1, d), fix), pl.BlockSpec(wo.shape, fix),
                  pl.BlockSpec((1, d), fix), pl.BlockSpec((1, d), fix),
                  pl.BlockSpec(rw.shape, lambda i: (0, 0, 0)), pl.BlockSpec((1, LANES), fix),
                  pl.BlockSpec((tm, tm), fix)],
        out_specs=[pl.BlockSpec((tm, d), row), pl.BlockSpec((tm, d), row), pl.BlockSpec((tm, LANES), row),
                   pl.BlockSpec((1, LANES), fix)],
        out_shape=[jax.ShapeDtypeStruct((n, d), F32), jax.ShapeDtypeStruct((n, d), BF16),
                   jax.ShapeDtypeStruct((n, LANES), F32), jax.ShapeDtypeStruct((1, LANES), F32)],
        scratch_shapes=[pltpu.VMEM((1, LANES), F32)],
        compiler_params=_params("arbitrary"),
        name="mix",
    )(x2, og, od, li_g, li_b, wo, g1, b1, rw, rb, tri)


def _expert_kernel(be_ref, nu_ref, x_ref, w1_ref, w3_ref, w2_ref, y_ref, w1b, w3b, w2b):
    i = pl.program_id(0)

    @pl.when((i == 0) | (be_ref[i] != be_ref[jnp.maximum(i - 1, 0)]))
    def _():
        w1b[...] = w1_ref[0].astype(BF16)
        w3b[...] = w3_ref[0].astype(BF16)
        w2b[...] = w2_ref[0].astype(BF16)

    @pl.when(i < nu_ref[0])
    def _():
        x = x_ref[...]
        h1 = _dot(x, w1b[...])
        h3 = _dot(x, w3b[...])
        y_ref[...] = _dot((_silu(h1) * h3).astype(BF16), w2b[...]).astype(BF16)

    @pl.when(i >= nu_ref[0])
    def _():
        y_ref[...] = jnp.zeros(y_ref.shape, y_ref.dtype)


def _expert_block(n_tokens):
    per_expert = max(1, n_tokens * TOP_K // N_EXPERTS)
    return int(min(EXPERT_BLOCK_MAX, max(EXPERT_BLOCK_MIN, 2 ** int(math.log2(per_expert)))))


def _experts(xb, block_expert, n_used, w1, w3, w2, blk):
    p_pad, d = xb.shape
    de = w1.shape[2]
    row = lambda i, be, nu: (i, 0)
    grid_spec = pltpu.PrefetchScalarGridSpec(
        num_scalar_prefetch=2,
        grid=(p_pad // blk,),
        in_specs=[pl.BlockSpec((blk, d), row),
                  pl.BlockSpec((1, d, de), lambda i, be, nu: (be[i], 0, 0)),
                  pl.BlockSpec((1, d, de), lambda i, be, nu: (be[i], 0, 0)),
                  pl.BlockSpec((1, de, d), lambda i, be, nu: (be[i], 0, 0))],
        out_specs=pl.BlockSpec((blk, d), row),
        scratch_shapes=[pltpu.VMEM((d, de), BF16), pltpu.VMEM((d, de), BF16), pltpu.VMEM((de, d), BF16)],
    )
    return pl.pallas_call(
        _expert_kernel,
        grid_spec=grid_spec,
        out_shape=jax.ShapeDtypeStruct((p_pad, d), BF16),
        compiler_params=_params("arbitrary"),
        name="experts",
    )(block_expert, n_used, xb, w1, w3, w2)


def _dispatch_plan(plan, counts_row, blk):
    n = plan.shape[1]
    p = n * TOP_K
    n_blocks = -(-p // blk) + N_EXPERTS
    counts = counts_row[0, 0:N_EXPERTS].astype(jnp.int32)
    starts = jnp.cumsum(counts) - counts
    padded = (counts + blk - 1) // blk * blk
    padded_end = jnp.cumsum(padded)
    padded_start = padded_end - padded
    expert = plan[0:TOP_K].astype(jnp.int32)
    rank = plan[4:4 + TOP_K].astype(jnp.int32)
    dest = rank + padded_start[expert]
    token = jnp.broadcast_to(jnp.arange(n, dtype=jnp.int32), (TOP_K, n))
    _, order_tok = lax.sort((dest.reshape(p), token.reshape(p)), num_keys=1)
    block_start = jnp.arange(n_blocks, dtype=jnp.int32) * blk
    block_expert = jnp.minimum(
        jnp.sum((padded_end[None, :] <= block_start[:, None]).astype(jnp.int32), axis=1), N_EXPERTS - 1)
    row_id = jnp.arange(n_blocks * blk, dtype=jnp.int32).reshape(n_blocks, blk)
    j = row_id - padded_start[block_expert][:, None]
    src = jnp.clip(starts[block_expert][:, None] + j, 0, p - 1)
    valid = (j < counts[block_expert][:, None]).reshape(-1)
    tok_pad = jnp.where(valid, order_tok[src.reshape(-1)], row_id.reshape(-1) % n)
    n_used = (padded_end[-1:] // blk).astype(jnp.int32)
    return tok_pad, dest, block_expert, n_used


def _final_kernel(x1_ref, y0_ref, y1_ref, route_ref, g_ref, b_ref, o_ref, *, alpha):
    route = route_ref[...]
    moe = route[:, 2:3] * y0_ref[...].astype(F32) + route[:, 3:4] * y1_ref[...].astype(F32)
    o_ref[...] = _layernorm(alpha * x1_ref[...] + moe, g_ref[...], b_ref[...])


def _final(x1, y0, y1, route, g2, b2, alpha):
    n, d = x1.shape
    tm = min(TOKEN_TILE, n)
    row = lambda i: (i, 0)
    fix = lambda i: (0, 0)
    return pl.pallas_call(
        functools.partial(_final_kernel, alpha=alpha),
        grid=(n // tm,),
        in_specs=[pl.BlockSpec((tm, d), row), pl.BlockSpec((tm, d), row), pl.BlockSpec((tm, d), row),
                  pl.BlockSpec((tm, LANES), row), pl.BlockSpec((1, d), fix), pl.BlockSpec((1, d), fix)],
        out_specs=pl.BlockSpec((tm, d), row),
        out_shape=jax.ShapeDtypeStruct((n, d), F32),
        compiler_params=_params("parallel"),
        name="final",
    )(x1, y0, y1, route, g2, b2)


def _pad_lanes(v, width=LANES):
    v = v.reshape(1, -1).astype(F32)
    return jnp.pad(v, ((0, 0), (0, width - v.shape[1])))


def _layer(x, conv_prev, s_prev, k_cache, v_cache, p, alpha, lam_init):
    b, l, d = x.shape
    n = b * l
    x2 = x.reshape(n, d)
    conv_prev8 = jnp.pad(conv_prev.astype(F32), ((0, 0), (CONV_PAD - (CONV_W - 1), 0), (0, 0)))
    yc, z, q, k, v, kb, vb, ab, convn = _proj(x2, l, p["ln_in_g"], p["ln_in_b"], p["w_main"], p["w_ab"],
                                              conv_prev8, p["conv_w"])
    conv_new = convn[:, CONV_PAD - (CONV_W - 1):, :]
    og, s_new = _gdn(yc.reshape(b, l, -1), z.reshape(b, l, -1), ab.reshape(b, l, LANES),
                     s_prev.astype(F32), p["alog_row"], p["dtb_row"], p["nw_row"])

    q3 = q.reshape(b, l, -1)
    kb3 = kb.reshape(b, l, -1)
    vb3 = vb.reshape(b, l, -1)
    if k_cache is None:
        od = _attn_prompt(q3, kb3, vb3, p["slopes"], p["lamv"], p["sw_row"], lam_init)
    else:
        past = k_cache.shape[1]
        od = _attn_sample(q3, kb3, vb3, k_cache.reshape(b, past * DIFF_HEADS, DIFF_DV),
                          v_cache.reshape(b, past * DIFF_HEADS, DIFF_DV),
                          p["slopes"], p["lamv"], p["sw_row"], lam_init)

    x1, x1b, route, counts_row = _mix(x2, og.reshape(n, -1), od.reshape(n, -1), p["ln_in_g"], p["ln_in_b"],
                                      p["wo"], p["ln1_g"], p["ln1_b"], p["rw"], p["rb"], alpha)

    blk = _expert_block(n)
    plan = jnp.transpose(route[:, 0:PLAN_ROWS])
    tok_pad, dest, block_expert, n_used = _dispatch_plan(plan, counts_row, blk)
    xb = jnp.take(x1b, tok_pad, axis=0, mode="clip")
    yb = _experts(xb, block_expert, n_used, p["w1"], p["w3"], p["w2"], blk)
    y0 = jnp.take(yb, dest[0], axis=0, mode="clip")
    y1 = jnp.take(yb, dest[1], axis=0, mode="clip")
    y = _final(x1, y0, y1, route, p["ln2_g"], p["ln2_b"], alpha)

    return (y.reshape(b, l, d), conv_new, s_new,
            k.reshape(b, l, DIFF_HEADS, 2 * DIFF_DH), v.reshape(b, l, DIFF_HEADS, DIFF_DV))


def kernel(x_prompt, x_sample, cache_attn_k, cache_attn_v, state_gdn, state_conv, ln_in_g, ln_in_b, w_in, conv_w, gdn_a_log, gdn_dt_bias, gdn_norm_w, lam_q1, lam_k1, lam_q2, lam_k2, subln_w, w_o, ln1_g, ln1_b, router_g_w, router_g_b, router_e_w, router_e_b, w1, w3, w2, ln2_g, ln2_b):
    depth = w_in.shape[0]
    assert depth == 1, "single-layer step"
    d = x_prompt.shape[-1]
    alpha = (2 * depth) ** 0.25
    lam_init = 0.8 - 0.6 * math.exp(-0.3 * 0)
    row = lambda t: t.reshape(1, -1).astype(F32)

    wi = w_in[0]
    conv_ch = conv_w.shape[-1]
    gw = GDN_HEADS * GDN_D
    c_ab = conv_ch + gw
    c_q = c_ab + 2 * GDN_HEADS
    w_main = jnp.concatenate([wi[:, :c_ab], wi[:, c_q:]], axis=1).astype(BF16)
    w_ab = jnp.pad(wi[:, c_ab:c_q], ((0, 0), (0, LANES - 2 * GDN_HEADS))).astype(BF16)
    rcat = jnp.concatenate([router_g_w[0], router_e_w[0]], axis=1)
    rcat = jnp.pad(rcat, ((0, 0), (0, LANES - rcat.shape[1])))
    r_hi = rcat.astype(BF16)
    r_lo = (rcat - r_hi.astype(F32)).astype(BF16)
    p = {
        "ln_in_g": row(ln_in_g), "ln_in_b": row(ln_in_b), "w_main": w_main, "w_ab": w_ab,
        "conv_w": conv_w[0].astype(F32), "alog_row": _pad_lanes(gdn_a_log[0]),
        "dtb_row": _pad_lanes(gdn_dt_bias[0]), "nw_row": row(gdn_norm_w[0]),
        "slopes": jnp.asarray(2.0 ** (-8.0 * (np.arange(DIFF_HEADS) + 1) / DIFF_HEADS), F32),
        "lamv": jnp.stack([lam_q1[0], lam_k1[0], lam_q2[0], lam_k2[0]]).astype(F32),
        "sw_row": row(subln_w[0]), "wo": w_o[0].astype(BF16),
        "ln1_g": row(ln1_g[0]), "ln1_b": row(ln1_b[0]),
        "rw": jnp.stack([r_hi, r_lo]), "rb": _pad_lanes(jnp.concatenate([router_g_b[0], router_e_b[0]])),
        "w1": w1[0], "w3": w3[0], "w2": w2[0],
        "ln2_g": row(ln2_g[0]), "ln2_b": row(ln2_b[0]),
    }

    bp = x_prompt.shape[0]
    conv0 = jnp.zeros((bp, CONV_W - 1, conv_ch), F32)
    s0 = jnp.zeros((bp, GDN_HEADS, GDN_D, GDN_D), F32)
    yp, cp, sp, kp, vp = _layer(x_prompt, conv0, s0, None, None, p, alpha, lam_init)
    ys, cs, ss, ks, vs = _layer(x_sample, state_conv[0], state_gdn[0], cache_attn_k[0], cache_attn_v[0],
                                p, alpha, lam_init)
    return (yp, ys, kp[None], vp[None], sp[None], cp[None], ks[None], vs[None], ss[None], cs[None])
```

```python
import functools
import math

import jax
import jax.numpy as jnp
import numpy as np
from jax import lax
from jax.experimental import pallas as pl
from jax.experimental.pallas import tpu as pltpu

F32 = jnp.float32
BF16 = jnp.bfloat16

CHUNK = 64
CONV_W = 4
GDN_HEADS = 4
GDN_D = 128
DIFF_HEADS = 4
DIFF_DH = 64
DIFF_DV = 2 * DIFF_DH
N_GROUPS = 4
EXPERTS_PER_GROUP = 8
N_EXPERTS = N_GROUPS * EXPERTS_PER_GROUP
TOP_K = 2
NEG_INF = -1e30
LOG2E = math.log2(math.e)
LANES = 128
CONV_PAD = 8

TOKEN_TILE = 512
GDN_TILE = 64
GDN_SEQS = 4
ATTN_TILE = 256
ATTN_ROWS = 64
CACHE_TILE = 1024
EXPERT_BLOCK_MAX = 512
EXPERT_BLOCK_MIN = 128
PLAN_ROWS = 8
VMEM_LIMIT = 48 * 1024 * 1024


def _params(*sem):
    return pltpu.CompilerParams(dimension_semantics=sem, vmem_limit_bytes=VMEM_LIMIT)


def _dot(a, b):
    return jnp.dot(a, b, preferred_element_type=F32)


def _dot_nt(a, b):
    return lax.dot_general(a, b, (((1,), (1,)), ((), ())), preferred_element_type=F32)


def _sigmoid(x):
    return 1.0 / (1.0 + jnp.exp(-x))


def _silu(x):
    return x * _sigmoid(x)


def _softplus(x):
    return jnp.maximum(x, 0.0) + jnp.log(1.0 + jnp.exp(-jnp.abs(x)))


def _layernorm(x, g, b, eps=1e-5):
    mu = jnp.mean(x, axis=-1, keepdims=True)
    xc = x - mu
    var = jnp.mean(xc * xc, axis=-1, keepdims=True)
    return xc * lax.rsqrt(var + eps) * g + b


def _split3(x):
    hi = x.astype(BF16)
    r1 = x - hi.astype(F32)
    mid = r1.astype(BF16)
    lo = (r1 - mid.astype(F32)).astype(BF16)
    return hi, mid, lo


def _proj_kernel(x_ref, g_ref, b_ref, w_ref, wab_ref, convp_ref, cw_ref,
                 y_ref, z_ref, q_ref, k_ref, v_ref, kb_ref, vb_ref, ab_ref, convn_ref,
                 xc_scr, *, rows, n_seq, tiles_per_seq):
    xb = _layernorm(x_ref[...], g_ref[...], b_ref[...]).astype(BF16)
    w = 512
    stride = rows + CONV_PAD
    if tiles_per_seq > 1:
        first = pl.program_id(0) % tiles_per_seq == 0

        @pl.when(first)
        def _():
            xc_scr[0:CONV_PAD, :] = convp_ref[0]

        @pl.when(jnp.logical_not(first))
        def _():
            xc_scr[0:CONV_PAD, :] = xc_scr[rows:rows + CONV_PAD, :]
    else:
        for s in range(n_seq):
            xc_scr[s * stride:s * stride + CONV_PAD, :] = convp_ref[s]
    cw = cw_ref[...]

    for j in range(3):
        cs = slice(j * w, (j + 1) * w)
        qkv = _dot(xb, w_ref[:, cs])
        for s in range(n_seq):
            base = s * stride + CONV_PAD
            xc_scr[base:base + rows, cs] = qkv[s * rows:(s + 1) * rows, :]
            lo = base - (CONV_W - 1)
            y = cw[0:1, cs] * xc_scr[lo:lo + rows, cs]
            for t in range(1, CONV_W):
                y = y + cw[t:t + 1, cs] * xc_scr[lo + t:lo + t + rows, cs]
            hy = 0.5 * y
            y_ref[s * rows:(s + 1) * rows, cs] = (hy + hy * jnp.tanh(hy)).astype(BF16)
    for s in range(n_seq):
        convn_ref[s] = xc_scr[s * stride + rows:s * stride + rows + CONV_PAD, :]
    z_ref[...] = _dot(xb, w_ref[:, 3 * w:4 * w]).astype(BF16)
    q_ref[...] = _dot(xb, w_ref[:, 4 * w:5 * w]).astype(BF16)
    tm = x_ref.shape[0]
    k = _dot(xb, w_ref[:, 5 * w:6 * w])
    kb_ref[...] = k.astype(BF16)
    v = _dot(xb, w_ref[:, 6 * w:7 * w])
    vb_ref[...] = v.astype(BF16)
    for h in range(DIFF_HEADS):
        k_ref[pl.ds(h, tm, stride=DIFF_HEADS), :] = k[:, h * DIFF_DV:(h + 1) * DIFF_DV]
        v_ref[pl.ds(h, tm, stride=DIFF_HEADS), :] = v[:, h * DIFF_DV:(h + 1) * DIFF_DV]
    ab_ref[...] = _dot(xb, wab_ref[...])


def _proj(x2, seq_len, ln_g, ln_b, w_main, w_ab, conv_prev8, conv_w):
    n, d = x2.shape
    tm = min(TOKEN_TILE, n)
    cc = conv_w.shape[1]
    rows = min(seq_len, tm)
    n_seq = tm // rows
    tiles_per_seq = seq_len // rows
    assert tm % rows == 0 and seq_len % rows == 0 and rows % CONV_PAD == 0
    row = lambda i: (i, 0)
    fix = lambda i: (0, 0)
    seq = lambda i: (i // tiles_per_seq, 0, 0)
    hd = DIFF_HEADS * DIFF_DV
    outs = [(1, cc, BF16), (1, hd, BF16), (1, hd, BF16), (DIFF_HEADS, DIFF_DV, F32),
            (DIFF_HEADS, DIFF_DV, F32), (1, hd, BF16), (1, hd, BF16), (1, LANES, F32)]
    return pl.pallas_call(
        functools.partial(_proj_kernel, rows=rows, n_seq=n_seq, tiles_per_seq=tiles_per_seq),
        grid=(n // tm,),
        in_specs=[pl.BlockSpec((tm, d), row), pl.BlockSpec((1, d), fix), pl.BlockSpec((1, d), fix),
                  pl.BlockSpec(w_main.shape, fix), pl.BlockSpec(w_ab.shape, fix),
                  pl.BlockSpec((n_seq, CONV_PAD, cc), seq), pl.BlockSpec((CONV_W, cc), fix)],
        out_specs=[pl.BlockSpec((tm * r, c), row) for r, c, _ in outs]
                  + [pl.BlockSpec((n_seq, CONV_PAD, cc), seq)],
        out_shape=[jax.ShapeDtypeStruct((n * r, c), t) for r, c, t in outs]
                  + [jax.ShapeDtypeStruct(conv_prev8.shape, F32)],
        scratch_shapes=[pltpu.VMEM((n_seq * (rows + CONV_PAD), cc), F32)],
        compiler_params=_params("arbitrary"),
        name="proj",
    )(x2, ln_g, ln_b, w_main, w_ab, conv_prev8, conv_w)


def _gdn_kernel(y_ref, z_ref, ab_ref, s0_ref, alog_ref, dtb_ref, nw_ref,
                o_ref, sn_ref,
                s_scr, cum_scr, a_scr, p_scr, attn_scr, rhs_scr, sol_scr, qg_scr, kdt_scr, u_scr, os_scr,
                *, nb, tl, c):
    l = pl.program_id(1)
    d = GDN_D
    hw = GDN_HEADS * d

    @pl.when(l == 0)
    def _():
        s_scr[...] = s0_ref[...]

    g_all = [-jnp.exp(alog_ref[...]) * _softplus(ab_ref[bi] + dtb_ref[...]) for bi in range(nb)]
    beta_all = [_sigmoid(ab_ref[bi]) for bi in range(nb)]
    nw = nw_ref[...]

    rows = lax.broadcasted_iota(jnp.int32, (c, c), 0)
    cols = lax.broadcasted_iota(jnp.int32, (c, c), 1)
    incl = cols <= rows
    strict = cols < rows
    tri = jnp.where(incl, 1.0, 0.0).astype(BF16)
    grow = lax.broadcasted_iota(jnp.int32, (c, LANES), 0)
    glane = lax.broadcasted_iota(jnp.int32, (c, LANES), 1)

    n_chunks = tl // c
    groups = [(ci, bi, h) for ci in range(n_chunks) for bi in range(nb) for h in range(GDN_HEADS)]
    eye = jnp.where(rows == cols, 1.0, 0.0)

    gmat = jnp.concatenate(
        [jnp.where(glane < c, jnp.where(grow > glane, g_all[bi][ci * c:(ci + 1) * c, h:h + 1], 0.0),
                   g_all[bi][ci * c:(ci + 1) * c, h:h + 1]) for ci, bi, h in groups], axis=1)
    g_hi, g_mid, g_lo = _split3(gmat)
    cum_scr[...] = _dot(tri, g_hi) + _dot(tri, g_mid) + _dot(tri, g_lo)

    for g, (ci, bi, h) in enumerate(groups):
        r0 = ci * c
        beta = beta_all[bi][r0:r0 + c, GDN_HEADS + h:GDN_HEADS + h + 1]
        rel = cum_scr[:, g * LANES:g * LANES + c]
        gam = cum_scr[:, g * LANES + c:g * LANES + c + 1]
        g_last = gam[c - 1:c, :]
        decay = jnp.where(incl, jnp.exp(jnp.where(incl, rel, 0.0)), 0.0)
        eg = jnp.exp(gam)
        ek = jnp.exp(g_last - gam)

        qh = y_ref[bi, r0:r0 + c, h * d:(h + 1) * d].astype(F32)
        kh = y_ref[bi, r0:r0 + c, hw + h * d:hw + (h + 1) * d].astype(F32)
        vh = y_ref[bi, r0:r0 + c, 2 * hw + h * d:2 * hw + (h + 1) * d].astype(F32)
        qn = qh * lax.rsqrt(jnp.sum(qh * qh, axis=-1, keepdims=True) + 1e-6) * (d ** -0.5)
        kn = kh * lax.rsqrt(jnp.sum(kh * kh, axis=-1, keepdims=True) + 1e-6)
        qb = qn.astype(BF16)
        kb = kn.astype(BF16)
        a = jnp.where(strict, beta * _dot_nt(kb, kb) * decay, 0.0)
        a_scr[g] = a
        p_scr[g] = eye - a
        attn_scr[g] = (_dot_nt(qb, kb) * decay).astype(BF16)
        rhs_scr[g, :, 0:d] = (vh * beta).astype(BF16)
        rhs_scr[g, :, d:2 * d] = (kn * (beta * eg)).astype(BF16)
        qg_scr[g] = (qn * eg).astype(BF16)
        kdt_scr[g] = jnp.transpose(kn * ek).astype(BF16)

    n_sq = int(math.log2(c)) - 1
    for it in range(n_sq):
        for g in range(len(groups)):
            akb = a_scr[g].astype(BF16)
            a_scr[g] = _dot(akb, akb)
        for g in range(len(groups)):
            p = p_scr[g]
            p_scr[g] = p + _dot(p.astype(BF16), a_scr[g].astype(BF16))

    for g in range(len(groups)):
        sol_scr[g] = _dot(p_scr[g].astype(BF16), rhs_scr[g])

    for g, (ci, bi, h) in enumerate(groups):
        sb = s_scr[bi, h].astype(BF16)
        u_scr[g] = (sol_scr[g, :, 0:d] - _dot(sol_scr[g, :, d:2 * d].astype(BF16), sb)).astype(BF16)
        os_scr[g] = _dot(qg_scr[g], sb)

    for g, (ci, bi, h) in enumerate(groups):
        r0 = ci * c
        ub = u_scr[g]
        o = os_scr[g] + _dot(attn_scr[g], ub)
        g_last = cum_scr[c - 1:c, g * LANES + c:g * LANES + c + 1]
        s_scr[bi, h] = s_scr[bi, h] * jnp.exp(g_last) + _dot(kdt_scr[g], ub)

        zh = z_ref[bi, r0:r0 + c, h * d:(h + 1) * d].astype(F32)
        on = o * lax.rsqrt(jnp.mean(o * o, axis=-1, keepdims=True) + 1e-6) * nw
        o_ref[bi, r0:r0 + c, h * d:(h + 1) * d] = (on * _silu(zh)).astype(BF16)

    @pl.when(l == pl.num_programs(1) - 1)
    def _():
        sn_ref[...] = s_scr[...]


def _gdn(y, z, ab, s0, alog_row, dtb_row, nw_row):
    b, l, cc = y.shape
    c = min(l, CHUNK)
    tl = min(l, GDN_TILE)
    nb = math.gcd(b, GDN_SEQS)
    assert tl == c, "one chunk per grid step: the state recurrence is staged across groups"
    ng = nb * GDN_HEADS
    hw = GDN_HEADS * GDN_D
    tile = lambda i, j: (i, j, 0)
    fix = lambda i, j: (0, 0)
    return pl.pallas_call(
        functools.partial(_gdn_kernel, nb=nb, tl=tl, c=c),
        grid=(b // nb, l // tl),
        in_specs=[pl.BlockSpec((nb, tl, cc), tile), pl.BlockSpec((nb, tl, hw), tile),
                  pl.BlockSpec((nb, tl, LANES), tile),
                  pl.BlockSpec((nb, GDN_HEADS, GDN_D, GDN_D), lambda i, j: (i, 0, 0, 0)),
                  pl.BlockSpec((1, LANES), fix),
                  pl.BlockSpec((1, LANES), fix), pl.BlockSpec((1, GDN_D), fix)],
        out_specs=[pl.BlockSpec((nb, tl, hw), tile),
                   pl.BlockSpec((nb, GDN_HEADS, GDN_D, GDN_D), lambda i, j: (i, 0, 0, 0))],
        out_shape=[jax.ShapeDtypeStruct((b, l, hw), BF16),
                   jax.ShapeDtypeStruct((b, GDN_HEADS, GDN_D, GDN_D), F32)],
        scratch_shapes=[pltpu.VMEM((nb, GDN_HEADS, GDN_D, GDN_D), F32),
                        pltpu.VMEM((c, ng * LANES), F32), pltpu.VMEM((ng, c, c), F32),
                        pltpu.VMEM((ng, c, c), F32), pltpu.VMEM((ng, c, c), BF16),
                        pltpu.VMEM((ng, c, 2 * GDN_D), BF16), pltpu.VMEM((ng, c, 2 * GDN_D), F32),
                        pltpu.VMEM((ng, c, GDN_D), BF16), pltpu.VMEM((ng, GDN_D, c), BF16),
                        pltpu.VMEM((ng, c, GDN_D), BF16), pltpu.VMEM((ng, c, GDN_D), F32)],
        compiler_params=_params("parallel", "arbitrary"),
        name="gdn",
    )(y, z, ab, s0, alog_row, dtb_row, nw_row)


def _lambda(lamv, lam_init):
    l1 = jnp.sum(lamv[0:1] * lamv[1:2], axis=-1, keepdims=True)
    l2 = jnp.sum(lamv[2:3] * lamv[3:4], axis=-1, keepdims=True)
    return jnp.exp(l1) - jnp.exp(l2) + lam_init


def _attn_bias(q0, k0, nq, nk, slope):
    qpos = q0 + lax.broadcasted_iota(jnp.int32, (nq, nk), 0)
    kpos = k0 + lax.broadcasted_iota(jnp.int32, (nq, nk), 1)
    dist = jnp.abs(qpos - kpos).astype(F32)
    shift = int(math.log2(CHUNK))
    allowed = (kpos >> shift) <= (qpos >> shift)
    return jnp.where(allowed, -slope * dist, NEG_INF)


def _attn_init(stats):
    for m, l, a in stats:
        m[...] = jnp.full(m.shape, NEG_INF, F32)
        l[...] = jnp.zeros(l.shape, F32)
        a[...] = jnp.zeros(a.shape, F32)


def _attn_update(q, k, v, bias, stats):
    for i, (m, l, a) in enumerate(stats):
        lo, hi = i * DIFF_DH, (i + 1) * DIFF_DH
        s = _dot_nt(q[:, lo:hi], k[:, lo:hi]) + bias
        m_new = jnp.maximum(m[...], jnp.max(s, axis=-1, keepdims=True))
        alpha = jnp.exp(m[...] - m_new)
        p = jnp.exp(s - m_new)
        l[...] = alpha * l[...] + jnp.sum(p, axis=-1, keepdims=True)
        a[...] = alpha * a[...] + _dot(p.astype(BF16), v)
        m[...] = m_new


def _attn_finish(stats, lam, sw, lam_init):
    (_, l1, a1), (_, l2, a2) = stats
    o = a1[...] / l1[...] - lam * (a2[...] / l2[...])
    return o * lax.rsqrt(jnp.mean(o * o, axis=-1, keepdims=True) + 1e-6) * sw * (1.0 - lam_init)


def _attn_prompt_kernel(slopes_ref, lamv_ref, sw_ref, q_ref, k_ref, v_ref, o_ref,
                        vx_scr, bias_scr, q_scr, m_scr, acc_scr, sa_scr, sb_scr, pa_scr, pb_scr, ala_scr, alb_scr,
                        *, seq, tile, lam_init):
    t = tile
    r = 2 * t
    rb = ATTN_ROWS
    dv = DIFF_DV
    slope = slopes_ref[pl.program_id(1)] * LOG2E
    q_scale = (DIFF_DH ** -0.5) * LOG2E
    lam = _lambda(lamv_ref[...], lam_init)
    sw = sw_ref[...]
    shift = int(math.log2(CHUNK))

    vx_scr[:, 0:dv] = v_ref[0]
    vx_scr[:, dv:2 * dv] = jnp.ones((seq, dv), BF16)
    qi_ = lax.broadcasted_iota(jnp.int32, (t, t), 0)
    kj_ = lax.broadcasted_iota(jnp.int32, (t, t), 1)
    rel = (qi_ - kj_).astype(F32)
    bias_scr[0] = -slope * rel
    bias_scr[1] = jnp.where((kj_ >> shift) <= (qi_ >> shift), -slope * jnp.abs(rel), NEG_INF)
    lane = lax.broadcasted_iota(jnp.int32, (t, dv), 1)
    frame_shift = slope * t

    def finalize(qi):
        q0 = qi * t
        o1 = acc_scr[0:t, 0:dv] / acc_scr[0:t, dv:2 * dv]
        o2 = acc_scr[t:r, 0:dv] / acc_scr[t:r, dv:2 * dv]
        o = o1 - lam * o2
        on = o * lax.rsqrt(jnp.mean(o * o, axis=-1, keepdims=True) + 1e-6) * sw * (1.0 - lam_init)
        o_ref[0, q0:q0 + t, :] = on.astype(BF16)

    def q_body(qi):
        q0 = qi * t
        qf = q_ref[0, q0:q0 + t, :].astype(F32) * q_scale
        q_scr[0:t, :] = jnp.where(lane < DIFF_DH, qf, 0.0).astype(BF16)
        q_scr[t:r, :] = jnp.where(lane >= DIFF_DH, qf, 0.0).astype(BF16)
        m_scr[...] = jnp.full(m_scr.shape, NEG_INF, F32)
        if qi > 0:
            finalize(qi - 1)
        acc_scr[...] = jnp.zeros(acc_scr.shape, F32)

        def qk(kj, s_buf):
            k0 = min(kj, qi) * t
            s_buf[...] = _dot_nt(q_scr[...], k_ref[0, k0:k0 + t, :])

        def softmax_accumulate(kj, s_buf, p_buf, al_buf):
            k0 = kj * t
            diag = int(kj == qi)
            for i in range(r // rb):
                r0 = i * rb
                b_lo = r0 % t
                s = s_buf[r0:r0 + rb, :] + bias_scr[diag, b_lo:b_lo + rb, :]
                cols = [s[:, j * LANES:(j + 1) * LANES] for j in range(t // LANES)]
                m_prev = m_scr[r0:r0 + rb, :] - frame_shift
                m_cur = jnp.max(functools.reduce(jnp.maximum, cols), axis=-1, keepdims=True)
                m_new = jnp.maximum(m_prev, m_cur)
                al_buf[r0:r0 + rb, :] = jnp.exp2(m_prev - m_new)
                for j, col in enumerate(cols):
                    p_buf[r0:r0 + rb, j * LANES:(j + 1) * LANES] = jnp.exp2(col - m_new).astype(BF16)
                m_scr[r0:r0 + rb, :] = m_new
            pv = _dot(p_buf[...], vx_scr[k0:k0 + t, :])
            for i in range(r // rb):
                r0 = i * rb
                al = al_buf[r0:r0 + rb, :]
                acc_scr[r0:r0 + rb, :] = (jnp.concatenate([al, al], axis=1) * acc_scr[r0:r0 + rb, :]
                                          + pv[r0:r0 + rb, :])

        sets = ((sa_scr, pa_scr, ala_scr), (sb_scr, pb_scr, alb_scr))
        qk(0, sa_scr)
        for kj in range(qi + 1):
            qk(kj + 1, sets[(kj + 1) % 2][0])
            softmax_accumulate(kj, *sets[kj % 2])

    for qi in range(seq // t):
        q_body(qi)
    finalize(seq // t - 1)


def _attn_prompt(q, kb, vb, slopes, lamv, sw_row, lam_init):
    b, l, _ = q.shape
    t = min(ATTN_TILE, l)
    assert t % CHUNK == 0 and l % t == 0 and t % LANES == 0 and t % ATTN_ROWS == 0
    head = lambda i, h: (i, 0, h)
    fix = lambda i, h: (0, 0)
    dv = DIFF_DV
    scratch = [pltpu.VMEM((l, 2 * dv), BF16), pltpu.VMEM((2, t, t), F32),
               pltpu.VMEM((2 * t, dv), BF16), pltpu.VMEM((2 * t, LANES), F32),
               pltpu.VMEM((2 * t, 2 * dv), F32),
               pltpu.VMEM((2 * t, t), F32), pltpu.VMEM((2 * t, t), F32),
               pltpu.VMEM((2 * t, t), BF16), pltpu.VMEM((2 * t, t), BF16),
               pltpu.VMEM((2 * t, LANES), F32), pltpu.VMEM((2 * t, LANES), F32)]
    return pl.pallas_call(
        functools.partial(_attn_prompt_kernel, seq=l, tile=t, lam_init=lam_init),
        grid=(b, DIFF_HEADS),
        in_specs=[pl.BlockSpec(memory_space=pltpu.SMEM),
                  pl.BlockSpec(lamv.shape, fix), pl.BlockSpec(sw_row.shape, fix),
                  pl.BlockSpec((1, l, dv), head), pl.BlockSpec((1, l, dv), head),
                  pl.BlockSpec((1, l, dv), head)],
        out_specs=pl.BlockSpec((1, l, dv), head),
        out_shape=jax.ShapeDtypeStruct((b, l, DIFF_HEADS * dv), BF16),
        scratch_shapes=scratch,
        compiler_params=_params("parallel", "parallel"),
        name="attn_prompt",
    )(slopes, lamv, sw_row, q, kb, vb)


def _attn_sample_kernel(slopes_ref, lamv_ref, sw_ref, q_ref, kc_ref, vc_ref, kn_ref, vn_ref, o_ref,
                        m1, l1, a1, m2, l2, a2, *, past, tk, lam_init):
    j = pl.program_id(1)
    n_cache = past // tk
    lq = q_ref.shape[1]
    heads = range(DIFF_HEADS)
    cols = [slice(h * DIFF_DV, (h + 1) * DIFF_DV) for h in heads]
    stats = [((m1.at[h], l1.at[h], a1.at[h]), (m2.at[h], l2.at[h], a2.at[h])) for h in heads]
    qs = [(q_ref[0, :, cols[h]].astype(F32) * (DIFF_DH ** -0.5)).astype(BF16) for h in heads]

    @pl.when(j == 0)
    def _():
        for h in heads:
            _attn_init(stats[h])

    @pl.when(j < n_cache)
    def _():
        for h in heads:
            k = kc_ref[0, pl.ds(h, tk, stride=DIFF_HEADS), :].astype(BF16)
            v = vc_ref[0, pl.ds(h, tk, stride=DIFF_HEADS), :].astype(BF16)
            _attn_update(qs[h], k, v, _attn_bias(past, j * tk, lq, tk, slopes_ref[h]), stats[h])

    @pl.when(j == n_cache)
    def _():
        lam = _lambda(lamv_ref[...], lam_init)
        for h in heads:
            _attn_update(qs[h], kn_ref[0, :, cols[h]], vn_ref[0, :, cols[h]],
                         _attn_bias(past, past, lq, lq, slopes_ref[h]), stats[h])
            o_ref[0, :, cols[h]] = _attn_finish(stats[h], lam, sw_ref[...], lam_init).astype(BF16)


def _attn_sample(q, kb, vb, k_cache, v_cache, slopes, lamv, sw_row, lam_init):
    b, l, hd = q.shape
    past = k_cache.shape[1] // DIFF_HEADS
    tk = min(CACHE_TILE, past)
    assert past % tk == 0 and past % CHUNK == 0 and l <= CHUNK
    n_cache = past // tk
    per_b = lambda i, j: (i, 0, 0)
    cache = lambda i, j: (i, jnp.minimum(j, n_cache - 1), 0)
    fix = lambda i, j: (0, 0)
    stat = [pltpu.VMEM((DIFF_HEADS, l, 1), F32), pltpu.VMEM((DIFF_HEADS, l, 1), F32),
            pltpu.VMEM((DIFF_HEADS, l, DIFF_DV), F32)]
    return pl.pallas_call(
        functools.partial(_attn_sample_kernel, past=past, tk=tk, lam_init=lam_init),
        grid=(b, n_cache + 1),
        in_specs=[pl.BlockSpec(memory_space=pltpu.SMEM),
                  pl.BlockSpec(lamv.shape, fix), pl.BlockSpec(sw_row.shape, fix),
                  pl.BlockSpec((1, l, hd), per_b),
                  pl.BlockSpec((1, tk * DIFF_HEADS, DIFF_DV), cache),
                  pl.BlockSpec((1, tk * DIFF_HEADS, DIFF_DV), cache),
                  pl.BlockSpec((1, l, hd), per_b), pl.BlockSpec((1, l, hd), per_b)],
        out_specs=pl.BlockSpec((1, l, hd), per_b),
        out_shape=jax.ShapeDtypeStruct((b, l, hd), BF16),
        scratch_shapes=stat + stat,
        compiler_params=_params("parallel", "arbitrary"),
        name="attn_sample",
    )(slopes, lamv, sw_row, q, k_cache, v_cache, kb, vb)


def _mix_kernel(x_ref, og_ref, od_ref, lig_ref, lib_ref, wo_ref, g1_ref, b1_ref, rw_ref, rb_ref, tri_ref,
                x1_ref, x1b_ref, route_ref, cnt_ref, cnt_scr, *, alpha):
    xn = _layernorm(x_ref[...], lig_ref[...], lib_ref[...])
    hw = og_ref.shape[1]
    mixed = _dot(og_ref[...], wo_ref[0:hw, :]) + _dot(od_ref[...], wo_ref[hw:, :])
    x1 = _layernorm(alpha * xn + mixed, g1_ref[...], b1_ref[...])
    x1_ref[...] = x1
    x1b_ref[...] = x1.astype(BF16)

    x_hi = x1.astype(BF16)
    x_lo = (x1 - x_hi.astype(F32)).astype(BF16)
    lg = (_dot(x_hi, rw_ref[0]) + _dot(x_lo, rw_ref[0]) + _dot(x_hi, rw_ref[1])) + rb_ref[...]
    lane = lax.broadcasted_iota(jnp.int32, lg.shape, 1)
    lanef = lane.astype(F32)
    big = float(LANES)

    gl = jnp.where(lane < N_GROUPS, lg, -jnp.inf)
    gmax = jnp.max(gl, axis=-1, keepdims=True)
    grp = jnp.min(jnp.where(gl == gmax, lanef, big), axis=-1, keepdims=True)
    g_gate = 1.0 / jnp.sum(jnp.where(lane < N_GROUPS, jnp.exp(gl - gmax), 0.0), axis=-1, keepdims=True)

    e_lo = N_GROUPS + grp * EXPERTS_PER_GROUP
    el = jnp.where((lanef >= e_lo) & (lanef < e_lo + EXPERTS_PER_GROUP), lg, -jnp.inf)
    v1 = jnp.max(el, axis=-1, keepdims=True)
    i1 = jnp.min(jnp.where(el == v1, lanef, big), axis=-1, keepdims=True)
    el2 = jnp.where(lanef == i1, -jnp.inf, el)
    v2 = jnp.max(el2, axis=-1, keepdims=True)
    i2 = jnp.min(jnp.where(el2 == v2, lanef, big), axis=-1, keepdims=True)
    e21 = jnp.exp(v2 - v1)
    p1 = 1.0 / (1.0 + e21)

    @pl.when(pl.program_id(0) == 0)
    def _():
        cnt_scr[...] = jnp.zeros(cnt_scr.shape, F32)

    e0 = i1 - N_GROUPS
    e1 = i2 - N_GROUPS
    hot0 = lanef == e0
    hot1 = lanef == e1
    onehot = jnp.where(hot0 | hot1, 1.0, 0.0)
    before = _dot(tri_ref[...], onehot.astype(BF16)) + cnt_scr[...]
    rank0 = jnp.sum(jnp.where(hot0, before, 0.0), axis=-1, keepdims=True)
    rank1 = jnp.sum(jnp.where(hot1, before, 0.0), axis=-1, keepdims=True)
    tm = onehot.shape[0]
    cnt_scr[...] = before[tm - 1:tm, :] + onehot[tm - 1:tm, :]
    cnt_ref[...] = cnt_scr[...]

    route_ref[...] = jnp.where(lane == 0, e0,
                     jnp.where(lane == 1, e1,
                     jnp.where(lane == 2, g_gate * p1,
                     jnp.where(lane == 3, g_gate * (e21 * p1),
                     jnp.where(lane == 4, rank0,
                     jnp.where(lane == 5, rank1, 0.0))))))


def _mix(x2, og, od, li_g, li_b, wo, g1, b1, rw, rb, alpha):
    n, d = x2.shape
    tm = min(TOKEN_TILE, n)
    row = lambda i: (i, 0)
    fix = lambda i: (0, 0)
    tri = jnp.tri(tm, k=-1, dtype=BF16)
    return pl.pallas_call(
        functools.partial(_mix_kernel, alpha=alpha),
        grid=(n // tm,),
        in_specs=[pl.BlockSpec((tm, d), row), pl.BlockSpec((tm, og.shape[1]), row),
                  pl.BlockSpec((tm, od.shape[1]), row),
                  pl.BlockSpec((1, d), fix), pl.BlockSpec((1, d), fix), pl.BlockSpec(wo.shape, fix),
                  pl.BlockSpec((1, d), fix), pl.BlockSpec((1, d), fix),
                  pl.BlockSpec(rw.shape, lambda i: (0, 0, 0)), pl.BlockSpec((1, LANES), fix),
                  pl.BlockSpec((tm, tm), fix)],
        out_specs=[pl.BlockSpec((tm, d), row), pl.BlockSpec((tm, d), row), pl.BlockSpec((tm, LANES), row),
                   pl.BlockSpec((1, LANES), fix)],
        out_shape=[jax.ShapeDtypeStruct((n, d), F32), jax.ShapeDtypeStruct((n, d), BF16),
                   jax.ShapeDtypeStruct((n, LANES), F32), jax.ShapeDtypeStruct((1, LANES), F32)],
        scratch_shapes=[pltpu.VMEM((1, LANES), F32)],
        compiler_params=_params("arbitrary"),
        name="mix",
    )(x2, og, od, li_g, li_b, wo, g1, b1, rw, rb, tri)


def _expert_kernel(be_ref, nu_ref, x_ref, w1_ref, w3_ref, w2_ref, y_ref, w1b, w3b, w2b):
    i = pl.program_id(0)

    @pl.when((i == 0) | (be_ref[i] != be_ref[jnp.maximum(i - 1, 0)]))
    def _():
        w1b[...] = w1_ref[0].astype(BF16)
        w3b[...] = w3_ref[0].astype(BF16)
        w2b[...] = w2_ref[0].astype(BF16)

    @pl.when(i < nu_ref[0])
    def _():
        x = x_ref[...]
        h1 = _dot(x, w1b[...])
        h3 = _dot(x, w3b[...])
        y_ref[...] = _dot((_silu(h1) * h3).astype(BF16), w2b[...]).astype(BF16)

    @pl.when(i >= nu_ref[0])
    def _():
        y_ref[...] = jnp.zeros(y_ref.shape, y_ref.dtype)


def _expert_block(n_tokens):
    per_expert = max(1, n_tokens * TOP_K // N_EXPERTS)
    return int(min(EXPERT_BLOCK_MAX, max(EXPERT_BLOCK_MIN, 2 ** int(math.log2(per_expert)))))


def _experts(xb, block_expert, n_used, w1, w3, w2, blk):
    p_pad, d = xb.shape
    de = w1.shape[2]
    row = lambda i, be, nu: (i, 0)
    grid_spec = pltpu.PrefetchScalarGridSpec(
        num_scalar_prefetch=2,
        grid=(p_pad // blk,),
        in_specs=[pl.BlockSpec((blk, d), row),
                  pl.BlockSpec((1, d, de), lambda i, be, nu: (be[i], 0, 0)),
                  pl.BlockSpec((1, d, de), lambda i, be, nu: (be[i], 0, 0)),
                  pl.BlockSpec((1, de, d), lambda i, be, nu: (be[i], 0, 0))],
        out_specs=pl.BlockSpec((blk, d), row),
        scratch_shapes=[pltpu.VMEM((d, de), BF16), pltpu.VMEM((d, de), BF16), pltpu.VMEM((de, d), BF16)],
    )
    return pl.pallas_call(
        _expert_kernel,
        grid_spec=grid_spec,
        out_shape=jax.ShapeDtypeStruct((p_pad, d), BF16),
        compiler_params=_params("arbitrary"),
        name="experts",
    )(block_expert, n_used, xb, w1, w3, w2)


def _dispatch_plan(plan, counts_row, blk):
    n = plan.shape[1]
    p = n * TOP_K
    n_blocks = -(-p // blk) + N_EXPERTS
    counts = counts_row[0, 0:N_EXPERTS].astype(jnp.int32)
    starts = jnp.cumsum(counts) - counts
    padded = (counts + blk - 1) // blk * blk
    padded_end = jnp.cumsum(padded)
    padded_start = padded_end - padded
    expert = plan[0:TOP_K].astype(jnp.int32)
    rank = plan[4:4 + TOP_K].astype(jnp.int32)
    eids = jnp.arange(N_EXPERTS, dtype=jnp.int32)
    dest = rank + jnp.sum(jnp.where(expert[..., None] == eids, padded_start, 0), axis=-1)
    token = jnp.broadcast_to(jnp.arange(n, dtype=jnp.int32), (TOP_K, n))
    _, order_tok = lax.sort((dest.reshape(p), token.reshape(p)), num_keys=1)
    block_start = jnp.arange(n_blocks, dtype=jnp.int32) * blk
    block_expert = jnp.minimum(
        jnp.sum((padded_end[None, :] <= block_start[:, None]).astype(jnp.int32), axis=1), N_EXPERTS - 1)
    row_id = jnp.arange(n_blocks * blk, dtype=jnp.int32).reshape(n_blocks, blk)
    j = row_id - padded_start[block_expert][:, None]
    src = jnp.clip(starts[block_expert][:, None] + j, 0, p - 1)
    valid = (j < counts[block_expert][:, None]).reshape(-1)
    tok_pad = jnp.where(valid, order_tok[src.reshape(-1)], row_id.reshape(-1) % n)
    n_used = (padded_end[-1:] // blk).astype(jnp.int32)
    return tok_pad, dest, block_expert, n_used


def _final_kernel(x1_ref, y0_ref, y1_ref, route_ref, g_ref, b_ref, o_ref, *, alpha):
    route = route_ref[...]
    moe = route[:, 2:3] * y0_ref[...].astype(F32) + route[:, 3:4] * y1_ref[...].astype(F32)
    o_ref[...] = _layernorm(alpha * x1_ref[...] + moe, g_ref[...], b_ref[...])


def _final(x1, y0, y1, route, g2, b2, alpha):
    n, d = x1.shape
    tm = min(TOKEN_TILE, n)
    row = lambda i: (i, 0)
    fix = lambda i: (0, 0)
    return pl.pallas_call(
        functools.partial(_final_kernel, alpha=alpha),
        grid=(n // tm,),
        in_specs=[pl.BlockSpec((tm, d), row), pl.BlockSpec((tm, d), row), pl.BlockSpec((tm, d), row),
                  pl.BlockSpec((tm, LANES), row), pl.BlockSpec((1, d), fix), pl.BlockSpec((1, d), fix)],
        out_specs=pl.BlockSpec((tm, d), row),
        out_shape=jax.ShapeDtypeStruct((n, d), F32),
        compiler_params=_params("parallel"),
        name="final",
    )(x1, y0, y1, route, g2, b2)


def _pad_lanes(v, width=LANES):
    v = v.reshape(1, -1).astype(F32)
    return jnp.pad(v, ((0, 0), (0, width - v.shape[1])))


def _layer(x, conv_prev, s_prev, k_cache, v_cache, p, alpha, lam_init):
    b, l, d = x.shape
    n = b * l
    x2 = x.reshape(n, d)
    conv_prev8 = jnp.pad(conv_prev.astype(F32), ((0, 0), (CONV_PAD - (CONV_W - 1), 0), (0, 0)))
    yc, z, q, k, v, kb, vb, ab, convn = _proj(x2, l, p["ln_in_g"], p["ln_in_b"], p["w_main"], p["w_ab"],
                                              conv_prev8, p["conv_w"])
    conv_new = convn[:, CONV_PAD - (CONV_W - 1):, :]
    og, s_new = _gdn(yc.reshape(b, l, -1), z.reshape(b, l, -1), ab.reshape(b, l, LANES),
                     s_prev.astype(F32), p["alog_row"], p["dtb_row"], p["nw_row"])

    q3 = q.reshape(b, l, -1)
    kb3 = kb.reshape(b, l, -1)
    vb3 = vb.reshape(b, l, -1)
    if k_cache is None:
        od = _attn_prompt(q3, kb3, vb3, p["slopes"], p["lamv"], p["sw_row"], lam_init)
    else:
        past = k_cache.shape[1]
        od = _attn_sample(q3, kb3, vb3, k_cache.reshape(b, past * DIFF_HEADS, DIFF_DV),
                          v_cache.reshape(b, past * DIFF_HEADS, DIFF_DV),
                          p["slopes"], p["lamv"], p["sw_row"], lam_init)

    x1, x1b, route, counts_row = _mix(x2, og.reshape(n, -1), od.reshape(n, -1), p["ln_in_g"], p["ln_in_b"],
                                      p["wo"], p["ln1_g"], p["ln1_b"], p["rw"], p["rb"], alpha)

    blk = _expert_block(n)
    plan = jnp.transpose(route[:, 0:PLAN_ROWS])
    tok_pad, dest, block_expert, n_used = _dispatch_plan(plan, counts_row, blk)
    xb = jnp.take(x1b, tok_pad, axis=0, mode="clip")
    yb = _experts(xb, block_expert, n_used, p["w1"], p["w3"], p["w2"], blk)
    y0 = jnp.take(yb, dest[0], axis=0, mode="clip")
    y1 = jnp.take(yb, dest[1], axis=0, mode="clip")
    y = _final(x1, y0, y1, route, p["ln2_g"], p["ln2_b"], alpha)

    return (y.reshape(b, l, d), conv_new, s_new,
            k.reshape(b, l, DIFF_HEADS, 2 * DIFF_DH), v.reshape(b, l, DIFF_HEADS, DIFF_DV))


def kernel(x_prompt, x_sample, cache_attn_k, cache_attn_v, state_gdn, state_conv, ln_in_g, ln_in_b, w_in, conv_w, gdn_a_log, gdn_dt_bias, gdn_norm_w, lam_q1, lam_k1, lam_q2, lam_k2, subln_w, w_o, ln1_g, ln1_b, router_g_w, router_g_b, router_e_w, router_e_b, w1, w3, w2, ln2_g, ln2_b):
    depth = w_in.shape[0]
    assert depth == 1, "single-layer step"
    d = x_prompt.shape[-1]
    alpha = (2 * depth) ** 0.25
    lam_init = 0.8 - 0.6 * math.exp(-0.3 * 0)
    row = lambda t: t.reshape(1, -1).astype(F32)

    wi = w_in[0]
    conv_ch = conv_w.shape[-1]
    gw = GDN_HEADS * GDN_D
    c_ab = conv_ch + gw
    c_q = c_ab + 2 * GDN_HEADS
    w_main = jnp.concatenate([wi[:, :c_ab], wi[:, c_q:]], axis=1).astype(BF16)
    w_ab = jnp.pad(wi[:, c_ab:c_q], ((0, 0), (0, LANES - 2 * GDN_HEADS))).astype(BF16)
    rcat = jnp.concatenate([router_g_w[0], router_e_w[0]], axis=1)
    rcat = jnp.pad(rcat, ((0, 0), (0, LANES - rcat.shape[1])))
    r_hi = rcat.astype(BF16)
    r_lo = (rcat - r_hi.astype(F32)).astype(BF16)
    p = {
        "ln_in_g": row(ln_in_g), "ln_in_b": row(ln_in_b), "w_main": w_main, "w_ab": w_ab,
        "conv_w": conv_w[0].astype(F32), "alog_row": _pad_lanes(gdn_a_log[0]),
        "dtb_row": _pad_lanes(gdn_dt_bias[0]), "nw_row": row(gdn_norm_w[0]),
        "slopes": jnp.asarray(2.0 ** (-8.0 * (np.arange(DIFF_HEADS) + 1) / DIFF_HEADS), F32),
        "lamv": jnp.stack([lam_q1[0], lam_k1[0], lam_q2[0], lam_k2[0]]).astype(F32),
        "sw_row": row(subln_w[0]), "wo": w_o[0].astype(BF16),
        "ln1_g": row(ln1_g[0]), "ln1_b": row(ln1_b[0]),
        "rw": jnp.stack([r_hi, r_lo]), "rb": _pad_lanes(jnp.concatenate([router_g_b[0], router_e_b[0]])),
        "w1": w1[0], "w3": w3[0], "w2": w2[0],
        "ln2_g": row(ln2_g[0]), "ln2_b": row(ln2_b[0]),
    }

    bp = x_prompt.shape[0]
    conv0 = jnp.zeros((bp, CONV_W - 1, conv_ch), F32)
    s0 = jnp.zeros((bp, GDN_HEADS, GDN_D, GDN_D), F32)
    yp, cp, sp, kp, vp = _layer(x_prompt, conv0, s0, None, None, p, alpha, lam_init)
    ys, cs, ss, ks, vs = _layer(x_sample, state_conv[0], state_gdn[0], cache_attn_k[0], cache_attn_v[0],
                                p, alpha, lam_init)
    return (yp, ys, kp[None], vp[None], sp[None], cp[None], ks[None], vs[None], ss[None], cs[None])
```

```python
import functools
import math

import jax
import jax.numpy as jnp
import numpy as np
from jax import lax
from jax.experimental import pallas as pl
from jax.experimental.pallas import tpu as pltpu

F32 = jnp.float32
BF16 = jnp.bfloat16

CHUNK = 64
CONV_W = 4
GDN_HEADS = 4
GDN_D = 128
DIFF_HEADS = 4
DIFF_DH = 64
DIFF_DV = 2 * DIFF_DH
N_GROUPS = 4
EXPERTS_PER_GROUP = 8
N_EXPERTS = N_GROUPS * EXPERTS_PER_GROUP
TOP_K = 2
NEG_INF = -1e30
LOG2E = math.log2(math.e)
LANES = 128
CONV_PAD = 8

TOKEN_TILE = 512
GDN_TILE = 64
GDN_SEQS = 4
ATTN_TILE = 256
ATTN_ROWS = 64
CACHE_TILE = 1024
EXPERT_BLOCK_MAX = 512
EXPERT_BLOCK_MIN = 128
EXPERT_PARTS = 2
PLAN_ROWS = 8
VMEM_LIMIT = 48 * 1024 * 1024


def _params(*sem):
    return pltpu.CompilerParams(dimension_semantics=sem, vmem_limit_bytes=VMEM_LIMIT)


def _dot(a, b):
    return jnp.dot(a, b, preferred_element_type=F32)


def _dot_nt(a, b):
    return lax.dot_general(a, b, (((1,), (1,)), ((), ())), preferred_element_type=F32)


def _sigmoid(x):
    return 1.0 / (1.0 + jnp.exp(-x))


def _silu(x):
    return x * _sigmoid(x)


def _softplus(x):
    return jnp.maximum(x, 0.0) + jnp.log(1.0 + jnp.exp(-jnp.abs(x)))


def _layernorm(x, g, b, eps=1e-5):
    mu = jnp.mean(x, axis=-1, keepdims=True)
    xc = x - mu
    var = jnp.mean(xc * xc, axis=-1, keepdims=True)
    return xc * lax.rsqrt(var + eps) * g + b


def _split3(x):
    hi = x.astype(BF16)
    r1 = x - hi.astype(F32)
    mid = r1.astype(BF16)
    lo = (r1 - mid.astype(F32)).astype(BF16)
    return hi, mid, lo


def _proj_kernel(x_ref, g_ref, b_ref, w_ref, wab_ref, convp_ref, cw_ref,
                 y_ref, z_ref, q_ref, k_ref, v_ref, kb_ref, vb_ref, ab_ref, convn_ref,
                 xc_scr, *, rows, n_seq, tiles_per_seq):
    xb = _layernorm(x_ref[...], g_ref[...], b_ref[...]).astype(BF16)
    w = 512
    stride = rows + CONV_PAD
    if tiles_per_seq > 1:
        first = pl.program_id(0) % tiles_per_seq == 0

        @pl.when(first)
        def _():
            xc_scr[0:CONV_PAD, :] = convp_ref[0]

        @pl.when(jnp.logical_not(first))
        def _():
            xc_scr[0:CONV_PAD, :] = xc_scr[rows:rows + CONV_PAD, :]
    else:
        for s in range(n_seq):
            xc_scr[s * stride:s * stride + CONV_PAD, :] = convp_ref[s]
    cw = cw_ref[...]

    for j in range(3):
        cs = slice(j * w, (j + 1) * w)
        qkv = _dot(xb, w_ref[:, cs])
        for s in range(n_seq):
            base = s * stride + CONV_PAD
            xc_scr[base:base + rows, cs] = qkv[s * rows:(s + 1) * rows, :]
            lo = base - (CONV_W - 1)
            y = cw[0:1, cs] * xc_scr[lo:lo + rows, cs]
            for t in range(1, CONV_W):
                y = y + cw[t:t + 1, cs] * xc_scr[lo + t:lo + t + rows, cs]
            hy = 0.5 * y
            y_ref[s * rows:(s + 1) * rows, cs] = (hy + hy * jnp.tanh(hy)).astype(BF16)
    for s in range(n_seq):
        convn_ref[s] = xc_scr[s * stride + rows:s * stride + rows + CONV_PAD, :]
    z_ref[...] = _dot(xb, w_ref[:, 3 * w:4 * w]).astype(BF16)
    q_ref[...] = _dot(xb, w_ref[:, 4 * w:5 * w]).astype(BF16)
    tm = x_ref.shape[0]
    k = _dot(xb, w_ref[:, 5 * w:6 * w])
    kb_ref[...] = k.astype(BF16)
    v = _dot(xb, w_ref[:, 6 * w:7 * w])
    vb_ref[...] = v.astype(BF16)
    for h in range(DIFF_HEADS):
        k_ref[pl.ds(h, tm, stride=DIFF_HEADS), :] = k[:, h * DIFF_DV:(h + 1) * DIFF_DV]
        v_ref[pl.ds(h, tm, stride=DIFF_HEADS), :] = v[:, h * DIFF_DV:(h + 1) * DIFF_DV]
    ab_ref[...] = _dot(xb, wab_ref[...])


def _proj(x2, seq_len, ln_g, ln_b, w_main, w_ab, conv_prev8, conv_w):
    n, d = x2.shape
    tm = min(TOKEN_TILE, n)
    cc = conv_w.shape[1]
    rows = min(seq_len, tm)
    n_seq = tm // rows
    tiles_per_seq = seq_len // rows
    assert tm % rows == 0 and seq_len % rows == 0 and rows % CONV_PAD == 0
    row = lambda i: (i, 0)
    fix = lambda i: (0, 0)
    seq = lambda i: (i // tiles_per_seq, 0, 0)
    hd = DIFF_HEADS * DIFF_DV
    outs = [(1, cc, BF16), (1, hd, BF16), (1, hd, BF16), (DIFF_HEADS, DIFF_DV, F32),
            (DIFF_HEADS, DIFF_DV, F32), (1, hd, BF16), (1, hd, BF16), (1, LANES, F32)]
    return pl.pallas_call(
        functools.partial(_proj_kernel, rows=rows, n_seq=n_seq, tiles_per_seq=tiles_per_seq),
        grid=(n // tm,),
        in_specs=[pl.BlockSpec((tm, d), row), pl.BlockSpec((1, d), fix), pl.BlockSpec((1, d), fix),
                  pl.BlockSpec(w_main.shape, fix), pl.BlockSpec(w_ab.shape, fix),
                  pl.BlockSpec((n_seq, CONV_PAD, cc), seq), pl.BlockSpec((CONV_W, cc), fix)],
        out_specs=[pl.BlockSpec((tm * r, c), row) for r, c, _ in outs]
                  + [pl.BlockSpec((n_seq, CONV_PAD, cc), seq)],
        out_shape=[jax.ShapeDtypeStruct((n * r, c), t) for r, c, t in outs]
                  + [jax.ShapeDtypeStruct(conv_prev8.shape, F32)],
        scratch_shapes=[pltpu.VMEM((n_seq * (rows + CONV_PAD), cc), F32)],
        compiler_params=_params("arbitrary"),
        name="proj",
    )(x2, ln_g, ln_b, w_main, w_ab, conv_prev8, conv_w)


def _gdn_kernel(y_ref, z_ref, ab_ref, s0_ref, alog_ref, dtb_ref, nw_ref,
                o_ref, sn_ref,
                s_scr, cum_scr, a_scr, p_scr, attn_scr, rhs_scr, sol_scr, qg_scr, kdt_scr, u_scr, os_scr,
                *, nb, tl, c):
    l = pl.program_id(1)
    d = GDN_D
    hw = GDN_HEADS * d

    @pl.when(l == 0)
    def _():
        s_scr[...] = s0_ref[...]

    g_all = [-jnp.exp(alog_ref[...]) * _softplus(ab_ref[bi] + dtb_ref[...]) for bi in range(nb)]
    beta_all = [_sigmoid(ab_ref[bi]) for bi in range(nb)]
    nw = nw_ref[...]

    rows = lax.broadcasted_iota(jnp.int32, (c, c), 0)
    cols = lax.broadcasted_iota(jnp.int32, (c, c), 1)
    incl = cols <= rows
    strict = cols < rows
    tri = jnp.where(incl, 1.0, 0.0).astype(BF16)
    grow = lax.broadcasted_iota(jnp.int32, (c, LANES), 0)
    glane = lax.broadcasted_iota(jnp.int32, (c, LANES), 1)

    n_chunks = tl // c
    groups = [(ci, bi, h) for ci in range(n_chunks) for bi in range(nb) for h in range(GDN_HEADS)]
    eye = jnp.where(rows == cols, 1.0, 0.0)

    gmat = jnp.concatenate(
        [jnp.where(glane < c, jnp.where(grow > glane, g_all[bi][ci * c:(ci + 1) * c, h:h + 1], 0.0),
                   g_all[bi][ci * c:(ci + 1) * c, h:h + 1]) for ci, bi, h in groups], axis=1)
    g_hi, g_mid, g_lo = _split3(gmat)
    cum_scr[...] = _dot(tri, g_hi) + _dot(tri, g_mid) + _dot(tri, g_lo)

    for g, (ci, bi, h) in enumerate(groups):
        r0 = ci * c
        beta = beta_all[bi][r0:r0 + c, GDN_HEADS + h:GDN_HEADS + h + 1]
        rel = cum_scr[:, g * LANES:g * LANES + c]
        gam = cum_scr[:, g * LANES + c:g * LANES + c + 1]
        g_last = gam[c - 1:c, :]
        decay = jnp.where(incl, jnp.exp(jnp.where(incl, rel, 0.0)), 0.0)
        eg = jnp.exp(gam)
        ek = jnp.exp(g_last - gam)

        qh = y_ref[bi, r0:r0 + c, h * d:(h + 1) * d].astype(F32)
        kh = y_ref[bi, r0:r0 + c, hw + h * d:hw + (h + 1) * d].astype(F32)
        vh = y_ref[bi, r0:r0 + c, 2 * hw + h * d:2 * hw + (h + 1) * d].astype(F32)
        qn = qh * lax.rsqrt(jnp.sum(qh * qh, axis=-1, keepdims=True) + 1e-6) * (d ** -0.5)
        kn = kh * lax.rsqrt(jnp.sum(kh * kh, axis=-1, keepdims=True) + 1e-6)
        qb = qn.astype(BF16)
        kb = kn.astype(BF16)
        a = jnp.where(strict, beta * _dot_nt(kb, kb) * decay, 0.0)
        a_scr[g] = a
        p_scr[g] = eye - a
        attn_scr[g] = (_dot_nt(qb, kb) * decay).astype(BF16)
        rhs_scr[g, :, 0:d] = (vh * beta).astype(BF16)
        rhs_scr[g, :, d:2 * d] = (kn * (beta * eg)).astype(BF16)
        qg_scr[g] = (qn * eg).astype(BF16)
        kdt_scr[g] = jnp.transpose(kn * ek).astype(BF16)

    n_sq = int(math.log2(c)) - 1
    for it in range(n_sq):
        for g in range(len(groups)):
            akb = a_scr[g].astype(BF16)
            a_scr[g] = _dot(akb, akb)
        for g in range(len(groups)):
            p = p_scr[g]
            p_scr[g] = p + _dot(p.astype(BF16), a_scr[g].astype(BF16))

    for g in range(len(groups)):
        sol_scr[g] = _dot(p_scr[g].astype(BF16), rhs_scr[g])

    for g, (ci, bi, h) in enumerate(groups):
        sb = s_scr[bi, h].astype(BF16)
        u_scr[g] = (sol_scr[g, :, 0:d] - _dot(sol_scr[g, :, d:2 * d].astype(BF16), sb)).astype(BF16)
        os_scr[g] = _dot(qg_scr[g], sb)

    for g, (ci, bi, h) in enumerate(groups):
        r0 = ci * c
        ub = u_scr[g]
        o = os_scr[g] + _dot(attn_scr[g], ub)
        g_last = cum_scr[c - 1:c, g * LANES + c:g * LANES + c + 1]
        s_scr[bi, h] = s_scr[bi, h] * jnp.exp(g_last) + _dot(kdt_scr[g], ub)

        zh = z_ref[bi, r0:r0 + c, h * d:(h + 1) * d].astype(F32)
        on = o * lax.rsqrt(jnp.mean(o * o, axis=-1, keepdims=True) + 1e-6) * nw
        o_ref[bi, r0:r0 + c, h * d:(h + 1) * d] = (on * _silu(zh)).astype(BF16)

    @pl.when(l == pl.num_programs(1) - 1)
    def _():
        sn_ref[...] = s_scr[...]


def _gdn(y, z, ab, s0, alog_row, dtb_row, nw_row):
    b, l, cc = y.shape
    c = min(l, CHUNK)
    tl = min(l, GDN_TILE)
    nb = math.gcd(b, GDN_SEQS)
    assert tl == c, "one chunk per grid step: the state recurrence is staged across groups"
    ng = nb * GDN_HEADS
    hw = GDN_HEADS * GDN_D
    tile = lambda i, j: (i, j, 0)
    fix = lambda i, j: (0, 0)
    return pl.pallas_call(
        functools.partial(_gdn_kernel, nb=nb, tl=tl, c=c),
        grid=(b // nb, l // tl),
        in_specs=[pl.BlockSpec((nb, tl, cc), tile), pl.BlockSpec((nb, tl, hw), tile),
                  pl.BlockSpec((nb, tl, LANES), tile),
                  pl.BlockSpec((nb, GDN_HEADS, GDN_D, GDN_D), lambda i, j: (i, 0, 0, 0)),
                  pl.BlockSpec((1, LANES), fix),
                  pl.BlockSpec((1, LANES), fix), pl.BlockSpec((1, GDN_D), fix)],
        out_specs=[pl.BlockSpec((nb, tl, hw), tile),
                   pl.BlockSpec((nb, GDN_HEADS, GDN_D, GDN_D), lambda i, j: (i, 0, 0, 0))],
        out_shape=[jax.ShapeDtypeStruct((b, l, hw), BF16),
                   jax.ShapeDtypeStruct((b, GDN_HEADS, GDN_D, GDN_D), F32)],
        scratch_shapes=[pltpu.VMEM((nb, GDN_HEADS, GDN_D, GDN_D), F32),
                        pltpu.VMEM((c, ng * LANES), F32), pltpu.VMEM((ng, c, c), F32),
                        pltpu.VMEM((ng, c, c), F32), pltpu.VMEM((ng, c, c), BF16),
                        pltpu.VMEM((ng, c, 2 * GDN_D), BF16), pltpu.VMEM((ng, c, 2 * GDN_D), F32),
                        pltpu.VMEM((ng, c, GDN_D), BF16), pltpu.VMEM((ng, GDN_D, c), BF16),
                        pltpu.VMEM((ng, c, GDN_D), BF16), pltpu.VMEM((ng, c, GDN_D), F32)],
        compiler_params=_params("parallel", "arbitrary"),
        name="gdn",
    )(y, z, ab, s0, alog_row, dtb_row, nw_row)


def _lambda(lamv, lam_init):
    l1 = jnp.sum(lamv[0:1] * lamv[1:2], axis=-1, keepdims=True)
    l2 = jnp.sum(lamv[2:3] * lamv[3:4], axis=-1, keepdims=True)
    return jnp.exp(l1) - jnp.exp(l2) + lam_init


def _attn_bias(q0, k0, nq, nk, slope):
    qpos = q0 + lax.broadcasted_iota(jnp.int32, (nq, nk), 0)
    kpos = k0 + lax.broadcasted_iota(jnp.int32, (nq, nk), 1)
    dist = jnp.abs(qpos - kpos).astype(F32)
    shift = int(math.log2(CHUNK))
    allowed = (kpos >> shift) <= (qpos >> shift)
    return jnp.where(allowed, -slope * dist, NEG_INF)


def _attn_init(stats):
    for m, l, a in stats:
        m[...] = jnp.full(m.shape, NEG_INF, F32)
        l[...] = jnp.zeros(l.shape, F32)
        a[...] = jnp.zeros(a.shape, F32)


def _attn_update(q, k, v, bias, stats):
    for i, (m, l, a) in enumerate(stats):
        lo, hi = i * DIFF_DH, (i + 1) * DIFF_DH
        s = _dot_nt(q[:, lo:hi], k[:, lo:hi]) + bias
        m_new = jnp.maximum(m[...], jnp.max(s, axis=-1, keepdims=True))
        alpha = jnp.exp(m[...] - m_new)
        p = jnp.exp(s - m_new)
        l[...] = alpha * l[...] + jnp.sum(p, axis=-1, keepdims=True)
        a[...] = alpha * a[...] + _dot(p.astype(BF16), v)
        m[...] = m_new


def _attn_finish(stats, lam, sw, lam_init):
    (_, l1, a1), (_, l2, a2) = stats
    o = a1[...] / l1[...] - lam * (a2[...] / l2[...])
    return o * lax.rsqrt(jnp.mean(o * o, axis=-1, keepdims=True) + 1e-6) * sw * (1.0 - lam_init)


def _attn_prompt_kernel(slopes_ref, lamv_ref, sw_ref, q_ref, k_ref, v_ref, o_ref,
                        vx_scr, bias_scr, q_scr, m_scr, acc_scr, sa_scr, sb_scr, pa_scr, pb_scr, ala_scr, alb_scr,
                        *, seq, tile, lam_init):
    t = tile
    r = 2 * t
    rb = ATTN_ROWS
    dv = DIFF_DV
    slope = slopes_ref[pl.program_id(1)] * LOG2E
    q_scale = (DIFF_DH ** -0.5) * LOG2E
    lam = _lambda(lamv_ref[...], lam_init)
    sw = sw_ref[...]
    shift = int(math.log2(CHUNK))

    vx_scr[:, 0:dv] = v_ref[0]
    vx_scr[:, dv:2 * dv] = jnp.ones((seq, dv), BF16)
    qi_ = lax.broadcasted_iota(jnp.int32, (t, t), 0)
    kj_ = lax.broadcasted_iota(jnp.int32, (t, t), 1)
    rel = (qi_ - kj_).astype(F32)
    bias_scr[0] = -slope * rel
    bias_scr[1] = jnp.where((kj_ >> shift) <= (qi_ >> shift), -slope * jnp.abs(rel), NEG_INF)
    lane = lax.broadcasted_iota(jnp.int32, (t, dv), 1)
    frame_shift = slope * t

    def finalize(qi):
        q0 = qi * t
        o1 = acc_scr[0:t, 0:dv] / acc_scr[0:t, dv:2 * dv]
        o2 = acc_scr[t:r, 0:dv] / acc_scr[t:r, dv:2 * dv]
        o = o1 - lam * o2
        on = o * lax.rsqrt(jnp.mean(o * o, axis=-1, keepdims=True) + 1e-6) * sw * (1.0 - lam_init)
        o_ref[0, q0:q0 + t, :] = on.astype(BF16)

    def q_body(qi):
        q0 = qi * t
        qf = q_ref[0, q0:q0 + t, :].astype(F32) * q_scale
        q_scr[0:t, :] = jnp.where(lane < DIFF_DH, qf, 0.0).astype(BF16)
        q_scr[t:r, :] = jnp.where(lane >= DIFF_DH, qf, 0.0).astype(BF16)
        m_scr[...] = jnp.full(m_scr.shape, NEG_INF, F32)
        if qi > 0:
            finalize(qi - 1)
        acc_scr[...] = jnp.zeros(acc_scr.shape, F32)

        def qk(kj, s_buf):
            k0 = min(kj, qi) * t
            s_buf[...] = _dot_nt(q_scr[...], k_ref[0, k0:k0 + t, :])

        def softmax_accumulate(kj, s_buf, p_buf, al_buf):
            k0 = kj * t
            diag = int(kj == qi)
            for i in range(r // rb):
                r0 = i * rb
                b_lo = r0 % t
                s = s_buf[r0:r0 + rb, :] + bias_scr[diag, b_lo:b_lo + rb, :]
                cols = [s[:, j * LANES:(j + 1) * LANES] for j in range(t // LANES)]
                m_prev = m_scr[r0:r0 + rb, :] - frame_shift
                m_cur = jnp.max(functools.reduce(jnp.maximum, cols), axis=-1, keepdims=True)
                m_new = jnp.maximum(m_prev, m_cur)
                al_buf[r0:r0 + rb, :] = jnp.exp2(m_prev - m_new)
                for j, col in enumerate(cols):
                    p_buf[r0:r0 + rb, j * LANES:(j + 1) * LANES] = jnp.exp2(col - m_new).astype(BF16)
                m_scr[r0:r0 + rb, :] = m_new
            pv = _dot(p_buf[...], vx_scr[k0:k0 + t, :])
            for i in range(r // rb):
                r0 = i * rb
                al = al_buf[r0:r0 + rb, :]
                acc_scr[r0:r0 + rb, :] = (jnp.concatenate([al, al], axis=1) * acc_scr[r0:r0 + rb, :]
                                          + pv[r0:r0 + rb, :])

        sets = ((sa_scr, pa_scr, ala_scr), (sb_scr, pb_scr, alb_scr))
        qk(0, sa_scr)
        for kj in range(qi + 1):
            qk(kj + 1, sets[(kj + 1) % 2][0])
            softmax_accumulate(kj, *sets[kj % 2])

    for qi in range(seq // t):
        q_body(qi)
    finalize(seq // t - 1)


def _attn_prompt(q, kb, vb, slopes, lamv, sw_row, lam_init):
    b, l, _ = q.shape
    t = min(ATTN_TILE, l)
    assert t % CHUNK == 0 and l % t == 0 and t % LANES == 0 and t % ATTN_ROWS == 0
    head = lambda i, h: (i, 0, h)
    fix = lambda i, h: (0, 0)
    dv = DIFF_DV
    scratch = [pltpu.VMEM((l, 2 * dv), BF16), pltpu.VMEM((2, t, t), F32),
               pltpu.VMEM((2 * t, dv), BF16), pltpu.VMEM((2 * t, LANES), F32),
               pltpu.VMEM((2 * t, 2 * dv), F32),
               pltpu.VMEM((2 * t, t), F32), pltpu.VMEM((2 * t, t), F32),
               pltpu.VMEM((2 * t, t), BF16), pltpu.VMEM((2 * t, t), BF16),
               pltpu.VMEM((2 * t, LANES), F32), pltpu.VMEM((2 * t, LANES), F32)]
    return pl.pallas_call(
        functools.partial(_attn_prompt_kernel, seq=l, tile=t, lam_init=lam_init),
        grid=(b, DIFF_HEADS),
        in_specs=[pl.BlockSpec(memory_space=pltpu.SMEM),
                  pl.BlockSpec(lamv.shape, fix), pl.BlockSpec(sw_row.shape, fix),
                  pl.BlockSpec((1, l, dv), head), pl.BlockSpec((1, l, dv), head),
                  pl.BlockSpec((1, l, dv), head)],
        out_specs=pl.BlockSpec((1, l, dv), head),
        out_shape=jax.ShapeDtypeStruct((b, l, DIFF_HEADS * dv), BF16),
        scratch_shapes=scratch,
        compiler_params=_params("parallel", "parallel"),
        name="attn_prompt",
    )(slopes, lamv, sw_row, q, kb, vb)


def _attn_sample_kernel(slopes_ref, lamv_ref, sw_ref, q_ref, kc_ref, vc_ref, kn_ref, vn_ref, o_ref,
                        m1, l1, a1, m2, l2, a2, *, past, tk, lam_init):
    j = pl.program_id(1)
    n_cache = past // tk
    lq = q_ref.shape[1]
    heads = range(DIFF_HEADS)
    cols = [slice(h * DIFF_DV, (h + 1) * DIFF_DV) for h in heads]
    stats = [((m1.at[h], l1.at[h], a1.at[h]), (m2.at[h], l2.at[h], a2.at[h])) for h in heads]
    qs = [(q_ref[0, :, cols[h]].astype(F32) * (DIFF_DH ** -0.5)).astype(BF16) for h in heads]

    @pl.when(j == 0)
    def _():
        for h in heads:
            _attn_init(stats[h])

    @pl.when(j < n_cache)
    def _():
        for h in heads:
            k = kc_ref[0, pl.ds(h, tk, stride=DIFF_HEADS), :].astype(BF16)
            v = vc_ref[0, pl.ds(h, tk, stride=DIFF_HEADS), :].astype(BF16)
            _attn_update(qs[h], k, v, _attn_bias(past, j * tk, lq, tk, slopes_ref[h]), stats[h])

    @pl.when(j == n_cache)
    def _():
        lam = _lambda(lamv_ref[...], lam_init)
        for h in heads:
            _attn_update(qs[h], kn_ref[0, :, cols[h]], vn_ref[0, :, cols[h]],
                         _attn_bias(past, past, lq, lq, slopes_ref[h]), stats[h])
            o_ref[0, :, cols[h]] = _attn_finish(stats[h], lam, sw_ref[...], lam_init).astype(BF16)


def _attn_sample(q, kb, vb, k_cache, v_cache, slopes, lamv, sw_row, lam_init):
    b, l, hd = q.shape
    past = k_cache.shape[1] // DIFF_HEADS
    tk = min(CACHE_TILE, past)
    assert past % tk == 0 and past % CHUNK == 0 and l <= CHUNK
    n_cache = past // tk
    per_b = lambda i, j: (i, 0, 0)
    cache = lambda i, j: (i, jnp.minimum(j, n_cache - 1), 0)
    fix = lambda i, j: (0, 0)
    stat = [pltpu.VMEM((DIFF_HEADS, l, 1), F32), pltpu.VMEM((DIFF_HEADS, l, 1), F32),
            pltpu.VMEM((DIFF_HEADS, l, DIFF_DV), F32)]
    return pl.pallas_call(
        functools.partial(_attn_sample_kernel, past=past, tk=tk, lam_init=lam_init),
        grid=(b, n_cache + 1),
        in_specs=[pl.BlockSpec(memory_space=pltpu.SMEM),
                  pl.BlockSpec(lamv.shape, fix), pl.BlockSpec(sw_row.shape, fix),
                  pl.BlockSpec((1, l, hd), per_b),
                  pl.BlockSpec((1, tk * DIFF_HEADS, DIFF_DV), cache),
                  pl.BlockSpec((1, tk * DIFF_HEADS, DIFF_DV), cache),
                  pl.BlockSpec((1, l, hd), per_b), pl.BlockSpec((1, l, hd), per_b)],
        out_specs=pl.BlockSpec((1, l, hd), per_b),
        out_shape=jax.ShapeDtypeStruct((b, l, hd), BF16),
        scratch_shapes=stat + stat,
        compiler_params=_params("parallel", "arbitrary"),
        name="attn_sample",
    )(slopes, lamv, sw_row, q, k_cache, v_cache, kb, vb)


def _mix_kernel(x_ref, og_ref, od_ref, lig_ref, lib_ref, wo_ref, g1_ref, b1_ref, rw_ref, rb_ref, tri_ref,
                x1_ref, x1b_ref, route_ref, cnt_ref, cnt_scr, *, alpha):
    xn = _layernorm(x_ref[...], lig_ref[...], lib_ref[...])
    hw = og_ref.shape[1]
    mixed = _dot(og_ref[...], wo_ref[0:hw, :]) + _dot(od_ref[...], wo_ref[hw:, :])
    x1 = _layernorm(alpha * xn + mixed, g1_ref[...], b1_ref[...])
    x1_ref[...] = x1
    x1b_ref[...] = x1.astype(BF16)

    x_hi = x1.astype(BF16)
    x_lo = (x1 - x_hi.astype(F32)).astype(BF16)
    lg = (_dot(x_hi, rw_ref[0]) + _dot(x_lo, rw_ref[0]) + _dot(x_hi, rw_ref[1])) + rb_ref[...]
    lane = lax.broadcasted_iota(jnp.int32, lg.shape, 1)
    lanef = lane.astype(F32)
    big = float(LANES)

    gl = jnp.where(lane < N_GROUPS, lg, -jnp.inf)
    gmax = jnp.max(gl, axis=-1, keepdims=True)
    grp = jnp.min(jnp.where(gl == gmax, lanef, big), axis=-1, keepdims=True)
    g_gate = 1.0 / jnp.sum(jnp.where(lane < N_GROUPS, jnp.exp(gl - gmax), 0.0), axis=-1, keepdims=True)

    e_lo = N_GROUPS + grp * EXPERTS_PER_GROUP
    el = jnp.where((lanef >= e_lo) & (lanef < e_lo + EXPERTS_PER_GROUP), lg, -jnp.inf)
    v1 = jnp.max(el, axis=-1, keepdims=True)
    i1 = jnp.min(jnp.where(el == v1, lanef, big), axis=-1, keepdims=True)
    el2 = jnp.where(lanef == i1, -jnp.inf, el)
    v2 = jnp.max(el2, axis=-1, keepdims=True)
    i2 = jnp.min(jnp.where(el2 == v2, lanef, big), axis=-1, keepdims=True)
    e21 = jnp.exp(v2 - v1)
    p1 = 1.0 / (1.0 + e21)

    @pl.when(pl.program_id(0) == 0)
    def _():
        cnt_scr[...] = jnp.zeros(cnt_scr.shape, F32)

    e0 = i1 - N_GROUPS
    e1 = i2 - N_GROUPS
    hot0 = lanef == e0
    hot1 = lanef == e1
    onehot = jnp.where(hot0 | hot1, 1.0, 0.0)
    before = _dot(tri_ref[...], onehot.astype(BF16)) + cnt_scr[...]
    rank0 = jnp.sum(jnp.where(hot0, before, 0.0), axis=-1, keepdims=True)
    rank1 = jnp.sum(jnp.where(hot1, before, 0.0), axis=-1, keepdims=True)
    tm = onehot.shape[0]
    cnt_scr[...] = before[tm - 1:tm, :] + onehot[tm - 1:tm, :]
    cnt_ref[...] = cnt_scr[...]

    route_ref[...] = jnp.where(lane == 0, e0,
                     jnp.where(lane == 1, e1,
                     jnp.where(lane == 2, g_gate * p1,
                     jnp.where(lane == 3, g_gate * (e21 * p1),
                     jnp.where(lane == 4, rank0,
                     jnp.where(lane == 5, rank1, 0.0))))))


def _mix(x2, og, od, li_g, li_b, wo, g1, b1, rw, rb, alpha):
    n, d = x2.shape
    tm = min(TOKEN_TILE, n)
    row = lambda i: (i, 0)
    fix = lambda i: (0, 0)
    tri = jnp.tri(tm, k=-1, dtype=BF16)
    return pl.pallas_call(
        functools.partial(_mix_kernel, alpha=alpha),
        grid=(n // tm,),
        in_specs=[pl.BlockSpec((tm, d), row), pl.BlockSpec((tm, og.shape[1]), row),
                  pl.BlockSpec((tm, od.shape[1]), row),
                  pl.BlockSpec((1, d), fix), pl.BlockSpec((1, d), fix), pl.BlockSpec(wo.shape, fix),
                  pl.BlockSpec((1, d), fix), pl.BlockSpec((1, d), fix),
                  pl.BlockSpec(rw.shape, lambda i: (0, 0, 0)), pl.BlockSpec((1, LANES), fix),
                  pl.BlockSpec((tm, tm), fix)],
        out_specs=[pl.BlockSpec((tm, d), row), pl.BlockSpec((tm, d), row), pl.BlockSpec((tm, LANES), row),
                   pl.BlockSpec((1, LANES), fix)],
        out_shape=[jax.ShapeDtypeStruct((n, d), F32), jax.ShapeDtypeStruct((n, d), BF16),
                   jax.ShapeDtypeStruct((n, LANES), F32), jax.ShapeDtypeStruct((1, LANES), F32)],
        scratch_shapes=[pltpu.VMEM((1, LANES), F32)],
        compiler_params=_params("arbitrary"),
        name="mix",
    )(x2, og, od, li_g, li_b, wo, g1, b1, rw, rb, tri)


def _expert_kernel(be_ref, nu_ref, x_ref, w1_ref, w3_ref, w2_ref, *refs, block0):
    y_ref, w1b, w3b, w2b = refs[-4:]
    i = pl.program_id(0)
    g = i + block0

    @pl.when((i == 0) | (be_ref[g] != be_ref[jnp.maximum(g - 1, 0)]))
    def _():
        w1b[...] = w1_ref[0].astype(BF16)
        w3b[...] = w3_ref[0].astype(BF16)
        w2b[...] = w2_ref[0].astype(BF16)

    @pl.when(g < nu_ref[0])
    def _():
        x = x_ref[...]
        h1 = _dot(x, w1b[...])
        h3 = _dot(x, w3b[...])
        y_ref[...] = _dot((_silu(h1) * h3).astype(BF16), w2b[...]).astype(BF16)

    @pl.when(g >= nu_ref[0])
    def _():
        y_ref[...] = jnp.zeros(y_ref.shape, y_ref.dtype)


def _expert_block(n_tokens):
    per_expert = max(1, n_tokens * TOP_K // N_EXPERTS)
    return int(min(EXPERT_BLOCK_MAX, max(EXPERT_BLOCK_MIN, 2 ** int(math.log2(per_expert)))))


def _experts(xb, block0, y_prev, n_blocks, block_expert, n_used, w1, w3, w2, blk):
    rows, d = xb.shape
    de = w1.shape[2]
    weight = lambda i, be, nu: (be[i + block0], 0, 0)
    in_specs = [pl.BlockSpec((blk, d), lambda i, be, nu: (i, 0)),
                pl.BlockSpec((1, d, de), weight), pl.BlockSpec((1, d, de), weight),
                pl.BlockSpec((1, de, d), weight)]
    operands = [block_expert, n_used, xb, w1, w3, w2]
    aliases = {}
    if y_prev is not None:
        in_specs.append(pl.BlockSpec(memory_space=pl.ANY))
        aliases = {len(operands): 0}
        operands.append(y_prev)
    grid_spec = pltpu.PrefetchScalarGridSpec(
        num_scalar_prefetch=2,
        grid=(rows // blk,),
        in_specs=in_specs,
        out_specs=pl.BlockSpec((blk, d), lambda i, be, nu: (i + block0, 0)),
        scratch_shapes=[pltpu.VMEM((d, de), BF16), pltpu.VMEM((d, de), BF16), pltpu.VMEM((de, d), BF16)],
    )
    return pl.pallas_call(
        functools.partial(_expert_kernel, block0=block0),
        grid_spec=grid_spec,
        out_shape=jax.ShapeDtypeStruct((n_blocks * blk, d), BF16),
        input_output_aliases=aliases,
        compiler_params=_params("arbitrary"),
        name="experts",
    )(*operands)


def _dispatch_plan(plan, counts_row, blk):
    n = plan.shape[1]
    p = n * TOP_K
    n_blocks = -(-p // blk) + N_EXPERTS
    counts = counts_row[0, 0:N_EXPERTS].astype(jnp.int32)
    starts = jnp.cumsum(counts) - counts
    padded = (counts + blk - 1) // blk * blk
    padded_end = jnp.cumsum(padded)
    padded_start = padded_end - padded
    expert = plan[0:TOP_K].astype(jnp.int32)
    rank = plan[4:4 + TOP_K].astype(jnp.int32)
    eids = jnp.arange(N_EXPERTS, dtype=jnp.int32)
    dest = rank + jnp.sum(jnp.where(expert[..., None] == eids, padded_start, 0), axis=-1)
    token = jnp.broadcast_to(jnp.arange(n, dtype=jnp.int32), (TOP_K, n))
    _, order_tok = lax.sort((dest.reshape(p), token.reshape(p)), num_keys=1)
    block_start = jnp.arange(n_blocks, dtype=jnp.int32) * blk
    block_expert = jnp.minimum(
        jnp.sum((padded_end[None, :] <= block_start[:, None]).astype(jnp.int32), axis=1), N_EXPERTS - 1)
    row_id = jnp.arange(n_blocks * blk, dtype=jnp.int32).reshape(n_blocks, blk)
    j = row_id - padded_start[block_expert][:, None]
    src = jnp.clip(starts[block_expert][:, None] + j, 0, p - 1)
    valid = (j < counts[block_expert][:, None]).reshape(-1)
    tok_pad = jnp.where(valid, order_tok[src.reshape(-1)], row_id.reshape(-1) % n)
    n_used = (padded_end[-1:] // blk).astype(jnp.int32)
    return tok_pad, dest, block_expert, n_used


def _final_kernel(x1_ref, y0_ref, y1_ref, route_ref, g_ref, b_ref, o_ref, *, alpha):
    route = route_ref[...]
    moe = route[:, 2:3] * y0_ref[...].astype(F32) + route[:, 3:4] * y1_ref[...].astype(F32)
    o_ref[...] = _layernorm(alpha * x1_ref[...] + moe, g_ref[...], b_ref[...])


def _final(x1, y0, y1, route, g2, b2, alpha):
    n, d = x1.shape
    tm = min(TOKEN_TILE, n)
    row = lambda i: (i, 0)
    fix = lambda i: (0, 0)
    return pl.pallas_call(
        functools.partial(_final_kernel, alpha=alpha),
        grid=(n // tm,),
        in_specs=[pl.BlockSpec((tm, d), row), pl.BlockSpec((tm, d), row), pl.BlockSpec((tm, d), row),
                  pl.BlockSpec((tm, LANES), row), pl.BlockSpec((1, d), fix), pl.BlockSpec((1, d), fix)],
        out_specs=pl.BlockSpec((tm, d), row),
        out_shape=jax.ShapeDtypeStruct((n, d), F32),
        compiler_params=_params("parallel"),
        name="final",
    )(x1, y0, y1, route, g2, b2)


def _pad_lanes(v, width=LANES):
    v = v.reshape(1, -1).astype(F32)
    return jnp.pad(v, ((0, 0), (0, width - v.shape[1])))


def _layer(x, conv_prev, s_prev, k_cache, v_cache, p, alpha, lam_init):
    b, l, d = x.shape
    n = b * l
    x2 = x.reshape(n, d)
    conv_prev8 = jnp.pad(conv_prev.astype(F32), ((0, 0), (CONV_PAD - (CONV_W - 1), 0), (0, 0)))
    yc, z, q, k, v, kb, vb, ab, convn = _proj(x2, l, p["ln_in_g"], p["ln_in_b"], p["w_main"], p["w_ab"],
                                              conv_prev8, p["conv_w"])
    conv_new = convn[:, CONV_PAD - (CONV_W - 1):, :]
    og, s_new = _gdn(yc.reshape(b, l, -1), z.reshape(b, l, -1), ab.reshape(b, l, LANES),
                     s_prev.astype(F32), p["alog_row"], p["dtb_row"], p["nw_row"])

    q3 = q.reshape(b, l, -1)
    kb3 = kb.reshape(b, l, -1)
    vb3 = vb.reshape(b, l, -1)
    if k_cache is None:
        od = _attn_prompt(q3, kb3, vb3, p["slopes"], p["lamv"], p["sw_row"], lam_init)
    else:
        past = k_cache.shape[1]
        od = _attn_sample(q3, kb3, vb3, k_cache.reshape(b, past * DIFF_HEADS, DIFF_DV),
                          v_cache.reshape(b, past * DIFF_HEADS, DIFF_DV),
                          p["slopes"], p["lamv"], p["sw_row"], lam_init)

    x1, x1b, route, counts_row = _mix(x2, og.reshape(n, -1), od.reshape(n, -1), p["ln_in_g"], p["ln_in_b"],
                                      p["wo"], p["ln1_g"], p["ln1_b"], p["rw"], p["rb"], alpha)

    blk = _expert_block(n)
    plan = jnp.transpose(route[:, 0:PLAN_ROWS])
    tok_pad, dest, block_expert, n_used = _dispatch_plan(plan, counts_row, blk)
    n_blocks = block_expert.shape[0]
    n_parts = EXPERT_PARTS if n_blocks >= EXPERT_PARTS * N_EXPERTS else 1
    bounds = [n_blocks * i // n_parts for i in range(n_parts + 1)]
    xbs = [jnp.take(x1b, tok_pad[lo * blk:hi * blk], axis=0, mode="clip") for lo, hi in zip(bounds, bounds[1:])]
    yb = None
    for lo, xb in zip(bounds, xbs):
        yb = _experts(xb, lo, yb, n_blocks, block_expert, n_used, p["w1"], p["w3"], p["w2"], blk)
    y0 = jnp.take(yb, dest[0], axis=0, mode="clip")
    y1 = jnp.take(yb, dest[1], axis=0, mode="clip")
    y = _final(x1, y0, y1, route, p["ln2_g"], p["ln2_b"], alpha)

    return (y.reshape(b, l, d), conv_new, s_new,
            k.reshape(b, l, DIFF_HEADS, 2 * DIFF_DH), v.reshape(b, l, DIFF_HEADS, DIFF_DV))


def kernel(x_prompt, x_sample, cache_attn_k, cache_attn_v, state_gdn, state_conv, ln_in_g, ln_in_b, w_in, conv_w, gdn_a_log, gdn_dt_bias, gdn_norm_w, lam_q1, lam_k1, lam_q2, lam_k2, subln_w, w_o, ln1_g, ln1_b, router_g_w, router_g_b, router_e_w, router_e_b, w1, w3, w2, ln2_g, ln2_b):
    depth = w_in.shape[0]
    assert depth == 1, "single-layer step"
    d = x_prompt.shape[-1]
    alpha = (2 * depth) ** 0.25
    lam_init = 0.8 - 0.6 * math.exp(-0.3 * 0)
    row = lambda t: t.reshape(1, -1).astype(F32)

    wi = w_in[0]
    conv_ch = conv_w.shape[-1]
    gw = GDN_HEADS * GDN_D
    c_ab = conv_ch + gw
    c_q = c_ab + 2 * GDN_HEADS
    w_main = jnp.concatenate([wi[:, :c_ab], wi[:, c_q:]], axis=1).astype(BF16)
    w_ab = jnp.pad(wi[:, c_ab:c_q], ((0, 0), (0, LANES - 2 * GDN_HEADS))).astype(BF16)
    rcat = jnp.concatenate([router_g_w[0], router_e_w[0]], axis=1)
    rcat = jnp.pad(rcat, ((0, 0), (0, LANES - rcat.shape[1])))
    r_hi = rcat.astype(BF16)
    r_lo = (rcat - r_hi.astype(F32)).astype(BF16)
    p = {
        "ln_in_g": row(ln_in_g), "ln_in_b": row(ln_in_b), "w_main": w_main, "w_ab": w_ab,
        "conv_w": conv_w[0].astype(F32), "alog_row": _pad_lanes(gdn_a_log[0]),
        "dtb_row": _pad_lanes(gdn_dt_bias[0]), "nw_row": row(gdn_norm_w[0]),
        "slopes": jnp.asarray(2.0 ** (-8.0 * (np.arange(DIFF_HEADS) + 1) / DIFF_HEADS), F32),
        "lamv": jnp.stack([lam_q1[0], lam_k1[0], lam_q2[0], lam_k2[0]]).astype(F32),
        "sw_row": row(subln_w[0]), "wo": w_o[0].astype(BF16),
        "ln1_g": row(ln1_g[0]), "ln1_b": row(ln1_b[0]),
        "rw": jnp.stack([r_hi, r_lo]), "rb": _pad_lanes(jnp.concatenate([router_g_b[0], router_e_b[0]])),
        "w1": w1[0], "w3": w3[0], "w2": w2[0],
        "ln2_g": row(ln2_g[0]), "ln2_b": row(ln2_b[0]),
    }

    bp = x_prompt.shape[0]
    conv0 = jnp.zeros((bp, CONV_W - 1, conv_ch), F32)
    s0 = jnp.zeros((bp, GDN_HEADS, GDN_D, GDN_D), F32)
    yp, cp, sp, kp, vp = _layer(x_prompt, conv0, s0, None, None, p, alpha, lam_init)
    ys, cs, ss, ks, vs = _layer(x_sample, state_conv[0], state_gdn[0], cache_attn_k[0], cache_attn_v[0],
                                p, alpha, lam_init)
    return (yp, ys, kp[None], vp[None], sp[None], cp[None], ks[None], vs[None], ss[None], cs[None])
```

```python
import functools
import math

import jax
import jax.numpy as jnp
import numpy as np
from jax import lax
from jax.experimental import pallas as pl
from jax.experimental.pallas import tpu as pltpu

F32 = jnp.float32
BF16 = jnp.bfloat16

CHUNK = 64
CONV_W = 4
GDN_HEADS = 4
GDN_D = 128
DIFF_HEADS = 4
DIFF_DH = 64
DIFF_DV = 2 * DIFF_DH
N_GROUPS = 4
EXPERTS_PER_GROUP = 8
N_EXPERTS = N_GROUPS * EXPERTS_PER_GROUP
TOP_K = 2
NEG_INF = -1e30
LOG2E = math.log2(math.e)
LANES = 128
CONV_PAD = 8

TOKEN_TILE = 512
GDN_TILE = 64
GDN_SEQS = 4
ATTN_TILE = 256
ATTN_ROWS = 64
CACHE_TILE = 1024
EXPERT_BLOCK_MAX = 512
EXPERT_BLOCK_MIN = 128
EXPERT_PARTS = 3
PLAN_ROWS = 8
VMEM_LIMIT = 48 * 1024 * 1024


def _params(*sem):
    return pltpu.CompilerParams(dimension_semantics=sem, vmem_limit_bytes=VMEM_LIMIT)


def _dot(a, b):
    return jnp.dot(a, b, preferred_element_type=F32)


def _dot_nt(a, b):
    return lax.dot_general(a, b, (((1,), (1,)), ((), ())), preferred_element_type=F32)


def _sigmoid(x):
    return 1.0 / (1.0 + jnp.exp(-x))


def _silu(x):
    return x * _sigmoid(x)


def _softplus(x):
    return jnp.maximum(x, 0.0) + jnp.log(1.0 + jnp.exp(-jnp.abs(x)))


def _layernorm(x, g, b, eps=1e-5):
    mu = jnp.mean(x, axis=-1, keepdims=True)
    xc = x - mu
    var = jnp.mean(xc * xc, axis=-1, keepdims=True)
    return xc * lax.rsqrt(var + eps) * g + b


def _split3(x):
    hi = x.astype(BF16)
    r1 = x - hi.astype(F32)
    mid = r1.astype(BF16)
    lo = (r1 - mid.astype(F32)).astype(BF16)
    return hi, mid, lo


def _proj_kernel(x_ref, g_ref, b_ref, w_ref, wab_ref, convp_ref, cw_ref,
                 y_ref, z_ref, q_ref, k_ref, v_ref, kb_ref, vb_ref, ab_ref, convn_ref,
                 xc_scr, *, rows, n_seq, tiles_per_seq):
    xb = _layernorm(x_ref[...], g_ref[...], b_ref[...]).astype(BF16)
    w = 512
    stride = rows + CONV_PAD
    if tiles_per_seq > 1:
        first = pl.program_id(0) % tiles_per_seq == 0

        @pl.when(first)
        def _():
            xc_scr[0:CONV_PAD, :] = convp_ref[0]

        @pl.when(jnp.logical_not(first))
        def _():
            xc_scr[0:CONV_PAD, :] = xc_scr[rows:rows + CONV_PAD, :]
    else:
        for s in range(n_seq):
            xc_scr[s * stride:s * stride + CONV_PAD, :] = convp_ref[s]
    cw = cw_ref[...]

    for j in range(3):
        cs = slice(j * w, (j + 1) * w)
        qkv = _dot(xb, w_ref[:, cs])
        for s in range(n_seq):
            base = s * stride + CONV_PAD
            xc_scr[base:base + rows, cs] = qkv[s * rows:(s + 1) * rows, :]
            lo = base - (CONV_W - 1)
            y = cw[0:1, cs] * xc_scr[lo:lo + rows, cs]
            for t in range(1, CONV_W):
                y = y + cw[t:t + 1, cs] * xc_scr[lo + t:lo + t + rows, cs]
            hy = 0.5 * y
            y_ref[s * rows:(s + 1) * rows, cs] = (hy + hy * jnp.tanh(hy)).astype(BF16)
    for s in range(n_seq):
        convn_ref[s] = xc_scr[s * stride + rows:s * stride + rows + CONV_PAD, :]
    z_ref[...] = _dot(xb, w_ref[:, 3 * w:4 * w]).astype(BF16)
    q_ref[...] = _dot(xb, w_ref[:, 4 * w:5 * w]).astype(BF16)
    tm = x_ref.shape[0]
    k = _dot(xb, w_ref[:, 5 * w:6 * w])
    kb_ref[...] = k.astype(BF16)
    v = _dot(xb, w_ref[:, 6 * w:7 * w])
    vb_ref[...] = v.astype(BF16)
    for h in range(DIFF_HEADS):
        k_ref[pl.ds(h, tm, stride=DIFF_HEADS), :] = k[:, h * DIFF_DV:(h + 1) * DIFF_DV]
        v_ref[pl.ds(h, tm, stride=DIFF_HEADS), :] = v[:, h * DIFF_DV:(h + 1) * DIFF_DV]
    ab_ref[...] = _dot(xb, wab_ref[...])


def _proj(x2, seq_len, ln_g, ln_b, w_main, w_ab, conv_prev8, conv_w):
    n, d = x2.shape
    tm = min(TOKEN_TILE, n)
    cc = conv_w.shape[1]
    rows = min(seq_len, tm)
    n_seq = tm // rows
    tiles_per_seq = seq_len // rows
    assert tm % rows == 0 and seq_len % rows == 0 and rows % CONV_PAD == 0
    row = lambda i: (i, 0)
    fix = lambda i: (0, 0)
    seq = lambda i: (i // tiles_per_seq, 0, 0)
    hd = DIFF_HEADS * DIFF_DV
    outs = [(1, cc, BF16), (1, hd, BF16), (1, hd, BF16), (DIFF_HEADS, DIFF_DV, F32),
            (DIFF_HEADS, DIFF_DV, F32), (1, hd, BF16), (1, hd, BF16), (1, LANES, F32)]
    return pl.pallas_call(
        functools.partial(_proj_kernel, rows=rows, n_seq=n_seq, tiles_per_seq=tiles_per_seq),
        grid=(n // tm,),
        in_specs=[pl.BlockSpec((tm, d), row), pl.BlockSpec((1, d), fix), pl.BlockSpec((1, d), fix),
                  pl.BlockSpec(w_main.shape, fix), pl.BlockSpec(w_ab.shape, fix),
                  pl.BlockSpec((n_seq, CONV_PAD, cc), seq), pl.BlockSpec((CONV_W, cc), fix)],
        out_specs=[pl.BlockSpec((tm * r, c), row) for r, c, _ in outs]
                  + [pl.BlockSpec((n_seq, CONV_PAD, cc), seq)],
        out_shape=[jax.ShapeDtypeStruct((n * r, c), t) for r, c, t in outs]
                  + [jax.ShapeDtypeStruct(conv_prev8.shape, F32)],
        scratch_shapes=[pltpu.VMEM((n_seq * (rows + CONV_PAD), cc), F32)],
        compiler_params=_params("arbitrary"),
        name="proj",
    )(x2, ln_g, ln_b, w_main, w_ab, conv_prev8, conv_w)


def _gdn_kernel(y_ref, z_ref, ab_ref, s0_ref, alog_ref, dtb_ref, nw_ref,
                o_ref, sn_ref,
                s_scr, cum_scr, a_scr, p_scr, attn_scr, rhs_scr, sol_scr, qg_scr, kdt_scr, u_scr, os_scr,
                *, nb, tl, c):
    l = pl.program_id(1)
    d = GDN_D
    hw = GDN_HEADS * d

    @pl.when(l == 0)
    def _():
        s_scr[...] = s0_ref[...]

    g_all = [-jnp.exp(alog_ref[...]) * _softplus(ab_ref[bi] + dtb_ref[...]) for bi in range(nb)]
    beta_all = [_sigmoid(ab_ref[bi]) for bi in range(nb)]
    nw = nw_ref[...]

    rows = lax.broadcasted_iota(jnp.int32, (c, c), 0)
    cols = lax.broadcasted_iota(jnp.int32, (c, c), 1)
    incl = cols <= rows
    strict = cols < rows
    tri = jnp.where(incl, 1.0, 0.0).astype(BF16)
    grow = lax.broadcasted_iota(jnp.int32, (c, LANES), 0)
    glane = lax.broadcasted_iota(jnp.int32, (c, LANES), 1)

    n_chunks = tl // c
    groups = [(ci, bi, h) for ci in range(n_chunks) for bi in range(nb) for h in range(GDN_HEADS)]
    eye = jnp.where(rows == cols, 1.0, 0.0)

    gmat = jnp.concatenate(
        [jnp.where(glane < c, jnp.where(grow > glane, g_all[bi][ci * c:(ci + 1) * c, h:h + 1], 0.0),
                   g_all[bi][ci * c:(ci + 1) * c, h:h + 1]) for ci, bi, h in groups], axis=1)
    g_hi, g_mid, g_lo = _split3(gmat)
    cum_scr[...] = _dot(tri, g_hi) + _dot(tri, g_mid) + _dot(tri, g_lo)

    for g, (ci, bi, h) in enumerate(groups):
        r0 = ci * c
        beta = beta_all[bi][r0:r0 + c, GDN_HEADS + h:GDN_HEADS + h + 1]
        rel = cum_scr[:, g * LANES:g * LANES + c]
        gam = cum_scr[:, g * LANES + c:g * LANES + c + 1]
        g_last = gam[c - 1:c, :]
        decay = jnp.where(incl, jnp.exp(jnp.where(incl, rel, 0.0)), 0.0)
        eg = jnp.exp(gam)
        ek = jnp.exp(g_last - gam)

        qh = y_ref[bi, r0:r0 + c, h * d:(h + 1) * d].astype(F32)
        kh = y_ref[bi, r0:r0 + c, hw + h * d:hw + (h + 1) * d].astype(F32)
        vh = y_ref[bi, r0:r0 + c, 2 * hw + h * d:2 * hw + (h + 1) * d].astype(F32)
        qn = qh * lax.rsqrt(jnp.sum(qh * qh, axis=-1, keepdims=True) + 1e-6) * (d ** -0.5)
        kn = kh * lax.rsqrt(jnp.sum(kh * kh, axis=-1, keepdims=True) + 1e-6)
        qb = qn.astype(BF16)
        kb = kn.astype(BF16)
        a = jnp.where(strict, beta * _dot_nt(kb, kb) * decay, 0.0)
        a_scr[g] = a
        p_scr[g] = eye - a
        attn_scr[g] = (_dot_nt(qb, kb) * decay).astype(BF16)
        rhs_scr[g, :, 0:d] = (vh * beta).astype(BF16)
        rhs_scr[g, :, d:2 * d] = (kn * (beta * eg)).astype(BF16)
        qg_scr[g] = (qn * eg).astype(BF16)
        kdt_scr[g] = jnp.transpose(kn * ek).astype(BF16)

    n_sq = int(math.log2(c)) - 1
    for it in range(n_sq):
        for g in range(len(groups)):
            akb = a_scr[g].astype(BF16)
            a_scr[g] = _dot(akb, akb)
        for g in range(len(groups)):
            p = p_scr[g]
            p_scr[g] = p + _dot(p.astype(BF16), a_scr[g].astype(BF16))

    for g in range(len(groups)):
        sol_scr[g] = _dot(p_scr[g].astype(BF16), rhs_scr[g])

    for g, (ci, bi, h) in enumerate(groups):
        sb = s_scr[bi, h].astype(BF16)
        u_scr[g] = (sol_scr[g, :, 0:d] - _dot(sol_scr[g, :, d:2 * d].astype(BF16), sb)).astype(BF16)
        os_scr[g] = _dot(qg_scr[g], sb)

    for g, (ci, bi, h) in enumerate(groups):
        r0 = ci * c
        ub = u_scr[g]
        o = os_scr[g] + _dot(attn_scr[g], ub)
        g_last = cum_scr[c - 1:c, g * LANES + c:g * LANES + c + 1]
        s_scr[bi, h] = s_scr[bi, h] * jnp.exp(g_last) + _dot(kdt_scr[g], ub)

        zh = z_ref[bi, r0:r0 + c, h * d:(h + 1) * d].astype(F32)
        on = o * lax.rsqrt(jnp.mean(o * o, axis=-1, keepdims=True) + 1e-6) * nw
        o_ref[bi, r0:r0 + c, h * d:(h + 1) * d] = (on * _silu(zh)).astype(BF16)

    @pl.when(l == pl.num_programs(1) - 1)
    def _():
        sn_ref[...] = s_scr[...]


def _gdn(y, z, ab, s0, alog_row, dtb_row, nw_row):
    b, l, cc = y.shape
    c = min(l, CHUNK)
    tl = min(l, GDN_TILE)
    nb = math.gcd(b, GDN_SEQS)
    assert tl == c, "one chunk per grid step: the state recurrence is staged across groups"
    ng = nb * GDN_HEADS
    hw = GDN_HEADS * GDN_D
    tile = lambda i, j: (i, j, 0)
    fix = lambda i, j: (0, 0)
    return pl.pallas_call(
        functools.partial(_gdn_kernel, nb=nb, tl=tl, c=c),
        grid=(b // nb, l // tl),
        in_specs=[pl.BlockSpec((nb, tl, cc), tile), pl.BlockSpec((nb, tl, hw), tile),
                  pl.BlockSpec((nb, tl, LANES), tile),
                  pl.BlockSpec((nb, GDN_HEADS, GDN_D, GDN_D), lambda i, j: (i, 0, 0, 0)),
                  pl.BlockSpec((1, LANES), fix),
                  pl.BlockSpec((1, LANES), fix), pl.BlockSpec((1, GDN_D), fix)],
        out_specs=[pl.BlockSpec((nb, tl, hw), tile),
                   pl.BlockSpec((nb, GDN_HEADS, GDN_D, GDN_D), lambda i, j: (i, 0, 0, 0))],
        out_shape=[jax.ShapeDtypeStruct((b, l, hw), BF16),
                   jax.ShapeDtypeStruct((b, GDN_HEADS, GDN_D, GDN_D), F32)],
        scratch_shapes=[pltpu.VMEM((nb, GDN_HEADS, GDN_D, GDN_D), F32),
                        pltpu.VMEM((c, ng * LANES), F32), pltpu.VMEM((ng, c, c), F32),
                        pltpu.VMEM((ng, c, c), F32), pltpu.VMEM((ng, c, c), BF16),
                        pltpu.VMEM((ng, c, 2 * GDN_D), BF16), pltpu.VMEM((ng, c, 2 * GDN_D), F32),
                        pltpu.VMEM((ng, c, GDN_D), BF16), pltpu.VMEM((ng, GDN_D, c), BF16),
                        pltpu.VMEM((ng, c, GDN_D), BF16), pltpu.VMEM((ng, c, GDN_D), F32)],
        compiler_params=_params("parallel", "arbitrary"),
        name="gdn",
    )(y, z, ab, s0, alog_row, dtb_row, nw_row)


def _lambda(lamv, lam_init):
    l1 = jnp.sum(lamv[0:1] * lamv[1:2], axis=-1, keepdims=True)
    l2 = jnp.sum(lamv[2:3] * lamv[3:4], axis=-1, keepdims=True)
    return jnp.exp(l1) - jnp.exp(l2) + lam_init


def _attn_bias(q0, k0, nq, nk, slope):
    qpos = q0 + lax.broadcasted_iota(jnp.int32, (nq, nk), 0)
    kpos = k0 + lax.broadcasted_iota(jnp.int32, (nq, nk), 1)
    dist = jnp.abs(qpos - kpos).astype(F32)
    shift = int(math.log2(CHUNK))
    allowed = (kpos >> shift) <= (qpos >> shift)
    return jnp.where(allowed, -slope * dist, NEG_INF)


def _attn_init(stats):
    for m, l, a in stats:
        m[...] = jnp.full(m.shape, NEG_INF, F32)
        l[...] = jnp.zeros(l.shape, F32)
        a[...] = jnp.zeros(a.shape, F32)


def _attn_update(q, k, v, bias, stats):
    for i, (m, l, a) in enumerate(stats):
        lo, hi = i * DIFF_DH, (i + 1) * DIFF_DH
        s = _dot_nt(q[:, lo:hi], k[:, lo:hi]) + bias
        m_new = jnp.maximum(m[...], jnp.max(s, axis=-1, keepdims=True))
        alpha = jnp.exp(m[...] - m_new)
        p = jnp.exp(s - m_new)
        l[...] = alpha * l[...] + jnp.sum(p, axis=-1, keepdims=True)
        a[...] = alpha * a[...] + _dot(p.astype(BF16), v)
        m[...] = m_new


def _attn_finish(stats, lam, sw, lam_init):
    (_, l1, a1), (_, l2, a2) = stats
    o = a1[...] / l1[...] - lam * (a2[...] / l2[...])
    return o * lax.rsqrt(jnp.mean(o * o, axis=-1, keepdims=True) + 1e-6) * sw * (1.0 - lam_init)


def _attn_prompt_kernel(slopes_ref, lamv_ref, sw_ref, q_ref, k_ref, v_ref, o_ref,
                        vx_scr, bias_scr, q_scr, m_scr, acc_scr, sa_scr, sb_scr, pa_scr, pb_scr, ala_scr, alb_scr,
                        *, seq, tile, lam_init):
    t = tile
    r = 2 * t
    rb = ATTN_ROWS
    dv = DIFF_DV
    slope = slopes_ref[pl.program_id(1)] * LOG2E
    q_scale = (DIFF_DH ** -0.5) * LOG2E
    lam = _lambda(lamv_ref[...], lam_init)
    sw = sw_ref[...]
    shift = int(math.log2(CHUNK))

    vx_scr[:, 0:dv] = v_ref[0]
    vx_scr[:, dv:2 * dv] = jnp.ones((seq, dv), BF16)
    qi_ = lax.broadcasted_iota(jnp.int32, (t, t), 0)
    kj_ = lax.broadcasted_iota(jnp.int32, (t, t), 1)
    rel = (qi_ - kj_).astype(F32)
    bias_scr[0] = -slope * rel
    bias_scr[1] = jnp.where((kj_ >> shift) <= (qi_ >> shift), -slope * jnp.abs(rel), NEG_INF)
    lane = lax.broadcasted_iota(jnp.int32, (t, dv), 1)
    frame_shift = slope * t

    def finalize(qi):
        q0 = qi * t
        o1 = acc_scr[0:t, 0:dv] / acc_scr[0:t, dv:2 * dv]
        o2 = acc_scr[t:r, 0:dv] / acc_scr[t:r, dv:2 * dv]
        o = o1 - lam * o2
        on = o * lax.rsqrt(jnp.mean(o * o, axis=-1, keepdims=True) + 1e-6) * sw * (1.0 - lam_init)
        o_ref[0, q0:q0 + t, :] = on.astype(BF16)

    def q_body(qi):
        q0 = qi * t
        qf = q_ref[0, q0:q0 + t, :].astype(F32) * q_scale
        q_scr[0:t, :] = jnp.where(lane < DIFF_DH, qf, 0.0).astype(BF16)
        q_scr[t:r, :] = jnp.where(lane >= DIFF_DH, qf, 0.0).astype(BF16)
        m_scr[...] = jnp.full(m_scr.shape, NEG_INF, F32)
        if qi > 0:
            finalize(qi - 1)
        acc_scr[...] = jnp.zeros(acc_scr.shape, F32)

        def qk(kj, s_buf):
            k0 = min(kj, qi) * t
            s_buf[...] = _dot_nt(q_scr[...], k_ref[0, k0:k0 + t, :])

        def softmax_accumulate(kj, s_buf, p_buf, al_buf):
            k0 = kj * t
            diag = int(kj == qi)
            for i in range(r // rb):
                r0 = i * rb
                b_lo = r0 % t
                s = s_buf[r0:r0 + rb, :] + bias_scr[diag, b_lo:b_lo + rb, :]
                cols = [s[:, j * LANES:(j + 1) * LANES] for j in range(t // LANES)]
                m_prev = m_scr[r0:r0 + rb, :] - frame_shift
                m_cur = jnp.max(functools.reduce(jnp.maximum, cols), axis=-1, keepdims=True)
                m_new = jnp.maximum(m_prev, m_cur)
                al_buf[r0:r0 + rb, :] = jnp.exp2(m_prev - m_new)
                for j, col in enumerate(cols):
                    p_buf[r0:r0 + rb, j * LANES:(j + 1) * LANES] = jnp.exp2(col - m_new).astype(BF16)
                m_scr[r0:r0 + rb, :] = m_new
            pv = _dot(p_buf[...], vx_scr[k0:k0 + t, :])
            for i in range(r // rb):
                r0 = i * rb
                al = al_buf[r0:r0 + rb, :]
                acc_scr[r0:r0 + rb, :] = (jnp.concatenate([al, al], axis=1) * acc_scr[r0:r0 + rb, :]
                                          + pv[r0:r0 + rb, :])

        sets = ((sa_scr, pa_scr, ala_scr), (sb_scr, pb_scr, alb_scr))
        qk(0, sa_scr)
        for kj in range(qi + 1):
            qk(kj + 1, sets[(kj + 1) % 2][0])
            softmax_accumulate(kj, *sets[kj % 2])

    for qi in range(seq // t):
        q_body(qi)
    finalize(seq // t - 1)


def _attn_prompt(q, kb, vb, slopes, lamv, sw_row, lam_init):
    b, l, _ = q.shape
    t = min(ATTN_TILE, l)
    assert t % CHUNK == 0 and l % t == 0 and t % LANES == 0 and t % ATTN_ROWS == 0
    head = lambda i, h: (i, 0, h)
    fix = lambda i, h: (0, 0)
    dv = DIFF_DV
    scratch = [pltpu.VMEM((l, 2 * dv), BF16), pltpu.VMEM((2, t, t), F32),
               pltpu.VMEM((2 * t, dv), BF16), pltpu.VMEM((2 * t, LANES), F32),
               pltpu.VMEM((2 * t, 2 * dv), F32),
               pltpu.VMEM((2 * t, t), F32), pltpu.VMEM((2 * t, t), F32),
               pltpu.VMEM((2 * t, t), BF16), pltpu.VMEM((2 * t, t), BF16),
               pltpu.VMEM((2 * t, LANES), F32), pltpu.VMEM((2 * t, LANES), F32)]
    return pl.pallas_call(
        functools.partial(_attn_prompt_kernel, seq=l, tile=t, lam_init=lam_init),
        grid=(b, DIFF_HEADS),
        in_specs=[pl.BlockSpec(memory_space=pltpu.SMEM),
                  pl.BlockSpec(lamv.shape, fix), pl.BlockSpec(sw_row.shape, fix),
                  pl.BlockSpec((1, l, dv), head), pl.BlockSpec((1, l, dv), head),
                  pl.BlockSpec((1, l, dv), head)],
        out_specs=pl.BlockSpec((1, l, dv), head),
        out_shape=jax.ShapeDtypeStruct((b, l, DIFF_HEADS * dv), BF16),
        scratch_shapes=scratch,
        compiler_params=_params("parallel", "parallel"),
        name="attn_prompt",
    )(slopes, lamv, sw_row, q, kb, vb)


def _attn_sample_kernel(slopes_ref, lamv_ref, sw_ref, q_ref, kc_ref, vc_ref, kn_ref, vn_ref, o_ref,
                        m1, l1, a1, m2, l2, a2, *, past, tk, lam_init):
    j = pl.program_id(1)
    n_cache = past // tk
    lq = q_ref.shape[1]
    heads = range(DIFF_HEADS)
    cols = [slice(h * DIFF_DV, (h + 1) * DIFF_DV) for h in heads]
    stats = [((m1.at[h], l1.at[h], a1.at[h]), (m2.at[h], l2.at[h], a2.at[h])) for h in heads]
    qs = [(q_ref[0, :, cols[h]].astype(F32) * (DIFF_DH ** -0.5)).astype(BF16) for h in heads]

    @pl.when(j == 0)
    def _():
        for h in heads:
            _attn_init(stats[h])

    @pl.when(j < n_cache)
    def _():
        for h in heads:
            k = kc_ref[0, pl.ds(h, tk, stride=DIFF_HEADS), :].astype(BF16)
            v = vc_ref[0, pl.ds(h, tk, stride=DIFF_HEADS), :].astype(BF16)
            _attn_update(qs[h], k, v, _attn_bias(past, j * tk, lq, tk, slopes_ref[h]), stats[h])

    @pl.when(j == n_cache)
    def _():
        lam = _lambda(lamv_ref[...], lam_init)
        for h in heads:
            _attn_update(qs[h], kn_ref[0, :, cols[h]], vn_ref[0, :, cols[h]],
                         _attn_bias(past, past, lq, lq, slopes_ref[h]), stats[h])
            o_ref[0, :, cols[h]] = _attn_finish(stats[h], lam, sw_ref[...], lam_init).astype(BF16)


def _attn_sample(q, kb, vb, k_cache, v_cache, slopes, lamv, sw_row, lam_init):
    b, l, hd = q.shape
    past = k_cache.shape[1] // DIFF_HEADS
    tk = min(CACHE_TILE, past)
    assert past % tk == 0 and past % CHUNK == 0 and l <= CHUNK
    n_cache = past // tk
    per_b = lambda i, j: (i, 0, 0)
    cache = lambda i, j: (i, jnp.minimum(j, n_cache - 1), 0)
    fix = lambda i, j: (0, 0)
    stat = [pltpu.VMEM((DIFF_HEADS, l, 1), F32), pltpu.VMEM((DIFF_HEADS, l, 1), F32),
            pltpu.VMEM((DIFF_HEADS, l, DIFF_DV), F32)]
    return pl.pallas_call(
        functools.partial(_attn_sample_kernel, past=past, tk=tk, lam_init=lam_init),
        grid=(b, n_cache + 1),
        in_specs=[pl.BlockSpec(memory_space=pltpu.SMEM),
                  pl.BlockSpec(lamv.shape, fix), pl.BlockSpec(sw_row.shape, fix),
                  pl.BlockSpec((1, l, hd), per_b),
                  pl.BlockSpec((1, tk * DIFF_HEADS, DIFF_DV), cache),
                  pl.BlockSpec((1, tk * DIFF_HEADS, DIFF_DV), cache),
                  pl.BlockSpec((1, l, hd), per_b), pl.BlockSpec((1, l, hd), per_b)],
        out_specs=pl.BlockSpec((1, l, hd), per_b),
        out_shape=jax.ShapeDtypeStruct((b, l, hd), BF16),
        scratch_shapes=stat + stat,
        compiler_params=_params("parallel", "arbitrary"),
        name="attn_sample",
    )(slopes, lamv, sw_row, q, k_cache, v_cache, kb, vb)


def _mix_kernel(x_ref, og_ref, od_ref, lig_ref, lib_ref, wo_ref, g1_ref, b1_ref, rw_ref, rb_ref, tri_ref,
                x1b_ref, route_ref, cnt_ref, cnt_scr, *, alpha):
    xn = _layernorm(x_ref[...], lig_ref[...], lib_ref[...])
    hw = og_ref.shape[1]
    mixed = _dot(og_ref[...], wo_ref[0:hw, :]) + _dot(od_ref[...], wo_ref[hw:, :])
    x1 = _layernorm(alpha * xn + mixed, g1_ref[...], b1_ref[...])
    x1b_ref[...] = x1.astype(BF16)

    x_hi = x1.astype(BF16)
    x_lo = (x1 - x_hi.astype(F32)).astype(BF16)
    lg = (_dot(x_hi, rw_ref[0]) + _dot(x_lo, rw_ref[0]) + _dot(x_hi, rw_ref[1])) + rb_ref[...]
    lane = lax.broadcasted_iota(jnp.int32, lg.shape, 1)
    lanef = lane.astype(F32)
    big = float(LANES)

    gl = jnp.where(lane < N_GROUPS, lg, -jnp.inf)
    gmax = jnp.max(gl, axis=-1, keepdims=True)
    grp = jnp.min(jnp.where(gl == gmax, lanef, big), axis=-1, keepdims=True)
    g_gate = 1.0 / jnp.sum(jnp.where(lane < N_GROUPS, jnp.exp(gl - gmax), 0.0), axis=-1, keepdims=True)

    e_lo = N_GROUPS + grp * EXPERTS_PER_GROUP
    el = jnp.where((lanef >= e_lo) & (lanef < e_lo + EXPERTS_PER_GROUP), lg, -jnp.inf)
    v1 = jnp.max(el, axis=-1, keepdims=True)
    i1 = jnp.min(jnp.where(el == v1, lanef, big), axis=-1, keepdims=True)
    el2 = jnp.where(lanef == i1, -jnp.inf, el)
    v2 = jnp.max(el2, axis=-1, keepdims=True)
    i2 = jnp.min(jnp.where(el2 == v2, lanef, big), axis=-1, keepdims=True)
    e21 = jnp.exp(v2 - v1)
    p1 = 1.0 / (1.0 + e21)

    @pl.when(pl.program_id(0) == 0)
    def _():
        cnt_scr[...] = jnp.zeros(cnt_scr.shape, F32)

    e0 = i1 - N_GROUPS
    e1 = i2 - N_GROUPS
    hot0 = lanef == e0
    hot1 = lanef == e1
    onehot = jnp.where(hot0 | hot1, 1.0, 0.0)
    before = _dot(tri_ref[...], onehot.astype(BF16)) + cnt_scr[...]
    rank0 = jnp.sum(jnp.where(hot0, before, 0.0), axis=-1, keepdims=True)
    rank1 = jnp.sum(jnp.where(hot1, before, 0.0), axis=-1, keepdims=True)
    tm = onehot.shape[0]
    cnt_scr[...] = before[tm - 1:tm, :] + onehot[tm - 1:tm, :]
    cnt_ref[...] = cnt_scr[...]

    route_ref[...] = jnp.where(lane == 0, e0,
                     jnp.where(lane == 1, e1,
                     jnp.where(lane == 2, g_gate * p1,
                     jnp.where(lane == 3, g_gate * (e21 * p1),
                     jnp.where(lane == 4, rank0,
                     jnp.where(lane == 5, rank1, 0.0))))))


def _mix(x2, og, od, li_g, li_b, wo, g1, b1, rw, rb, alpha):
    n, d = x2.shape
    tm = min(TOKEN_TILE, n)
    row = lambda i: (i, 0)
    fix = lambda i: (0, 0)
    tri = jnp.tri(tm, k=-1, dtype=BF16)
    return pl.pallas_call(
        functools.partial(_mix_kernel, alpha=alpha),
        grid=(n // tm,),
        in_specs=[pl.BlockSpec((tm, d), row), pl.BlockSpec((tm, og.shape[1]), row),
                  pl.BlockSpec((tm, od.shape[1]), row),
                  pl.BlockSpec((1, d), fix), pl.BlockSpec((1, d), fix), pl.BlockSpec(wo.shape, fix),
                  pl.BlockSpec((1, d), fix), pl.BlockSpec((1, d), fix),
                  pl.BlockSpec(rw.shape, lambda i: (0, 0, 0)), pl.BlockSpec((1, LANES), fix),
                  pl.BlockSpec((tm, tm), fix)],
        out_specs=[pl.BlockSpec((tm, d), row), pl.BlockSpec((tm, LANES), row), pl.BlockSpec((1, LANES), fix)],
        out_shape=[jax.ShapeDtypeStruct((n, d), BF16), jax.ShapeDtypeStruct((n, LANES), F32),
                   jax.ShapeDtypeStruct((1, LANES), F32)],
        scratch_shapes=[pltpu.VMEM((1, LANES), F32)],
        compiler_params=_params("arbitrary"),
        name="mix",
    )(x2, og, od, li_g, li_b, wo, g1, b1, rw, rb, tri)


def _expert_kernel(be_ref, nu_ref, x_ref, w1_ref, w3_ref, w2_ref, *refs, block0):
    y_ref, w1b, w3b, w2b = refs[-4:]
    i = pl.program_id(0)
    g = i + block0

    @pl.when((i == 0) | (be_ref[g] != be_ref[jnp.maximum(g - 1, 0)]))
    def _():
        w1b[...] = w1_ref[0].astype(BF16)
        w3b[...] = w3_ref[0].astype(BF16)
        w2b[...] = w2_ref[0].astype(BF16)

    @pl.when(g < nu_ref[0])
    def _():
        x = x_ref[...]
        h1 = _dot(x, w1b[...])
        h3 = _dot(x, w3b[...])
        y_ref[...] = _dot((_silu(h1) * h3).astype(BF16), w2b[...]).astype(BF16)

    @pl.when(g >= nu_ref[0])
    def _():
        y_ref[...] = jnp.zeros(y_ref.shape, y_ref.dtype)


def _expert_block(n_tokens):
    per_expert = max(1, n_tokens * TOP_K // N_EXPERTS)
    return int(min(EXPERT_BLOCK_MAX, max(EXPERT_BLOCK_MIN, 2 ** int(math.log2(per_expert)))))


def _experts(xb, block0, y_prev, n_blocks, block_expert, n_used, w1, w3, w2, blk):
    rows, d = xb.shape
    de = w1.shape[2]
    weight = lambda i, be, nu: (be[i + block0], 0, 0)
    in_specs = [pl.BlockSpec((blk, d), lambda i, be, nu: (i, 0)),
                pl.BlockSpec((1, d, de), weight), pl.BlockSpec((1, d, de), weight),
                pl.BlockSpec((1, de, d), weight)]
    operands = [block_expert, n_used, xb, w1, w3, w2]
    aliases = {}
    if y_prev is not None:
        in_specs.append(pl.BlockSpec(memory_space=pl.ANY))
        aliases = {len(operands): 0}
        operands.append(y_prev)
    grid_spec = pltpu.PrefetchScalarGridSpec(
        num_scalar_prefetch=2,
        grid=(rows // blk,),
        in_specs=in_specs,
        out_specs=pl.BlockSpec((blk, d), lambda i, be, nu: (i + block0, 0)),
        scratch_shapes=[pltpu.VMEM((d, de), BF16), pltpu.VMEM((d, de), BF16), pltpu.VMEM((de, d), BF16)],
    )
    return pl.pallas_call(
        functools.partial(_expert_kernel, block0=block0),
        grid_spec=grid_spec,
        out_shape=jax.ShapeDtypeStruct((n_blocks * blk, d), BF16),
        input_output_aliases=aliases,
        compiler_params=_params("arbitrary"),
        name="experts",
    )(*operands)


def _dispatch_plan(plan, counts_row, blk):
    n = plan.shape[1]
    p = n * TOP_K
    n_blocks = -(-p // blk) + N_EXPERTS
    counts = counts_row[0, 0:N_EXPERTS].astype(jnp.int32)
    starts = jnp.cumsum(counts) - counts
    padded = (counts + blk - 1) // blk * blk
    padded_end = jnp.cumsum(padded)
    padded_start = padded_end - padded
    expert = plan[0:TOP_K].astype(jnp.int32)
    rank = plan[4:4 + TOP_K].astype(jnp.int32)
    eids = jnp.arange(N_EXPERTS, dtype=jnp.int32)
    dest = rank + jnp.sum(jnp.where(expert[..., None] == eids, padded_start, 0), axis=-1)
    token = jnp.broadcast_to(jnp.arange(n, dtype=jnp.int32), (TOP_K, n))
    _, order_tok = lax.sort((dest.reshape(p), token.reshape(p)), num_keys=1)
    block_start = jnp.arange(n_blocks, dtype=jnp.int32) * blk
    block_expert = jnp.minimum(
        jnp.sum((padded_end[None, :] <= block_start[:, None]).astype(jnp.int32), axis=1), N_EXPERTS - 1)
    row_id = jnp.arange(n_blocks * blk, dtype=jnp.int32).reshape(n_blocks, blk)
    j = row_id - padded_start[block_expert][:, None]
    src = jnp.clip(starts[block_expert][:, None] + j, 0, p - 1)
    valid = (j < counts[block_expert][:, None]).reshape(-1)
    tok_pad = jnp.where(valid, order_tok[src.reshape(-1)], row_id.reshape(-1) % n)
    n_used = (padded_end[-1:] // blk).astype(jnp.int32)
    return tok_pad, dest, block_expert, n_used


def _final_kernel(x1_ref, y0_ref, y1_ref, route_ref, g_ref, b_ref, o_ref, *, alpha):
    route = route_ref[...]
    moe = route[:, 2:3] * y0_ref[...].astype(F32) + route[:, 3:4] * y1_ref[...].astype(F32)
    o_ref[...] = _layernorm(alpha * x1_ref[...].astype(F32) + moe, g_ref[...], b_ref[...])


def _final(x1, y0, y1, route, g2, b2, alpha):
    n, d = x1.shape
    tm = min(TOKEN_TILE, n)
    row = lambda i: (i, 0)
    fix = lambda i: (0, 0)
    return pl.pallas_call(
        functools.partial(_final_kernel, alpha=alpha),
        grid=(n // tm,),
        in_specs=[pl.BlockSpec((tm, d), row), pl.BlockSpec((tm, d), row), pl.BlockSpec((tm, d), row),
                  pl.BlockSpec((tm, LANES), row), pl.BlockSpec((1, d), fix), pl.BlockSpec((1, d), fix)],
        out_specs=pl.BlockSpec((tm, d), row),
        out_shape=jax.ShapeDtypeStruct((n, d), F32),
        compiler_params=_params("parallel"),
        name="final",
    )(x1, y0, y1, route, g2, b2)


def _pad_lanes(v, width=LANES):
    v = v.reshape(1, -1).astype(F32)
    return jnp.pad(v, ((0, 0), (0, width - v.shape[1])))


def _layer(x, conv_prev, s_prev, k_cache, v_cache, p, alpha, lam_init):
    b, l, d = x.shape
    n = b * l
    x2 = x.reshape(n, d)
    conv_prev8 = jnp.pad(conv_prev.astype(F32), ((0, 0), (CONV_PAD - (CONV_W - 1), 0), (0, 0)))
    yc, z, q, k, v, kb, vb, ab, convn = _proj(x2, l, p["ln_in_g"], p["ln_in_b"], p["w_main"], p["w_ab"],
                                              conv_prev8, p["conv_w"])
    conv_new = convn[:, CONV_PAD - (CONV_W - 1):, :]
    og, s_new = _gdn(yc.reshape(b, l, -1), z.reshape(b, l, -1), ab.reshape(b, l, LANES),
                     s_prev.astype(F32), p["alog_row"], p["dtb_row"], p["nw_row"])

    q3 = q.reshape(b, l, -1)
    kb3 = kb.reshape(b, l, -1)
    vb3 = vb.reshape(b, l, -1)
    if k_cache is None:
        od = _attn_prompt(q3, kb3, vb3, p["slopes"], p["lamv"], p["sw_row"], lam_init)
    else:
        past = k_cache.shape[1]
        od = _attn_sample(q3, kb3, vb3, k_cache.reshape(b, past * DIFF_HEADS, DIFF_DV),
                          v_cache.reshape(b, past * DIFF_HEADS, DIFF_DV),
                          p["slopes"], p["lamv"], p["sw_row"], lam_init)

    x1b, route, counts_row = _mix(x2, og.reshape(n, -1), od.reshape(n, -1), p["ln_in_g"], p["ln_in_b"],
                                  p["wo"], p["ln1_g"], p["ln1_b"], p["rw"], p["rb"], alpha)

    blk = _expert_block(n)
    plan = jnp.transpose(route[:, 0:PLAN_ROWS])
    tok_pad, dest, block_expert, n_used = _dispatch_plan(plan, counts_row, blk)
    n_blocks = block_expert.shape[0]
    n_parts = EXPERT_PARTS if n_blocks >= EXPERT_PARTS * N_EXPERTS else 1
    bounds = [n_blocks * i // n_parts for i in range(n_parts + 1)]
    xbs = [jnp.take(x1b, tok_pad[lo * blk:hi * blk], axis=0, mode="clip") for lo, hi in zip(bounds, bounds[1:])]
    yb = None
    for lo, xb in zip(bounds, xbs):
        yb = _experts(xb, lo, yb, n_blocks, block_expert, n_used, p["w1"], p["w3"], p["w2"], blk)
    y0 = jnp.take(yb, dest[0], axis=0, mode="clip")
    y1 = jnp.take(yb, dest[1], axis=0, mode="clip")
    y = _final(x1b, y0, y1, route, p["ln2_g"], p["ln2_b"], alpha)

    return (y.reshape(b, l, d), conv_new, s_new,
            k.reshape(b, l, DIFF_HEADS, 2 * DIFF_DH), v.reshape(b, l, DIFF_HEADS, DIFF_DV))


def kernel(x_prompt, x_sample, cache_attn_k, cache_attn_v, state_gdn, state_conv, ln_in_g, ln_in_b, w_in, conv_w, gdn_a_log, gdn_dt_bias, gdn_norm_w, lam_q1, lam_k1, lam_q2, lam_k2, subln_w, w_o, ln1_g, ln1_b, router_g_w, router_g_b, router_e_w, router_e_b, w1, w3, w2, ln2_g, ln2_b):
    depth = w_in.shape[0]
    assert depth == 1, "single-layer step"
    d = x_prompt.shape[-1]
    alpha = (2 * depth) ** 0.25
    lam_init = 0.8 - 0.6 * math.exp(-0.3 * 0)
    row = lambda t: t.reshape(1, -1).astype(F32)

    wi = w_in[0]
    conv_ch = conv_w.shape[-1]
    gw = GDN_HEADS * GDN_D
    c_ab = conv_ch + gw
    c_q = c_ab + 2 * GDN_HEADS
    w_main = jnp.concatenate([wi[:, :c_ab], wi[:, c_q:]], axis=1).astype(BF16)
    w_ab = jnp.pad(wi[:, c_ab:c_q], ((0, 0), (0, LANES - 2 * GDN_HEADS))).astype(BF16)
    rcat = jnp.concatenate([router_g_w[0], router_e_w[0]], axis=1)
    rcat = jnp.pad(rcat, ((0, 0), (0, LANES - rcat.shape[1])))
    r_hi = rcat.astype(BF16)
    r_lo = (rcat - r_hi.astype(F32)).astype(BF16)
    p = {
        "ln_in_g": row(ln_in_g), "ln_in_b": row(ln_in_b), "w_main": w_main, "w_ab": w_ab,
        "conv_w": conv_w[0].astype(F32), "alog_row": _pad_lanes(gdn_a_log[0]),
        "dtb_row": _pad_lanes(gdn_dt_bias[0]), "nw_row": row(gdn_norm_w[0]),
        "slopes": jnp.asarray(2.0 ** (-8.0 * (np.arange(DIFF_HEADS) + 1) / DIFF_HEADS), F32),
        "lamv": jnp.stack([lam_q1[0], lam_k1[0], lam_q2[0], lam_k2[0]]).astype(F32),
        "sw_row": row(subln_w[0]), "wo": w_o[0].astype(BF16),
        "ln1_g": row(ln1_g[0]), "ln1_b": row(ln1_b[0]),
        "rw": jnp.stack([r_hi, r_lo]), "rb": _pad_lanes(jnp.concatenate([router_g_b[0], router_e_b[0]])),
        "w1": w1[0], "w3": w3[0], "w2": w2[0],
        "ln2_g": row(ln2_g[0]), "ln2_b": row(ln2_b[0]),
    }

    bp = x_prompt.shape[0]
    conv0 = jnp.zeros((bp, CONV_W - 1, conv_ch), F32)
    s0 = jnp.zeros((bp, GDN_HEADS, GDN_D, GDN_D), F32)
    yp, cp, sp, kp, vp = _layer(x_prompt, conv0, s0, None, None, p, alpha, lam_init)
    ys, cs, ss, ks, vs = _layer(x_sample, state_conv[0], state_gdn[0], cache_attn_k[0], cache_attn_v[0],
                                p, alpha, lam_init)
    return (yp, ys, kp[None], vp[None], sp[None], cp[None], ks[None], vs[None], ss[None], cs[None])
```

```python
import functools
import math

import jax
import jax.numpy as jnp
import numpy as np
from jax import lax
from jax.experimental import pallas as pl
from jax.experimental.pallas import tpu as pltpu

F32 = jnp.float32
BF16 = jnp.bfloat16

CHUNK = 64
CONV_W = 4
GDN_HEADS = 4
GDN_D = 128
DIFF_HEADS = 4
DIFF_DH = 64
DIFF_DV = 2 * DIFF_DH
N_GROUPS = 4
EXPERTS_PER_GROUP = 8
N_EXPERTS = N_GROUPS * EXPERTS_PER_GROUP
TOP_K = 2
NEG_INF = -1e30
LOG2E = math.log2(math.e)
LANES = 128
CONV_PAD = 8

TOKEN_TILE = 512
GDN_TILE = 64
GDN_SEQS = 4
ATTN_TILE = 256
ATTN_ROWS = 64
CACHE_TILE = 1024
EXPERT_BLOCK_MAX = 512
EXPERT_BLOCK_MIN = 128
EXPERT_PARTS = 4
PLAN_ROWS = 8
VMEM_LIMIT = 48 * 1024 * 1024


def _params(*sem):
    return pltpu.CompilerParams(dimension_semantics=sem, vmem_limit_bytes=VMEM_LIMIT)


def _dot(a, b):
    return jnp.dot(a, b, preferred_element_type=F32)


def _dot_nt(a, b):
    return lax.dot_general(a, b, (((1,), (1,)), ((), ())), preferred_element_type=F32)


def _sigmoid(x):
    return 1.0 / (1.0 + jnp.exp(-x))


def _silu(x):
    return x * _sigmoid(x)


def _softplus(x):
    return jnp.maximum(x, 0.0) + jnp.log(1.0 + jnp.exp(-jnp.abs(x)))


def _layernorm(x, g, b, eps=1e-5):
    mu = jnp.mean(x, axis=-1, keepdims=True)
    xc = x - mu
    var = jnp.mean(xc * xc, axis=-1, keepdims=True)
    return xc * lax.rsqrt(var + eps) * g + b


def _split3(x):
    hi = x.astype(BF16)
    r1 = x - hi.astype(F32)
    mid = r1.astype(BF16)
    lo = (r1 - mid.astype(F32)).astype(BF16)
    return hi, mid, lo


def _proj_kernel(x_ref, g_ref, b_ref, w_ref, wab_ref, convp_ref, cw_ref,
                 y_ref, z_ref, q_ref, k_ref, v_ref, kb_ref, vb_ref, ab_ref, convn_ref,
                 xc_scr, *, rows, n_seq, tiles_per_seq):
    xb = _layernorm(x_ref[...], g_ref[...], b_ref[...]).astype(BF16)
    w = GDN_HEADS * GDN_D
    stride = rows + CONV_PAD
    if tiles_per_seq > 1:
        first = pl.program_id(0) % tiles_per_seq == 0

        @pl.when(first)
        def _():
            xc_scr[0:CONV_PAD, :] = convp_ref[0]

        @pl.when(jnp.logical_not(first))
        def _():
            xc_scr[0:CONV_PAD, :] = xc_scr[rows:rows + CONV_PAD, :]
    else:
        for s in range(n_seq):
            xc_scr[s * stride:s * stride + CONV_PAD, :] = convp_ref[s]
    cw = cw_ref[...]

    for j in range(3):
        cs = slice(j * w, (j + 1) * w)
        qkv = _dot(xb, w_ref[:, cs])
        for s in range(n_seq):
            base = s * stride + CONV_PAD
            xc_scr[base:base + rows, cs] = qkv[s * rows:(s + 1) * rows, :]
            lo = base - (CONV_W - 1)
            y = cw[0:1, cs] * xc_scr[lo:lo + rows, cs]
            for t in range(1, CONV_W):
                y = y + cw[t:t + 1, cs] * xc_scr[lo + t:lo + t + rows, cs]
            hy = 0.5 * y
            y_ref[s * rows:(s + 1) * rows, cs] = (hy + hy * jnp.tanh(hy)).astype(BF16)
    for s in range(n_seq):
        convn_ref[s] = xc_scr[s * stride + rows:s * stride + rows + CONV_PAD, :]
    z_ref[...] = _dot(xb, w_ref[:, 3 * w:4 * w]).astype(BF16)
    q_ref[...] = _dot(xb, w_ref[:, 4 * w:5 * w]).astype(BF16)
    tm = x_ref.shape[0]
    k = _dot(xb, w_ref[:, 5 * w:6 * w])
    kb_ref[...] = k.astype(BF16)
    v = _dot(xb, w_ref[:, 6 * w:7 * w])
    vb_ref[...] = v.astype(BF16)
    for h in range(DIFF_HEADS):
        k_ref[pl.ds(h, tm, stride=DIFF_HEADS), :] = k[:, h * DIFF_DV:(h + 1) * DIFF_DV]
        v_ref[pl.ds(h, tm, stride=DIFF_HEADS), :] = v[:, h * DIFF_DV:(h + 1) * DIFF_DV]
    ab_ref[...] = _dot(xb, wab_ref[...])


def _proj(x2, seq_len, ln_g, ln_b, w_main, w_ab, conv_prev8, conv_w):
    n, d = x2.shape
    tm = min(TOKEN_TILE, n)
    cc = conv_w.shape[1]
    rows = min(seq_len, tm)
    n_seq = tm // rows
    tiles_per_seq = seq_len // rows
    assert tm % rows == 0 and seq_len % rows == 0 and rows % CONV_PAD == 0
    row = lambda i: (i, 0)
    fix = lambda i: (0, 0)
    seq = lambda i: (i // tiles_per_seq, 0, 0)
    hd = DIFF_HEADS * DIFF_DV
    outs = [(1, cc, BF16), (1, hd, BF16), (1, hd, BF16), (DIFF_HEADS, DIFF_DV, F32),
            (DIFF_HEADS, DIFF_DV, F32), (1, hd, BF16), (1, hd, BF16), (1, LANES, F32)]
    return pl.pallas_call(
        functools.partial(_proj_kernel, rows=rows, n_seq=n_seq, tiles_per_seq=tiles_per_seq),
        grid=(n // tm,),
        in_specs=[pl.BlockSpec((tm, d), row), pl.BlockSpec((1, d), fix), pl.BlockSpec((1, d), fix),
                  pl.BlockSpec(w_main.shape, fix), pl.BlockSpec(w_ab.shape, fix),
                  pl.BlockSpec((n_seq, CONV_PAD, cc), seq), pl.BlockSpec((CONV_W, cc), fix)],
        out_specs=[pl.BlockSpec((tm * r, c), row) for r, c, _ in outs]
                  + [pl.BlockSpec((n_seq, CONV_PAD, cc), seq)],
        out_shape=[jax.ShapeDtypeStruct((n * r, c), t) for r, c, t in outs]
                  + [jax.ShapeDtypeStruct(conv_prev8.shape, F32)],
        scratch_shapes=[pltpu.VMEM((n_seq * (rows + CONV_PAD), cc), F32)],
        compiler_params=_params("arbitrary"),
        name="proj",
    )(x2, ln_g, ln_b, w_main, w_ab, conv_prev8, conv_w)


def _gdn_kernel(y_ref, z_ref, ab_ref, s0_ref, alog_ref, dtb_ref, nw_ref,
                o_ref, sn_ref,
                s_scr, cum_scr, a_scr, p_scr, attn_scr, rhs_scr, sol_scr, qg_scr, kdt_scr, u_scr, os_scr,
                *, nb, tl, c):
    l = pl.program_id(1)
    d = GDN_D
    hw = GDN_HEADS * d

    @pl.when(l == 0)
    def _():
        s_scr[...] = s0_ref[...]

    g_all = [-jnp.exp(alog_ref[...]) * _softplus(ab_ref[bi] + dtb_ref[...]) for bi in range(nb)]
    beta_all = [_sigmoid(ab_ref[bi]) for bi in range(nb)]
    nw = nw_ref[...]

    rows = lax.broadcasted_iota(jnp.int32, (c, c), 0)
    cols = lax.broadcasted_iota(jnp.int32, (c, c), 1)
    incl = cols <= rows
    strict = cols < rows
    tri = jnp.where(incl, 1.0, 0.0).astype(BF16)
    grow = lax.broadcasted_iota(jnp.int32, (c, LANES), 0)
    glane = lax.broadcasted_iota(jnp.int32, (c, LANES), 1)

    n_chunks = tl // c
    groups = [(ci, bi, h) for ci in range(n_chunks) for bi in range(nb) for h in range(GDN_HEADS)]
    eye = jnp.where(rows == cols, 1.0, 0.0)

    gmat = jnp.concatenate(
        [jnp.where(glane < c, jnp.where(grow > glane, g_all[bi][ci * c:(ci + 1) * c, h:h + 1], 0.0),
                   g_all[bi][ci * c:(ci + 1) * c, h:h + 1]) for ci, bi, h in groups], axis=1)
    g_hi, g_mid, g_lo = _split3(gmat)
    cum_scr[...] = _dot(tri, g_hi) + _dot(tri, g_mid) + _dot(tri, g_lo)

    for g, (ci, bi, h) in enumerate(groups):
        r0 = ci * c
        beta = beta_all[bi][r0:r0 + c, GDN_HEADS + h:GDN_HEADS + h + 1]
        rel = cum_scr[:, g * LANES:g * LANES + c]
        gam = cum_scr[:, g * LANES + c:g * LANES + c + 1]
        g_last = gam[c - 1:c, :]
        decay = jnp.where(incl, jnp.exp(jnp.where(incl, rel, 0.0)), 0.0)
        eg = jnp.exp(gam)
        ek = jnp.exp(g_last - gam)

        qh = y_ref[bi, r0:r0 + c, h * d:(h + 1) * d].astype(F32)
        kh = y_ref[bi, r0:r0 + c, hw + h * d:hw + (h + 1) * d].astype(F32)
        vh = y_ref[bi, r0:r0 + c, 2 * hw + h * d:2 * hw + (h + 1) * d].astype(F32)
        qn = qh * lax.rsqrt(jnp.sum(qh * qh, axis=-1, keepdims=True) + 1e-6) * (d ** -0.5)
        kn = kh * lax.rsqrt(jnp.sum(kh * kh, axis=-1, keepdims=True) + 1e-6)
        qb = qn.astype(BF16)
        kb = kn.astype(BF16)
        a = jnp.where(strict, beta * _dot_nt(kb, kb) * decay, 0.0)
        a_scr[g] = a
        p_scr[g] = eye - a
        attn_scr[g] = (_dot_nt(qb, kb) * decay).astype(BF16)
        rhs_scr[g, :, 0:d] = (vh * beta).astype(BF16)
        rhs_scr[g, :, d:2 * d] = (kn * (beta * eg)).astype(BF16)
        qg_scr[g] = (qn * eg).astype(BF16)
        kdt_scr[g] = jnp.transpose(kn * ek).astype(BF16)

    n_sq = int(math.log2(c)) - 1
    for it in range(n_sq):
        for g in range(len(groups)):
            akb = a_scr[g].astype(BF16)
            a_scr[g] = _dot(akb, akb)
        for g in range(len(groups)):
            p = p_scr[g]
            p_scr[g] = p + _dot(p.astype(BF16), a_scr[g].astype(BF16))

    for g in range(len(groups)):
        sol_scr[g] = _dot(p_scr[g].astype(BF16), rhs_scr[g])

    for g, (ci, bi, h) in enumerate(groups):
        sb = s_scr[bi, h].astype(BF16)
        u_scr[g] = (sol_scr[g, :, 0:d] - _dot(sol_scr[g, :, d:2 * d].astype(BF16), sb)).astype(BF16)
        os_scr[g] = _dot(qg_scr[g], sb)

    for g, (ci, bi, h) in enumerate(groups):
        r0 = ci * c
        ub = u_scr[g]
        o = os_scr[g] + _dot(attn_scr[g], ub)
        g_last = cum_scr[c - 1:c, g * LANES + c:g * LANES + c + 1]
        s_scr[bi, h] = s_scr[bi, h] * jnp.exp(g_last) + _dot(kdt_scr[g], ub)

        zh = z_ref[bi, r0:r0 + c, h * d:(h + 1) * d].astype(F32)
        on = o * lax.rsqrt(jnp.mean(o * o, axis=-1, keepdims=True) + 1e-6) * nw
        o_ref[bi, r0:r0 + c, h * d:(h + 1) * d] = (on * _silu(zh)).astype(BF16)

    @pl.when(l == pl.num_programs(1) - 1)
    def _():
        sn_ref[...] = s_scr[...]


def _gdn(y, z, ab, s0, alog_row, dtb_row, nw_row):
    b, l, cc = y.shape
    c = min(l, CHUNK)
    tl = min(l, GDN_TILE)
    nb = math.gcd(b, GDN_SEQS)
    assert tl == c, "one chunk per grid step: the state recurrence is staged across groups"
    ng = nb * GDN_HEADS
    hw = GDN_HEADS * GDN_D
    tile = lambda i, j: (i, j, 0)
    fix = lambda i, j: (0, 0)
    return pl.pallas_call(
        functools.partial(_gdn_kernel, nb=nb, tl=tl, c=c),
        grid=(b // nb, l // tl),
        in_specs=[pl.BlockSpec((nb, tl, cc), tile), pl.BlockSpec((nb, tl, hw), tile),
                  pl.BlockSpec((nb, tl, LANES), tile),
                  pl.BlockSpec((nb, GDN_HEADS, GDN_D, GDN_D), lambda i, j: (i, 0, 0, 0)),
                  pl.BlockSpec((1, LANES), fix),
                  pl.BlockSpec((1, LANES), fix), pl.BlockSpec((1, GDN_D), fix)],
        out_specs=[pl.BlockSpec((nb, tl, hw), tile),
                   pl.BlockSpec((nb, GDN_HEADS, GDN_D, GDN_D), lambda i, j: (i, 0, 0, 0))],
        out_shape=[jax.ShapeDtypeStruct((b, l, hw), BF16),
                   jax.ShapeDtypeStruct((b, GDN_HEADS, GDN_D, GDN_D), F32)],
        scratch_shapes=[pltpu.VMEM((nb, GDN_HEADS, GDN_D, GDN_D), F32),
                        pltpu.VMEM((c, ng * LANES), F32), pltpu.VMEM((ng, c, c), F32),
                        pltpu.VMEM((ng, c, c), F32), pltpu.VMEM((ng, c, c), BF16),
                        pltpu.VMEM((ng, c, 2 * GDN_D), BF16), pltpu.VMEM((ng, c, 2 * GDN_D), F32),
                        pltpu.VMEM((ng, c, GDN_D), BF16), pltpu.VMEM((ng, GDN_D, c), BF16),
                        pltpu.VMEM((ng, c, GDN_D), BF16), pltpu.VMEM((ng, c, GDN_D), F32)],
        compiler_params=_params("parallel", "arbitrary"),
        name="gdn",
    )(y, z, ab, s0, alog_row, dtb_row, nw_row)


def _lambda(lamv, lam_init):
    l1 = jnp.sum(lamv[0:1] * lamv[1:2], axis=-1, keepdims=True)
    l2 = jnp.sum(lamv[2:3] * lamv[3:4], axis=-1, keepdims=True)
    return jnp.exp(l1) - jnp.exp(l2) + lam_init


def _attn_bias(q0, k0, nq, nk, slope):
    qpos = q0 + lax.broadcasted_iota(jnp.int32, (nq, nk), 0)
    kpos = k0 + lax.broadcasted_iota(jnp.int32, (nq, nk), 1)
    dist = jnp.abs(qpos - kpos).astype(F32)
    shift = int(math.log2(CHUNK))
    allowed = (kpos >> shift) <= (qpos >> shift)
    return jnp.where(allowed, -slope * dist, NEG_INF)


def _attn_init(stats):
    for m, l, a in stats:
        m[...] = jnp.full(m.shape, NEG_INF, F32)
        l[...] = jnp.zeros(l.shape, F32)
        a[...] = jnp.zeros(a.shape, F32)


def _attn_update(q, k, v, bias, stats):
    for i, (m, l, a) in enumerate(stats):
        lo, hi = i * DIFF_DH, (i + 1) * DIFF_DH
        s = _dot_nt(q[:, lo:hi], k[:, lo:hi]) + bias
        m_new = jnp.maximum(m[...], jnp.max(s, axis=-1, keepdims=True))
        alpha = jnp.exp(m[...] - m_new)
        p = jnp.exp(s - m_new)
        l[...] = alpha * l[...] + jnp.sum(p, axis=-1, keepdims=True)
        a[...] = alpha * a[...] + _dot(p.astype(BF16), v)
        m[...] = m_new


def _attn_finish(stats, lam, sw, lam_init):
    (_, l1, a1), (_, l2, a2) = stats
    o = a1[...] / l1[...] - lam * (a2[...] / l2[...])
    return o * lax.rsqrt(jnp.mean(o * o, axis=-1, keepdims=True) + 1e-6) * sw * (1.0 - lam_init)


def _attn_prompt_kernel(slopes_ref, lamv_ref, sw_ref, q_ref, k_ref, v_ref, o_ref,
                        vx_scr, bias_scr, q_scr, m_scr, acc_scr, sa_scr, sb_scr, pa_scr, pb_scr, ala_scr, alb_scr,
                        *, seq, tile, lam_init):
    t = tile
    r = 2 * t
    rb = ATTN_ROWS
    dv = DIFF_DV
    slope = slopes_ref[pl.program_id(1)] * LOG2E
    q_scale = (DIFF_DH ** -0.5) * LOG2E
    lam = _lambda(lamv_ref[...], lam_init)
    sw = sw_ref[...]
    shift = int(math.log2(CHUNK))

    vx_scr[:, 0:dv] = v_ref[0]
    vx_scr[:, dv:2 * dv] = jnp.ones((seq, dv), BF16)
    qi_ = lax.broadcasted_iota(jnp.int32, (t, t), 0)
    kj_ = lax.broadcasted_iota(jnp.int32, (t, t), 1)
    rel = (qi_ - kj_).astype(F32)
    bias_scr[0] = -slope * rel
    bias_scr[1] = jnp.where((kj_ >> shift) <= (qi_ >> shift), -slope * jnp.abs(rel), NEG_INF)
    lane = lax.broadcasted_iota(jnp.int32, (t, dv), 1)
    frame_shift = slope * t

    def finalize(qi):
        q0 = qi * t
        o1 = acc_scr[0:t, 0:dv] / acc_scr[0:t, dv:2 * dv]
        o2 = acc_scr[t:r, 0:dv] / acc_scr[t:r, dv:2 * dv]
        o = o1 - lam * o2
        on = o * lax.rsqrt(jnp.mean(o * o, axis=-1, keepdims=True) + 1e-6) * sw * (1.0 - lam_init)
        o_ref[0, q0:q0 + t, :] = on.astype(BF16)

    def q_body(qi):
        q0 = qi * t
        qf = q_ref[0, q0:q0 + t, :].astype(F32) * q_scale
        q_scr[0:t, :] = jnp.where(lane < DIFF_DH, qf, 0.0).astype(BF16)
        q_scr[t:r, :] = jnp.where(lane >= DIFF_DH, qf, 0.0).astype(BF16)
        m_scr[...] = jnp.full(m_scr.shape, NEG_INF, F32)
        if qi > 0:
            finalize(qi - 1)
        acc_scr[...] = jnp.zeros(acc_scr.shape, F32)

        def qk(kj, s_buf):
            k0 = min(kj, qi) * t
            s_buf[...] = _dot_nt(q_scr[...], k_ref[0, k0:k0 + t, :])

        def softmax_accumulate(kj, s_buf, p_buf, al_buf):
            k0 = kj * t
            diag = int(kj == qi)
            for i in range(r // rb):
                r0 = i * rb
                b_lo = r0 % t
                s = s_buf[r0:r0 + rb, :] + bias_scr[diag, b_lo:b_lo + rb, :]
                cols = [s[:, j * LANES:(j + 1) * LANES] for j in range(t // LANES)]
                m_prev = m_scr[r0:r0 + rb, :] - frame_shift
                m_cur = jnp.max(functools.reduce(jnp.maximum, cols), axis=-1, keepdims=True)
                m_new = jnp.maximum(m_prev, m_cur)
                al_buf[r0:r0 + rb, :] = jnp.exp2(m_prev - m_new)
                for j, col in enumerate(cols):
                    p_buf[r0:r0 + rb, j * LANES:(j + 1) * LANES] = jnp.exp2(col - m_new).astype(BF16)
                m_scr[r0:r0 + rb, :] = m_new
            pv = _dot(p_buf[...], vx_scr[k0:k0 + t, :])
            for i in range(r // rb):
                r0 = i * rb
                al = al_buf[r0:r0 + rb, :]
                acc_scr[r0:r0 + rb, :] = (jnp.concatenate([al, al], axis=1) * acc_scr[r0:r0 + rb, :]
                                          + pv[r0:r0 + rb, :])

        sets = ((sa_scr, pa_scr, ala_scr), (sb_scr, pb_scr, alb_scr))
        qk(0, sa_scr)
        for kj in range(qi + 1):
            qk(kj + 1, sets[(kj + 1) % 2][0])
            softmax_accumulate(kj, *sets[kj % 2])

    for qi in range(seq // t):
        q_body(qi)
    finalize(seq // t - 1)


def _attn_prompt(q, kb, vb, slopes, lamv, sw_row, lam_init):
    b, l, _ = q.shape
    t = min(ATTN_TILE, l)
    assert t % CHUNK == 0 and l % t == 0 and t % LANES == 0 and t % ATTN_ROWS == 0
    head = lambda i, h: (i, 0, h)
    fix = lambda i, h: (0, 0)
    dv = DIFF_DV
    scratch = [pltpu.VMEM((l, 2 * dv), BF16), pltpu.VMEM((2, t, t), F32),
               pltpu.VMEM((2 * t, dv), BF16), pltpu.VMEM((2 * t, LANES), F32),
               pltpu.VMEM((2 * t, 2 * dv), F32),
               pltpu.VMEM((2 * t, t), F32), pltpu.VMEM((2 * t, t), F32),
               pltpu.VMEM((2 * t, t), BF16), pltpu.VMEM((2 * t, t), BF16),
               pltpu.VMEM((2 * t, LANES), F32), pltpu.VMEM((2 * t, LANES), F32)]
    return pl.pallas_call(
        functools.partial(_attn_prompt_kernel, seq=l, tile=t, lam_init=lam_init),
        grid=(b, DIFF_HEADS),
        in_specs=[pl.BlockSpec(memory_space=pltpu.SMEM),
                  pl.BlockSpec(lamv.shape, fix), pl.BlockSpec(sw_row.shape, fix),
                  pl.BlockSpec((1, l, dv), head), pl.BlockSpec((1, l, dv), head),
                  pl.BlockSpec((1, l, dv), head)],
        out_specs=pl.BlockSpec((1, l, dv), head),
        out_shape=jax.ShapeDtypeStruct((b, l, DIFF_HEADS * dv), BF16),
        scratch_shapes=scratch,
        compiler_params=_params("parallel", "parallel"),
        name="attn_prompt",
    )(slopes, lamv, sw_row, q, kb, vb)


def _attn_sample_kernel(slopes_ref, lamv_ref, sw_ref, q_ref, kc_ref, vc_ref, kn_ref, vn_ref, o_ref,
                        m1, l1, a1, m2, l2, a2, *, past, tk, lam_init):
    j = pl.program_id(1)
    n_cache = past // tk
    lq = q_ref.shape[1]
    heads = range(DIFF_HEADS)
    cols = [slice(h * DIFF_DV, (h + 1) * DIFF_DV) for h in heads]
    stats = [((m1.at[h], l1.at[h], a1.at[h]), (m2.at[h], l2.at[h], a2.at[h])) for h in heads]
    qs = [(q_ref[0, :, cols[h]].astype(F32) * (DIFF_DH ** -0.5)).astype(BF16) for h in heads]

    @pl.when(j == 0)
    def _():
        for h in heads:
            _attn_init(stats[h])

    @pl.when(j < n_cache)
    def _():
        for h in heads:
            k = kc_ref[0, pl.ds(h, tk, stride=DIFF_HEADS), :].astype(BF16)
            v = vc_ref[0, pl.ds(h, tk, stride=DIFF_HEADS), :].astype(BF16)
            _attn_update(qs[h], k, v, _attn_bias(past, j * tk, lq, tk, slopes_ref[h]), stats[h])

    @pl.when(j == n_cache)
    def _():
        lam = _lambda(lamv_ref[...], lam_init)
        for h in heads:
            _attn_update(qs[h], kn_ref[0, :, cols[h]], vn_ref[0, :, cols[h]],
                         _attn_bias(past, past, lq, lq, slopes_ref[h]), stats[h])
            o_ref[0, :, cols[h]] = _attn_finish(stats[h], lam, sw_ref[...], lam_init).astype(BF16)


def _attn_sample(q, kb, vb, k_cache, v_cache, slopes, lamv, sw_row, lam_init):
    b, l, hd = q.shape
    past = k_cache.shape[1] // DIFF_HEADS
    tk = min(CACHE_TILE, past)
    assert past % tk == 0 and past % CHUNK == 0 and l <= CHUNK
    n_cache = past // tk
    per_b = lambda i, j: (i, 0, 0)
    cache = lambda i, j: (i, jnp.minimum(j, n_cache - 1), 0)
    fix = lambda i, j: (0, 0)
    stat = [pltpu.VMEM((DIFF_HEADS, l, 1), F32), pltpu.VMEM((DIFF_HEADS, l, 1), F32),
            pltpu.VMEM((DIFF_HEADS, l, DIFF_DV), F32)]
    return pl.pallas_call(
        functools.partial(_attn_sample_kernel, past=past, tk=tk, lam_init=lam_init),
        grid=(b, n_cache + 1),
        in_specs=[pl.BlockSpec(memory_space=pltpu.SMEM),
                  pl.BlockSpec(lamv.shape, fix), pl.BlockSpec(sw_row.shape, fix),
                  pl.BlockSpec((1, l, hd), per_b),
                  pl.BlockSpec((1, tk * DIFF_HEADS, DIFF_DV), cache),
                  pl.BlockSpec((1, tk * DIFF_HEADS, DIFF_DV), cache),
                  pl.BlockSpec((1, l, hd), per_b), pl.BlockSpec((1, l, hd), per_b)],
        out_specs=pl.BlockSpec((1, l, hd), per_b),
        out_shape=jax.ShapeDtypeStruct((b, l, hd), BF16),
        scratch_shapes=stat + stat,
        compiler_params=_params("parallel", "arbitrary"),
        name="attn_sample",
    )(slopes, lamv, sw_row, q, k_cache, v_cache, kb, vb)


def _mix_kernel(x_ref, og_ref, od_ref, lig_ref, lib_ref, wo_ref, g1_ref, b1_ref, rw_ref, rb_ref, tri_ref,
                x1b_ref, route_ref, cnt_ref, cnt_scr, *, alpha):
    xn = _layernorm(x_ref[...], lig_ref[...], lib_ref[...])
    hw = og_ref.shape[1]
    mixed = _dot(og_ref[...], wo_ref[0:hw, :]) + _dot(od_ref[...], wo_ref[hw:, :])
    x1 = _layernorm(alpha * xn + mixed, g1_ref[...], b1_ref[...])
    x1b_ref[...] = x1.astype(BF16)

    x_hi = x1.astype(BF16)
    x_lo = (x1 - x_hi.astype(F32)).astype(BF16)
    lg = (_dot(x_hi, rw_ref[0]) + _dot(x_lo, rw_ref[0]) + _dot(x_hi, rw_ref[1])) + rb_ref[...]
    lane = lax.broadcasted_iota(jnp.int32, lg.shape, 1)
    lanef = lane.astype(F32)
    big = float(LANES)

    gl = jnp.where(lane < N_GROUPS, lg, -jnp.inf)
    gmax = jnp.max(gl, axis=-1, keepdims=True)
    grp = jnp.min(jnp.where(gl == gmax, lanef, big), axis=-1, keepdims=True)
    g_gate = 1.0 / jnp.sum(jnp.where(lane < N_GROUPS, jnp.exp(gl - gmax), 0.0), axis=-1, keepdims=True)

    e_lo = N_GROUPS + grp * EXPERTS_PER_GROUP
    el = jnp.where((lanef >= e_lo) & (lanef < e_lo + EXPERTS_PER_GROUP), lg, -jnp.inf)
    v1 = jnp.max(el, axis=-1, keepdims=True)
    i1 = jnp.min(jnp.where(el == v1, lanef, big), axis=-1, keepdims=True)
    el2 = jnp.where(lanef == i1, -jnp.inf, el)
    v2 = jnp.max(el2, axis=-1, keepdims=True)
    i2 = jnp.min(jnp.where(el2 == v2, lanef, big), axis=-1, keepdims=True)
    e21 = jnp.exp(v2 - v1)
    p1 = 1.0 / (1.0 + e21)

    @pl.when(pl.program_id(0) == 0)
    def _():
        cnt_scr[...] = jnp.zeros(cnt_scr.shape, F32)

    e0 = i1 - N_GROUPS
    e1 = i2 - N_GROUPS
    hot0 = lanef == e0
    hot1 = lanef == e1
    onehot = jnp.where(hot0 | hot1, 1.0, 0.0)
    before = _dot(tri_ref[...], onehot.astype(BF16)) + cnt_scr[...]
    rank0 = jnp.sum(jnp.where(hot0, before, 0.0), axis=-1, keepdims=True)
    rank1 = jnp.sum(jnp.where(hot1, before, 0.0), axis=-1, keepdims=True)
    tm = onehot.shape[0]
    cnt_scr[...] = before[tm - 1:tm, :] + onehot[tm - 1:tm, :]
    cnt_ref[...] = cnt_scr[...]

    route_ref[...] = jnp.where(lane == 0, e0,
                     jnp.where(lane == 1, e1,
                     jnp.where(lane == 2, g_gate * p1,
                     jnp.where(lane == 3, g_gate * (e21 * p1),
                     jnp.where(lane == 4, rank0,
                     jnp.where(lane == 5, rank1, 0.0))))))


def _mix(x2, og, od, li_g, li_b, wo, g1, b1, rw, rb, alpha):
    n, d = x2.shape
    tm = min(TOKEN_TILE, n)
    row = lambda i: (i, 0)
    fix = lambda i: (0, 0)
    tri = jnp.tri(tm, k=-1, dtype=BF16)
    return pl.pallas_call(
        functools.partial(_mix_kernel, alpha=alpha),
        grid=(n // tm,),
        in_specs=[pl.BlockSpec((tm, d), row), pl.BlockSpec((tm, og.shape[1]), row),
                  pl.BlockSpec((tm, od.shape[1]), row),
                  pl.BlockSpec((1, d), fix), pl.BlockSpec((1, d), fix), pl.BlockSpec(wo.shape, fix),
                  pl.BlockSpec((1, d), fix), pl.BlockSpec((1, d), fix),
                  pl.BlockSpec(rw.shape, lambda i: (0, 0, 0)), pl.BlockSpec((1, LANES), fix),
                  pl.BlockSpec((tm, tm), fix)],
        out_specs=[pl.BlockSpec((tm, d), row), pl.BlockSpec((tm, LANES), row), pl.BlockSpec((1, LANES), fix)],
        out_shape=[jax.ShapeDtypeStruct((n, d), BF16), jax.ShapeDtypeStruct((n, LANES), F32),
                   jax.ShapeDtypeStruct((1, LANES), F32)],
        scratch_shapes=[pltpu.VMEM((1, LANES), F32)],
        compiler_params=_params("arbitrary"),
        name="mix",
    )(x2, og, od, li_g, li_b, wo, g1, b1, rw, rb, tri)


def _expert_kernel(be_ref, nu_ref, x_ref, w1_ref, w3_ref, w2_ref, *refs, block0):
    y_ref, w1b, w3b, w2b = refs[-4:]
    i = pl.program_id(0)
    g = i + block0

    @pl.when((i == 0) | (be_ref[g] != be_ref[jnp.maximum(g - 1, 0)]))
    def _():
        w1b[...] = w1_ref[0].astype(BF16)
        w3b[...] = w3_ref[0].astype(BF16)
        w2b[...] = w2_ref[0].astype(BF16)

    @pl.when(g < nu_ref[0])
    def _():
        x = x_ref[...]
        h1 = _dot(x, w1b[...])
        h3 = _dot(x, w3b[...])
        y_ref[...] = _dot((_silu(h1) * h3).astype(BF16), w2b[...]).astype(BF16)

    @pl.when(g >= nu_ref[0])
    def _():
        y_ref[...] = jnp.zeros(y_ref.shape, y_ref.dtype)


def _expert_block(n_tokens):
    per_expert = max(1, n_tokens * TOP_K // N_EXPERTS)
    return int(min(EXPERT_BLOCK_MAX, max(EXPERT_BLOCK_MIN, 2 ** int(math.log2(per_expert)))))


def _experts(xb, block0, y_prev, n_blocks, block_expert, n_used, w1, w3, w2, blk):
    rows, d = xb.shape
    de = w1.shape[2]
    weight = lambda i, be, nu: (be[i + block0], 0, 0)
    in_specs = [pl.BlockSpec((blk, d), lambda i, be, nu: (i, 0)),
                pl.BlockSpec((1, d, de), weight), pl.BlockSpec((1, d, de), weight),
                pl.BlockSpec((1, de, d), weight)]
    operands = [block_expert, n_used, xb, w1, w3, w2]
    aliases = {}
    if y_prev is not None:
        in_specs.append(pl.BlockSpec(memory_space=pl.ANY))
        aliases = {len(operands): 0}
        operands.append(y_prev)
    grid_spec = pltpu.PrefetchScalarGridSpec(
        num_scalar_prefetch=2,
        grid=(rows // blk,),
        in_specs=in_specs,
        out_specs=pl.BlockSpec((blk, d), lambda i, be, nu: (i + block0, 0)),
        scratch_shapes=[pltpu.VMEM((d, de), BF16), pltpu.VMEM((d, de), BF16), pltpu.VMEM((de, d), BF16)],
    )
    return pl.pallas_call(
        functools.partial(_expert_kernel, block0=block0),
        grid_spec=grid_spec,
        out_shape=jax.ShapeDtypeStruct((n_blocks * blk, d), BF16),
        input_output_aliases=aliases,
        compiler_params=_params("arbitrary"),
        name="experts",
    )(*operands)


def _dispatch_plan(plan, counts_row, blk):
    n = plan.shape[1]
    p = n * TOP_K
    n_blocks = -(-p // blk) + N_EXPERTS
    counts = counts_row[0, 0:N_EXPERTS].astype(jnp.int32)
    starts = jnp.cumsum(counts) - counts
    padded = (counts + blk - 1) // blk * blk
    padded_end = jnp.cumsum(padded)
    padded_start = padded_end - padded
    expert = plan[0:TOP_K].astype(jnp.int32)
    rank = plan[4:4 + TOP_K].astype(jnp.int32)
    eids = jnp.arange(N_EXPERTS, dtype=jnp.int32)
    dest = rank + jnp.sum(jnp.where(expert[..., None] == eids, padded_start, 0), axis=-1)
    token = jnp.broadcast_to(jnp.arange(n, dtype=jnp.int32), (TOP_K, n))
    _, order_tok = lax.sort((dest.reshape(p), token.reshape(p)), num_keys=1)
    block_start = jnp.arange(n_blocks, dtype=jnp.int32) * blk
    block_expert = jnp.minimum(
        jnp.sum((padded_end[None, :] <= block_start[:, None]).astype(jnp.int32), axis=1), N_EXPERTS - 1)
    row_id = jnp.arange(n_blocks * blk, dtype=jnp.int32).reshape(n_blocks, blk)
    j = row_id - padded_start[block_expert][:, None]
    src = jnp.clip(starts[block_expert][:, None] + j, 0, p - 1)
    valid = (j < counts[block_expert][:, None]).reshape(-1)
    tok_pad = jnp.where(valid, order_tok[src.reshape(-1)], row_id.reshape(-1) % n)
    n_used = (padded_end[-1:] // blk).astype(jnp.int32)
    return tok_pad, dest, block_expert, n_used


def _final_kernel(x1_ref, y0_ref, y1_ref, route_ref, g_ref, b_ref, o_ref, *, alpha):
    route = route_ref[...]
    moe = route[:, 2:3] * y0_ref[...].astype(F32) + route[:, 3:4] * y1_ref[...].astype(F32)
    o_ref[...] = _layernorm(alpha * x1_ref[...].astype(F32) + moe, g_ref[...], b_ref[...])


def _final(x1, y0, y1, route, g2, b2, alpha):
    n, d = x1.shape
    tm = min(TOKEN_TILE, n)
    row = lambda i: (i, 0)
    fix = lambda i: (0, 0)
    return pl.pallas_call(
        functools.partial(_final_kernel, alpha=alpha),
        grid=(n // tm,),
        in_specs=[pl.BlockSpec((tm, d), row), pl.BlockSpec((tm, d), row), pl.BlockSpec((tm, d), row),
                  pl.BlockSpec((tm, LANES), row), pl.BlockSpec((1, d), fix), pl.BlockSpec((1, d), fix)],
        out_specs=pl.BlockSpec((tm, d), row),
        out_shape=jax.ShapeDtypeStruct((n, d), F32),
        compiler_params=_params("parallel"),
        name="final",
    )(x1, y0, y1, route, g2, b2)


def _pad_lanes(v, width=LANES):
    v = v.reshape(1, -1).astype(F32)
    return jnp.pad(v, ((0, 0), (0, width - v.shape[1])))


def _layer(x, conv_prev, s_prev, k_cache, v_cache, p, alpha, lam_init):
    b, l, d = x.shape
    n = b * l
    x2 = x.reshape(n, d)
    conv_prev8 = jnp.pad(conv_prev.astype(F32), ((0, 0), (CONV_PAD - (CONV_W - 1), 0), (0, 0)))
    yc, z, q, k, v, kb, vb, ab, convn = _proj(x2, l, p["ln_in_g"], p["ln_in_b"], p["w_main"], p["w_ab"],
                                              conv_prev8, p["conv_w"])
    conv_new = convn[:, CONV_PAD - (CONV_W - 1):, :]
    og, s_new = _gdn(yc.reshape(b, l, -1), z.reshape(b, l, -1), ab.reshape(b, l, LANES),
                     s_prev.astype(F32), p["alog_row"], p["dtb_row"], p["nw_row"])

    q3 = q.reshape(b, l, -1)
    kb3 = kb.reshape(b, l, -1)
    vb3 = vb.reshape(b, l, -1)
    if k_cache is None:
        od = _attn_prompt(q3, kb3, vb3, p["slopes"], p["lamv"], p["sw_row"], lam_init)
    else:
        past = k_cache.shape[1]
        od = _attn_sample(q3, kb3, vb3, k_cache.reshape(b, past * DIFF_HEADS, DIFF_DV),
                          v_cache.reshape(b, past * DIFF_HEADS, DIFF_DV),
                          p["slopes"], p["lamv"], p["sw_row"], lam_init)

    x1b, route, counts_row = _mix(x2, og.reshape(n, -1), od.reshape(n, -1), p["ln_in_g"], p["ln_in_b"],
                                  p["wo"], p["ln1_g"], p["ln1_b"], p["rw"], p["rb"], alpha)

    blk = _expert_block(n)
    plan = jnp.transpose(route[:, 0:PLAN_ROWS])
    tok_pad, dest, block_expert, n_used = _dispatch_plan(plan, counts_row, blk)
    n_blocks = block_expert.shape[0]
    n_parts = EXPERT_PARTS if n_blocks >= EXPERT_PARTS * N_EXPERTS else 1
    bounds = [n_blocks * i // n_parts for i in range(n_parts + 1)]
    xbs = [jnp.take(x1b, tok_pad[lo * blk:hi * blk], axis=0, mode="clip") for lo, hi in zip(bounds, bounds[1:])]
    yb = None
    for lo, xb in zip(bounds, xbs):
        yb = _experts(xb, lo, yb, n_blocks, block_expert, n_used, p["w1"], p["w3"], p["w2"], blk)
    y0 = jnp.take(yb, dest[0], axis=0, mode="clip")
    y1 = jnp.take(yb, dest[1], axis=0, mode="clip")
    y = _final(x1b, y0, y1, route, p["ln2_g"], p["ln2_b"], alpha)

    return (y.reshape(b, l, d), conv_new, s_new,
            k.reshape(b, l, DIFF_HEADS, 2 * DIFF_DH), v.reshape(b, l, DIFF_HEADS, DIFF_DV))


def kernel(x_prompt, x_sample, cache_attn_k, cache_attn_v, state_gdn, state_conv, ln_in_g, ln_in_b, w_in, conv_w, gdn_a_log, gdn_dt_bias, gdn_norm_w, lam_q1, lam_k1, lam_q2, lam_k2, subln_w, w_o, ln1_g, ln1_b, router_g_w, router_g_b, router_e_w, router_e_b, w1, w3, w2, ln2_g, ln2_b):
    depth = w_in.shape[0]
    assert depth == 1, "single-layer step"
    d = x_prompt.shape[-1]
    alpha = (2 * depth) ** 0.25
    lam_init = 0.8 - 0.6 * math.exp(-0.3 * 0)
    row = lambda t: t.reshape(1, -1).astype(F32)

    wi = w_in[0]
    conv_ch = conv_w.shape[-1]
    gw = GDN_HEADS * GDN_D
    c_ab = conv_ch + gw
    c_q = c_ab + 2 * GDN_HEADS
    w_main = jnp.concatenate([wi[:, :c_ab], wi[:, c_q:]], axis=1).astype(BF16)
    w_ab = jnp.pad(wi[:, c_ab:c_q], ((0, 0), (0, LANES - 2 * GDN_HEADS))).astype(BF16)
    rcat = jnp.concatenate([router_g_w[0], router_e_w[0]], axis=1)
    rcat = jnp.pad(rcat, ((0, 0), (0, LANES - rcat.shape[1])))
    r_hi = rcat.astype(BF16)
    r_lo = (rcat - r_hi.astype(F32)).astype(BF16)
    p = {
        "ln_in_g": row(ln_in_g), "ln_in_b": row(ln_in_b), "w_main": w_main, "w_ab": w_ab,
        "conv_w": conv_w[0].astype(F32), "alog_row": _pad_lanes(gdn_a_log[0]),
        "dtb_row": _pad_lanes(gdn_dt_bias[0]), "nw_row": row(gdn_norm_w[0]),
        "slopes": jnp.asarray(2.0 ** (-8.0 * (np.arange(DIFF_HEADS) + 1) / DIFF_HEADS), F32),
        "lamv": jnp.stack([lam_q1[0], lam_k1[0], lam_q2[0], lam_k2[0]]).astype(F32),
        "sw_row": row(subln_w[0]), "wo": w_o[0].astype(BF16),
        "ln1_g": row(ln1_g[0]), "ln1_b": row(ln1_b[0]),
        "rw": jnp.stack([r_hi, r_lo]), "rb": _pad_lanes(jnp.concatenate([router_g_b[0], router_e_b[0]])),
        "w1": w1[0], "w3": w3[0], "w2": w2[0],
        "ln2_g": row(ln2_g[0]), "ln2_b": row(ln2_b[0]),
    }

    bp = x_prompt.shape[0]
    conv0 = jnp.zeros((bp, CONV_W - 1, conv_ch), F32)
    s0 = jnp.zeros((bp, GDN_HEADS, GDN_D, GDN_D), F32)
    yp, cp, sp, kp, vp = _layer(x_prompt, conv0, s0, None, None, p, alpha, lam_init)
    ys, cs, ss, ks, vs = _layer(x_sample, state_conv[0], state_gdn[0], cache_attn_k[0], cache_attn_v[0],
                                p, alpha, lam_init)
    return (yp, ys, kp[None], vp[None], sp[None], cp[None], ks[None], vs[None], ss[None], cs[None])
```

```python
import functools
import math

import jax
import jax.numpy as jnp
import numpy as np
from jax import lax
from jax.experimental import pallas as pl
from jax.experimental.pallas import tpu as pltpu

F32 = jnp.float32
BF16 = jnp.bfloat16

CHUNK = 64
CONV_W = 4
GDN_HEADS = 4
GDN_D = 128
DIFF_HEADS = 4
DIFF_DH = 64
DIFF_DV = 2 * DIFF_DH
N_GROUPS = 4
EXPERTS_PER_GROUP = 8
N_EXPERTS = N_GROUPS * EXPERTS_PER_GROUP
TOP_K = 2
NEG_INF = -1e30
LOG2E = math.log2(math.e)
LANES = 128
CONV_PAD = 8

TOKEN_TILE = 512
GDN_TILE = 64
GDN_SEQS = 4
ATTN_TILE = 256
ATTN_ROWS = 64
CACHE_TILE = 1024
EXPERT_BLOCK_MAX = 512
EXPERT_BLOCK_MIN = 128
EXPERT_PARTS = 3
PLAN_ROWS = 8
VMEM_LIMIT = 48 * 1024 * 1024


def _params(*sem):
    return pltpu.CompilerParams(dimension_semantics=sem, vmem_limit_bytes=VMEM_LIMIT)


def _dot(a, b):
    return jnp.dot(a, b, preferred_element_type=F32)


def _dot_nt(a, b):
    return lax.dot_general(a, b, (((1,), (1,)), ((), ())), preferred_element_type=F32)


def _sigmoid(x):
    return 1.0 / (1.0 + jnp.exp(-x))


def _silu(x):
    return x * _sigmoid(x)


def _softplus(x):
    return jnp.maximum(x, 0.0) + jnp.log(1.0 + jnp.exp(-jnp.abs(x)))


def _layernorm(x, g, b, eps=1e-5):
    mu = jnp.mean(x, axis=-1, keepdims=True)
    xc = x - mu
    var = jnp.mean(xc * xc, axis=-1, keepdims=True)
    return xc * lax.rsqrt(var + eps) * g + b


def _split3(x):
    hi = x.astype(BF16)
    r1 = x - hi.astype(F32)
    mid = r1.astype(BF16)
    lo = (r1 - mid.astype(F32)).astype(BF16)
    return hi, mid, lo


def _proj_kernel(x_ref, g_ref, b_ref, w_ref, wab_ref, convp_ref, cw_ref,
                 y_ref, z_ref, q_ref, k_ref, v_ref, kb_ref, vb_ref, ab_ref, convn_ref,
                 xc_scr, *, rows, n_seq, tiles_per_seq):
    xb = _layernorm(x_ref[...], g_ref[...], b_ref[...]).astype(BF16)
    w = GDN_HEADS * GDN_D
    stride = rows + CONV_PAD
    if tiles_per_seq > 1:
        first = pl.program_id(0) % tiles_per_seq == 0

        @pl.when(first)
        def _():
            xc_scr[0:CONV_PAD, :] = convp_ref[0]

        @pl.when(jnp.logical_not(first))
        def _():
            xc_scr[0:CONV_PAD, :] = xc_scr[rows:rows + CONV_PAD, :]
    else:
        for s in range(n_seq):
            xc_scr[s * stride:s * stride + CONV_PAD, :] = convp_ref[s]
    cw = cw_ref[...]

    for j in range(3):
        cs = slice(j * w, (j + 1) * w)
        qkv = _dot(xb, w_ref[:, cs])
        for s in range(n_seq):
            base = s * stride + CONV_PAD
            xc_scr[base:base + rows, cs] = qkv[s * rows:(s + 1) * rows, :]
            lo = base - (CONV_W - 1)
            y = cw[0:1, cs] * xc_scr[lo:lo + rows, cs]
            for t in range(1, CONV_W):
                y = y + cw[t:t + 1, cs] * xc_scr[lo + t:lo + t + rows, cs]
            hy = 0.5 * y
            y_ref[s * rows:(s + 1) * rows, cs] = (hy + hy * jnp.tanh(hy)).astype(BF16)
    for s in range(n_seq):
        convn_ref[s] = xc_scr[s * stride + rows:s * stride + rows + CONV_PAD, :]
    z_ref[...] = _dot(xb, w_ref[:, 3 * w:4 * w]).astype(BF16)
    q_ref[...] = _dot(xb, w_ref[:, 4 * w:5 * w]).astype(BF16)
    tm = x_ref.shape[0]
    k = _dot(xb, w_ref[:, 5 * w:6 * w])
    kb_ref[...] = k.astype(BF16)
    v = _dot(xb, w_ref[:, 6 * w:7 * w])
    vb_ref[...] = v.astype(BF16)
    for h in range(DIFF_HEADS):
        k_ref[pl.ds(h, tm, stride=DIFF_HEADS), :] = k[:, h * DIFF_DV:(h + 1) * DIFF_DV]
        v_ref[pl.ds(h, tm, stride=DIFF_HEADS), :] = v[:, h * DIFF_DV:(h + 1) * DIFF_DV]
    ab_ref[...] = _dot(xb, wab_ref[...])


def _proj(x2, seq_len, ln_g, ln_b, w_main, w_ab, conv_prev8, conv_w):
    n, d = x2.shape
    tm = min(TOKEN_TILE, n)
    cc = conv_w.shape[1]
    rows = min(seq_len, tm)
    n_seq = tm // rows
    tiles_per_seq = seq_len // rows
    assert tm % rows == 0 and seq_len % rows == 0 and rows % CONV_PAD == 0
    row = lambda i: (i, 0)
    fix = lambda i: (0, 0)
    seq = lambda i: (i // tiles_per_seq, 0, 0)
    hd = DIFF_HEADS * DIFF_DV
    outs = [(1, cc, BF16), (1, hd, BF16), (1, hd, BF16), (DIFF_HEADS, DIFF_DV, F32),
            (DIFF_HEADS, DIFF_DV, F32), (1, hd, BF16), (1, hd, BF16), (1, LANES, F32)]
    return pl.pallas_call(
        functools.partial(_proj_kernel, rows=rows, n_seq=n_seq, tiles_per_seq=tiles_per_seq),
        grid=(n // tm,),
        in_specs=[pl.BlockSpec((tm, d), row), pl.BlockSpec((1, d), fix), pl.BlockSpec((1, d), fix),
                  pl.BlockSpec(w_main.shape, fix), pl.BlockSpec(w_ab.shape, fix),
                  pl.BlockSpec((n_seq, CONV_PAD, cc), seq), pl.BlockSpec((CONV_W, cc), fix)],
        out_specs=[pl.BlockSpec((tm * r, c), row) for r, c, _ in outs]
                  + [pl.BlockSpec((n_seq, CONV_PAD, cc), seq)],
        out_shape=[jax.ShapeDtypeStruct((n * r, c), t) for r, c, t in outs]
                  + [jax.ShapeDtypeStruct(conv_prev8.shape, F32)],
        scratch_shapes=[pltpu.VMEM((n_seq * (rows + CONV_PAD), cc), F32)],
        compiler_params=_params("arbitrary"),
        name="proj",
    )(x2, ln_g, ln_b, w_main, w_ab, conv_prev8, conv_w)


def _gdn_kernel(y_ref, z_ref, ab_ref, s0_ref, alog_ref, dtb_ref, nw_ref,
                o_ref, sn_ref,
                s_scr, cum_scr, a_scr, p_scr, attn_scr, rhs_scr, sol_scr, qg_scr, kdt_scr, u_scr, os_scr,
                *, nb, tl, c):
    l = pl.program_id(1)
    d = GDN_D
    hw = GDN_HEADS * d

    @pl.when(l == 0)
    def _():
        s_scr[...] = s0_ref[...]

    g_all = [-jnp.exp(alog_ref[...]) * _softplus(ab_ref[bi] + dtb_ref[...]) for bi in range(nb)]
    beta_all = [_sigmoid(ab_ref[bi]) for bi in range(nb)]
    nw = nw_ref[...]

    rows = lax.broadcasted_iota(jnp.int32, (c, c), 0)
    cols = lax.broadcasted_iota(jnp.int32, (c, c), 1)
    incl = cols <= rows
    strict = cols < rows
    tri = jnp.where(incl, 1.0, 0.0).astype(BF16)
    grow = lax.broadcasted_iota(jnp.int32, (c, LANES), 0)
    glane = lax.broadcasted_iota(jnp.int32, (c, LANES), 1)

    n_chunks = tl // c
    groups = [(ci, bi, h) for ci in range(n_chunks) for bi in range(nb) for h in range(GDN_HEADS)]
    eye = jnp.where(rows == cols, 1.0, 0.0)

    gmat = jnp.concatenate(
        [jnp.where(glane < c, jnp.where(grow > glane, g_all[bi][ci * c:(ci + 1) * c, h:h + 1], 0.0),
                   g_all[bi][ci * c:(ci + 1) * c, h:h + 1]) for ci, bi, h in groups], axis=1)
    g_hi, g_mid, g_lo = _split3(gmat)
    cum_scr[...] = _dot(tri, g_hi) + _dot(tri, g_mid) + _dot(tri, g_lo)

    for g, (ci, bi, h) in enumerate(groups):
        r0 = ci * c
        beta = beta_all[bi][r0:r0 + c, GDN_HEADS + h:GDN_HEADS + h + 1]
        rel = cum_scr[:, g * LANES:g * LANES + c]
        gam = cum_scr[:, g * LANES + c:g * LANES + c + 1]
        g_last = gam[c - 1:c, :]
        decay = jnp.where(incl, jnp.exp(jnp.where(incl, rel, 0.0)), 0.0)
        eg = jnp.exp(gam)
        ek = jnp.exp(g_last - gam)

        qh = y_ref[bi, r0:r0 + c, h * d:(h + 1) * d].astype(F32)
        kh = y_ref[bi, r0:r0 + c, hw + h * d:hw + (h + 1) * d].astype(F32)
        vh = y_ref[bi, r0:r0 + c, 2 * hw + h * d:2 * hw + (h + 1) * d].astype(F32)
        qn = qh * lax.rsqrt(jnp.sum(qh * qh, axis=-1, keepdims=True) + 1e-6) * (d ** -0.5)
        kn = kh * lax.rsqrt(jnp.sum(kh * kh, axis=-1, keepdims=True) + 1e-6)
        qb = qn.astype(BF16)
        kb = kn.astype(BF16)
        a = jnp.where(strict, beta * _dot_nt(kb, kb) * decay, 0.0)
        a_scr[g] = a
        p_scr[g] = eye - a
        attn_scr[g] = (_dot_nt(qb, kb) * decay).astype(BF16)
        rhs_scr[g, :, 0:d] = (vh * beta).astype(BF16)
        rhs_scr[g, :, d:2 * d] = (kn * (beta * eg)).astype(BF16)
        qg_scr[g] = (qn * eg).astype(BF16)
        kdt_scr[g] = jnp.transpose(kn * ek).astype(BF16)

    n_sq = int(math.log2(c)) - 1
    for it in range(n_sq):
        for g in range(len(groups)):
            akb = a_scr[g].astype(BF16)
            a_scr[g] = _dot(akb, akb)
        for g in range(len(groups)):
            p = p_scr[g]
            p_scr[g] = p + _dot(p.astype(BF16), a_scr[g].astype(BF16))

    for g in range(len(groups)):
        sol_scr[g] = _dot(p_scr[g].astype(BF16), rhs_scr[g])

    for g, (ci, bi, h) in enumerate(groups):
        sb = s_scr[bi, h].astype(BF16)
        u_scr[g] = (sol_scr[g, :, 0:d] - _dot(sol_scr[g, :, d:2 * d].astype(BF16), sb)).astype(BF16)
        os_scr[g] = _dot(qg_scr[g], sb)

    for g, (ci, bi, h) in enumerate(groups):
        r0 = ci * c
        ub = u_scr[g]
        o = os_scr[g] + _dot(attn_scr[g], ub)
        g_last = cum_scr[c - 1:c, g * LANES + c:g * LANES + c + 1]
        s_scr[bi, h] = s_scr[bi, h] * jnp.exp(g_last) + _dot(kdt_scr[g], ub)

        zh = z_ref[bi, r0:r0 + c, h * d:(h + 1) * d].astype(F32)
        on = o * lax.rsqrt(jnp.mean(o * o, axis=-1, keepdims=True) + 1e-6) * nw
        o_ref[bi, r0:r0 + c, h * d:(h + 1) * d] = (on * _silu(zh)).astype(BF16)

    @pl.when(l == pl.num_programs(1) - 1)
    def _():
        sn_ref[...] = s_scr[...]


def _gdn(y, z, ab, s0, alog_row, dtb_row, nw_row):
    b, l, cc = y.shape
    c = min(l, CHUNK)
    tl = min(l, GDN_TILE)
    nb = math.gcd(b, GDN_SEQS)
    assert tl == c, "one chunk per grid step: the state recurrence is staged across groups"
    ng = nb * GDN_HEADS
    hw = GDN_HEADS * GDN_D
    tile = lambda i, j: (i, j, 0)
    fix = lambda i, j: (0, 0)
    return pl.pallas_call(
        functools.partial(_gdn_kernel, nb=nb, tl=tl, c=c),
        grid=(b // nb, l // tl),
        in_specs=[pl.BlockSpec((nb, tl, cc), tile), pl.BlockSpec((nb, tl, hw), tile),
                  pl.BlockSpec((nb, tl, LANES), tile),
                  pl.BlockSpec((nb, GDN_HEADS, GDN_D, GDN_D), lambda i, j: (i, 0, 0, 0)),
                  pl.BlockSpec((1, LANES), fix),
                  pl.BlockSpec((1, LANES), fix), pl.BlockSpec((1, GDN_D), fix)],
        out_specs=[pl.BlockSpec((nb, tl, hw), tile),
                   pl.BlockSpec((nb, GDN_HEADS, GDN_D, GDN_D), lambda i, j: (i, 0, 0, 0))],
        out_shape=[jax.ShapeDtypeStruct((b, l, hw), BF16),
                   jax.ShapeDtypeStruct((b, GDN_HEADS, GDN_D, GDN_D), F32)],
        scratch_shapes=[pltpu.VMEM((nb, GDN_HEADS, GDN_D, GDN_D), F32),
                        pltpu.VMEM((c, ng * LANES), F32), pltpu.VMEM((ng, c, c), F32),
                        pltpu.VMEM((ng, c, c), F32), pltpu.VMEM((ng, c, c), BF16),
                        pltpu.VMEM((ng, c, 2 * GDN_D), BF16), pltpu.VMEM((ng, c, 2 * GDN_D), F32),
                        pltpu.VMEM((ng, c, GDN_D), BF16), pltpu.VMEM((ng, GDN_D, c), BF16),
                        pltpu.VMEM((ng, c, GDN_D), BF16), pltpu.VMEM((ng, c, GDN_D), F32)],
        compiler_params=_params("parallel", "arbitrary"),
        name="gdn",
    )(y, z, ab, s0, alog_row, dtb_row, nw_row)


def _lambda(lamv, lam_init):
    l1 = jnp.sum(lamv[0:1] * lamv[1:2], axis=-1, keepdims=True)
    l2 = jnp.sum(lamv[2:3] * lamv[3:4], axis=-1, keepdims=True)
    return jnp.exp(l1) - jnp.exp(l2) + lam_init


def _attn_bias(q0, k0, nq, nk, slope):
    qpos = q0 + lax.broadcasted_iota(jnp.int32, (nq, nk), 0)
    kpos = k0 + lax.broadcasted_iota(jnp.int32, (nq, nk), 1)
    dist = jnp.abs(qpos - kpos).astype(F32)
    shift = int(math.log2(CHUNK))
    allowed = (kpos >> shift) <= (qpos >> shift)
    return jnp.where(allowed, -slope * dist, NEG_INF)


def _attn_init(stats):
    for m, l, a in stats:
        m[...] = jnp.full(m.shape, NEG_INF, F32)
        l[...] = jnp.zeros(l.shape, F32)
        a[...] = jnp.zeros(a.shape, F32)


def _attn_update(q, k, v, bias, stats):
    for i, (m, l, a) in enumerate(stats):
        lo, hi = i * DIFF_DH, (i + 1) * DIFF_DH
        s = _dot_nt(q[:, lo:hi], k[:, lo:hi]) + bias
        m_new = jnp.maximum(m[...], jnp.max(s, axis=-1, keepdims=True))
        alpha = jnp.exp(m[...] - m_new)
        p = jnp.exp(s - m_new)
        l[...] = alpha * l[...] + jnp.sum(p, axis=-1, keepdims=True)
        a[...] = alpha * a[...] + _dot(p.astype(BF16), v)
        m[...] = m_new


def _attn_finish(stats, lam, sw, lam_init):
    (_, l1, a1), (_, l2, a2) = stats
    o = a1[...] / l1[...] - lam * (a2[...] / l2[...])
    return o * lax.rsqrt(jnp.mean(o * o, axis=-1, keepdims=True) + 1e-6) * sw * (1.0 - lam_init)


def _attn_prompt_kernel(slopes_ref, lamv_ref, sw_ref, q_ref, k_ref, v_ref, o_ref,
                        vx_scr, bias_scr, q_scr, m_scr, acc_scr, sa_scr, sb_scr, pa_scr, pb_scr, ala_scr, alb_scr,
                        *, seq, tile, lam_init):
    t = tile
    r = 2 * t
    rb = ATTN_ROWS
    dv = DIFF_DV
    slope = slopes_ref[pl.program_id(1)] * LOG2E
    q_scale = (DIFF_DH ** -0.5) * LOG2E
    lam = _lambda(lamv_ref[...], lam_init)
    sw = sw_ref[...]
    shift = int(math.log2(CHUNK))

    vx_scr[:, 0:dv] = v_ref[0]
    vx_scr[:, dv:2 * dv] = jnp.ones((seq, dv), BF16)
    qi_ = lax.broadcasted_iota(jnp.int32, (t, t), 0)
    kj_ = lax.broadcasted_iota(jnp.int32, (t, t), 1)
    rel = (qi_ - kj_).astype(F32)
    bias_scr[0] = -slope * rel
    bias_scr[1] = jnp.where((kj_ >> shift) <= (qi_ >> shift), -slope * jnp.abs(rel), NEG_INF)
    lane = lax.broadcasted_iota(jnp.int32, (t, dv), 1)
    frame_shift = slope * t

    def finalize(qi):
        q0 = qi * t
        o1 = acc_scr[0:t, 0:dv] / acc_scr[0:t, dv:2 * dv]
        o2 = acc_scr[t:r, 0:dv] / acc_scr[t:r, dv:2 * dv]
        o = o1 - lam * o2
        on = o * lax.rsqrt(jnp.mean(o * o, axis=-1, keepdims=True) + 1e-6) * sw * (1.0 - lam_init)
        o_ref[0, q0:q0 + t, :] = on.astype(BF16)

    def q_body(qi):
        q0 = qi * t
        qf = q_ref[0, q0:q0 + t, :].astype(F32) * q_scale
        q_scr[0:t, :] = jnp.where(lane < DIFF_DH, qf, 0.0).astype(BF16)
        q_scr[t:r, :] = jnp.where(lane >= DIFF_DH, qf, 0.0).astype(BF16)
        m_scr[...] = jnp.full(m_scr.shape, NEG_INF, F32)
        if qi > 0:
            finalize(qi - 1)
        acc_scr[...] = jnp.zeros(acc_scr.shape, F32)

        def qk(kj, s_buf):
            k0 = min(kj, qi) * t
            s_buf[...] = _dot_nt(q_scr[...], k_ref[0, k0:k0 + t, :])

        def softmax_accumulate(kj, s_buf, p_buf, al_buf):
            k0 = kj * t
            diag = int(kj == qi)
            for i in range(r // rb):
                r0 = i * rb
                b_lo = r0 % t
                s = s_buf[r0:r0 + rb, :] + bias_scr[diag, b_lo:b_lo + rb, :]
                cols = [s[:, j * LANES:(j + 1) * LANES] for j in range(t // LANES)]
                m_prev = m_scr[r0:r0 + rb, :] - frame_shift
                m_cur = jnp.max(functools.reduce(jnp.maximum, cols), axis=-1, keepdims=True)
                m_new = jnp.maximum(m_prev, m_cur)
                al_buf[r0:r0 + rb, :] = jnp.exp2(m_prev - m_new)
                for j, col in enumerate(cols):
                    p_buf[r0:r0 + rb, j * LANES:(j + 1) * LANES] = jnp.exp2(col - m_new).astype(BF16)
                m_scr[r0:r0 + rb, :] = m_new
            pv = _dot(p_buf[...], vx_scr[k0:k0 + t, :])
            for i in range(r // rb):
                r0 = i * rb
                al = al_buf[r0:r0 + rb, :]
                acc_scr[r0:r0 + rb, :] = (jnp.concatenate([al, al], axis=1) * acc_scr[r0:r0 + rb, :]
                                          + pv[r0:r0 + rb, :])

        sets = ((sa_scr, pa_scr, ala_scr), (sb_scr, pb_scr, alb_scr))
        qk(0, sa_scr)
        for kj in range(qi + 1):
            qk(kj + 1, sets[(kj + 1) % 2][0])
            softmax_accumulate(kj, *sets[kj % 2])

    for qi in range(seq // t):
        q_body(qi)
    finalize(seq // t - 1)


def _attn_prompt(q, kb, vb, slopes, lamv, sw_row, lam_init):
    b, l, _ = q.shape
    t = min(ATTN_TILE, l)
    assert t % CHUNK == 0 and l % t == 0 and t % LANES == 0 and t % ATTN_ROWS == 0
    head = lambda i, h: (i, 0, h)
    fix = lambda i, h: (0, 0)
    dv = DIFF_DV
    scratch = [pltpu.VMEM((l, 2 * dv), BF16), pltpu.VMEM((2, t, t), F32),
               pltpu.VMEM((2 * t, dv), BF16), pltpu.VMEM((2 * t, LANES), F32),
               pltpu.VMEM((2 * t, 2 * dv), F32),
               pltpu.VMEM((2 * t, t), F32), pltpu.VMEM((2 * t, t), F32),
               pltpu.VMEM((2 * t, t), BF16), pltpu.VMEM((2 * t, t), BF16),
               pltpu.VMEM((2 * t, LANES), F32), pltpu.VMEM((2 * t, LANES), F32)]
    return pl.pallas_call(
        functools.partial(_attn_prompt_kernel, seq=l, tile=t, lam_init=lam_init),
        grid=(b, DIFF_HEADS),
        in_specs=[pl.BlockSpec(memory_space=pltpu.SMEM),
                  pl.BlockSpec(lamv.shape, fix), pl.BlockSpec(sw_row.shape, fix),
                  pl.BlockSpec((1, l, dv), head), pl.BlockSpec((1, l, dv), head),
                  pl.BlockSpec((1, l, dv), head)],
        out_specs=pl.BlockSpec((1, l, dv), head),
        out_shape=jax.ShapeDtypeStruct((b, l, DIFF_HEADS * dv), BF16),
        scratch_shapes=scratch,
        compiler_params=_params("parallel", "parallel"),
        name="attn_prompt",
    )(slopes, lamv, sw_row, q, kb, vb)


def _attn_sample_kernel(slopes_ref, lamv_ref, sw_ref, q_ref, kc_ref, vc_ref, kn_ref, vn_ref, o_ref,
                        m1, l1, a1, m2, l2, a2, *, past, tk, lam_init):
    j = pl.program_id(1)
    n_cache = past // tk
    lq = q_ref.shape[1]
    heads = range(DIFF_HEADS)
    cols = [slice(h * DIFF_DV, (h + 1) * DIFF_DV) for h in heads]
    stats = [((m1.at[h], l1.at[h], a1.at[h]), (m2.at[h], l2.at[h], a2.at[h])) for h in heads]
    qs = [(q_ref[0, :, cols[h]].astype(F32) * (DIFF_DH ** -0.5)).astype(BF16) for h in heads]

    @pl.when(j == 0)
    def _():
        for h in heads:
            _attn_init(stats[h])

    @pl.when(j < n_cache)
    def _():
        for h in heads:
            k = kc_ref[0, pl.ds(h, tk, stride=DIFF_HEADS), :].astype(BF16)
            v = vc_ref[0, pl.ds(h, tk, stride=DIFF_HEADS), :].astype(BF16)
            _attn_update(qs[h], k, v, _attn_bias(past, j * tk, lq, tk, slopes_ref[h]), stats[h])

    @pl.when(j == n_cache)
    def _():
        lam = _lambda(lamv_ref[...], lam_init)
        for h in heads:
            _attn_update(qs[h], kn_ref[0, :, cols[h]], vn_ref[0, :, cols[h]],
                         _attn_bias(past, past, lq, lq, slopes_ref[h]), stats[h])
            o_ref[0, :, cols[h]] = _attn_finish(stats[h], lam, sw_ref[...], lam_init).astype(BF16)


def _attn_sample(q, kb, vb, k_cache, v_cache, slopes, lamv, sw_row, lam_init):
    b, l, hd = q.shape
    past = k_cache.shape[1] // DIFF_HEADS
    tk = min(CACHE_TILE, past)
    assert past % tk == 0 and past % CHUNK == 0 and l <= CHUNK
    n_cache = past // tk
    per_b = lambda i, j: (i, 0, 0)
    cache = lambda i, j: (i, jnp.minimum(j, n_cache - 1), 0)
    fix = lambda i, j: (0, 0)
    stat = [pltpu.VMEM((DIFF_HEADS, l, 1), F32), pltpu.VMEM((DIFF_HEADS, l, 1), F32),
            pltpu.VMEM((DIFF_HEADS, l, DIFF_DV), F32)]
    return pl.pallas_call(
        functools.partial(_attn_sample_kernel, past=past, tk=tk, lam_init=lam_init),
        grid=(b, n_cache + 1),
        in_specs=[pl.BlockSpec(memory_space=pltpu.SMEM),
                  pl.BlockSpec(lamv.shape, fix), pl.BlockSpec(sw_row.shape, fix),
                  pl.BlockSpec((1, l, hd), per_b),
                  pl.BlockSpec((1, tk * DIFF_HEADS, DIFF_DV), cache),
                  pl.BlockSpec((1, tk * DIFF_HEADS, DIFF_DV), cache),
                  pl.BlockSpec((1, l, hd), per_b), pl.BlockSpec((1, l, hd), per_b)],
        out_specs=pl.BlockSpec((1, l, hd), per_b),
        out_shape=jax.ShapeDtypeStruct((b, l, hd), BF16),
        scratch_shapes=stat + stat,
        compiler_params=_params("parallel", "arbitrary"),
        name="attn_sample",
    )(slopes, lamv, sw_row, q, k_cache, v_cache, kb, vb)


def _mix_kernel(x_ref, og_ref, od_ref, lig_ref, lib_ref, wo_ref, g1_ref, b1_ref, rw_ref, rb_ref, tri_ref,
                x1b_ref, route_ref, cnt_ref, cnt_scr, *, alpha):
    xn = _layernorm(x_ref[...], lig_ref[...], lib_ref[...])
    hw = og_ref.shape[1]
    mixed = _dot(og_ref[...], wo_ref[0:hw, :]) + _dot(od_ref[...], wo_ref[hw:, :])
    x1 = _layernorm(alpha * xn + mixed, g1_ref[...], b1_ref[...])
    x1b_ref[...] = x1.astype(BF16)

    x_hi = x1.astype(BF16)
    x_lo = (x1 - x_hi.astype(F32)).astype(BF16)
    lg = (_dot(x_hi, rw_ref[0]) + _dot(x_lo, rw_ref[0]) + _dot(x_hi, rw_ref[1])) + rb_ref[...]
    lane = lax.broadcasted_iota(jnp.int32, lg.shape, 1)
    lanef = lane.astype(F32)
    big = float(LANES)

    gl = jnp.where(lane < N_GROUPS, lg, -jnp.inf)
    gmax = jnp.max(gl, axis=-1, keepdims=True)
    grp = jnp.min(jnp.where(gl == gmax, lanef, big), axis=-1, keepdims=True)
    g_gate = 1.0 / jnp.sum(jnp.where(lane < N_GROUPS, jnp.exp(gl - gmax), 0.0), axis=-1, keepdims=True)

    e_lo = N_GROUPS + grp * EXPERTS_PER_GROUP
    el = jnp.where((lanef >= e_lo) & (lanef < e_lo + EXPERTS_PER_GROUP), lg, -jnp.inf)
    v1 = jnp.max(el, axis=-1, keepdims=True)
    i1 = jnp.min(jnp.where(el == v1, lanef, big), axis=-1, keepdims=True)
    el2 = jnp.where(lanef == i1, -jnp.inf, el)
    v2 = jnp.max(el2, axis=-1, keepdims=True)
    i2 = jnp.min(jnp.where(el2 == v2, lanef, big), axis=-1, keepdims=True)
    e21 = jnp.exp(v2 - v1)
    p1 = 1.0 / (1.0 + e21)

    @pl.when(pl.program_id(0) == 0)
    def _():
        cnt_scr[...] = jnp.zeros(cnt_scr.shape, F32)

    e0 = i1 - N_GROUPS
    e1 = i2 - N_GROUPS
    hot0 = lanef == e0
    hot1 = lanef == e1
    onehot = jnp.where(hot0 | hot1, 1.0, 0.0)
    before = _dot(tri_ref[...], onehot.astype(BF16)) + cnt_scr[...]
    rank0 = jnp.sum(jnp.where(hot0, before, 0.0), axis=-1, keepdims=True)
    rank1 = jnp.sum(jnp.where(hot1, before, 0.0), axis=-1, keepdims=True)
    tm = onehot.shape[0]
    cnt_scr[...] = before[tm - 1:tm, :] + onehot[tm - 1:tm, :]
    cnt_ref[...] = cnt_scr[...]

    route_ref[...] = jnp.where(lane == 0, e0,
                     jnp.where(lane == 1, e1,
                     jnp.where(lane == 2, g_gate * p1,
                     jnp.where(lane == 3, g_gate * (e21 * p1),
                     jnp.where(lane == 4, rank0,
                     jnp.where(lane == 5, rank1, 0.0))))))


def _mix(x2, og, od, li_g, li_b, wo, g1, b1, rw, rb, alpha):
    n, d = x2.shape
    tm = min(TOKEN_TILE, n)
    row = lambda i: (i, 0)
    fix = lambda i: (0, 0)
    tri = jnp.tri(tm, k=-1, dtype=BF16)
    return pl.pallas_call(
        functools.partial(_mix_kernel, alpha=alpha),
        grid=(n // tm,),
        in_specs=[pl.BlockSpec((tm, d), row), pl.BlockSpec((tm, og.shape[1]), row),
                  pl.BlockSpec((tm, od.shape[1]), row),
                  pl.BlockSpec((1, d), fix), pl.BlockSpec((1, d), fix), pl.BlockSpec(wo.shape, fix),
                  pl.BlockSpec((1, d), fix), pl.BlockSpec((1, d), fix),
                  pl.BlockSpec(rw.shape, lambda i: (0, 0, 0)), pl.BlockSpec((1, LANES), fix),
                  pl.BlockSpec((tm, tm), fix)],
        out_specs=[pl.BlockSpec((tm, d), row), pl.BlockSpec((tm, LANES), row), pl.BlockSpec((1, LANES), fix)],
        out_shape=[jax.ShapeDtypeStruct((n, d), BF16), jax.ShapeDtypeStruct((n, LANES), F32),
                   jax.ShapeDtypeStruct((1, LANES), F32)],
        scratch_shapes=[pltpu.VMEM((1, LANES), F32)],
        compiler_params=_params("arbitrary"),
        name="mix",
    )(x2, og, od, li_g, li_b, wo, g1, b1, rw, rb, tri)


def _expert_kernel(be_ref, nu_ref, x_ref, w1_ref, w3_ref, w2_ref, *refs, block0):
    y_ref, w1b, w3b, w2b = refs[-4:]
    i = pl.program_id(0)
    g = i + block0

    @pl.when((i == 0) | (be_ref[g] != be_ref[jnp.maximum(g - 1, 0)]))
    def _():
        w1b[...] = w1_ref[0].astype(BF16)
        w3b[...] = w3_ref[0].astype(BF16)
        w2b[...] = w2_ref[0].astype(BF16)

    @pl.when(g < nu_ref[0])
    def _():
        x = x_ref[...]
        h1 = _dot(x, w1b[...])
        h3 = _dot(x, w3b[...])
        y_ref[...] = _dot((_silu(h1) * h3).astype(BF16), w2b[...]).astype(BF16)

    @pl.when(g >= nu_ref[0])
    def _():
        y_ref[...] = jnp.zeros(y_ref.shape, y_ref.dtype)


def _expert_block(n_tokens):
    per_expert = max(1, n_tokens * TOP_K // N_EXPERTS)
    return int(min(EXPERT_BLOCK_MAX, max(EXPERT_BLOCK_MIN, 2 ** int(math.log2(per_expert)))))


def _experts(xb, block0, y_prev, n_blocks, block_expert, n_used, w1, w3, w2, blk):
    rows, d = xb.shape
    de = w1.shape[2]
    weight = lambda i, be, nu: (be[i + block0], 0, 0)
    in_specs = [pl.BlockSpec((blk, d), lambda i, be, nu: (i, 0)),
                pl.BlockSpec((1, d, de), weight), pl.BlockSpec((1, d, de), weight),
                pl.BlockSpec((1, de, d), weight)]
    operands = [block_expert, n_used, xb, w1, w3, w2]
    aliases = {}
    if y_prev is not None:
        in_specs.append(pl.BlockSpec(memory_space=pl.ANY))
        aliases = {len(operands): 0}
        operands.append(y_prev)
    grid_spec = pltpu.PrefetchScalarGridSpec(
        num_scalar_prefetch=2,
        grid=(rows // blk,),
        in_specs=in_specs,
        out_specs=pl.BlockSpec((blk, d), lambda i, be, nu: (i + block0, 0)),
        scratch_shapes=[pltpu.VMEM((d, de), BF16), pltpu.VMEM((d, de), BF16), pltpu.VMEM((de, d), BF16)],
    )
    return pl.pallas_call(
        functools.partial(_expert_kernel, block0=block0),
        grid_spec=grid_spec,
        out_shape=jax.ShapeDtypeStruct((n_blocks * blk, d), BF16),
        input_output_aliases=aliases,
        compiler_params=_params("arbitrary"),
        name="experts",
    )(*operands)


def _dispatch_plan(plan, counts_row, blk):
    n = plan.shape[1]
    p = n * TOP_K
    n_blocks = -(-p // blk) + N_EXPERTS
    counts = counts_row[0, 0:N_EXPERTS].astype(jnp.int32)
    starts = jnp.cumsum(counts) - counts
    padded = (counts + blk - 1) // blk * blk
    padded_end = jnp.cumsum(padded)
    padded_start = padded_end - padded
    expert = plan[0:TOP_K].astype(jnp.int32)
    rank = plan[4:4 + TOP_K].astype(jnp.int32)
    eids = jnp.arange(N_EXPERTS, dtype=jnp.int32)
    dest = rank + jnp.sum(jnp.where(expert[..., None] == eids, padded_start, 0), axis=-1)
    token = jnp.broadcast_to(jnp.arange(n, dtype=jnp.int32), (TOP_K, n))
    _, order_tok = lax.sort((dest.reshape(p), token.reshape(p)), num_keys=1)
    block_start = jnp.arange(n_blocks, dtype=jnp.int32) * blk
    block_expert = jnp.minimum(
        jnp.sum((padded_end[None, :] <= block_start[:, None]).astype(jnp.int32), axis=1), N_EXPERTS - 1)
    row_id = jnp.arange(n_blocks * blk, dtype=jnp.int32).reshape(n_blocks, blk)
    j = row_id - padded_start[block_expert][:, None]
    src = jnp.clip(starts[block_expert][:, None] + j, 0, p - 1)
    valid = (j < counts[block_expert][:, None]).reshape(-1)
    tok_pad = jnp.where(valid, order_tok[src.reshape(-1)], row_id.reshape(-1) % n)
    n_used = (padded_end[-1:] // blk).astype(jnp.int32)
    return tok_pad, dest, block_expert, n_used


def _final_kernel(x1_ref, y0_ref, y1_ref, route_ref, g_ref, b_ref, o_ref, *, alpha):
    route = route_ref[...]
    moe = route[:, 2:3] * y0_ref[...].astype(F32) + route[:, 3:4] * y1_ref[...].astype(F32)
    o_ref[...] = _layernorm(alpha * x1_ref[...].astype(F32) + moe, g_ref[...], b_ref[...])


def _final(x1, y0, y1, route, g2, b2, alpha):
    n, d = x1.shape
    tm = min(TOKEN_TILE, n)
    row = lambda i: (i, 0)
    fix = lambda i: (0, 0)
    return pl.pallas_call(
        functools.partial(_final_kernel, alpha=alpha),
        grid=(n // tm,),
        in_specs=[pl.BlockSpec((tm, d), row), pl.BlockSpec((tm, d), row), pl.BlockSpec((tm, d), row),
                  pl.BlockSpec((tm, LANES), row), pl.BlockSpec((1, d), fix), pl.BlockSpec((1, d), fix)],
        out_specs=pl.BlockSpec((tm, d), row),
        out_shape=jax.ShapeDtypeStruct((n, d), F32),
        compiler_params=_params("parallel"),
        name="final",
    )(x1, y0, y1, route, g2, b2)


def _pad_lanes(v, width=LANES):
    v = v.reshape(1, -1).astype(F32)
    return jnp.pad(v, ((0, 0), (0, width - v.shape[1])))


def _layer(x, conv_prev, s_prev, k_cache, v_cache, p, alpha, lam_init):
    b, l, d = x.shape
    n = b * l
    x2 = x.reshape(n, d)
    conv_prev8 = jnp.pad(conv_prev.astype(F32), ((0, 0), (CONV_PAD - (CONV_W - 1), 0), (0, 0)))
    yc, z, q, k, v, kb, vb, ab, convn = _proj(x2, l, p["ln_in_g"], p["ln_in_b"], p["w_main"], p["w_ab"],
                                              conv_prev8, p["conv_w"])
    conv_new = convn[:, CONV_PAD - (CONV_W - 1):, :]
    og, s_new = _gdn(yc.reshape(b, l, -1), z.reshape(b, l, -1), ab.reshape(b, l, LANES),
                     s_prev.astype(F32), p["alog_row"], p["dtb_row"], p["nw_row"])

    q3 = q.reshape(b, l, -1)
    kb3 = kb.reshape(b, l, -1)
    vb3 = vb.reshape(b, l, -1)
    if k_cache is None:
        od = _attn_prompt(q3, kb3, vb3, p["slopes"], p["lamv"], p["sw_row"], lam_init)
    else:
        past = k_cache.shape[1]
        od = _attn_sample(q3, kb3, vb3, k_cache.reshape(b, past * DIFF_HEADS, DIFF_DV),
                          v_cache.reshape(b, past * DIFF_HEADS, DIFF_DV),
                          p["slopes"], p["lamv"], p["sw_row"], lam_init)

    x1b, route, counts_row = _mix(x2, og.reshape(n, -1), od.reshape(n, -1), p["ln_in_g"], p["ln_in_b"],
                                  p["wo"], p["ln1_g"], p["ln1_b"], p["rw"], p["rb"], alpha)

    blk = _expert_block(n)
    plan = jnp.transpose(route[:, 0:PLAN_ROWS])
    tok_pad, dest, block_expert, n_used = _dispatch_plan(plan, counts_row, blk)
    n_blocks = block_expert.shape[0]
    n_parts = EXPERT_PARTS if n_blocks >= EXPERT_PARTS * N_EXPERTS else 1
    bounds = [n_blocks * i // n_parts for i in range(n_parts + 1)]
    xbs = [jnp.take(x1b, tok_pad[lo * blk:hi * blk], axis=0, mode="clip") for lo, hi in zip(bounds, bounds[1:])]
    yb = None
    for lo, xb in zip(bounds, xbs):
        yb = _experts(xb, lo, yb, n_blocks, block_expert, n_used, p["w1"], p["w3"], p["w2"], blk)
    y0 = jnp.take(yb, dest[0], axis=0, mode="clip")
    y1 = jnp.take(yb, dest[1], axis=0, mode="clip")
    y = _final(x1b, y0, y1, route, p["ln2_g"], p["ln2_b"], alpha)

    return (y.reshape(b, l, d), conv_new, s_new,
            k.reshape(b, l, DIFF_HEADS, 2 * DIFF_DH), v.reshape(b, l, DIFF_HEADS, DIFF_DV))


def kernel(x_prompt, x_sample, cache_attn_k, cache_attn_v, state_gdn, state_conv, ln_in_g, ln_in_b, w_in, conv_w, gdn_a_log, gdn_dt_bias, gdn_norm_w, lam_q1, lam_k1, lam_q2, lam_k2, subln_w, w_o, ln1_g, ln1_b, router_g_w, router_g_b, router_e_w, router_e_b, w1, w3, w2, ln2_g, ln2_b):
    depth = w_in.shape[0]
    assert depth == 1, "single-layer step"
    alpha = (2 * depth) ** 0.25
    lam_init = 0.8 - 0.6 * math.exp(-0.3 * 0)
    row = lambda t: t.reshape(1, -1).astype(F32)

    wi = w_in[0]
    conv_ch = conv_w.shape[-1]
    gw = GDN_HEADS * GDN_D
    c_ab = conv_ch + gw
    c_q = c_ab + 2 * GDN_HEADS
    w_main = jnp.concatenate([wi[:, :c_ab], wi[:, c_q:]], axis=1).astype(BF16)
    w_ab = jnp.pad(wi[:, c_ab:c_q], ((0, 0), (0, LANES - 2 * GDN_HEADS))).astype(BF16)
    rcat = jnp.concatenate([router_g_w[0], router_e_w[0]], axis=1)
    rcat = jnp.pad(rcat, ((0, 0), (0, LANES - rcat.shape[1])))
    r_hi = rcat.astype(BF16)
    r_lo = (rcat - r_hi.astype(F32)).astype(BF16)
    p = {
        "ln_in_g": row(ln_in_g), "ln_in_b": row(ln_in_b), "w_main": w_main, "w_ab": w_ab,
        "conv_w": conv_w[0].astype(F32), "alog_row": _pad_lanes(gdn_a_log[0]),
        "dtb_row": _pad_lanes(gdn_dt_bias[0]), "nw_row": row(gdn_norm_w[0]),
        "slopes": jnp.asarray(2.0 ** (-8.0 * (np.arange(DIFF_HEADS) + 1) / DIFF_HEADS), F32),
        "lamv": jnp.stack([lam_q1[0], lam_k1[0], lam_q2[0], lam_k2[0]]).astype(F32),
        "sw_row": row(subln_w[0]), "wo": w_o[0].astype(BF16),
        "ln1_g": row(ln1_g[0]), "ln1_b": row(ln1_b[0]),
        "rw": jnp.stack([r_hi, r_lo]), "rb": _pad_lanes(jnp.concatenate([router_g_b[0], router_e_b[0]])),
        "w1": w1[0], "w3": w3[0], "w2": w2[0],
        "ln2_g": row(ln2_g[0]), "ln2_b": row(ln2_b[0]),
    }

    bp = x_prompt.shape[0]
    conv0 = jnp.zeros((bp, CONV_W - 1, conv_ch), F32)
    s0 = jnp.zeros((bp, GDN_HEADS, GDN_D, GDN_D), F32)
    yp, cp, sp, kp, vp = _layer(x_prompt, conv0, s0, None, None, p, alpha, lam_init)
    ys, cs, ss, ks, vs = _layer(x_sample, state_conv[0], state_gdn[0], cache_attn_k[0], cache_attn_v[0],
                                p, alpha, lam_init)
    return (yp, ys, kp[None], vp[None], sp[None], cp[None], ks[None], vs[None], ss[None], cs[None])
```

```python
import functools
import math

import jax
import jax.numpy as jnp
import numpy as np
from jax import lax
from jax.experimental import pallas as pl
from jax.experimental.pallas import tpu as pltpu

F32 = jnp.float32
BF16 = jnp.bfloat16

CHUNK = 64
CONV_W = 4
GDN_HEADS = 4
GDN_D = 128
DIFF_HEADS = 4
DIFF_DH = 64
DIFF_DV = 2 * DIFF_DH
N_GROUPS = 4
EXPERTS_PER_GROUP = 8
N_EXPERTS = N_GROUPS * EXPERTS_PER_GROUP
TOP_K = 2
NEG_INF = -1e30
LOG2E = math.log2(math.e)
LANES = 128
CONV_PAD = 8

TOKEN_TILE = 512
GDN_TILE = 64
GDN_SEQS = 4
ATTN_TILE = 256
ATTN_ROWS = 64
CACHE_TILE = 1024
EXPERT_BLOCK_MAX = 512
EXPERT_BLOCK_MIN = 128
EXPERT_PARTS = 3
PLAN_ROWS = 8
VMEM_LIMIT = 48 * 1024 * 1024


def _params(*sem):
    return pltpu.CompilerParams(dimension_semantics=sem, vmem_limit_bytes=VMEM_LIMIT)


def _dot(a, b):
    return jnp.dot(a, b, preferred_element_type=F32)


def _dot_nt(a, b):
    return lax.dot_general(a, b, (((1,), (1,)), ((), ())), preferred_element_type=F32)


def _sigmoid(x):
    return 1.0 / (1.0 + jnp.exp(-x))


def _silu(x):
    return x * _sigmoid(x)


def _softplus(x):
    return jnp.maximum(x, 0.0) + jnp.log(1.0 + jnp.exp(-jnp.abs(x)))


def _layernorm(x, g, b, eps=1e-5):
    mu = jnp.mean(x, axis=-1, keepdims=True)
    xc = x - mu
    var = jnp.mean(xc * xc, axis=-1, keepdims=True)
    return xc * lax.rsqrt(var + eps) * g + b


def _split3(x):
    hi = x.astype(BF16)
    r1 = x - hi.astype(F32)
    mid = r1.astype(BF16)
    lo = (r1 - mid.astype(F32)).astype(BF16)
    return hi, mid, lo


def _proj_kernel(x_ref, g_ref, b_ref, w_ref, wab_ref, convp_ref, cw_ref,
                 y_ref, z_ref, q_ref, k_ref, v_ref, kb_ref, vb_ref, ab_ref, convn_ref,
                 xc_scr, *, rows, n_seq, tiles_per_seq):
    xb = _layernorm(x_ref[...], g_ref[...], b_ref[...]).astype(BF16)
    w = GDN_HEADS * GDN_D
    stride = rows + CONV_PAD
    if tiles_per_seq > 1:
        first = pl.program_id(0) % tiles_per_seq == 0

        @pl.when(first)
        def _():
            xc_scr[0:CONV_PAD, :] = convp_ref[0]

        @pl.when(jnp.logical_not(first))
        def _():
            xc_scr[0:CONV_PAD, :] = xc_scr[rows:rows + CONV_PAD, :]
    else:
        for s in range(n_seq):
            xc_scr[s * stride:s * stride + CONV_PAD, :] = convp_ref[s]
    cw = cw_ref[...]

    for j in range(3):
        cs = slice(j * w, (j + 1) * w)
        qkv = _dot(xb, w_ref[:, cs])
        for s in range(n_seq):
            base = s * stride + CONV_PAD
            xc_scr[base:base + rows, cs] = qkv[s * rows:(s + 1) * rows, :]
            lo = base - (CONV_W - 1)
            y = cw[0:1, cs] * xc_scr[lo:lo + rows, cs]
            for t in range(1, CONV_W):
                y = y + cw[t:t + 1, cs] * xc_scr[lo + t:lo + t + rows, cs]
            hy = 0.5 * y
            y_ref[s * rows:(s + 1) * rows, cs] = (hy + hy * jnp.tanh(hy)).astype(BF16)
    for s in range(n_seq):
        convn_ref[s] = xc_scr[s * stride + rows:s * stride + rows + CONV_PAD, :]
    z_ref[...] = _dot(xb, w_ref[:, 3 * w:4 * w]).astype(BF16)
    q_ref[...] = _dot(xb, w_ref[:, 4 * w:5 * w]).astype(BF16)
    tm = x_ref.shape[0]
    k = _dot(xb, w_ref[:, 5 * w:6 * w])
    kb_ref[...] = k.astype(BF16)
    v = _dot(xb, w_ref[:, 6 * w:7 * w])
    vb_ref[...] = v.astype(BF16)
    for h in range(DIFF_HEADS):
        k_ref[pl.ds(h, tm, stride=DIFF_HEADS), :] = k[:, h * DIFF_DV:(h + 1) * DIFF_DV]
        v_ref[pl.ds(h, tm, stride=DIFF_HEADS), :] = v[:, h * DIFF_DV:(h + 1) * DIFF_DV]
    ab_ref[...] = _dot(xb, wab_ref[...])


def _proj(x2, seq_len, ln_g, ln_b, w_main, w_ab, conv_prev8, conv_w):
    n, d = x2.shape
    tm = min(TOKEN_TILE, n)
    cc = conv_w.shape[1]
    rows = min(seq_len, tm)
    n_seq = tm // rows
    tiles_per_seq = seq_len // rows
    assert tm % rows == 0 and seq_len % rows == 0 and rows % CONV_PAD == 0
    row = lambda i: (i, 0)
    fix = lambda i: (0, 0)
    seq = lambda i: (i // tiles_per_seq, 0, 0)
    hd = DIFF_HEADS * DIFF_DV
    outs = [(1, cc, BF16), (1, hd, BF16), (1, hd, BF16), (DIFF_HEADS, DIFF_DV, F32),
            (DIFF_HEADS, DIFF_DV, F32), (1, hd, BF16), (1, hd, BF16), (1, LANES, F32)]
    return pl.pallas_call(
        functools.partial(_proj_kernel, rows=rows, n_seq=n_seq, tiles_per_seq=tiles_per_seq),
        grid=(n // tm,),
        in_specs=[pl.BlockSpec((tm, d), row), pl.BlockSpec((1, d), fix), pl.BlockSpec((1, d), fix),
                  pl.BlockSpec(w_main.shape, fix), pl.BlockSpec(w_ab.shape, fix),
                  pl.BlockSpec((n_seq, CONV_PAD, cc), seq), pl.BlockSpec((CONV_W, cc), fix)],
        out_specs=[pl.BlockSpec((tm * r, c), row) for r, c, _ in outs]
                  + [pl.BlockSpec((n_seq, CONV_PAD, cc), seq)],
        out_shape=[jax.ShapeDtypeStruct((n * r, c), t) for r, c, t in outs]
                  + [jax.ShapeDtypeStruct(conv_prev8.shape, F32)],
        scratch_shapes=[pltpu.VMEM((n_seq * (rows + CONV_PAD), cc), F32)],
        compiler_params=_params("arbitrary"),
        name="proj",
    )(x2, ln_g, ln_b, w_main, w_ab, conv_prev8, conv_w)


def _gdn_kernel(y_ref, z_ref, ab_ref, s0_ref, alog_ref, dtb_ref, nw_ref,
                o_ref, sn_ref,
                s_scr, cum_scr, a_scr, p_scr, attn_scr, rhs_scr, sol_scr, qg_scr, kd_scr, u_scr, os_scr,
                *, nb, tl, c):
    l = pl.program_id(1)
    d = GDN_D
    hw = GDN_HEADS * d

    @pl.when(l == 0)
    def _():
        s_scr[...] = s0_ref[...]

    g_all = [-jnp.exp(alog_ref[...]) * _softplus(ab_ref[bi] + dtb_ref[...]) for bi in range(nb)]
    beta_all = [_sigmoid(ab_ref[bi]) for bi in range(nb)]
    nw = nw_ref[...]

    rows = lax.broadcasted_iota(jnp.int32, (c, c), 0)
    cols = lax.broadcasted_iota(jnp.int32, (c, c), 1)
    incl = cols <= rows
    strict = cols < rows
    tri = jnp.where(incl, 1.0, 0.0).astype(BF16)
    grow = lax.broadcasted_iota(jnp.int32, (c, LANES), 0)
    glane = lax.broadcasted_iota(jnp.int32, (c, LANES), 1)

    n_chunks = tl // c
    groups = [(ci, bi, h) for ci in range(n_chunks) for bi in range(nb) for h in range(GDN_HEADS)]
    eye = jnp.where(rows == cols, 1.0, 0.0)

    gcols = [jnp.broadcast_to(g_all[bi][ci * c:(ci + 1) * c, h:h + 1], (c, LANES)) for ci, bi, h in groups]
    gmat = jnp.concatenate([blk for gc in gcols for blk in (jnp.where(grow > glane, gc, 0.0), gc)], axis=1)
    g_hi, g_mid, g_lo = _split3(gmat)
    cum_scr[...] = _dot(tri, g_hi) + _dot(tri, g_mid) + _dot(tri, g_lo)

    for g, (ci, bi, h) in enumerate(groups):
        r0 = ci * c
        beta = jnp.broadcast_to(beta_all[bi][r0:r0 + c, GDN_HEADS + h:GDN_HEADS + h + 1], (c, LANES))
        rel = cum_scr[:, 2 * g * LANES:2 * g * LANES + c]
        gam = cum_scr[:, (2 * g + 1) * LANES:(2 * g + 2) * LANES]
        g_last = gam[c - 1:c, :]
        decay = jnp.where(incl, jnp.exp(jnp.where(incl, rel, 0.0)), 0.0)
        eg = jnp.exp(gam)
        ek = jnp.exp(g_last - gam)

        qh = y_ref[bi, r0:r0 + c, h * d:(h + 1) * d].astype(F32)
        kh = y_ref[bi, r0:r0 + c, hw + h * d:hw + (h + 1) * d].astype(F32)
        vh = y_ref[bi, r0:r0 + c, 2 * hw + h * d:2 * hw + (h + 1) * d].astype(F32)
        qn = qh * lax.rsqrt(jnp.sum(qh * qh, axis=-1, keepdims=True) + 1e-6) * (d ** -0.5)
        kn = kh * lax.rsqrt(jnp.sum(kh * kh, axis=-1, keepdims=True) + 1e-6)
        qb = qn.astype(BF16)
        kb = kn.astype(BF16)
        a = jnp.where(strict, beta[:, 0:c] * _dot_nt(kb, kb) * decay, 0.0)
        a_scr[g] = a
        p_scr[g] = eye - a
        attn_scr[g] = (_dot_nt(qb, kb) * decay).astype(BF16)
        rhs_scr[g, :, 0:d] = (vh * beta).astype(BF16)
        rhs_scr[g, :, d:2 * d] = (kn * (beta * eg)).astype(BF16)
        qg_scr[g] = (qn * eg).astype(BF16)
        kd_scr[g] = (kn * ek).astype(BF16)

    n_sq = int(math.log2(c)) - 1
    for it in range(n_sq):
        for g in range(len(groups)):
            akb = a_scr[g].astype(BF16)
            a_scr[g] = _dot(akb, akb)
        for g in range(len(groups)):
            p = p_scr[g]
            p_scr[g] = p + _dot(p.astype(BF16), a_scr[g].astype(BF16))

    for g in range(len(groups)):
        sol_scr[g] = _dot(p_scr[g].astype(BF16), rhs_scr[g])

    for g, (ci, bi, h) in enumerate(groups):
        sb = s_scr[bi, h].astype(BF16)
        u_scr[g] = (sol_scr[g, :, 0:d] - _dot(sol_scr[g, :, d:2 * d].astype(BF16), sb)).astype(BF16)
        os_scr[g] = _dot(qg_scr[g], sb)

    for g, (ci, bi, h) in enumerate(groups):
        r0 = ci * c
        ub = u_scr[g]
        o = os_scr[g] + _dot(attn_scr[g], ub)
        g_last = cum_scr[c - 1:c, (2 * g + 1) * LANES:(2 * g + 2) * LANES]
        s_scr[bi, h] = s_scr[bi, h] * jnp.exp(g_last) + lax.dot_general(
            kd_scr[g], ub, (((0,), (0,)), ((), ())), preferred_element_type=F32)

        zh = z_ref[bi, r0:r0 + c, h * d:(h + 1) * d].astype(F32)
        on = o * lax.rsqrt(jnp.mean(o * o, axis=-1, keepdims=True) + 1e-6) * nw
        o_ref[bi, r0:r0 + c, h * d:(h + 1) * d] = (on * _silu(zh)).astype(BF16)

    @pl.when(l == pl.num_programs(1) - 1)
    def _():
        sn_ref[...] = s_scr[...]


def _gdn(y, z, ab, s0, alog_row, dtb_row, nw_row):
    b, l, cc = y.shape
    c = min(l, CHUNK)
    tl = min(l, GDN_TILE)
    nb = math.gcd(b, GDN_SEQS)
    assert tl == c, "one chunk per grid step: the state recurrence is staged across groups"
    ng = nb * GDN_HEADS
    hw = GDN_HEADS * GDN_D
    tile = lambda i, j: (i, j, 0)
    fix = lambda i, j: (0, 0)
    return pl.pallas_call(
        functools.partial(_gdn_kernel, nb=nb, tl=tl, c=c),
        grid=(b // nb, l // tl),
        in_specs=[pl.BlockSpec((nb, tl, cc), tile), pl.BlockSpec((nb, tl, hw), tile),
                  pl.BlockSpec((nb, tl, LANES), tile),
                  pl.BlockSpec((nb, GDN_HEADS, GDN_D, GDN_D), lambda i, j: (i, 0, 0, 0)),
                  pl.BlockSpec((1, LANES), fix),
                  pl.BlockSpec((1, LANES), fix), pl.BlockSpec((1, GDN_D), fix)],
        out_specs=[pl.BlockSpec((nb, tl, hw), tile),
                   pl.BlockSpec((nb, GDN_HEADS, GDN_D, GDN_D), lambda i, j: (i, 0, 0, 0))],
        out_shape=[jax.ShapeDtypeStruct((b, l, hw), BF16),
                   jax.ShapeDtypeStruct((b, GDN_HEADS, GDN_D, GDN_D), F32)],
        scratch_shapes=[pltpu.VMEM((nb, GDN_HEADS, GDN_D, GDN_D), F32),
                        pltpu.VMEM((c, 2 * ng * LANES), F32), pltpu.VMEM((ng, c, c), F32),
                        pltpu.VMEM((ng, c, c), F32), pltpu.VMEM((ng, c, c), BF16),
                        pltpu.VMEM((ng, c, 2 * GDN_D), BF16), pltpu.VMEM((ng, c, 2 * GDN_D), F32),
                        pltpu.VMEM((ng, c, GDN_D), BF16), pltpu.VMEM((ng, c, GDN_D), BF16),
                        pltpu.VMEM((ng, c, GDN_D), BF16), pltpu.VMEM((ng, c, GDN_D), F32)],
        compiler_params=_params("parallel", "arbitrary"),
        name="gdn",
    )(y, z, ab, s0, alog_row, dtb_row, nw_row)


def _lambda(lamv, lam_init):
    l1 = jnp.sum(lamv[0:1] * lamv[1:2], axis=-1, keepdims=True)
    l2 = jnp.sum(lamv[2:3] * lamv[3:4], axis=-1, keepdims=True)
    return jnp.exp(l1) - jnp.exp(l2) + lam_init


def _attn_bias(q0, k0, nq, nk, slope):
    qpos = q0 + lax.broadcasted_iota(jnp.int32, (nq, nk), 0)
    kpos = k0 + lax.broadcasted_iota(jnp.int32, (nq, nk), 1)
    dist = jnp.abs(qpos - kpos).astype(F32)
    shift = int(math.log2(CHUNK))
    allowed = (kpos >> shift) <= (qpos >> shift)
    return jnp.where(allowed, -slope * dist, NEG_INF)


def _attn_init(stats):
    for m, l, a in stats:
        m[...] = jnp.full(m.shape, NEG_INF, F32)
        l[...] = jnp.zeros(l.shape, F32)
        a[...] = jnp.zeros(a.shape, F32)


def _attn_update(q, k, v, bias, stats):
    for i, (m, l, a) in enumerate(stats):
        lo, hi = i * DIFF_DH, (i + 1) * DIFF_DH
        s = _dot_nt(q[:, lo:hi], k[:, lo:hi]) + bias
        m_new = jnp.maximum(m[...], jnp.max(s, axis=-1, keepdims=True))
        alpha = jnp.exp(m[...] - m_new)
        p = jnp.exp(s - m_new)
        l[...] = alpha * l[...] + jnp.sum(p, axis=-1, keepdims=True)
        a[...] = alpha * a[...] + _dot(p.astype(BF16), v)
        m[...] = m_new


def _attn_finish(stats, lam, sw, lam_init):
    (_, l1, a1), (_, l2, a2) = stats
    o = a1[...] / l1[...] - lam * (a2[...] / l2[...])
    return o * lax.rsqrt(jnp.mean(o * o, axis=-1, keepdims=True) + 1e-6) * sw * (1.0 - lam_init)


def _attn_prompt_kernel(slopes_ref, lamv_ref, sw_ref, q_ref, k_ref, v_ref, o_ref,
                        vx_scr, bias_scr, q_scr, m_scr, acc_scr, sa_scr, sb_scr, pa_scr, pb_scr, ala_scr, alb_scr,
                        *, seq, tile, lam_init):
    t = tile
    r = 2 * t
    rb = ATTN_ROWS
    dv = DIFF_DV
    slope = slopes_ref[pl.program_id(1)] * LOG2E
    q_scale = (DIFF_DH ** -0.5) * LOG2E
    lam = _lambda(lamv_ref[...], lam_init)
    sw = sw_ref[...]
    shift = int(math.log2(CHUNK))

    vx_scr[:, 0:dv] = v_ref[0]
    vx_scr[:, dv:2 * dv] = jnp.ones((seq, dv), BF16)
    qi_ = lax.broadcasted_iota(jnp.int32, (t, t), 0)
    kj_ = lax.broadcasted_iota(jnp.int32, (t, t), 1)
    rel = (qi_ - kj_).astype(F32)
    bias_scr[0] = -slope * rel
    bias_scr[1] = jnp.where((kj_ >> shift) <= (qi_ >> shift), -slope * jnp.abs(rel), NEG_INF)
    lane = lax.broadcasted_iota(jnp.int32, (t, dv), 1)
    frame_shift = slope * t

    def finalize(qi):
        q0 = qi * t
        o1 = acc_scr[0:t, 0:dv] / acc_scr[0:t, dv:2 * dv]
        o2 = acc_scr[t:r, 0:dv] / acc_scr[t:r, dv:2 * dv]
        o = o1 - lam * o2
        on = o * lax.rsqrt(jnp.mean(o * o, axis=-1, keepdims=True) + 1e-6) * sw * (1.0 - lam_init)
        o_ref[0, q0:q0 + t, :] = on.astype(BF16)

    def q_body(qi):
        q0 = qi * t
        qf = q_ref[0, q0:q0 + t, :].astype(F32) * q_scale
        q_scr[0:t, :] = jnp.where(lane < DIFF_DH, qf, 0.0).astype(BF16)
        q_scr[t:r, :] = jnp.where(lane >= DIFF_DH, qf, 0.0).astype(BF16)
        m_scr[...] = jnp.full(m_scr.shape, NEG_INF, F32)
        if qi > 0:
            finalize(qi - 1)
        acc_scr[...] = jnp.zeros(acc_scr.shape, F32)

        def qk(kj, s_buf):
            k0 = min(kj, qi) * t
            s_buf[...] = _dot_nt(q_scr[...], k_ref[0, k0:k0 + t, :])

        def softmax_accumulate(kj, s_buf, p_buf, al_buf):
            k0 = kj * t
            diag = int(kj == qi)
            for i in range(r // rb):
                r0 = i * rb
                b_lo = r0 % t
                s = s_buf[r0:r0 + rb, :] + bias_scr[diag, b_lo:b_lo + rb, :]
                cols = [s[:, j * LANES:(j + 1) * LANES] for j in range(t // LANES)]
                m_prev = m_scr[r0:r0 + rb, :] - frame_shift
                m_cur = jnp.max(functools.reduce(jnp.maximum, cols), axis=-1, keepdims=True)
                m_new = jnp.maximum(m_prev, m_cur)
                al_buf[r0:r0 + rb, :] = jnp.exp2(m_prev - m_new)
                for j, col in enumerate(cols):
                    p_buf[r0:r0 + rb, j * LANES:(j + 1) * LANES] = jnp.exp2(col - m_new).astype(BF16)
                m_scr[r0:r0 + rb, :] = m_new
            pv = _dot(p_buf[...], vx_scr[k0:k0 + t, :])
            for i in range(r // rb):
                r0 = i * rb
                al = al_buf[r0:r0 + rb, :]
                acc_scr[r0:r0 + rb, :] = (jnp.concatenate([al, al], axis=1) * acc_scr[r0:r0 + rb, :]
                                          + pv[r0:r0 + rb, :])

        sets = ((sa_scr, pa_scr, ala_scr), (sb_scr, pb_scr, alb_scr))
        qk(0, sa_scr)
        for kj in range(qi + 1):
            qk(kj + 1, sets[(kj + 1) % 2][0])
            softmax_accumulate(kj, *sets[kj % 2])

    for qi in range(seq // t):
        q_body(qi)
    finalize(seq // t - 1)


def _attn_prompt(q, kb, vb, slopes, lamv, sw_row, lam_init):
    b, l, _ = q.shape
    t = min(ATTN_TILE, l)
    assert t % CHUNK == 0 and l % t == 0 and t % LANES == 0 and t % ATTN_ROWS == 0
    head = lambda i, h: (i, 0, h)
    fix = lambda i, h: (0, 0)
    dv = DIFF_DV
    scratch = [pltpu.VMEM((l, 2 * dv), BF16), pltpu.VMEM((2, t, t), F32),
               pltpu.VMEM((2 * t, dv), BF16), pltpu.VMEM((2 * t, LANES), F32),
               pltpu.VMEM((2 * t, 2 * dv), F32),
               pltpu.VMEM((2 * t, t), F32), pltpu.VMEM((2 * t, t), F32),
               pltpu.VMEM((2 * t, t), BF16), pltpu.VMEM((2 * t, t), BF16),
               pltpu.VMEM((2 * t, LANES), F32), pltpu.VMEM((2 * t, LANES), F32)]
    return pl.pallas_call(
        functools.partial(_attn_prompt_kernel, seq=l, tile=t, lam_init=lam_init),
        grid=(b, DIFF_HEADS),
        in_specs=[pl.BlockSpec(memory_space=pltpu.SMEM),
                  pl.BlockSpec(lamv.shape, fix), pl.BlockSpec(sw_row.shape, fix),
                  pl.BlockSpec((1, l, dv), head), pl.BlockSpec((1, l, dv), head),
                  pl.BlockSpec((1, l, dv), head)],
        out_specs=pl.BlockSpec((1, l, dv), head),
        out_shape=jax.ShapeDtypeStruct((b, l, DIFF_HEADS * dv), BF16),
        scratch_shapes=scratch,
        compiler_params=_params("parallel", "parallel"),
        name="attn_prompt",
    )(slopes, lamv, sw_row, q, kb, vb)


def _attn_sample_kernel(slopes_ref, lamv_ref, sw_ref, q_ref, kc_ref, vc_ref, kn_ref, vn_ref, o_ref,
                        m1, l1, a1, m2, l2, a2, *, past, tk, lam_init):
    j = pl.program_id(1)
    n_cache = past // tk
    lq = q_ref.shape[1]
    heads = range(DIFF_HEADS)
    cols = [slice(h * DIFF_DV, (h + 1) * DIFF_DV) for h in heads]
    stats = [((m1.at[h], l1.at[h], a1.at[h]), (m2.at[h], l2.at[h], a2.at[h])) for h in heads]
    qs = [(q_ref[0, :, cols[h]].astype(F32) * (DIFF_DH ** -0.5)).astype(BF16) for h in heads]

    @pl.when(j == 0)
    def _():
        for h in heads:
            _attn_init(stats[h])

    @pl.when(j < n_cache)
    def _():
        for h in heads:
            k = kc_ref[0, pl.ds(h, tk, stride=DIFF_HEADS), :].astype(BF16)
            v = vc_ref[0, pl.ds(h, tk, stride=DIFF_HEADS), :].astype(BF16)
            _attn_update(qs[h], k, v, _attn_bias(past, j * tk, lq, tk, slopes_ref[h]), stats[h])

    @pl.when(j == n_cache)
    def _():
        lam = _lambda(lamv_ref[...], lam_init)
        for h in heads:
            _attn_update(qs[h], kn_ref[0, :, cols[h]], vn_ref[0, :, cols[h]],
                         _attn_bias(past, past, lq, lq, slopes_ref[h]), stats[h])
            o_ref[0, :, cols[h]] = _attn_finish(stats[h], lam, sw_ref[...], lam_init).astype(BF16)


def _attn_sample(q, kb, vb, k_cache, v_cache, slopes, lamv, sw_row, lam_init):
    b, l, hd = q.shape
    past = k_cache.shape[1] // DIFF_HEADS
    tk = min(CACHE_TILE, past)
    assert past % tk == 0 and past % CHUNK == 0 and l <= CHUNK
    n_cache = past // tk
    per_b = lambda i, j: (i, 0, 0)
    cache = lambda i, j: (i, jnp.minimum(j, n_cache - 1), 0)
    fix = lambda i, j: (0, 0)
    stat = [pltpu.VMEM((DIFF_HEADS, l, 1), F32), pltpu.VMEM((DIFF_HEADS, l, 1), F32),
            pltpu.VMEM((DIFF_HEADS, l, DIFF_DV), F32)]
    return pl.pallas_call(
        functools.partial(_attn_sample_kernel, past=past, tk=tk, lam_init=lam_init),
        grid=(b, n_cache + 1),
        in_specs=[pl.BlockSpec(memory_space=pltpu.SMEM),
                  pl.BlockSpec(lamv.shape, fix), pl.BlockSpec(sw_row.shape, fix),
                  pl.BlockSpec((1, l, hd), per_b),
                  pl.BlockSpec((1, tk * DIFF_HEADS, DIFF_DV), cache),
                  pl.BlockSpec((1, tk * DIFF_HEADS, DIFF_DV), cache),
                  pl.BlockSpec((1, l, hd), per_b), pl.BlockSpec((1, l, hd), per_b)],
        out_specs=pl.BlockSpec((1, l, hd), per_b),
        out_shape=jax.ShapeDtypeStruct((b, l, hd), BF16),
        scratch_shapes=stat + stat,
        compiler_params=_params("parallel", "arbitrary"),
        name="attn_sample",
    )(slopes, lamv, sw_row, q, k_cache, v_cache, kb, vb)


def _mix_kernel(x_ref, og_ref, od_ref, lig_ref, lib_ref, wo_ref, g1_ref, b1_ref, rw_ref, rb_ref, tri_ref,
                x1b_ref, route_ref, cnt_ref, cnt_scr, *, alpha):
    xn = _layernorm(x_ref[...], lig_ref[...], lib_ref[...])
    hw = og_ref.shape[1]
    mixed = _dot(og_ref[...], wo_ref[0:hw, :]) + _dot(od_ref[...], wo_ref[hw:, :])
    x1 = _layernorm(alpha * xn + mixed, g1_ref[...], b1_ref[...])
    x1b_ref[...] = x1.astype(BF16)

    x_hi = x1.astype(BF16)
    x_lo = (x1 - x_hi.astype(F32)).astype(BF16)
    lg = (_dot(x_hi, rw_ref[0]) + _dot(x_lo, rw_ref[0]) + _dot(x_hi, rw_ref[1])) + rb_ref[...]
    lane = lax.broadcasted_iota(jnp.int32, lg.shape, 1)
    lanef = lane.astype(F32)
    big = float(LANES)

    gl = jnp.where(lane < N_GROUPS, lg, -jnp.inf)
    gmax = jnp.max(gl, axis=-1, keepdims=True)
    grp = jnp.min(jnp.where(gl == gmax, lanef, big), axis=-1, keepdims=True)
    g_gate = 1.0 / jnp.sum(jnp.where(lane < N_GROUPS, jnp.exp(gl - gmax), 0.0), axis=-1, keepdims=True)

    e_lo = N_GROUPS + grp * EXPERTS_PER_GROUP
    el = jnp.where((lanef >= e_lo) & (lanef < e_lo + EXPERTS_PER_GROUP), lg, -jnp.inf)
    v1 = jnp.max(el, axis=-1, keepdims=True)
    i1 = jnp.min(jnp.where(el == v1, lanef, big), axis=-1, keepdims=True)
    el2 = jnp.where(lanef == i1, -jnp.inf, el)
    v2 = jnp.max(el2, axis=-1, keepdims=True)
    i2 = jnp.min(jnp.where(el2 == v2, lanef, big), axis=-1, keepdims=True)
    e21 = jnp.exp(v2 - v1)
    p1 = 1.0 / (1.0 + e21)

    @pl.when(pl.program_id(0) == 0)
    def _():
        cnt_scr[...] = jnp.zeros(cnt_scr.shape, F32)

    e0 = i1 - N_GROUPS
    e1 = i2 - N_GROUPS
    hot0 = lanef == e0
    hot1 = lanef == e1
    onehot = jnp.where(hot0 | hot1, 1.0, 0.0)
    before = _dot(tri_ref[...], onehot.astype(BF16)) + cnt_scr[...]
    rank0 = jnp.sum(jnp.where(hot0, before, 0.0), axis=-1, keepdims=True)
    rank1 = jnp.sum(jnp.where(hot1, before, 0.0), axis=-1, keepdims=True)
    tm = onehot.shape[0]
    cnt_scr[...] = before[tm - 1:tm, :] + onehot[tm - 1:tm, :]
    cnt_ref[...] = cnt_scr[...]

    route_ref[...] = jnp.where(lane == 0, e0,
                     jnp.where(lane == 1, e1,
                     jnp.where(lane == 2, g_gate * p1,
                     jnp.where(lane == 3, g_gate * (e21 * p1),
                     jnp.where(lane == 4, rank0,
                     jnp.where(lane == 5, rank1, 0.0))))))


def _mix(x2, og, od, li_g, li_b, wo, g1, b1, rw, rb, alpha):
    n, d = x2.shape
    tm = min(TOKEN_TILE, n)
    row = lambda i: (i, 0)
    fix = lambda i: (0, 0)
    tri = jnp.tri(tm, k=-1, dtype=BF16)
    return pl.pallas_call(
        functools.partial(_mix_kernel, alpha=alpha),
        grid=(n // tm,),
        in_specs=[pl.BlockSpec((tm, d), row), pl.BlockSpec((tm, og.shape[1]), row),
                  pl.BlockSpec((tm, od.shape[1]), row),
                  pl.BlockSpec((1, d), fix), pl.BlockSpec((1, d), fix), pl.BlockSpec(wo.shape, fix),
                  pl.BlockSpec((1, d), fix), pl.BlockSpec((1, d), fix),
                  pl.BlockSpec(rw.shape, lambda i: (0, 0, 0)), pl.BlockSpec((1, LANES), fix),
                  pl.BlockSpec((tm, tm), fix)],
        out_specs=[pl.BlockSpec((tm, d), row), pl.BlockSpec((tm, LANES), row), pl.BlockSpec((1, LANES), fix)],
        out_shape=[jax.ShapeDtypeStruct((n, d), BF16), jax.ShapeDtypeStruct((n, LANES), F32),
                   jax.ShapeDtypeStruct((1, LANES), F32)],
        scratch_shapes=[pltpu.VMEM((1, LANES), F32)],
        compiler_params=_params("arbitrary"),
        name="mix",
    )(x2, og, od, li_g, li_b, wo, g1, b1, rw, rb, tri)


def _expert_kernel(be_ref, nu_ref, x_ref, w1_ref, w3_ref, w2_ref, *refs, block0):
    y_ref, w1b, w3b, w2b = refs[-4:]
    i = pl.program_id(0)
    g = i + block0

    @pl.when((i == 0) | (be_ref[g] != be_ref[jnp.maximum(g - 1, 0)]))
    def _():
        w1b[...] = w1_ref[0].astype(BF16)
        w3b[...] = w3_ref[0].astype(BF16)
        w2b[...] = w2_ref[0].astype(BF16)

    @pl.when(g < nu_ref[0])
    def _():
        x = x_ref[...]
        h1 = _dot(x, w1b[...])
        h3 = _dot(x, w3b[...])
        y_ref[...] = _dot((_silu(h1) * h3).astype(BF16), w2b[...]).astype(BF16)

    @pl.when(g >= nu_ref[0])
    def _():
        y_ref[...] = jnp.zeros(y_ref.shape, y_ref.dtype)


def _expert_block(n_tokens):
    per_expert = max(1, n_tokens * TOP_K // N_EXPERTS)
    return int(min(EXPERT_BLOCK_MAX, max(EXPERT_BLOCK_MIN, 2 ** int(math.log2(per_expert)))))


def _experts(xb, block0, y_prev, n_blocks, block_expert, n_used, w1, w3, w2, blk):
    rows, d = xb.shape
    de = w1.shape[2]
    weight = lambda i, be, nu: (be[i + block0], 0, 0)
    in_specs = [pl.BlockSpec((blk, d), lambda i, be, nu: (i, 0)),
                pl.BlockSpec((1, d, de), weight), pl.BlockSpec((1, d, de), weight),
                pl.BlockSpec((1, de, d), weight)]
    operands = [block_expert, n_used, xb, w1, w3, w2]
    aliases = {}
    if y_prev is not None:
        in_specs.append(pl.BlockSpec(memory_space=pl.ANY))
        aliases = {len(operands): 0}
        operands.append(y_prev)
    grid_spec = pltpu.PrefetchScalarGridSpec(
        num_scalar_prefetch=2,
        grid=(rows // blk,),
        in_specs=in_specs,
        out_specs=pl.BlockSpec((blk, d), lambda i, be, nu: (i + block0, 0)),
        scratch_shapes=[pltpu.VMEM((d, de), BF16), pltpu.VMEM((d, de), BF16), pltpu.VMEM((de, d), BF16)],
    )
    return pl.pallas_call(
        functools.partial(_expert_kernel, block0=block0),
        grid_spec=grid_spec,
        out_shape=jax.ShapeDtypeStruct((n_blocks * blk, d), BF16),
        input_output_aliases=aliases,
        compiler_params=_params("arbitrary"),
        name="experts",
    )(*operands)


def _dispatch_plan(plan, counts_row, blk):
    n = plan.shape[1]
    p = n * TOP_K
    n_blocks = -(-p // blk) + N_EXPERTS
    counts = counts_row[0, 0:N_EXPERTS].astype(jnp.int32)
    starts = jnp.cumsum(counts) - counts
    padded = (counts + blk - 1) // blk * blk
    padded_end = jnp.cumsum(padded)
    padded_start = padded_end - padded
    expert = plan[0:TOP_K].astype(jnp.int32)
    rank = plan[4:4 + TOP_K].astype(jnp.int32)
    eids = jnp.arange(N_EXPERTS, dtype=jnp.int32)
    dest = rank + jnp.sum(jnp.where(expert[..., None] == eids, padded_start, 0), axis=-1)
    token = jnp.broadcast_to(jnp.arange(n, dtype=jnp.int32), (TOP_K, n))
    _, order_tok = lax.sort((dest.reshape(p), token.reshape(p)), num_keys=1)
    block_start = jnp.arange(n_blocks, dtype=jnp.int32) * blk
    block_expert = jnp.minimum(
        jnp.sum((padded_end[None, :] <= block_start[:, None]).astype(jnp.int32), axis=1), N_EXPERTS - 1)
    row_id = jnp.arange(n_blocks * blk, dtype=jnp.int32).reshape(n_blocks, blk)
    j = row_id - padded_start[block_expert][:, None]
    src = jnp.clip(starts[block_expert][:, None] + j, 0, p - 1)
    valid = (j < counts[block_expert][:, None]).reshape(-1)
    tok_pad = jnp.where(valid, order_tok[src.reshape(-1)], row_id.reshape(-1) % n)
    n_used = (padded_end[-1:] // blk).astype(jnp.int32)
    return tok_pad, dest, block_expert, n_used


def _final_kernel(x1_ref, y0_ref, y1_ref, route_ref, g_ref, b_ref, o_ref, *, alpha):
    route = route_ref[...]
    moe = route[:, 2:3] * y0_ref[...].astype(F32) + route[:, 3:4] * y1_ref[...].astype(F32)
    o_ref[...] = _layernorm(alpha * x1_ref[...].astype(F32) + moe, g_ref[...], b_ref[...])


def _final(x1, y0, y1, route, g2, b2, alpha):
    n, d = x1.shape
    tm = min(TOKEN_TILE, n)
    row = lambda i: (i, 0)
    fix = lambda i: (0, 0)
    return pl.pallas_call(
        functools.partial(_final_kernel, alpha=alpha),
        grid=(n // tm,),
        in_specs=[pl.BlockSpec((tm, d), row), pl.BlockSpec((tm, d), row), pl.BlockSpec((tm, d), row),
                  pl.BlockSpec((tm, LANES), row), pl.BlockSpec((1, d), fix), pl.BlockSpec((1, d), fix)],
        out_specs=pl.BlockSpec((tm, d), row),
        out_shape=jax.ShapeDtypeStruct((n, d), F32),
        compiler_params=_params("parallel"),
        name="final",
    )(x1, y0, y1, route, g2, b2)


def _pad_lanes(v, width=LANES):
    v = v.reshape(1, -1).astype(F32)
    return jnp.pad(v, ((0, 0), (0, width - v.shape[1])))


def _layer(x, conv_prev, s_prev, k_cache, v_cache, p, alpha, lam_init):
    b, l, d = x.shape
    n = b * l
    x2 = x.reshape(n, d)
    conv_prev8 = jnp.pad(conv_prev.astype(F32), ((0, 0), (CONV_PAD - (CONV_W - 1), 0), (0, 0)))
    yc, z, q, k, v, kb, vb, ab, convn = _proj(x2, l, p["ln_in_g"], p["ln_in_b"], p["w_main"], p["w_ab"],
                                              conv_prev8, p["conv_w"])
    conv_new = convn[:, CONV_PAD - (CONV_W - 1):, :]
    og, s_new = _gdn(yc.reshape(b, l, -1), z.reshape(b, l, -1), ab.reshape(b, l, LANES),
                     s_prev.astype(F32), p["alog_row"], p["dtb_row"], p["nw_row"])

    q3 = q.reshape(b, l, -1)
    kb3 = kb.reshape(b, l, -1)
    vb3 = vb.reshape(b, l, -1)
    if k_cache is None:
        od = _attn_prompt(q3, kb3, vb3, p["slopes"], p["lamv"], p["sw_row"], lam_init)
    else:
        past = k_cache.shape[1]
        od = _attn_sample(q3, kb3, vb3, k_cache.reshape(b, past * DIFF_HEADS, DIFF_DV),
                          v_cache.reshape(b, past * DIFF_HEADS, DIFF_DV),
                          p["slopes"], p["lamv"], p["sw_row"], lam_init)

    x1b, route, counts_row = _mix(x2, og.reshape(n, -1), od.reshape(n, -1), p["ln_in_g"], p["ln_in_b"],
                                  p["wo"], p["ln1_g"], p["ln1_b"], p["rw"], p["rb"], alpha)

    blk = _expert_block(n)
    plan = jnp.transpose(route[:, 0:PLAN_ROWS])
    tok_pad, dest, block_expert, n_used = _dispatch_plan(plan, counts_row, blk)
    n_blocks = block_expert.shape[0]
    n_parts = EXPERT_PARTS if n_blocks >= EXPERT_PARTS * N_EXPERTS else 1
    bounds = [n_blocks * i // n_parts for i in range(n_parts + 1)]
    xbs = [jnp.take(x1b, tok_pad[lo * blk:hi * blk], axis=0, mode="clip") for lo, hi in zip(bounds, bounds[1:])]
    yb = None
    for lo, xb in zip(bounds, xbs):
        yb = _experts(xb, lo, yb, n_blocks, block_expert, n_used, p["w1"], p["w3"], p["w2"], blk)
    y0 = jnp.take(yb, dest[0], axis=0, mode="clip")
    y1 = jnp.take(yb, dest[1], axis=0, mode="clip")
    y = _final(x1b, y0, y1, route, p["ln2_g"], p["ln2_b"], alpha)

    return (y.reshape(b, l, d), conv_new, s_new,
            k.reshape(b, l, DIFF_HEADS, 2 * DIFF_DH), v.reshape(b, l, DIFF_HEADS, DIFF_DV))


def kernel(x_prompt, x_sample, cache_attn_k, cache_attn_v, state_gdn, state_conv, ln_in_g, ln_in_b, w_in, conv_w, gdn_a_log, gdn_dt_bias, gdn_norm_w, lam_q1, lam_k1, lam_q2, lam_k2, subln_w, w_o, ln1_g, ln1_b, router_g_w, router_g_b, router_e_w, router_e_b, w1, w3, w2, ln2_g, ln2_b):
    depth = w_in.shape[0]
    assert depth == 1, "single-layer step"
    alpha = (2 * depth) ** 0.25
    lam_init = 0.8 - 0.6 * math.exp(-0.3 * 0)
    row = lambda t: t.reshape(1, -1).astype(F32)

    wi = w_in[0]
    conv_ch = conv_w.shape[-1]
    gw = GDN_HEADS * GDN_D
    c_ab = conv_ch + gw
    c_q = c_ab + 2 * GDN_HEADS
    w_main = jnp.concatenate([wi[:, :c_ab], wi[:, c_q:]], axis=1).astype(BF16)
    w_ab = jnp.pad(wi[:, c_ab:c_q], ((0, 0), (0, LANES - 2 * GDN_HEADS))).astype(BF16)
    rcat = jnp.concatenate([router_g_w[0], router_e_w[0]], axis=1)
    rcat = jnp.pad(rcat, ((0, 0), (0, LANES - rcat.shape[1])))
    r_hi = rcat.astype(BF16)
    r_lo = (rcat - r_hi.astype(F32)).astype(BF16)
    p = {
        "ln_in_g": row(ln_in_g), "ln_in_b": row(ln_in_b), "w_main": w_main, "w_ab": w_ab,
        "conv_w": conv_w[0].astype(F32), "alog_row": _pad_lanes(gdn_a_log[0]),
        "dtb_row": _pad_lanes(gdn_dt_bias[0]), "nw_row": row(gdn_norm_w[0]),
        "slopes": jnp.asarray(2.0 ** (-8.0 * (np.arange(DIFF_HEADS) + 1) / DIFF_HEADS), F32),
        "lamv": jnp.stack([lam_q1[0], lam_k1[0], lam_q2[0], lam_k2[0]]).astype(F32),
        "sw_row": row(subln_w[0]), "wo": w_o[0].astype(BF16),
        "ln1_g": row(ln1_g[0]), "ln1_b": row(ln1_b[0]),
        "rw": jnp.stack([r_hi, r_lo]), "rb": _pad_lanes(jnp.concatenate([router_g_b[0], router_e_b[0]])),
        "w1": w1[0], "w3": w3[0], "w2": w2[0],
        "ln2_g": row(ln2_g[0]), "ln2_b": row(ln2_b[0]),
    }

    bp = x_prompt.shape[0]
    conv0 = jnp.zeros((bp, CONV_W - 1, conv_ch), F32)
    s0 = jnp.zeros((bp, GDN_HEADS, GDN_D, GDN_D), F32)
    yp, cp, sp, kp, vp = _layer(x_prompt, conv0, s0, None, None, p, alpha, lam_init)
    ys, cs, ss, ks, vs = _layer(x_sample, state_conv[0], state_gdn[0], cache_attn_k[0], cache_attn_v[0],
                                p, alpha, lam_init)
    return (yp, ys, kp[None], vp[None], sp[None], cp[None], ks[None], vs[None], ss[None], cs[None])
```

```python
import functools
import math

import jax
import jax.numpy as jnp
import numpy as np
from jax import lax
from jax.experimental import pallas as pl
from jax.experimental.pallas import tpu as pltpu

F32 = jnp.float32
BF16 = jnp.bfloat16

CHUNK = 64
CONV_W = 4
GDN_HEADS = 4
GDN_D = 128
DIFF_HEADS = 4
DIFF_DH = 64
DIFF_DV = 2 * DIFF_DH
N_GROUPS = 4
EXPERTS_PER_GROUP = 8
N_EXPERTS = N_GROUPS * EXPERTS_PER_GROUP
TOP_K = 2
NEG_INF = -1e30
LOG2E = math.log2(math.e)
LANES = 128
CONV_PAD = 8

TOKEN_TILE = 512
GDN_TILE = 64
GDN_SEQS = 4
ATTN_TILE = 256
ATTN_ROWS = 64
CACHE_TILE = 1024
EXPERT_BLOCK_MAX = 512
EXPERT_BLOCK_MIN = 128
PLAN_ROWS = 8
VMEM_LIMIT = 48 * 1024 * 1024


def _params(*sem):
    return pltpu.CompilerParams(dimension_semantics=sem, vmem_limit_bytes=VMEM_LIMIT)


def _dot(a, b):
    return jnp.dot(a, b, preferred_element_type=F32)


def _dot_nt(a, b):
    return lax.dot_general(a, b, (((1,), (1,)), ((), ())), preferred_element_type=F32)


def _sigmoid(x):
    return 1.0 / (1.0 + jnp.exp(-x))


def _silu(x):
    return x * _sigmoid(x)


def _softplus(x):
    return jnp.maximum(x, 0.0) + jnp.log(1.0 + jnp.exp(-jnp.abs(x)))


def _layernorm(x, g, b, eps=1e-5):
    mu = jnp.mean(x, axis=-1, keepdims=True)
    xc = x - mu
    var = jnp.mean(xc * xc, axis=-1, keepdims=True)
    return xc * lax.rsqrt(var + eps) * g + b


def _split3(x):
    hi = x.astype(BF16)
    r1 = x - hi.astype(F32)
    mid = r1.astype(BF16)
    lo = (r1 - mid.astype(F32)).astype(BF16)
    return hi, mid, lo


def _proj_kernel(x_ref, g_ref, b_ref, w_ref, wab_ref, convp_ref, cw_ref,
                 y_ref, z_ref, q_ref, k_ref, v_ref, kb_ref, vb_ref, ab_ref, convn_ref,
                 xc_scr, *, rows, n_seq, tiles_per_seq):
    xb = _layernorm(x_ref[...], g_ref[...], b_ref[...]).astype(BF16)
    w = GDN_HEADS * GDN_D
    stride = rows + CONV_PAD
    if tiles_per_seq > 1:
        first = pl.program_id(0) % tiles_per_seq == 0

        @pl.when(first)
        def _():
            xc_scr[0:CONV_PAD, :] = convp_ref[0]

        @pl.when(jnp.logical_not(first))
        def _():
            xc_scr[0:CONV_PAD, :] = xc_scr[rows:rows + CONV_PAD, :]
    else:
        for s in range(n_seq):
            xc_scr[s * stride:s * stride + CONV_PAD, :] = convp_ref[s]
    cw = cw_ref[...]

    for j in range(3):
        cs = slice(j * w, (j + 1) * w)
        qkv = _dot(xb, w_ref[:, cs])
        for s in range(n_seq):
            base = s * stride + CONV_PAD
            xc_scr[base:base + rows, cs] = qkv[s * rows:(s + 1) * rows, :]
            lo = base - (CONV_W - 1)
            y = cw[0:1, cs] * xc_scr[lo:lo + rows, cs]
            for t in range(1, CONV_W):
                y = y + cw[t:t + 1, cs] * xc_scr[lo + t:lo + t + rows, cs]
            hy = 0.5 * y
            y_ref[s * rows:(s + 1) * rows, cs] = (hy + hy * jnp.tanh(hy)).astype(BF16)
    for s in range(n_seq):
        convn_ref[s] = xc_scr[s * stride + rows:s * stride + rows + CONV_PAD, :]
    z_ref[...] = _dot(xb, w_ref[:, 3 * w:4 * w]).astype(BF16)
    q_ref[...] = _dot(xb, w_ref[:, 4 * w:5 * w]).astype(BF16)
    tm = x_ref.shape[0]
    k = _dot(xb, w_ref[:, 5 * w:6 * w])
    kb_ref[...] = k.astype(BF16)
    v = _dot(xb, w_ref[:, 6 * w:7 * w])
    vb_ref[...] = v.astype(BF16)
    for h in range(DIFF_HEADS):
        k_ref[pl.ds(h, tm, stride=DIFF_HEADS), :] = k[:, h * DIFF_DV:(h + 1) * DIFF_DV]
        v_ref[pl.ds(h, tm, stride=DIFF_HEADS), :] = v[:, h * DIFF_DV:(h + 1) * DIFF_DV]
    ab_ref[...] = _dot(xb, wab_ref[...])


def _proj(x2, seq_len, ln_g, ln_b, w_main, w_ab, conv_prev8, conv_w):
    n, d = x2.shape
    tm = min(TOKEN_TILE, n)
    cc = conv_w.shape[1]
    rows = min(seq_len, tm)
    n_seq = tm // rows
    tiles_per_seq = seq_len // rows
    assert tm % rows == 0 and seq_len % rows == 0 and rows % CONV_PAD == 0
    row = lambda i: (i, 0)
    fix = lambda i: (0, 0)
    seq = lambda i: (i // tiles_per_seq, 0, 0)
    hd = DIFF_HEADS * DIFF_DV
    outs = [(1, cc, BF16), (1, hd, BF16), (1, hd, BF16), (DIFF_HEADS, DIFF_DV, F32),
            (DIFF_HEADS, DIFF_DV, F32), (1, hd, BF16), (1, hd, BF16), (1, LANES, F32)]
    return pl.pallas_call(
        functools.partial(_proj_kernel, rows=rows, n_seq=n_seq, tiles_per_seq=tiles_per_seq),
        grid=(n // tm,),
        in_specs=[pl.BlockSpec((tm, d), row), pl.BlockSpec((1, d), fix), pl.BlockSpec((1, d), fix),
                  pl.BlockSpec(w_main.shape, fix), pl.BlockSpec(w_ab.shape, fix),
                  pl.BlockSpec((n_seq, CONV_PAD, cc), seq), pl.BlockSpec((CONV_W, cc), fix)],
        out_specs=[pl.BlockSpec((tm * r, c), row) for r, c, _ in outs]
                  + [pl.BlockSpec((n_seq, CONV_PAD, cc), seq)],
        out_shape=[jax.ShapeDtypeStruct((n * r, c), t) for r, c, t in outs]
                  + [jax.ShapeDtypeStruct(conv_prev8.shape, F32)],
        scratch_shapes=[pltpu.VMEM((n_seq * (rows + CONV_PAD), cc), F32)],
        compiler_params=_params("arbitrary"),
        name="proj",
    )(x2, ln_g, ln_b, w_main, w_ab, conv_prev8, conv_w)


def _gdn_kernel(y_ref, z_ref, ab_ref, s0_ref, alog_ref, dtb_ref, nw_ref,
                o_ref, sn_ref,
                s_scr, cum_scr, a_scr, p_scr, attn_scr, rhs_scr, sol_scr, qg_scr, kd_scr, u_scr, os_scr,
                *, nb, tl, c):
    l = pl.program_id(1)
    d = GDN_D
    hw = GDN_HEADS * d

    @pl.when(l == 0)
    def _():
        s_scr[...] = s0_ref[...]

    g_all = [-jnp.exp(alog_ref[...]) * _softplus(ab_ref[bi] + dtb_ref[...]) for bi in range(nb)]
    beta_all = [_sigmoid(ab_ref[bi]) for bi in range(nb)]
    nw = nw_ref[...]

    rows = lax.broadcasted_iota(jnp.int32, (c, c), 0)
    cols = lax.broadcasted_iota(jnp.int32, (c, c), 1)
    incl = cols <= rows
    strict = cols < rows
    tri = jnp.where(incl, 1.0, 0.0).astype(BF16)
    grow = lax.broadcasted_iota(jnp.int32, (c, LANES), 0)
    glane = lax.broadcasted_iota(jnp.int32, (c, LANES), 1)

    n_chunks = tl // c
    groups = [(ci, bi, h) for ci in range(n_chunks) for bi in range(nb) for h in range(GDN_HEADS)]
    eye = jnp.where(rows == cols, 1.0, 0.0)

    gcols = [jnp.broadcast_to(g_all[bi][ci * c:(ci + 1) * c, h:h + 1], (c, LANES)) for ci, bi, h in groups]
    gmat = jnp.concatenate([blk for gc in gcols for blk in (jnp.where(grow > glane, gc, 0.0), gc)], axis=1)
    g_hi, g_mid, g_lo = _split3(gmat)
    cum_scr[...] = _dot(tri, g_hi) + _dot(tri, g_mid) + _dot(tri, g_lo)

    for g, (ci, bi, h) in enumerate(groups):
        r0 = ci * c
        beta = jnp.broadcast_to(beta_all[bi][r0:r0 + c, GDN_HEADS + h:GDN_HEADS + h + 1], (c, LANES))
        rel = cum_scr[:, 2 * g * LANES:2 * g * LANES + c]
        gam = cum_scr[:, (2 * g + 1) * LANES:(2 * g + 2) * LANES]
        g_last = gam[c - 1:c, :]
        decay = jnp.where(incl, jnp.exp(jnp.where(incl, rel, 0.0)), 0.0)
        eg = jnp.exp(gam)
        ek = jnp.exp(g_last - gam)

        qh = y_ref[bi, r0:r0 + c, h * d:(h + 1) * d].astype(F32)
        kh = y_ref[bi, r0:r0 + c, hw + h * d:hw + (h + 1) * d].astype(F32)
        vh = y_ref[bi, r0:r0 + c, 2 * hw + h * d:2 * hw + (h + 1) * d].astype(F32)
        qn = qh * lax.rsqrt(jnp.sum(qh * qh, axis=-1, keepdims=True) + 1e-6) * (d ** -0.5)
        kn = kh * lax.rsqrt(jnp.sum(kh * kh, axis=-1, keepdims=True) + 1e-6)
        qb = qn.astype(BF16)
        kb = kn.astype(BF16)
        a = jnp.where(strict, beta[:, 0:c] * _dot_nt(kb, kb) * decay, 0.0)
        a_scr[g] = a
        p_scr[g] = eye - a
        attn_scr[g] = (_dot_nt(qb, kb) * decay).astype(BF16)
        rhs_scr[g, :, 0:d] = (vh * beta).astype(BF16)
        rhs_scr[g, :, d:2 * d] = (kn * (beta * eg)).astype(BF16)
        qg_scr[g] = (qn * eg).astype(BF16)
        kd_scr[g] = (kn * ek).astype(BF16)

    n_sq = int(math.log2(c)) - 1
    for it in range(n_sq):
        for g in range(len(groups)):
            akb = a_scr[g].astype(BF16)
            a_scr[g] = _dot(akb, akb)
        for g in range(len(groups)):
            p = p_scr[g]
            p_scr[g] = p + _dot(p.astype(BF16), a_scr[g].astype(BF16))

    for g in range(len(groups)):
        sol_scr[g] = _dot(p_scr[g].astype(BF16), rhs_scr[g])

    for g, (ci, bi, h) in enumerate(groups):
        sb = s_scr[bi, h].astype(BF16)
        u_scr[g] = (sol_scr[g, :, 0:d] - _dot(sol_scr[g, :, d:2 * d].astype(BF16), sb)).astype(BF16)
        os_scr[g] = _dot(qg_scr[g], sb)

    for g, (ci, bi, h) in enumerate(groups):
        r0 = ci * c
        ub = u_scr[g]
        o = os_scr[g] + _dot(attn_scr[g], ub)
        g_last = cum_scr[c - 1:c, (2 * g + 1) * LANES:(2 * g + 2) * LANES]
        s_scr[bi, h] = s_scr[bi, h] * jnp.exp(g_last) + lax.dot_general(
            kd_scr[g], ub, (((0,), (0,)), ((), ())), preferred_element_type=F32)

        zh = z_ref[bi, r0:r0 + c, h * d:(h + 1) * d].astype(F32)
        on = o * lax.rsqrt(jnp.mean(o * o, axis=-1, keepdims=True) + 1e-6) * nw
        o_ref[bi, r0:r0 + c, h * d:(h + 1) * d] = (on * _silu(zh)).astype(BF16)

    @pl.when(l == pl.num_programs(1) - 1)
    def _():
        sn_ref[...] = s_scr[...]


def _gdn(y, z, ab, s0, alog_row, dtb_row, nw_row):
    b, l, cc = y.shape
    c = min(l, CHUNK)
    tl = min(l, GDN_TILE)
    nb = math.gcd(b, GDN_SEQS)
    assert tl == c, "one chunk per grid step: the state recurrence is staged across groups"
    ng = nb * GDN_HEADS
    hw = GDN_HEADS * GDN_D
    tile = lambda i, j: (i, j, 0)
    fix = lambda i, j: (0, 0)
    return pl.pallas_call(
        functools.partial(_gdn_kernel, nb=nb, tl=tl, c=c),
        grid=(b // nb, l // tl),
        in_specs=[pl.BlockSpec((nb, tl, cc), tile), pl.BlockSpec((nb, tl, hw), tile),
                  pl.BlockSpec((nb, tl, LANES), tile),
                  pl.BlockSpec((nb, GDN_HEADS, GDN_D, GDN_D), lambda i, j: (i, 0, 0, 0)),
                  pl.BlockSpec((1, LANES), fix),
                  pl.BlockSpec((1, LANES), fix), pl.BlockSpec((1, GDN_D), fix)],
        out_specs=[pl.BlockSpec((nb, tl, hw), tile),
                   pl.BlockSpec((nb, GDN_HEADS, GDN_D, GDN_D), lambda i, j: (i, 0, 0, 0))],
        out_shape=[jax.ShapeDtypeStruct((b, l, hw), BF16),
                   jax.ShapeDtypeStruct((b, GDN_HEADS, GDN_D, GDN_D), F32)],
        scratch_shapes=[pltpu.VMEM((nb, GDN_HEADS, GDN_D, GDN_D), F32),
                        pltpu.VMEM((c, 2 * ng * LANES), F32), pltpu.VMEM((ng, c, c), F32),
                        pltpu.VMEM((ng, c, c), F32), pltpu.VMEM((ng, c, c), BF16),
                        pltpu.VMEM((ng, c, 2 * GDN_D), BF16), pltpu.VMEM((ng, c, 2 * GDN_D), F32),
                        pltpu.VMEM((ng, c, GDN_D), BF16), pltpu.VMEM((ng, c, GDN_D), BF16),
                        pltpu.VMEM((ng, c, GDN_D), BF16), pltpu.VMEM((ng, c, GDN_D), F32)],
        compiler_params=_params("parallel", "arbitrary"),
        name="gdn",
    )(y, z, ab, s0, alog_row, dtb_row, nw_row)


def _lambda(lamv, lam_init):
    l1 = jnp.sum(lamv[0:1] * lamv[1:2], axis=-1, keepdims=True)
    l2 = jnp.sum(lamv[2:3] * lamv[3:4], axis=-1, keepdims=True)
    return jnp.exp(l1) - jnp.exp(l2) + lam_init


def _attn_bias(q0, k0, nq, nk, slope):
    qpos = q0 + lax.broadcasted_iota(jnp.int32, (nq, nk), 0)
    kpos = k0 + lax.broadcasted_iota(jnp.int32, (nq, nk), 1)
    dist = jnp.abs(qpos - kpos).astype(F32)
    shift = int(math.log2(CHUNK))
    allowed = (kpos >> shift) <= (qpos >> shift)
    return jnp.where(allowed, -slope * dist, NEG_INF)


def _attn_init(stats):
    for m, l, a in stats:
        m[...] = jnp.full(m.shape, NEG_INF, F32)
        l[...] = jnp.zeros(l.shape, F32)
        a[...] = jnp.zeros(a.shape, F32)


def _attn_update(q, k, v, bias, stats):
    for i, (m, l, a) in enumerate(stats):
        lo, hi = i * DIFF_DH, (i + 1) * DIFF_DH
        s = _dot_nt(q[:, lo:hi], k[:, lo:hi]) + bias
        m_new = jnp.maximum(m[...], jnp.max(s, axis=-1, keepdims=True))
        alpha = jnp.exp(m[...] - m_new)
        p = jnp.exp(s - m_new)
        l[...] = alpha * l[...] + jnp.sum(p, axis=-1, keepdims=True)
        a[...] = alpha * a[...] + _dot(p.astype(BF16), v)
        m[...] = m_new


def _attn_finish(stats, lam, sw, lam_init):
    (_, l1, a1), (_, l2, a2) = stats
    o = a1[...] / l1[...] - lam * (a2[...] / l2[...])
    return o * lax.rsqrt(jnp.mean(o * o, axis=-1, keepdims=True) + 1e-6) * sw * (1.0 - lam_init)


def _attn_prompt_kernel(slopes_ref, lamv_ref, sw_ref, q_ref, k_ref, v_ref, o_ref,
                        vx_scr, bias_scr, q_scr, m_scr, acc_scr, sa_scr, sb_scr, pa_scr, pb_scr, ala_scr, alb_scr,
                        *, seq, tile, lam_init):
    t = tile
    r = 2 * t
    rb = ATTN_ROWS
    dv = DIFF_DV
    slope = slopes_ref[pl.program_id(1)] * LOG2E
    q_scale = (DIFF_DH ** -0.5) * LOG2E
    lam = _lambda(lamv_ref[...], lam_init)
    sw = sw_ref[...]
    shift = int(math.log2(CHUNK))

    vx_scr[:, 0:dv] = v_ref[0]
    vx_scr[:, dv:2 * dv] = jnp.ones((seq, dv), BF16)
    qi_ = lax.broadcasted_iota(jnp.int32, (t, t), 0)
    kj_ = lax.broadcasted_iota(jnp.int32, (t, t), 1)
    rel = (qi_ - kj_).astype(F32)
    bias_scr[0] = -slope * rel
    bias_scr[1] = jnp.where((kj_ >> shift) <= (qi_ >> shift), -slope * jnp.abs(rel), NEG_INF)
    lane = lax.broadcasted_iota(jnp.int32, (t, dv), 1)
    frame_shift = slope * t

    def finalize(qi):
        q0 = qi * t
        o1 = acc_scr[0:t, 0:dv] / acc_scr[0:t, dv:2 * dv]
        o2 = acc_scr[t:r, 0:dv] / acc_scr[t:r, dv:2 * dv]
        o = o1 - lam * o2
        on = o * lax.rsqrt(jnp.mean(o * o, axis=-1, keepdims=True) + 1e-6) * sw * (1.0 - lam_init)
        o_ref[0, q0:q0 + t, :] = on.astype(BF16)

    def q_body(qi):
        q0 = qi * t
        qf = q_ref[0, q0:q0 + t, :].astype(F32) * q_scale
        q_scr[0:t, :] = jnp.where(lane < DIFF_DH, qf, 0.0).astype(BF16)
        q_scr[t:r, :] = jnp.where(lane >= DIFF_DH, qf, 0.0).astype(BF16)
        m_scr[...] = jnp.full(m_scr.shape, NEG_INF, F32)
        if qi > 0:
            finalize(qi - 1)
        acc_scr[...] = jnp.zeros(acc_scr.shape, F32)

        def qk(kj, s_buf):
            k0 = min(kj, qi) * t
            s_buf[...] = _dot_nt(q_scr[...], k_ref[0, k0:k0 + t, :])

        def softmax_accumulate(kj, s_buf, p_buf, al_buf):
            k0 = kj * t
            diag = int(kj == qi)
            for i in range(r // rb):
                r0 = i * rb
                b_lo = r0 % t
                s = s_buf[r0:r0 + rb, :] + bias_scr[diag, b_lo:b_lo + rb, :]
                cols = [s[:, j * LANES:(j + 1) * LANES] for j in range(t // LANES)]
                m_prev = m_scr[r0:r0 + rb, :] - frame_shift
                m_cur = jnp.max(functools.reduce(jnp.maximum, cols), axis=-1, keepdims=True)
                m_new = jnp.maximum(m_prev, m_cur)
                al_buf[r0:r0 + rb, :] = jnp.exp2(m_prev - m_new)
                for j, col in enumerate(cols):
                    p_buf[r0:r0 + rb, j * LANES:(j + 1) * LANES] = jnp.exp2(col - m_new).astype(BF16)
                m_scr[r0:r0 + rb, :] = m_new
            pv = _dot(p_buf[...], vx_scr[k0:k0 + t, :])
            for i in range(r // rb):
                r0 = i * rb
                al = al_buf[r0:r0 + rb, :]
                acc_scr[r0:r0 + rb, :] = (jnp.concatenate([al, al], axis=1) * acc_scr[r0:r0 + rb, :]
                                          + pv[r0:r0 + rb, :])

        sets = ((sa_scr, pa_scr, ala_scr), (sb_scr, pb_scr, alb_scr))
        qk(0, sa_scr)
        for kj in range(qi + 1):
            qk(kj + 1, sets[(kj + 1) % 2][0])
            softmax_accumulate(kj, *sets[kj % 2])

    for qi in range(seq // t):
        q_body(qi)
    finalize(seq // t - 1)


def _attn_prompt(q, kb, vb, slopes, lamv, sw_row, lam_init):
    b, l, _ = q.shape
    t = min(ATTN_TILE, l)
    assert t % CHUNK == 0 and l % t == 0 and t % LANES == 0 and t % ATTN_ROWS == 0
    head = lambda i, h: (i, 0, h)
    fix = lambda i, h: (0, 0)
    dv = DIFF_DV
    scratch = [pltpu.VMEM((l, 2 * dv), BF16), pltpu.VMEM((2, t, t), F32),
               pltpu.VMEM((2 * t, dv), BF16), pltpu.VMEM((2 * t, LANES), F32),
               pltpu.VMEM((2 * t, 2 * dv), F32),
               pltpu.VMEM((2 * t, t), F32), pltpu.VMEM((2 * t, t), F32),
               pltpu.VMEM((2 * t, t), BF16), pltpu.VMEM((2 * t, t), BF16),
               pltpu.VMEM((2 * t, LANES), F32), pltpu.VMEM((2 * t, LANES), F32)]
    return pl.pallas_call(
        functools.partial(_attn_prompt_kernel, seq=l, tile=t, lam_init=lam_init),
        grid=(b, DIFF_HEADS),
        in_specs=[pl.BlockSpec(memory_space=pltpu.SMEM),
                  pl.BlockSpec(lamv.shape, fix), pl.BlockSpec(sw_row.shape, fix),
                  pl.BlockSpec((1, l, dv), head), pl.BlockSpec((1, l, dv), head),
                  pl.BlockSpec((1, l, dv), head)],
        out_specs=pl.BlockSpec((1, l, dv), head),
        out_shape=jax.ShapeDtypeStruct((b, l, DIFF_HEADS * dv), BF16),
        scratch_shapes=scratch,
        compiler_params=_params("parallel", "parallel"),
        name="attn_prompt",
    )(slopes, lamv, sw_row, q, kb, vb)


def _attn_sample_kernel(slopes_ref, lamv_ref, sw_ref, q_ref, kc_ref, vc_ref, kn_ref, vn_ref, o_ref,
                        m1, l1, a1, m2, l2, a2, *, past, tk, lam_init):
    j = pl.program_id(1)
    n_cache = past // tk
    lq = q_ref.shape[1]
    heads = range(DIFF_HEADS)
    cols = [slice(h * DIFF_DV, (h + 1) * DIFF_DV) for h in heads]
    stats = [((m1.at[h], l1.at[h], a1.at[h]), (m2.at[h], l2.at[h], a2.at[h])) for h in heads]
    qs = [(q_ref[0, :, cols[h]].astype(F32) * (DIFF_DH ** -0.5)).astype(BF16) for h in heads]

    @pl.when(j == 0)
    def _():
        for h in heads:
            _attn_init(stats[h])

    @pl.when(j < n_cache)
    def _():
        for h in heads:
            k = kc_ref[0, pl.ds(h, tk, stride=DIFF_HEADS), :].astype(BF16)
            v = vc_ref[0, pl.ds(h, tk, stride=DIFF_HEADS), :].astype(BF16)
            _attn_update(qs[h], k, v, _attn_bias(past, j * tk, lq, tk, slopes_ref[h]), stats[h])

    @pl.when(j == n_cache)
    def _():
        lam = _lambda(lamv_ref[...], lam_init)
        for h in heads:
            _attn_update(qs[h], kn_ref[0, :, cols[h]], vn_ref[0, :, cols[h]],
                         _attn_bias(past, past, lq, lq, slopes_ref[h]), stats[h])
            o_ref[0, :, cols[h]] = _attn_finish(stats[h], lam, sw_ref[...], lam_init).astype(BF16)


def _attn_sample(q, kb, vb, k_cache, v_cache, slopes, lamv, sw_row, lam_init):
    b, l, hd = q.shape
    past = k_cache.shape[1] // DIFF_HEADS
    tk = min(CACHE_TILE, past)
    assert past % tk == 0 and past % CHUNK == 0 and l <= CHUNK
    n_cache = past // tk
    per_b = lambda i, j: (i, 0, 0)
    cache = lambda i, j: (i, jnp.minimum(j, n_cache - 1), 0)
    fix = lambda i, j: (0, 0)
    stat = [pltpu.VMEM((DIFF_HEADS, l, 1), F32), pltpu.VMEM((DIFF_HEADS, l, 1), F32),
            pltpu.VMEM((DIFF_HEADS, l, DIFF_DV), F32)]
    return pl.pallas_call(
        functools.partial(_attn_sample_kernel, past=past, tk=tk, lam_init=lam_init),
        grid=(b, n_cache + 1),
        in_specs=[pl.BlockSpec(memory_space=pltpu.SMEM),
                  pl.BlockSpec(lamv.shape, fix), pl.BlockSpec(sw_row.shape, fix),
                  pl.BlockSpec((1, l, hd), per_b),
                  pl.BlockSpec((1, tk * DIFF_HEADS, DIFF_DV), cache),
                  pl.BlockSpec((1, tk * DIFF_HEADS, DIFF_DV), cache),
                  pl.BlockSpec((1, l, hd), per_b), pl.BlockSpec((1, l, hd), per_b)],
        out_specs=pl.BlockSpec((1, l, hd), per_b),
        out_shape=jax.ShapeDtypeStruct((b, l, hd), BF16),
        scratch_shapes=stat + stat,
        compiler_params=_params("parallel", "arbitrary"),
        name="attn_sample",
    )(slopes, lamv, sw_row, q, k_cache, v_cache, kb, vb)


def _mix_kernel(x_ref, og_ref, od_ref, lig_ref, lib_ref, wo_ref, g1_ref, b1_ref, rw_ref, rb_ref, tri_ref,
                x1b_ref, route_ref, cnt_ref, cnt_scr, *, alpha):
    xn = _layernorm(x_ref[...], lig_ref[...], lib_ref[...])
    hw = og_ref.shape[1]
    mixed = _dot(og_ref[...], wo_ref[0:hw, :]) + _dot(od_ref[...], wo_ref[hw:, :])
    x1 = _layernorm(alpha * xn + mixed, g1_ref[...], b1_ref[...])
    x1b_ref[...] = x1.astype(BF16)

    x_hi = x1.astype(BF16)
    x_lo = (x1 - x_hi.astype(F32)).astype(BF16)
    lg = (_dot(x_hi, rw_ref[0]) + _dot(x_lo, rw_ref[0]) + _dot(x_hi, rw_ref[1])) + rb_ref[...]
    lane = lax.broadcasted_iota(jnp.int32, lg.shape, 1)
    lanef = lane.astype(F32)
    big = float(LANES)

    gl = jnp.where(lane < N_GROUPS, lg, -jnp.inf)
    gmax = jnp.max(gl, axis=-1, keepdims=True)
    grp = jnp.min(jnp.where(gl == gmax, lanef, big), axis=-1, keepdims=True)
    g_gate = 1.0 / jnp.sum(jnp.where(lane < N_GROUPS, jnp.exp(gl - gmax), 0.0), axis=-1, keepdims=True)

    e_lo = N_GROUPS + grp * EXPERTS_PER_GROUP
    el = jnp.where((lanef >= e_lo) & (lanef < e_lo + EXPERTS_PER_GROUP), lg, -jnp.inf)
    v1 = jnp.max(el, axis=-1, keepdims=True)
    i1 = jnp.min(jnp.where(el == v1, lanef, big), axis=-1, keepdims=True)
    el2 = jnp.where(lanef == i1, -jnp.inf, el)
    v2 = jnp.max(el2, axis=-1, keepdims=True)
    i2 = jnp.min(jnp.where(el2 == v2, lanef, big), axis=-1, keepdims=True)
    e21 = jnp.exp(v2 - v1)
    p1 = 1.0 / (1.0 + e21)

    @pl.when(pl.program_id(0) == 0)
    def _():
        cnt_scr[...] = jnp.zeros(cnt_scr.shape, F32)

    e0 = i1 - N_GROUPS
    e1 = i2 - N_GROUPS
    hot0 = lanef == e0
    hot1 = lanef == e1
    onehot = jnp.where(hot0 | hot1, 1.0, 0.0)
    before = _dot(tri_ref[...], onehot.astype(BF16)) + cnt_scr[...]
    rank0 = jnp.sum(jnp.where(hot0, before, 0.0), axis=-1, keepdims=True)
    rank1 = jnp.sum(jnp.where(hot1, before, 0.0), axis=-1, keepdims=True)
    tm = onehot.shape[0]
    cnt_scr[...] = before[tm - 1:tm, :] + onehot[tm - 1:tm, :]
    cnt_ref[...] = cnt_scr[...]

    route_ref[...] = jnp.where(lane == 0, e0,
                     jnp.where(lane == 1, e1,
                     jnp.where(lane == 2, g_gate * p1,
                     jnp.where(lane == 3, g_gate * (e21 * p1),
                     jnp.where(lane == 4, rank0,
                     jnp.where(lane == 5, rank1, 0.0))))))


def _mix(x2, og, od, li_g, li_b, wo, g1, b1, rw, rb, alpha):
    n, d = x2.shape
    tm = min(TOKEN_TILE, n)
    row = lambda i: (i, 0)
    fix = lambda i: (0, 0)
    tri = jnp.tri(tm, k=-1, dtype=BF16)
    return pl.pallas_call(
        functools.partial(_mix_kernel, alpha=alpha),
        grid=(n // tm,),
        in_specs=[pl.BlockSpec((tm, d), row), pl.BlockSpec((tm, og.shape[1]), row),
                  pl.BlockSpec((tm, od.shape[1]), row),
                  pl.BlockSpec((1, d), fix), pl.BlockSpec((1, d), fix), pl.BlockSpec(wo.shape, fix),
                  pl.BlockSpec((1, d), fix), pl.BlockSpec((1, d), fix),
                  pl.BlockSpec(rw.shape, lambda i: (0, 0, 0)), pl.BlockSpec((1, LANES), fix),
                  pl.BlockSpec((tm, tm), fix)],
        out_specs=[pl.BlockSpec((tm, d), row), pl.BlockSpec((tm, LANES), row), pl.BlockSpec((1, LANES), fix)],
        out_shape=[jax.ShapeDtypeStruct((n, d), BF16), jax.ShapeDtypeStruct((n, LANES), F32),
                   jax.ShapeDtypeStruct((1, LANES), F32)],
        scratch_shapes=[pltpu.VMEM((1, LANES), F32)],
        compiler_params=_params("arbitrary"),
        name="mix",
    )(x2, og, od, li_g, li_b, wo, g1, b1, rw, rb, tri)


def _expert_kernel(be_ref, nu_ref, x_ref, w1_ref, w3_ref, w2_ref, y_ref, w1b, w3b, w2b):
    g = pl.program_id(0)

    @pl.when((g == 0) | (be_ref[g] != be_ref[jnp.maximum(g - 1, 0)]))
    def _():
        w1b[...] = w1_ref[0].astype(BF16)
        w3b[...] = w3_ref[0].astype(BF16)
        w2b[...] = w2_ref[0].astype(BF16)

    @pl.when(g < nu_ref[0])
    def _():
        x = x_ref[...]
        h1 = _dot(x, w1b[...])
        h3 = _dot(x, w3b[...])
        y_ref[...] = _dot((_silu(h1) * h3).astype(BF16), w2b[...]).astype(BF16)

    @pl.when(g >= nu_ref[0])
    def _():
        y_ref[...] = jnp.zeros(y_ref.shape, y_ref.dtype)


def _expert_block(n_tokens):
    per_expert = max(1, n_tokens * TOP_K // N_EXPERTS)
    return int(min(EXPERT_BLOCK_MAX, max(EXPERT_BLOCK_MIN, 2 ** int(math.log2(per_expert)))))


def _experts(xb, block_expert, n_used, w1, w3, w2, blk):
    rows, d = xb.shape
    de = w1.shape[2]
    row = lambda i, be, nu: (i, 0)
    weight = lambda i, be, nu: (be[i], 0, 0)
    grid_spec = pltpu.PrefetchScalarGridSpec(
        num_scalar_prefetch=2,
        grid=(rows // blk,),
        in_specs=[pl.BlockSpec((blk, d), row), pl.BlockSpec((1, d, de), weight),
                  pl.BlockSpec((1, d, de), weight), pl.BlockSpec((1, de, d), weight)],
        out_specs=pl.BlockSpec((blk, d), row),
        scratch_shapes=[pltpu.VMEM((d, de), BF16), pltpu.VMEM((d, de), BF16), pltpu.VMEM((de, d), BF16)],
    )
    return pl.pallas_call(
        _expert_kernel,
        grid_spec=grid_spec,
        out_shape=jax.ShapeDtypeStruct((rows, d), BF16),
        compiler_params=_params("arbitrary"),
        name="experts",
    )(block_expert, n_used, xb, w1, w3, w2)


def _dispatch_plan(plan, counts_row, blk):
    n = plan.shape[1]
    p = n * TOP_K
    n_blocks = -(-p // blk) + N_EXPERTS
    counts = counts_row[0, 0:N_EXPERTS].astype(jnp.int32)
    starts = jnp.cumsum(counts) - counts
    padded = (counts + blk - 1) // blk * blk
    padded_end = jnp.cumsum(padded)
    padded_start = padded_end - padded
    expert = plan[0:TOP_K].astype(jnp.int32)
    rank = plan[4:4 + TOP_K].astype(jnp.int32)
    eids = jnp.arange(N_EXPERTS, dtype=jnp.int32)
    dest = rank + jnp.sum(jnp.where(expert[..., None] == eids, padded_start, 0), axis=-1)
    token = jnp.broadcast_to(jnp.arange(n, dtype=jnp.int32), (TOP_K, n))
    _, order_tok = lax.sort((dest.reshape(p), token.reshape(p)), num_keys=1)
    block_start = jnp.arange(n_blocks, dtype=jnp.int32) * blk
    block_expert = jnp.minimum(
        jnp.sum((padded_end[None, :] <= block_start[:, None]).astype(jnp.int32), axis=1), N_EXPERTS - 1)
    row_id = jnp.arange(n_blocks * blk, dtype=jnp.int32).reshape(n_blocks, blk)
    j = row_id - padded_start[block_expert][:, None]
    src = jnp.clip(starts[block_expert][:, None] + j, 0, p - 1)
    valid = (j < counts[block_expert][:, None]).reshape(-1)
    tok_pad = jnp.where(valid, order_tok[src.reshape(-1)], row_id.reshape(-1) % n)
    n_used = (padded_end[-1:] // blk).astype(jnp.int32)
    return tok_pad, dest, block_expert, n_used


def _final_kernel(x1_ref, y0_ref, y1_ref, route_ref, g_ref, b_ref, o_ref, *, alpha):
    route = route_ref[...]
    moe = route[:, 2:3] * y0_ref[...].astype(F32) + route[:, 3:4] * y1_ref[...].astype(F32)
    o_ref[...] = _layernorm(alpha * x1_ref[...].astype(F32) + moe, g_ref[...], b_ref[...])


def _final(x1, y0, y1, route, g2, b2, alpha):
    n, d = x1.shape
    tm = min(TOKEN_TILE, n)
    row = lambda i: (i, 0)
    fix = lambda i: (0, 0)
    return pl.pallas_call(
        functools.partial(_final_kernel, alpha=alpha),
        grid=(n // tm,),
        in_specs=[pl.BlockSpec((tm, d), row), pl.BlockSpec((tm, d), row), pl.BlockSpec((tm, d), row),
                  pl.BlockSpec((tm, LANES), row), pl.BlockSpec((1, d), fix), pl.BlockSpec((1, d), fix)],
        out_specs=pl.BlockSpec((tm, d), row),
        out_shape=jax.ShapeDtypeStruct((n, d), F32),
        compiler_params=_params("parallel"),
        name="final",
    )(x1, y0, y1, route, g2, b2)


def _pad_lanes(v, width=LANES):
    v = v.reshape(1, -1).astype(F32)
    return jnp.pad(v, ((0, 0), (0, width - v.shape[1])))


def _layer(x, conv_prev, s_prev, k_cache, v_cache, p, alpha, lam_init):
    b, l, d = x.shape
    n = b * l
    x2 = x.reshape(n, d)
    conv_prev8 = jnp.pad(conv_prev.astype(F32), ((0, 0), (CONV_PAD - (CONV_W - 1), 0), (0, 0)))
    yc, z, q, k, v, kb, vb, ab, convn = _proj(x2, l, p["ln_in_g"], p["ln_in_b"], p["w_main"], p["w_ab"],
                                              conv_prev8, p["conv_w"])
    conv_new = convn[:, CONV_PAD - (CONV_W - 1):, :]
    og, s_new = _gdn(yc.reshape(b, l, -1), z.reshape(b, l, -1), ab.reshape(b, l, LANES),
                     s_prev.astype(F32), p["alog_row"], p["dtb_row"], p["nw_row"])

    q3 = q.reshape(b, l, -1)
    kb3 = kb.reshape(b, l, -1)
    vb3 = vb.reshape(b, l, -1)
    if k_cache is None:
        od = _attn_prompt(q3, kb3, vb3, p["slopes"], p["lamv"], p["sw_row"], lam_init)
    else:
        past = k_cache.shape[1]
        od = _attn_sample(q3, kb3, vb3, k_cache.reshape(b, past * DIFF_HEADS, DIFF_DV),
                          v_cache.reshape(b, past * DIFF_HEADS, DIFF_DV),
                          p["slopes"], p["lamv"], p["sw_row"], lam_init)

    x1b, route, counts_row = _mix(x2, og.reshape(n, -1), od.reshape(n, -1), p["ln_in_g"], p["ln_in_b"],
                                  p["wo"], p["ln1_g"], p["ln1_b"], p["rw"], p["rb"], alpha)

    blk = _expert_block(n)
    plan = jnp.transpose(route[:, 0:PLAN_ROWS])
    tok_pad, dest, block_expert, n_used = _dispatch_plan(plan, counts_row, blk)
    xb = jnp.take(x1b, tok_pad, axis=0, mode="clip")
    yb = _experts(xb, block_expert, n_used, p["w1"], p["w3"], p["w2"], blk)
    y0 = jnp.take(yb, dest[0], axis=0, mode="clip")
    y1 = jnp.take(yb, dest[1], axis=0, mode="clip")
    y = _final(x1b, y0, y1, route, p["ln2_g"], p["ln2_b"], alpha)

    return (y.reshape(b, l, d), conv_new, s_new,
            k.reshape(b, l, DIFF_HEADS, 2 * DIFF_DH), v.reshape(b, l, DIFF_HEADS, DIFF_DV))


def kernel(x_prompt, x_sample, cache_attn_k, cache_attn_v, state_gdn, state_conv, ln_in_g, ln_in_b, w_in, conv_w, gdn_a_log, gdn_dt_bias, gdn_norm_w, lam_q1, lam_k1, lam_q2, lam_k2, subln_w, w_o, ln1_g, ln1_b, router_g_w, router_g_b, router_e_w, router_e_b, w1, w3, w2, ln2_g, ln2_b):
    depth = w_in.shape[0]
    assert depth == 1, "single-layer step"
    alpha = (2 * depth) ** 0.25
    lam_init = 0.8 - 0.6 * math.exp(-0.3 * 0)
    row = lambda t: t.reshape(1, -1).astype(F32)

    wi = w_in[0]
    conv_ch = conv_w.shape[-1]
    gw = GDN_HEADS * GDN_D
    c_ab = conv_ch + gw
    c_q = c_ab + 2 * GDN_HEADS
    w_main = jnp.concatenate([wi[:, :c_ab], wi[:, c_q:]], axis=1).astype(BF16)
    w_ab = jnp.pad(wi[:, c_ab:c_q], ((0, 0), (0, LANES - 2 * GDN_HEADS))).astype(BF16)
    rcat = jnp.concatenate([router_g_w[0], router_e_w[0]], axis=1)
    rcat = jnp.pad(rcat, ((0, 0), (0, LANES - rcat.shape[1])))
    r_hi = rcat.astype(BF16)
    r_lo = (rcat - r_hi.astype(F32)).astype(BF16)
    p = {
        "ln_in_g": row(ln_in_g), "ln_in_b": row(ln_in_b), "w_main": w_main, "w_ab": w_ab,
        "conv_w": conv_w[0].astype(F32), "alog_row": _pad_lanes(gdn_a_log[0]),
        "dtb_row": _pad_lanes(gdn_dt_bias[0]), "nw_row": row(gdn_norm_w[0]),
        "slopes": jnp.asarray(2.0 ** (-8.0 * (np.arange(DIFF_HEADS) + 1) / DIFF_HEADS), F32),
        "lamv": jnp.stack([lam_q1[0], lam_k1[0], lam_q2[0], lam_k2[0]]).astype(F32),
        "sw_row": row(subln_w[0]), "wo": w_o[0].astype(BF16),
        "ln1_g": row(ln1_g[0]), "ln1_b": row(ln1_b[0]),
        "rw": jnp.stack([r_hi, r_lo]), "rb": _pad_lanes(jnp.concatenate([router_g_b[0], router_e_b[0]])),
        "w1": w1[0], "w3": w3[0], "w2": w2[0],
        "ln2_g": row(ln2_g[0]), "ln2_b": row(ln2_b[0]),
    }

    bp = x_prompt.shape[0]
    conv0 = jnp.zeros((bp, CONV_W - 1, conv_ch), F32)
    s0 = jnp.zeros((bp, GDN_HEADS, GDN_D, GDN_D), F32)
    yp, cp, sp, kp, vp = _layer(x_prompt, conv0, s0, None, None, p, alpha, lam_init)
    ys, cs, ss, ks, vs = _layer(x_sample, state_conv[0], state_gdn[0], cache_attn_k[0], cache_attn_v[0],
                                p, alpha, lam_init)
    return (yp, ys, kp[None], vp[None], sp[None], cp[None], ks[None], vs[None], ss[None], cs[None])
```
